```python
import math
import jax, jax.numpy as jnp
from jax import lax
import numpy as np

D_MODEL = 1024
BATCH = 8
SEQ = 2048
DEPTH = 4

CHUNK = 64
Q_BLOCK = 128
N_BRANCH = 3
MIX_WIDTH = D_MODEL // 2

N_HEADS_A = 4
HEAD_DIM_A = MIX_WIDTH // (2 * N_HEADS_A)
LAMBDA_INIT_BASE = 0.8
LAMBDA_INIT_SCALE = 0.6
LAMBDA_INIT_DECAY = 0.3
N_REL_BUCKETS = 32
REL_MAX_DIST = 128

SGU_CHUNK = 128
SGU_GROUPS = 4
SGU_WIDTH = MIX_WIDTH
SGU_GROUP_CH = SGU_WIDTH // SGU_GROUPS

N_HEADS_C = 8
Q_LORA = D_MODEL // 4
KV_LORA = D_MODEL // 8
NOPE_DIM = 64
ROPE_DIM = 32
V_DIM_C = MIX_WIDTH // N_HEADS_C
ROPE_THETA = 10000.0

N_GROUPS = 4
EXPERTS_PER_GROUP = 8
TOP_K_IN_GROUP = 2
D_FF_EXPERT = D_MODEL // 4

EPS = 1e-6
NEG_INF = -1e30
MAX_POS_OFFSET = 8192

IN_SIZES = (N_HEADS_A * 2 * HEAD_DIM_A,
            N_HEADS_A * 2 * HEAD_DIM_A,
            N_HEADS_A * 2 * HEAD_DIM_A,
            2 * SGU_WIDTH,
            Q_LORA,
            KV_LORA,
            ROPE_DIM,
            N_BRANCH * D_MODEL)
IN_WIDTH = sum(IN_SIZES)
IN_SPLITS = tuple(int(v) for v in np.cumsum(IN_SIZES)[:-1])

kernel_name = "hybrid_chunk_causal_diffattn_sgu_mla_hmoe"


def rms_norm(x, g):
    xf = x.astype(jnp.float32)
    y = xf * lax.rsqrt(jnp.mean(xf * xf, axis=-1, keepdims=True) + EPS)
    return (y * g.astype(jnp.float32)).astype(x.dtype)


def chunk_mask(q_start, kv_len):
    q_chunk = (q_start + jnp.arange(Q_BLOCK)) // CHUNK
    k_chunk = jnp.arange(kv_len) // CHUNK
    return k_chunk[None, :] <= q_chunk[:, None]


def t5_bucket(rel):
    nb = N_REL_BUCKETS // 2
    max_exact = nb // 2
    n = jnp.abs(rel)
    large = max_exact + (jnp.log(jnp.maximum(n, 1).astype(jnp.float32) / max_exact)
                         / math.log(REL_MAX_DIST / max_exact) * (nb - max_exact)).astype(jnp.int32)
    large = jnp.minimum(large, nb - 1)
    return jnp.where(rel > 0, nb, 0) + jnp.where(n < max_exact, n, large)


def rel_bias_block(table, q_start, kv_len):
    rel = jnp.arange(kv_len)[None, :] - (q_start + jnp.arange(Q_BLOCK))[:, None]
    return jnp.transpose(table[t5_bucket(rel)], (2, 0, 1)).astype(jnp.float32)


def masked_softmax(scores, mask):
    return jax.nn.softmax(jnp.where(mask, scores, NEG_INF), axis=-1)


def block_sweep(fn, seq_len):
    return jnp.concatenate([fn(start, start + Q_BLOCK) for start in range(0, seq_len, Q_BLOCK)], axis=1)


def apply_rope(x, cos, sin):
    half = x.shape[-1] // 2
    x1, x2 = x[..., :half], x[..., half:]
    return jnp.concatenate([x1 * cos - x2 * sin, x2 * cos + x1 * sin], axis=-1)


def differential_attention(q, k, v, lam, lambda_init, out_g, rel_table):
    b, s = q.shape[:2]
    scale = HEAD_DIM_A ** -0.5

    def block(start, end):
        sc = jnp.einsum('bqhmd,bkhmd->bmhqk', q[:, start:end], k[:, :end]).astype(jnp.float32) * scale
        sc = sc + rel_bias_block(rel_table, start, end)
        p = masked_softmax(sc, chunk_mask(start, end))
        attn = p[:, 0] - lam * p[:, 1]
        return jnp.einsum('bhqk,bkhe->bqhe', attn.astype(v.dtype), v[:, :end])

    o = block_sweep(block, s)
    o = rms_norm(o, out_g) * (1.0 - lambda_init)
    return o.reshape(b, s, N_HEADS_A * 2 * HEAD_DIM_A)


def spatial_gating(z, v_g, w_s, b_s):
    b, s = z.shape[:2]
    z = jax.nn.gelu(z)
    u, v = jnp.split(z, 2, axis=-1)
    v = rms_norm(v, v_g).reshape(b, s // SGU_CHUNK, SGU_CHUNK, SGU_GROUPS, SGU_GROUP_CH)
    pos_chunk = jnp.arange(SGU_CHUNK) // CHUNK
    allowed = pos_chunk[None, :] <= pos_chunk[:, None]
    w = jnp.where(allowed[None], w_s, 0)
    vs = jnp.einsum('gij,bnjgc->bnigc', w, v) + jnp.transpose(b_s)[:, :, None]
    return u * vs.reshape(b, s, SGU_WIDTH)


def latent_attention(xq, xkv, xr, cos, sin, lat_g, w_uq, w_ukv, qk_g):
    b, s = xq.shape[:2]
    cq = rms_norm(xq, lat_g[:Q_LORA])
    ckv = rms_norm(xkv, lat_g[Q_LORA:])
    q = (cq @ w_uq).reshape(b, s, N_HEADS_C, NOPE_DIM + ROPE_DIM)
    kv = (ckv @ w_ukv).reshape(b, s, N_HEADS_C, NOPE_DIM + V_DIM_C)
    q_nope = rms_norm(q[..., :NOPE_DIM], qk_g[0, :NOPE_DIM])
    q_rope = apply_rope(rms_norm(q[..., NOPE_DIM:], qk_g[0, NOPE_DIM:]), cos[:, :, None], sin[:, :, None])
    k_nope = rms_norm(kv[..., :NOPE_DIM], qk_g[1, :NOPE_DIM])
    v = kv[..., NOPE_DIM:]
    k_rope = apply_rope(rms_norm(xr, qk_g[1, NOPE_DIM:]), cos, sin)
    scale = (NOPE_DIM + ROPE_DIM) ** -0.5

    def block(start, end):
        sc = (jnp.einsum('bqhd,bkhd->bhqk', q_nope[:, start:end], k_nope[:, :end])
              + jnp.einsum('bqhr,bkr->bhqk', q_rope[:, start:end], k_rope[:, :end])).astype(jnp.float32) * scale
        p = masked_softmax(sc, chunk_mask(start, end))
        return jnp.einsum('bhqk,bkhe->bqhe', p.astype(v.dtype), v[:, :end])

    return block_sweep(block, s).reshape(b, s, N_HEADS_C * V_DIM_C)


def hierarchical_moe(h, wg, bg, we, be, w_gate, w_up, w_down):
    b, s, d = h.shape
    hf = h.reshape(-1, d)
    n = hf.shape[0]
    g_prob = jax.nn.softmax((hf @ wg + bg).astype(jnp.float32), axis=-1)
    g_w, g_idx = lax.top_k(g_prob, 1)
    g_onehot = jax.nn.one_hot(g_idx[:, 0], N_GROUPS, dtype=jnp.float32)
    e_logits = (hf @ we + be).astype(jnp.float32).reshape(n, N_GROUPS, EXPERTS_PER_GROUP)
    e_logits = jnp.einsum('nge,ng->ne', e_logits, g_onehot)
    e_w, e_idx = lax.top_k(jax.nn.softmax(e_logits, axis=-1), TOP_K_IN_GROUP)
    e_w = e_w / jnp.sum(e_w, axis=-1, keepdims=True)
    e_gate = jnp.sum(jax.nn.one_hot(e_idx, EXPERTS_PER_GROUP, dtype=jnp.float32) * e_w[..., None], axis=1)
    comb = (g_onehot[:, :, None] * (g_w * e_gate)[:, None, :]).astype(h.dtype)
    out = jnp.zeros_like(hf)
    for g in range(N_GROUPS):
        a = jnp.einsum('nd,edf->nef', hf, w_gate[g])
        u = jnp.einsum('nd,edf->nef', hf, w_up[g])
        act = jax.nn.silu(a) * u * comb[:, g, :, None]
        out = out + jnp.einsum('nef,efd->nd', act, w_down[g])
    return out.reshape(b, s, d)


def setup_inputs(seed: int = 0) -> dict:
    key = jax.random.key(seed)
    ks = jax.random.split(key, 32)
    f32 = jnp.float32

    def nrm(k, shape, scale):
        return jax.random.normal(k, shape, f32) * scale

    def gain(k, shape):
        return 1.0 + 0.02 * jax.random.normal(k, shape, f32)

    G, E, F, D = N_GROUPS, EXPERTS_PER_GROUP, D_FF_EXPERT, D_MODEL
    offset = jax.random.randint(ks[2], (BATCH, 1), 0, MAX_POS_OFFSET, dtype=jnp.int32)
    positions = (offset + jnp.arange(SEQ, dtype=jnp.int32)[None, :]).astype(jnp.int32)
    return {
        "x": nrm(ks[0], (BATCH, SEQ, D), 1.0),
        "c": nrm(ks[1], (BATCH, D), 1.0),
        "positions": positions,
        "w_ada": nrm(ks[3], (DEPTH, D, 6 * D), 0.3 * D ** -0.5),
        "b_ada": nrm(ks[4], (DEPTH, 6 * D), 0.01),
        "norm_g": gain(ks[5], (DEPTH, 2, D)),
        "w_in": nrm(ks[6], (DEPTH, D, IN_WIDTH), D ** -0.5),
        "diff_qk_g": gain(ks[7], (DEPTH, 2, HEAD_DIM_A)),
        "diff_lambda": nrm(ks[8], (DEPTH, 4, HEAD_DIM_A), 0.1),
        "diff_out_g": gain(ks[9], (DEPTH, 2 * HEAD_DIM_A)),
        "rel_bias": nrm(ks[10], (N_REL_BUCKETS, N_HEADS_A), 0.5),
        "sgu_v_g": gain(ks[11], (DEPTH, SGU_WIDTH)),
        "sgu_w": nrm(ks[12], (DEPTH, SGU_GROUPS, SGU_CHUNK, SGU_CHUNK), 0.5 * SGU_CHUNK ** -0.5),
        "sgu_b": 1.0 + nrm(ks[13], (DEPTH, SGU_GROUPS, SGU_CHUNK), 0.05),
        "mla_lat_g": gain(ks[14], (DEPTH, Q_LORA + KV_LORA)),
        "mla_w_uq": nrm(ks[15], (DEPTH, Q_LORA, N_HEADS_C * (NOPE_DIM + ROPE_DIM)), Q_LORA ** -0.5),
        "mla_w_ukv": nrm(ks[16], (DEPTH, KV_LORA, N_HEADS_C * (NOPE_DIM + V_DIM_C)), KV_LORA ** -0.5),
        "mla_qk_g": gain(ks[17], (DEPTH, 2, NOPE_DIM + ROPE_DIM)),
        "w_branch": nrm(ks[18], (DEPTH, N_BRANCH, MIX_WIDTH, D), MIX_WIDTH ** -0.5),
        "w_out": nrm(ks[19], (DEPTH, D, D), D ** -0.5),
        "router_g_w": nrm(ks[20], (DEPTH, D, G), D ** -0.5),
        "router_g_b": nrm(ks[21], (DEPTH, G), 0.01),
        "router_e_w": nrm(ks[22], (DEPTH, D, G * E), D ** -0.5),
        "router_e_b": nrm(ks[23], (DEPTH, G * E), 0.01),
        "w_e_gate": nrm(ks[24], (DEPTH, G, E, D, F), D ** -0.5),
        "w_e_up": nrm(ks[25], (DEPTH, G, E, D, F), D ** -0.5),
        "w_e_down": nrm(ks[26], (DEPTH, G, E, F, D), F ** -0.5),
    }


def reference(x, c, positions, w_ada, b_ada, norm_g, w_in, diff_qk_g, diff_lambda, diff_out_g, rel_bias,
              sgu_v_g, sgu_w, sgu_b, mla_lat_g, mla_w_uq, mla_w_ukv, mla_qk_g, w_branch, w_out,
              router_g_w, router_g_b, router_e_w, router_e_b, w_e_gate, w_e_up, w_e_down):
    b, s, d = x.shape
    inv_freq = ROPE_THETA ** (-jnp.arange(0, ROPE_DIM, 2, dtype=jnp.float32) / ROPE_DIM)
    ang = positions.astype(jnp.float32)[..., None] * inv_freq
    cos = jnp.cos(ang).astype(x.dtype)
    sin = jnp.sin(ang).astype(x.dtype)
    c_act = jax.nn.silu(c)

    for l in range(DEPTH):
        mod = c_act @ w_ada[l] + b_ada[l]
        shift1, scale1, gate1, shift2, scale2, gate2 = [m[:, None, :] for m in jnp.split(mod, 6, axis=-1)]

        h = rms_norm(x, norm_g[l, 0]) * (1 + scale1) + shift1
        xa_q, xa_k, xa_v, xb, xc_q, xc_kv, xc_r, xg = jnp.split(h @ w_in[l], IN_SPLITS, axis=-1)

        lambda_init = LAMBDA_INIT_BASE - LAMBDA_INIT_SCALE * math.exp(-LAMBDA_INIT_DECAY * l)
        dl = diff_lambda[l].astype(jnp.float32)
        lam = jnp.exp(jnp.sum(dl[0] * dl[1])) - jnp.exp(jnp.sum(dl[2] * dl[3])) + lambda_init
        qa = rms_norm(xa_q.reshape(b, s, N_HEADS_A, 2, HEAD_DIM_A), diff_qk_g[l, 0])
        ka = rms_norm(xa_k.reshape(b, s, N_HEADS_A, 2, HEAD_DIM_A), diff_qk_g[l, 1])
        va = xa_v.reshape(b, s, N_HEADS_A, 2 * HEAD_DIM_A)
        o_a = differential_attention(qa, ka, va, lam, lambda_init, diff_out_g[l], rel_bias)

        o_b = spatial_gating(xb, sgu_v_g[l], sgu_w[l], sgu_b[l])

        o_c = latent_attention(xc_q, xc_kv, xc_r, cos, sin, mla_lat_g[l], mla_w_uq[l], mla_w_ukv[l], mla_qk_g[l])

        branches = jnp.stack([o_a, o_b, o_c], axis=2)
        gates = jax.nn.sigmoid(xg.reshape(b, s, N_BRANCH, d))
        merged = jnp.einsum('bsnd,bsnd->bsd', gates, jnp.einsum('bsnw,nwd->bsnd', branches, w_branch[l]))
        x = x + gate1 * (merged @ w_out[l])

        h = rms_norm(x, norm_g[l, 1]) * (1 + scale2) + shift2
        x = x + gate2 * hierarchical_moe(h, router_g_w[l], router_g_b[l], router_e_w[l], router_e_b[l],
                                         w_e_gate[l], w_e_up[l], w_e_down[l])
    return x
```

```python
import functools
import math

import jax
import jax.numpy as jnp
import numpy as np
from jax import lax
from jax.experimental import pallas as pl
from jax.experimental.pallas import tpu as pltpu

f32 = jnp.float32
bf16 = jnp.bfloat16

D = 1024
DEPTH = 4
CHUNK = 64
MIX = 512
HA = 4
HC = 8
Q_LORA, KV_LORA, ROPE = 256, 128, 32
NOPE, VC = 64, 64
N_GROUPS, N_EXP, F_EXP = 4, 8, 256
N_REL_BUCKETS = 32
ROPE_THETA = 10000.0
LAMBDA_INIT_BASE, LAMBDA_INIT_SCALE, LAMBDA_INIT_DECAY = 0.8, 0.6, 0.3
EPS = 1e-6
NEG = -1e30

IN_REAL = 6048
IN_PAD = 6144
XC_OFF = 2560
GATE_OFF = 3072

TM = 256
TQ = 256
TK = 256
LANES = 128
VMEM_LIMIT = 56 * 1024 * 1024


def _cparams(sem, vmem=VMEM_LIMIT):
    return pltpu.CompilerParams(dimension_semantics=sem, vmem_limit_bytes=vmem)


def _const_spec(shape):
    nd = len(shape)
    return pl.BlockSpec(shape, lambda *_: (0,) * nd)


def _rms(x, g_row):
    return x * lax.rsqrt(jnp.mean(x * x, axis=-1, keepdims=True) + EPS) * g_row


def _seg_rinv(x, sd_ref, ex_ref):
    ss = jnp.dot((x * x).astype(bf16), sd_ref[...], preferred_element_type=f32)
    hi = ss.astype(bf16)
    lo = (ss - hi.astype(f32)).astype(bf16)
    full = jnp.dot(jnp.concatenate([hi, lo], axis=-1), ex_ref[...], preferred_element_type=f32)
    return lax.rsqrt(full + EPS)


def _ada_kernel(c_ref, w_ref, b_ref, o_ref):
    c = c_ref[...]
    ca = (c * jax.nn.sigmoid(c)).astype(bf16)
    o_ref[0] = jnp.dot(ca, w_ref[0].astype(bf16), preferred_element_type=f32) + b_ref[0]


def _ada(c, w_ada, b_ada):
    nb = c.shape[0]
    tn = 1024
    return pl.pallas_call(
        _ada_kernel,
        grid=(DEPTH, 6 * D // tn),
        in_specs=[pl.BlockSpec((nb, D), lambda l, j: (0, 0)),
                  pl.BlockSpec((1, D, tn), lambda l, j: (l, 0, j)),
                  pl.BlockSpec((1, 1, tn), lambda l, j: (l, 0, j))],
        out_specs=pl.BlockSpec((1, nb, tn), lambda l, j: (l, 0, j)),
        out_shape=jax.ShapeDtypeStruct((DEPTH, nb, 6 * D), f32),
        compiler_params=_cparams(("arbitrary", "arbitrary")),
        name="ada_mod",
    )(c, w_ada, b_ada.reshape(DEPTH, 1, 6 * D))


def _rope_kernel(pos_ref, invf_ref, cos_ref, sin_ref):
    ang = pos_ref[...].astype(f32) * invf_ref[...]
    lane = lax.broadcasted_iota(jnp.int32, ang.shape, 1)
    rot = (lane >= NOPE) & (lane < NOPE + ROPE)
    first = lane < NOPE + ROPE // 2
    cos_ref[...] = jnp.where(rot, jnp.cos(ang), 1.0)
    s = jnp.sin(ang)
    sin_ref[...] = jnp.where(rot, jnp.where(first, -s, s), 0.0)


def _rope_tables(positions):
    n = positions.size
    inv_freq = ROPE_THETA ** (-jnp.arange(0, ROPE, 2, dtype=f32) / ROPE)
    invf = jnp.zeros((1, LANES), f32).at[0, NOPE:NOPE + ROPE].set(jnp.tile(inv_freq, 2))
    t = math.gcd(n, 2048)
    return pl.pallas_call(
        _rope_kernel,
        grid=(n // t,),
        in_specs=[pl.BlockSpec((t, 1), lambda i: (i, 0)), _const_spec((1, LANES))],
        out_specs=[pl.BlockSpec((t, LANES), lambda i: (i, 0))] * 2,
        out_shape=[jax.ShapeDtypeStruct((n, LANES), f32)] * 2,
        compiler_params=_cparams(("arbitrary",)),
        name="rope_tables",
    )(positions.reshape(n, 1), invf)


def _t5_bucket(rel):
    n = jnp.abs(rel)
    large = 8 + sum((n >= t).astype(jnp.int32) for t in (12, 16, 23, 32, 46, 64, 91))
    return jnp.where(rel > 0, 16, 0) + jnp.where(n < 8, n, large)


def _bias_kernel(tbl_ref, t0_ref, t1_ref, mk_ref):
    r = lax.broadcasted_iota(jnp.int32, (TQ, TK), 0)
    c = lax.broadcasted_iota(jnp.int32, (TQ, TK), 1)
    allowed = (c // CHUNK) <= (r // CHUNK)
    mk = jnp.where(allowed, 0.0, NEG).astype(f32)
    mk_ref[...] = mk
    b0 = _t5_bucket(c - r)
    b1 = _t5_bucket(c - r - TK)
    for h in range(HA):
        far = tbl_ref[N_REL_BUCKETS // 2 - 1, h]
        t0 = jnp.zeros((TQ, TK), f32)
        t1 = jnp.zeros((TQ, TK), f32)
        for b in range(N_REL_BUCKETS):
            v = tbl_ref[b, h] - far
            t0 = jnp.where(b0 == b, v, t0)
            t1 = jnp.where(b1 == b, v, t1)
        t0 = t0 + mk
        t0_ref[h, 0:TQ, :] = t0
        t0_ref[h, TQ:2 * TQ, :] = t0
        t1_ref[h, 0:TQ, :] = t1
        t1_ref[h, TQ:2 * TQ, :] = t1


def _bias_tiles(rel_bias):
    return pl.pallas_call(
        _bias_kernel,
        in_specs=[pl.BlockSpec(memory_space=pltpu.SMEM)],
        out_specs=[pl.BlockSpec(memory_space=pltpu.VMEM)] * 3,
        out_shape=[jax.ShapeDtypeStruct((HA, 2 * TQ, TK), f32),
                   jax.ShapeDtypeStruct((HA, 2 * TQ, TK), f32),
                   jax.ShapeDtypeStruct((TQ, TK), f32)],
        name="bias_tiles",
    )(rel_bias)


def _inproj_kernel(*refs, combine):
    if combine:
        (xa_ref, y_ref, g2_ref, sh_ref, sc_ref, ng_ref, w_ref, gq_ref, gk_ref, sd_ref, ex_ref,
         vg_ref, ws_ref, bs_ref,
         xo_ref, qa_ref, ka_ref, va_ref, ob_ref, xc_ref, gt_ref) = refs
        x = xa_ref[...] + g2_ref[0] * y_ref[...]
        xo_ref[...] = x
    else:
        (xa_ref, sh_ref, sc_ref, ng_ref, w_ref, gq_ref, gk_ref, sd_ref, ex_ref,
         vg_ref, ws_ref, bs_ref,
         qa_ref, ka_ref, va_ref, ob_ref, xc_ref, gt_ref) = refs
        x = xa_ref[...]
    h = _rms(x, ng_ref[...]) * (1.0 + sc_ref[0]) + sh_ref[0]
    hb = h.astype(bf16)

    q = jnp.dot(hb, w_ref[:, 0:MIX], preferred_element_type=f32)
    qa_ref[...] = (q * _seg_rinv(q, sd_ref, ex_ref) * gq_ref[...]).astype(bf16)
    k = jnp.dot(hb, w_ref[:, MIX:2 * MIX], preferred_element_type=f32)
    ka_ref[...] = (k * _seg_rinv(k, sd_ref, ex_ref) * gk_ref[...]).astype(bf16)
    va_ref[...] = jnp.dot(hb, w_ref[:, 2 * MIX:3 * MIX], preferred_element_type=f32).astype(bf16)

    z = jnp.dot(hb, w_ref[:, 3 * MIX:5 * MIX], preferred_element_type=f32)
    z = 0.5 * z * (1.0 + jnp.tanh(math.sqrt(2.0 / math.pi) * (z + 0.044715 * (z * z * z))))
    u = z[:, :MIX]
    v = _rms(z[:, MIX:], vg_ref[...]).astype(bf16)
    ri = lax.broadcasted_iota(jnp.int32, (LANES, LANES), 0)
    ci = lax.broadcasted_iota(jnp.int32, (LANES, LANES), 1)
    allowed = (ci // CHUNK) <= (ri // CHUNK)
    for g in range(4):
        wm = jnp.where(allowed, ws_ref[g], 0.0).astype(bf16)
        bcol = bs_ref[:, g:g + 1]
        for wd in range(TM // LANES):
            rows = slice(wd * LANES, (wd + 1) * LANES)
            cols = slice(g * LANES, (g + 1) * LANES)
            vs = jnp.dot(wm, v[rows, cols], preferred_element_type=f32) + bcol
            ob_ref[rows, cols] = (u[rows, cols] * vs).astype(bf16)

    xc_ref[...] = jnp.dot(hb, w_ref[:, XC_OFF:GATE_OFF], preferred_element_type=f32)

    for j in range(3):
        gsl = slice(GATE_OFF + j * D, GATE_OFF + (j + 1) * D)
        gl = jnp.dot(hb, w_ref[:, gsl], preferred_element_type=f32)
        gt_ref[:, j * D:(j + 1) * D] = jax.nn.sigmoid(gl).astype(bf16)


def _inproj(xa, y, g2, shift, scale, ng, w, gq, gk, sd, ex, vg, ws, bs, seq):
    n = xa.shape[0]
    spb = seq // TM
    combine = y is not None
    row = lambda w_: pl.BlockSpec((TM, w_), lambda i: (i, 0))
    modspec = pl.BlockSpec((1, 1, D), lambda i: (i // spb, 0, 0))
    in_specs = [row(D)]
    args = [xa]
    if combine:
        in_specs += [row(D), modspec]
        args += [y, g2]
    in_specs += [modspec, modspec, _const_spec((1, D)),
                 pl.BlockSpec((D, IN_PAD), lambda i: (0, 0), pipeline_mode=pl.Buffered(1)),
                 _const_spec((1, MIX)), _const_spec((1, MIX)),
                 _const_spec((MIX, LANES)), _const_spec((2 * LANES, MIX)),
                 _const_spec((1, MIX)), _const_spec((4, LANES, LANES)), _const_spec((LANES, 4))]
    args += [shift, scale, ng, w, gq, gk, sd, ex, vg, ws, bs]
    out_specs = [row(MIX)] * 4 + [row(MIX), row(3 * D)]
    out_shape = [jax.ShapeDtypeStruct((n, MIX), bf16)] * 4 + [
        jax.ShapeDtypeStruct((n, MIX), f32), jax.ShapeDtypeStruct((n, 3 * D), bf16)]
    if combine:
        out_specs = [row(D)] + out_specs
        out_shape = [jax.ShapeDtypeStruct((n, D), f32)] + out_shape
    return pl.pallas_call(
        functools.partial(_inproj_kernel, combine=combine),
        grid=(n // TM,),
        in_specs=in_specs, out_specs=out_specs, out_shape=out_shape,
        compiler_params=_cparams(("arbitrary",)),
        name="inproj",
    )(*args)


def _online_update(s, vj, m, l, acc):
    m_new = jnp.maximum(m, jnp.max(s, axis=-1, keepdims=True))
    alpha = jnp.exp(m - m_new)
    p = jnp.exp(s - m_new)
    l = alpha * l + jnp.sum(p, axis=-1, keepdims=True)
    acc = alpha * acc + jnp.dot(p.astype(bf16), vj, preferred_element_type=f32)
    return m_new, l, acc


def _attend(qs, k_ref, v_ref, kcols, vcols, qi, diag_bias, sub_bias):
    rows = qs.shape[0]

    def scores(j):
        kj = k_ref[pl.ds(pl.multiple_of(j * TK, TK), TK), kcols]
        return lax.dot_general(qs, kj, (((1,), (1,)), ((), ())), preferred_element_type=f32)

    def vblk(j):
        return v_ref[pl.ds(pl.multiple_of(j * TK, TK), TK), vcols]

    m = jnp.full((rows, 1), NEG, f32)
    l = jnp.zeros((rows, 1), f32)
    acc = jnp.zeros((rows, LANES), f32)
    m, l, acc = _online_update(scores(qi) + diag_bias, vblk(qi), m, l, acc)
    jp = jnp.maximum(qi - 1, 0)
    s = scores(jp) + jnp.where(qi >= 1, 0.0, NEG)
    if sub_bias is not None:
        s = s + sub_bias
    m, l, acc = _online_update(s, vblk(jp), m, l, acc)

    def far(j, carry):
        return _online_update(scores(j), vblk(j), *carry)

    m, l, acc = lax.fori_loop(0, jnp.maximum(qi - 1, 0), far, (m, l, acc))
    return acc / l


def _attn_a_kernel(q_ref, k_ref, v_ref, t0_ref, t1_ref, dl_ref, og_ref, o_ref, *, lambda_init):
    qi = pl.program_id(1)
    dl = dl_ref[...]
    lam = (jnp.exp(jnp.sum(dl[0:1] * dl[1:2], axis=-1, keepdims=True))
           - jnp.exp(jnp.sum(dl[2:3] * dl[3:4], axis=-1, keepdims=True)) + lambda_init)
    lane = lax.broadcasted_iota(jnp.int32, (TQ, LANES), 1)
    for h in range(HA):
        cols = slice(h * LANES, (h + 1) * LANES)
        qh = q_ref[:, cols]
        zero = jnp.zeros_like(qh)
        qs = jnp.concatenate([jnp.where(lane < 64, qh, zero), jnp.where(lane >= 64, qh, zero)], axis=0)
        o = _attend(qs, k_ref, v_ref, cols, cols, qi, t0_ref[h], t1_ref[h])
        oh = o[:TQ] - lam * o[TQ:]
        oh = _rms(oh, og_ref[...]) * (1.0 - lambda_init)
        o_ref[:, cols] = oh.astype(bf16)


def _attn_a(qa, ka, va, t0, t1, dl, og, nb, seq, lambda_init):
    n = qa.shape[0]
    nq = seq // TQ
    return pl.pallas_call(
        functools.partial(_attn_a_kernel, lambda_init=lambda_init),
        grid=(nb, nq),
        in_specs=[pl.BlockSpec((TQ, MIX), lambda b, i: (b * nq + i, 0)),
                  pl.BlockSpec((seq, MIX), lambda b, i: (b, 0)),
                  pl.BlockSpec((seq, MIX), lambda b, i: (b, 0)),
                  _const_spec((HA, 2 * TQ, TK)), _const_spec((HA, 2 * TQ, TK)),
                  _const_spec((4, 64)), _const_spec((1, LANES))],
        out_specs=pl.BlockSpec((TQ, MIX), lambda b, i: (b * nq + i, 0)),
        out_shape=jax.ShapeDtypeStruct((n, MIX), bf16),
        compiler_params=_cparams(("arbitrary", "arbitrary")),
        name="attn_diff",
    )(qa, ka, va, t0, t1, dl, og)


def _attn_c_kernel(q_ref, k_ref, v_ref, mk_ref, o_ref):
    qi = pl.program_id(1)
    lane = lax.broadcasted_iota(jnp.int32, (TQ, LANES), 1)
    mk = mk_ref[...]
    for hp in range(HC // 2):
        vcols = slice(hp * LANES, (hp + 1) * LANES)
        outs = []
        for h in (2 * hp, 2 * hp + 1):
            cols = slice(h * LANES, (h + 1) * LANES)
            outs.append(_attend(q_ref[:, cols], k_ref, v_ref, cols, vcols, qi, mk, None))
        o_ref[:, vcols] = jnp.where(lane < VC, outs[0], outs[1]).astype(bf16)


def _attn_c(qc, kc, vc, mk, nb, seq):
    n = qc.shape[0]
    nq = seq // TQ
    return pl.pallas_call(
        _attn_c_kernel,
        grid=(nb, nq),
        in_specs=[pl.BlockSpec((TQ, HC * LANES), lambda b, i: (b * nq + i, 0)),
                  pl.BlockSpec((seq, HC * LANES), lambda b, i: (b, 0)),
                  pl.BlockSpec((seq, MIX), lambda b, i: (b, 0)),
                  _const_spec((TQ, TK))],
        out_specs=pl.BlockSpec((TQ, MIX), lambda b, i: (b * nq + i, 0)),
        out_shape=jax.ShapeDtypeStruct((n, MIX), bf16),
        compiler_params=_cparams(("arbitrary", "arbitrary")),
        name="attn_latent",
    )(qc, kc, vc, mk)


def _rope_apply(x, cosf, sinf, lane):
    w = x.shape[-1]
    partner = jnp.where((lane % LANES) < NOPE + ROPE // 2,
                        pltpu.roll(x, w - ROPE // 2, 1), pltpu.roll(x, ROPE // 2, 1))
    return x * cosf + partner * sinf


def _mla_prep_kernel(xc_ref, cos_ref, sin_ref, glq_ref, glkv_ref, wq_ref, wkv_ref, gq_ref, gk_ref, gr_ref,
                     sd_ref, ex_ref, q_ref, k_ref, v_ref):
    cq = _rms(xc_ref[:, 0:Q_LORA], glq_ref[...]).astype(bf16)
    ckv = _rms(xc_ref[:, Q_LORA:Q_LORA + KV_LORA], glkv_ref[...]).astype(bf16)
    cos8 = jnp.concatenate([cos_ref[...]] * HC, axis=-1)
    sin8 = jnp.concatenate([sin_ref[...]] * HC, axis=-1)
    lane8 = lax.broadcasted_iota(jnp.int32, (TM, HC * LANES), 1)

    q = jnp.dot(cq, wq_ref[...], preferred_element_type=f32)
    q = q * _seg_rinv(q, sd_ref, ex_ref) * gq_ref[...]
    q_ref[...] = _rope_apply(q, cos8, sin8, lane8).astype(bf16)

    kv = jnp.dot(ckv, wkv_ref[...], preferred_element_type=f32)
    v_ref[...] = kv[:, HC * LANES:].astype(bf16)
    xr = xc_ref[:, Q_LORA + KV_LORA:Q_LORA + KV_LORA + LANES]
    kr = xr * lax.rsqrt(jnp.sum(xr * xr, axis=-1, keepdims=True) * (1.0 / ROPE) + EPS) * gr_ref[...]
    kr = pltpu.roll(kr, NOPE, 1)
    lane1 = lax.broadcasted_iota(jnp.int32, (TM, LANES), 1)
    kr = _rope_apply(kr, cos_ref[...], sin_ref[...], lane1)
    for h in range(HC):
        cols = slice(h * LANES, (h + 1) * LANES)
        kn = kv[:, cols]
        kn = kn * lax.rsqrt(jnp.sum(kn * kn, axis=-1, keepdims=True) * (1.0 / NOPE) + EPS) * gk_ref[...]
        k_ref[:, cols] = (kn + kr).astype(bf16)


def _mla_prep(xc, cosf, sinf, glq, glkv, wq, wkv, gq, gk, gr, sd, ex):
    n = xc.shape[0]
    row = lambda w_: pl.BlockSpec((TM, w_), lambda i: (i, 0))
    return pl.pallas_call(
        _mla_prep_kernel,
        grid=(n // TM,),
        in_specs=[row(MIX), row(LANES), row(LANES), _const_spec((1, Q_LORA)), _const_spec((1, KV_LORA)),
                  _const_spec((Q_LORA, HC * LANES)), _const_spec((KV_LORA, HC * LANES + MIX)),
                  _const_spec((1, HC * LANES)), _const_spec((1, LANES)), _const_spec((1, LANES)),
                  _const_spec((HC * LANES, LANES)), _const_spec((2 * LANES, HC * LANES))],
        out_specs=[row(HC * LANES), row(HC * LANES), row(MIX)],
        out_shape=[jax.ShapeDtypeStruct((n, HC * LANES), bf16), jax.ShapeDtypeStruct((n, HC * LANES), bf16),
                   jax.ShapeDtypeStruct((n, MIX), bf16)],
        compiler_params=_cparams(("arbitrary",)),
        name="mla_prep",
    )(xc, cosf, sinf, glq, glkv, wq, wkv, gq, gk, gr, sd, ex)


def _merge_kernel(oa_ref, ob_ref, oc_ref, gt_ref, x_ref, g1_ref, sh_ref, sc_ref, ng_ref, wb_ref, wo_ref,
                  wr_ref, br_ref, x1_ref, h2_ref, meta_ref, cnt_ref, run_ref):
    i = pl.program_id(0)

    @pl.when(i == 0)
    def _():
        run_ref[...] = jnp.zeros_like(run_ref)

    merged = jnp.zeros((TM, D), f32)
    for j, o_ref in enumerate((oa_ref, ob_ref, oc_ref)):
        pj = jnp.dot(o_ref[...], wb_ref[j], preferred_element_type=f32)
        merged = merged + gt_ref[:, j * D:(j + 1) * D].astype(f32) * pj
    y = jnp.dot(merged.astype(bf16), wo_ref[...], preferred_element_type=f32)
    x1 = x_ref[...] + g1_ref[0] * y
    x1_ref[...] = x1
    h2 = _rms(x1, ng_ref[...]) * (1.0 + sc_ref[0]) + sh_ref[0]
    h2_ref[...] = h2.astype(bf16)

    lg = jnp.dot(h2, wr_ref[...], preferred_element_type=f32, precision=lax.Precision.HIGHEST) + br_ref[...]
    lane = lax.broadcasted_iota(jnp.int32, (TM, LANES), 1)
    n_e = N_GROUPS * N_EXP
    gmask = (lane >= n_e) & (lane < n_e + N_GROUPS)
    gl = jnp.where(gmask, lg, NEG)
    gmax = jnp.max(gl, axis=-1, keepdims=True)
    g_w = 1.0 / jnp.sum(jnp.where(gmask, jnp.exp(gl - gmax), 0.0), axis=-1, keepdims=True)
    gidx = jnp.min(jnp.where(gl == gmax, lane - n_e, LANES), axis=-1, keepdims=True)
    emask = (lane < n_e) & ((lane // N_EXP) == gidx)
    el = jnp.where(emask, lg, NEG)
    m1 = jnp.max(el, axis=-1, keepdims=True)
    i1 = jnp.min(jnp.where(el == m1, lane, LANES), axis=-1, keepdims=True)
    el2 = jnp.where(lane == i1, NEG, el)
    m2 = jnp.max(el2, axis=-1, keepdims=True)
    i2 = jnp.min(jnp.where(el2 == m2, lane, LANES), axis=-1, keepdims=True)
    t = jnp.exp(m2 - m1)
    w1 = g_w / (1.0 + t)
    w2 = g_w * t / (1.0 + t)
    lo = jnp.minimum(i1, i2)
    hi = jnp.maximum(i1, i2)
    w_lo = jnp.where(i1 < i2, w1, w2)
    w_hi = jnp.where(i1 < i2, w2, w1)
    bucket = gidx * (N_EXP * N_EXP) + (lo % N_EXP) * N_EXP + (hi % N_EXP)

    lane2 = lax.broadcasted_iota(jnp.int32, (TM, 2 * LANES), 1)
    onehot = (lane2 == bucket).astype(f32)
    rr = lax.broadcasted_iota(jnp.int32, (TM, TM), 0)
    cc = lax.broadcasted_iota(jnp.int32, (TM, TM), 1)
    tri = (cc < rr).astype(bf16)
    before = jnp.dot(tri, onehot.astype(bf16), preferred_element_type=f32) + run_ref[...]
    rank = jnp.sum(onehot * before, axis=-1, keepdims=True)
    run_ref[...] = run_ref[...] + jnp.sum(onehot, axis=0, keepdims=True)
    cnt_ref[...] = run_ref[...]

    meta = (jnp.where(lane == i1, w1, 0.0) + jnp.where(lane == i2, w2, 0.0)
            + jnp.where(lane == 32, bucket.astype(f32), 0.0) + jnp.where(lane == 33, rank, 0.0)
            + jnp.where(lane == 34, w_lo, 0.0) + jnp.where(lane == 35, w_hi, 0.0))
    meta_ref[...] = meta


def _merge(oa, ob, oc, gt, x, g1, shift, scale, ng, wb, wo, wr, br, seq):
    n = x.shape[0]
    spb = seq // TM
    row = lambda w_: pl.BlockSpec((TM, w_), lambda i: (i, 0))
    modspec = pl.BlockSpec((1, 1, D), lambda i: (i // spb, 0, 0))
    return pl.pallas_call(
        _merge_kernel,
        grid=(n // TM,),
        in_specs=[row(MIX), row(MIX), row(MIX), row(3 * D), row(D), modspec, modspec, modspec,
                  _const_spec((1, D)), _const_spec((3, MIX, D)), _const_spec((D, D)),
                  _const_spec((D, LANES)), _const_spec((1, LANES))],
        out_specs=[row(D), row(D), row(LANES), _const_spec((1, 2 * LANES))],
        out_shape=[jax.ShapeDtypeStruct((n, D), f32), jax.ShapeDtypeStruct((n, D), bf16),
                   jax.ShapeDtypeStruct((n, LANES), f32), jax.ShapeDtypeStruct((1, 2 * LANES), f32)],
        scratch_shapes=[pltpu.VMEM((1, 2 * LANES), f32)],
        compiler_params=_cparams(("arbitrary",)),
        name="merge_route",
    )(oa, ob, oc, gt, x, g1, shift, scale, ng, wb, wo, wr, br)


def _moe_dense_kernel(h_ref, meta_ref, wg_ref, wu_ref, wd_ref, y_ref):
    e = pl.program_id(1)

    @pl.when(e == 0)
    def _():
        y_ref[...] = jnp.zeros_like(y_ref)

    hb = h_ref[...]
    lane = lax.broadcasted_iota(jnp.int32, (TM, LANES), 1)
    wcol = jnp.sum(jnp.where(lane == e, meta_ref[...], 0.0), axis=-1, keepdims=True)
    a = jnp.dot(hb, wg_ref[0], preferred_element_type=f32)
    u = jnp.dot(hb, wu_ref[0], preferred_element_type=f32)
    act = (a * jax.nn.sigmoid(a)) * u * wcol
    y_ref[...] += jnp.dot(act.astype(bf16), wd_ref[0], preferred_element_type=f32)


def _moe_dense(h2, meta, wg, wu, wd):
    n = h2.shape[0]
    ne = N_GROUPS * N_EXP
    return pl.pallas_call(
        _moe_dense_kernel,
        grid=(n // TM, ne),
        in_specs=[pl.BlockSpec((TM, D), lambda i, e: (i, 0)), pl.BlockSpec((TM, LANES), lambda i, e: (i, 0)),
                  pl.BlockSpec((1, D, F_EXP), lambda i, e: (e, 0, 0)),
                  pl.BlockSpec((1, D, F_EXP), lambda i, e: (e, 0, 0)),
                  pl.BlockSpec((1, F_EXP, D), lambda i, e: (e, 0, 0))],
        out_specs=pl.BlockSpec((TM, D), lambda i, e: (i, 0)),
        out_shape=jax.ShapeDtypeStruct((n, D), f32),
        compiler_params=_cparams(("arbitrary", "arbitrary")),
        name="moe_dense",
    )(h2, meta, wg, wu, wd)


def _final_kernel(x_ref, y_ref, g_ref, o_ref):
    o_ref[...] = x_ref[...] + g_ref[0] * y_ref[...]


def _final_combine(x1, y, g2, seq):
    n = x1.shape[0]
    t = 512
    spb = seq // t
    row = pl.BlockSpec((t, D), lambda i: (i, 0))
    return pl.pallas_call(
        _final_kernel,
        grid=(n // t,),
        in_specs=[row, row, pl.BlockSpec((1, 1, D), lambda i: (i // spb, 0, 0))],
        out_specs=row,
        out_shape=jax.ShapeDtypeStruct((n, D), f32),
        compiler_params=_cparams(("arbitrary",)),
        name="final_combine",
    )(x1, y, g2)


def _seg_matrices(width, segs):
    sd = np.zeros((width, LANES), np.float32)
    ex = np.zeros((LANES, width), np.float32)
    for j, (s, ln) in enumerate(segs):
        sd[s:s + ln, j] = 1.0 / ln
        ex[j, s:s + ln] = 1.0
    return jnp.asarray(sd, bf16), jnp.asarray(np.concatenate([ex, ex], axis=0), bf16)


def _head_pad(w, heads, per_head, keep):
    k = w.shape[0]
    w3 = w.reshape(k, heads, per_head)[:, :, :keep]
    return jnp.pad(w3, ((0, 0), (0, 0), (0, LANES - keep))).reshape(k, heads * LANES)


def kernel(x, c, positions, w_ada, b_ada, norm_g, w_in, diff_qk_g, diff_lambda, diff_out_g, rel_bias, sgu_v_g, sgu_w, sgu_b, mla_lat_g, mla_w_uq, mla_w_ukv, mla_qk_g, w_branch, w_out, router_g_w, router_g_b, router_e_w, router_e_b, w_e_gate, w_e_up, w_e_down):
    nb, seq, _ = x.shape
    n = nb * seq
    assert seq % TQ == 0 and seq % TM == 0 and x.shape[2] == D

    mod = _ada(c, w_ada, b_ada)
    cosf, sinf = _rope_tables(positions)
    t0, t1, mk = _bias_tiles(rel_bias)

    sd_a, ex_a = _seg_matrices(MIX, [(s * 64, 64) for s in range(8)])
    segs_q = []
    for h in range(HC):
        segs_q += [(h * LANES, NOPE), (h * LANES + NOPE, ROPE)]
    sd_q, ex_q = _seg_matrices(HC * LANES, segs_q)
    ne = N_GROUPS * N_EXP
    zpad = lambda a, w_: jnp.pad(a, ((0, 0), (0, w_ - a.shape[1])))

    xcur = x.reshape(n, D)
    y_prev, g2_prev = None, None
    for l in range(DEPTH):
        m3 = mod[l].reshape(nb, 1, 6 * D)
        shift1, scale1, gate1, shift2, scale2, gate2 = [m3[:, :, k * D:(k + 1) * D] for k in range(6)]
        lambda_init = LAMBDA_INIT_BASE - LAMBDA_INIT_SCALE * math.exp(-LAMBDA_INIT_DECAY * l)

        w = w_in[l]
        w_pad = jnp.concatenate([w[:, :XC_OFF + 416], jnp.zeros((D, IN_PAD - IN_REAL), f32), w[:, XC_OFF + 416:]],
                                axis=1).astype(bf16)
        gq = (jnp.tile(diff_qk_g[l, 0], 8) * (64 ** -0.5)).reshape(1, MIX)
        gk = jnp.tile(diff_qk_g[l, 1], 8).reshape(1, MIX)
        outs = _inproj(xcur, y_prev, g2_prev, shift1, scale1, norm_g[l, 0].reshape(1, D), w_pad, gq, gk,
                       sd_a, ex_a, sgu_v_g[l].reshape(1, MIX), sgu_w[l], jnp.transpose(sgu_b[l]), seq)
        if y_prev is not None:
            xcur = outs[0]
            outs = outs[1:]
        qa, ka, va, ob, xc, gt = outs

        oa = _attn_a(qa, ka, va, t0, t1, diff_lambda[l], diff_out_g[l].reshape(1, LANES), nb, seq, lambda_init)

        qkg = mla_qk_g[l]
        wq = _head_pad(mla_w_uq[l], HC, NOPE + ROPE, NOPE + ROPE).astype(bf16)
        wkv3 = mla_w_ukv[l].reshape(KV_LORA, HC, NOPE + VC)
        wk = jnp.pad(wkv3[:, :, :NOPE], ((0, 0), (0, 0), (0, LANES - NOPE))).reshape(KV_LORA, HC * LANES)
        wv = wkv3[:, :, NOPE:].reshape(KV_LORA, HC * VC)
        wkv = jnp.concatenate([wk, wv], axis=1).astype(bf16)
        gq_c = jnp.tile(jnp.pad(qkg[0], (0, LANES - NOPE - ROPE)), HC).reshape(1, HC * LANES) * ((NOPE + ROPE) ** -0.5)
        gk_c = jnp.pad(qkg[1, :NOPE], (0, LANES - NOPE)).reshape(1, LANES)
        gr_c = jnp.pad(qkg[1, NOPE:], (0, LANES - ROPE)).reshape(1, LANES)
        qc, kc, vc = _mla_prep(xc, cosf, sinf, mla_lat_g[l, :Q_LORA].reshape(1, Q_LORA),
                               mla_lat_g[l, Q_LORA:].reshape(1, KV_LORA), wq, wkv, gq_c, gk_c, gr_c, sd_q, ex_q)
        oc = _attn_c(qc, kc, vc, mk, nb, seq)

        wr = zpad(jnp.concatenate([router_e_w[l], router_g_w[l]], axis=1), LANES)
        br = zpad(jnp.concatenate([router_e_b[l], router_g_b[l]]).reshape(1, ne + N_GROUPS), LANES)
        x1, h2, meta, _cnt = _merge(oa, ob, oc, gt, xcur, gate1, shift2, scale2, norm_g[l, 1].reshape(1, D),
                                    w_branch[l].astype(bf16), w_out[l].astype(bf16), wr, br, seq)

        y_prev = _moe_dense(h2, meta, w_e_gate[l].reshape(ne, D, F_EXP).astype(bf16),
                            w_e_up[l].reshape(ne, D, F_EXP).astype(bf16),
                            w_e_down[l].reshape(ne, F_EXP, D).astype(bf16))
        g2_prev = gate2
        xcur = x1

    out = _final_combine(xcur, y_prev, g2_prev, seq)
    return out.reshape(nb, seq, D)
```

```python
import functools
import math

import jax
import jax.numpy as jnp
import numpy as np
from jax import lax
from jax.experimental import pallas as pl
from jax.experimental.pallas import tpu as pltpu

f32 = jnp.float32
bf16 = jnp.bfloat16

D = 1024
DEPTH = 4
CHUNK = 64
MIX = 512
HA = 4
HC = 8
Q_LORA, KV_LORA, ROPE = 256, 128, 32
NOPE, VC = 64, 64
N_GROUPS, N_EXP, F_EXP = 4, 8, 256
N_REL_BUCKETS = 32
ROPE_THETA = 10000.0
LAMBDA_INIT_BASE, LAMBDA_INIT_SCALE, LAMBDA_INIT_DECAY = 0.8, 0.6, 0.3
EPS = 1e-6
NEG = -1e30

IN_REAL = 6048
IN_PAD = 6144
XC_OFF = 2560
GATE_OFF = 3072

TM = 256
TQ = 256
TK = 256
LANES = 128
VMEM_LIMIT = 56 * 1024 * 1024


def _cparams(sem, vmem=VMEM_LIMIT):
    return pltpu.CompilerParams(dimension_semantics=sem, vmem_limit_bytes=vmem)


def _const_spec(shape):
    nd = len(shape)
    return pl.BlockSpec(shape, lambda *_: (0,) * nd)


def _rms(x, g_row):
    return x * lax.rsqrt(jnp.mean(x * x, axis=-1, keepdims=True) + EPS) * g_row


def _seg_rinv(x, sd_ref, ex_ref):
    ss = jnp.dot((x * x).astype(bf16), sd_ref[...], preferred_element_type=f32)
    hi = ss.astype(bf16)
    lo = (ss - hi.astype(f32)).astype(bf16)
    full = jnp.dot(jnp.concatenate([hi, lo], axis=-1), ex_ref[...], preferred_element_type=f32)
    return lax.rsqrt(full + EPS)


def _ada_kernel(c_ref, w_ref, b_ref, o_ref):
    c = c_ref[...]
    ca = (c * jax.nn.sigmoid(c)).astype(bf16)
    o_ref[0] = jnp.dot(ca, w_ref[0].astype(bf16), preferred_element_type=f32) + b_ref[0]


def _ada(c, w_ada, b_ada):
    nb = c.shape[0]
    tn = 1024
    return pl.pallas_call(
        _ada_kernel,
        grid=(DEPTH, 6 * D // tn),
        in_specs=[pl.BlockSpec((nb, D), lambda l, j: (0, 0)),
                  pl.BlockSpec((1, D, tn), lambda l, j: (l, 0, j)),
                  pl.BlockSpec((1, 1, tn), lambda l, j: (l, 0, j))],
        out_specs=pl.BlockSpec((1, nb, tn), lambda l, j: (l, 0, j)),
        out_shape=jax.ShapeDtypeStruct((DEPTH, nb, 6 * D), f32),
        compiler_params=_cparams(("arbitrary", "arbitrary")),
        name="ada_mod",
    )(c, w_ada, b_ada.reshape(DEPTH, 1, 6 * D))


def _rope_kernel(pos_ref, invf_ref, cos_ref, sin_ref):
    ang = pos_ref[...].astype(f32) * invf_ref[...]
    lane = lax.broadcasted_iota(jnp.int32, ang.shape, 1)
    rot = (lane >= NOPE) & (lane < NOPE + ROPE)
    first = lane < NOPE + ROPE // 2
    cos_ref[...] = jnp.where(rot, jnp.cos(ang), 1.0)
    s = jnp.sin(ang)
    sin_ref[...] = jnp.where(rot, jnp.where(first, -s, s), 0.0)


def _rope_tables(positions):
    n = positions.size
    inv_freq = ROPE_THETA ** (-jnp.arange(0, ROPE, 2, dtype=f32) / ROPE)
    invf = jnp.zeros((1, LANES), f32).at[0, NOPE:NOPE + ROPE].set(jnp.tile(inv_freq, 2))
    t = math.gcd(n, 2048)
    return pl.pallas_call(
        _rope_kernel,
        grid=(n // t,),
        in_specs=[pl.BlockSpec((t, 1), lambda i: (i, 0)), _const_spec((1, LANES))],
        out_specs=[pl.BlockSpec((t, LANES), lambda i: (i, 0))] * 2,
        out_shape=[jax.ShapeDtypeStruct((n, LANES), f32)] * 2,
        compiler_params=_cparams(("arbitrary",)),
        name="rope_tables",
    )(positions.reshape(n, 1), invf)


def _t5_bucket(rel):
    n = jnp.abs(rel)
    large = 8 + sum((n >= t).astype(jnp.int32) for t in (12, 16, 23, 32, 46, 64, 91))
    return jnp.where(rel > 0, 16, 0) + jnp.where(n < 8, n, large)


def _bias_kernel(tbl_ref, t0_ref, t1_ref, mk_ref):
    r = lax.broadcasted_iota(jnp.int32, (TQ, TK), 0)
    c = lax.broadcasted_iota(jnp.int32, (TQ, TK), 1)
    allowed = (c // CHUNK) <= (r // CHUNK)
    mk = jnp.where(allowed, 0.0, NEG).astype(f32)
    mk_ref[...] = mk
    b0 = _t5_bucket(c - r)
    b1 = _t5_bucket(c - r - TK)
    for h in range(HA):
        far = tbl_ref[N_REL_BUCKETS // 2 - 1, h]
        t0 = jnp.zeros((TQ, TK), f32)
        t1 = jnp.zeros((TQ, TK), f32)
        for b in range(N_REL_BUCKETS):
            v = tbl_ref[b, h] - far
            t0 = jnp.where(b0 == b, v, t0)
            t1 = jnp.where(b1 == b, v, t1)
        t0 = t0 + mk
        t0_ref[h, 0:TQ, :] = t0
        t0_ref[h, TQ:2 * TQ, :] = t0
        t1_ref[h, 0:TQ, :] = t1
        t1_ref[h, TQ:2 * TQ, :] = t1


def _bias_tiles(rel_bias):
    return pl.pallas_call(
        _bias_kernel,
        in_specs=[pl.BlockSpec(memory_space=pltpu.SMEM)],
        out_specs=[pl.BlockSpec(memory_space=pltpu.VMEM)] * 3,
        out_shape=[jax.ShapeDtypeStruct((HA, 2 * TQ, TK), f32),
                   jax.ShapeDtypeStruct((HA, 2 * TQ, TK), f32),
                   jax.ShapeDtypeStruct((TQ, TK), f32)],
        name="bias_tiles",
    )(rel_bias)


def _inproj_kernel(*refs, combine):
    if combine:
        (xa_ref, y_ref, g2_ref, sh_ref, sc_ref, ng_ref, w_ref, gq_ref, gk_ref, sd_ref, ex_ref,
         vg_ref, ws_ref, bs_ref,
         xo_ref, qa_ref, ka_ref, va_ref, ob_ref, xc_ref, gt_ref) = refs
        x = xa_ref[...] + g2_ref[0] * y_ref[...]
        xo_ref[...] = x
    else:
        (xa_ref, sh_ref, sc_ref, ng_ref, w_ref, gq_ref, gk_ref, sd_ref, ex_ref,
         vg_ref, ws_ref, bs_ref,
         qa_ref, ka_ref, va_ref, ob_ref, xc_ref, gt_ref) = refs
        x = xa_ref[...]
    h = _rms(x, ng_ref[...]) * (1.0 + sc_ref[0]) + sh_ref[0]
    hb = h.astype(bf16)

    q = jnp.dot(hb, w_ref[:, 0:MIX], preferred_element_type=f32)
    qa_ref[...] = (q * _seg_rinv(q, sd_ref, ex_ref) * gq_ref[...]).astype(bf16)
    k = jnp.dot(hb, w_ref[:, MIX:2 * MIX], preferred_element_type=f32)
    ka_ref[...] = (k * _seg_rinv(k, sd_ref, ex_ref) * gk_ref[...]).astype(bf16)
    va_ref[...] = jnp.dot(hb, w_ref[:, 2 * MIX:3 * MIX], preferred_element_type=f32).astype(bf16)

    z = jnp.dot(hb, w_ref[:, 3 * MIX:5 * MIX], preferred_element_type=f32)
    z = 0.5 * z * (1.0 + jnp.tanh(math.sqrt(2.0 / math.pi) * (z + 0.044715 * (z * z * z))))
    u = z[:, :MIX]
    v = _rms(z[:, MIX:], vg_ref[...]).astype(bf16)
    ri = lax.broadcasted_iota(jnp.int32, (LANES, LANES), 0)
    ci = lax.broadcasted_iota(jnp.int32, (LANES, LANES), 1)
    allowed = (ci // CHUNK) <= (ri // CHUNK)
    for g in range(4):
        wm = jnp.where(allowed, ws_ref[g], 0.0).astype(bf16)
        bcol = bs_ref[:, g:g + 1]
        for wd in range(TM // LANES):
            rows = slice(wd * LANES, (wd + 1) * LANES)
            cols = slice(g * LANES, (g + 1) * LANES)
            vs = jnp.dot(wm, v[rows, cols], preferred_element_type=f32) + bcol
            ob_ref[rows, cols] = (u[rows, cols] * vs).astype(bf16)

    xc_ref[...] = jnp.dot(hb, w_ref[:, XC_OFF:GATE_OFF], preferred_element_type=f32)

    for j in range(3):
        gsl = slice(GATE_OFF + j * D, GATE_OFF + (j + 1) * D)
        gl = jnp.dot(hb, w_ref[:, gsl], preferred_element_type=f32)
        gt_ref[:, j * D:(j + 1) * D] = jax.nn.sigmoid(gl).astype(bf16)


def _inproj(xa, y, g2, shift, scale, ng, w, gq, gk, sd, ex, vg, ws, bs, seq):
    n = xa.shape[0]
    spb = seq // TM
    combine = y is not None
    row = lambda w_: pl.BlockSpec((TM, w_), lambda i: (i, 0))
    modspec = pl.BlockSpec((1, 1, D), lambda i: (i // spb, 0, 0))
    in_specs = [row(D)]
    args = [xa]
    if combine:
        in_specs += [row(D), modspec]
        args += [y, g2]
    in_specs += [modspec, modspec, _const_spec((1, D)),
                 pl.BlockSpec((D, IN_PAD), lambda i: (0, 0), pipeline_mode=pl.Buffered(1)),
                 _const_spec((1, MIX)), _const_spec((1, MIX)),
                 _const_spec((MIX, LANES)), _const_spec((2 * LANES, MIX)),
                 _const_spec((1, MIX)), _const_spec((4, LANES, LANES)), _const_spec((LANES, 4))]
    args += [shift, scale, ng, w, gq, gk, sd, ex, vg, ws, bs]
    out_specs = [row(MIX)] * 4 + [row(MIX), row(3 * D)]
    out_shape = [jax.ShapeDtypeStruct((n, MIX), bf16)] * 4 + [
        jax.ShapeDtypeStruct((n, MIX), f32), jax.ShapeDtypeStruct((n, 3 * D), bf16)]
    if combine:
        out_specs = [row(D)] + out_specs
        out_shape = [jax.ShapeDtypeStruct((n, D), f32)] + out_shape
    return pl.pallas_call(
        functools.partial(_inproj_kernel, combine=combine),
        grid=(n // TM,),
        in_specs=in_specs, out_specs=out_specs, out_shape=out_shape,
        compiler_params=_cparams(("arbitrary",)),
        name="inproj",
    )(*args)


def _online_update(s, vj, m, l, acc):
    m_new = jnp.maximum(m, jnp.max(s, axis=-1, keepdims=True))
    alpha = jnp.exp(m - m_new)
    p = jnp.exp(s - m_new)
    l = alpha * l + jnp.sum(p, axis=-1, keepdims=True)
    acc = alpha * acc + jnp.dot(p.astype(bf16), vj, preferred_element_type=f32)
    return m_new, l, acc


def _attend(qs, k_ref, v_ref, kcols, vcols, qi, diag_bias, sub_bias):
    rows = qs.shape[0]

    def scores(j):
        kj = k_ref[pl.ds(pl.multiple_of(j * TK, TK), TK), kcols]
        return lax.dot_general(qs, kj, (((1,), (1,)), ((), ())), preferred_element_type=f32)

    def vblk(j):
        return v_ref[pl.ds(pl.multiple_of(j * TK, TK), TK), vcols]

    m = jnp.full((rows, 1), NEG, f32)
    l = jnp.zeros((rows, 1), f32)
    acc = jnp.zeros((rows, LANES), f32)
    m, l, acc = _online_update(scores(qi) + diag_bias, vblk(qi), m, l, acc)
    jp = jnp.maximum(qi - 1, 0)
    s = scores(jp) + jnp.where(qi >= 1, 0.0, NEG)
    if sub_bias is not None:
        s = s + sub_bias
    m, l, acc = _online_update(s, vblk(jp), m, l, acc)

    def far(j, carry):
        return _online_update(scores(j), vblk(j), *carry)

    m, l, acc = lax.fori_loop(0, jnp.maximum(qi - 1, 0), far, (m, l, acc))
    return acc / l


def _attn_a_kernel(q_ref, k_ref, v_ref, t0_ref, t1_ref, dl_ref, og_ref, o_ref, *, lambda_init):
    qi = pl.program_id(1)
    dl = dl_ref[...]
    lam = (jnp.exp(jnp.sum(dl[0:1] * dl[1:2], axis=-1, keepdims=True))
           - jnp.exp(jnp.sum(dl[2:3] * dl[3:4], axis=-1, keepdims=True)) + lambda_init)
    lane = lax.broadcasted_iota(jnp.int32, (TQ, LANES), 1)
    for h in range(HA):
        cols = slice(h * LANES, (h + 1) * LANES)
        qh = q_ref[:, cols]
        zero = jnp.zeros_like(qh)
        qs = jnp.concatenate([jnp.where(lane < 64, qh, zero), jnp.where(lane >= 64, qh, zero)], axis=0)
        o = _attend(qs, k_ref, v_ref, cols, cols, qi, t0_ref[h], t1_ref[h])
        oh = o[:TQ] - lam * o[TQ:]
        oh = _rms(oh, og_ref[...]) * (1.0 - lambda_init)
        o_ref[:, cols] = oh.astype(bf16)


def _attn_a(qa, ka, va, t0, t1, dl, og, nb, seq, lambda_init):
    n = qa.shape[0]
    nq = seq // TQ
    return pl.pallas_call(
        functools.partial(_attn_a_kernel, lambda_init=lambda_init),
        grid=(nb, nq),
        in_specs=[pl.BlockSpec((TQ, MIX), lambda b, i: (b * nq + i, 0)),
                  pl.BlockSpec((seq, MIX), lambda b, i: (b, 0)),
                  pl.BlockSpec((seq, MIX), lambda b, i: (b, 0)),
                  _const_spec((HA, 2 * TQ, TK)), _const_spec((HA, 2 * TQ, TK)),
                  _const_spec((4, 64)), _const_spec((1, LANES))],
        out_specs=pl.BlockSpec((TQ, MIX), lambda b, i: (b * nq + i, 0)),
        out_shape=jax.ShapeDtypeStruct((n, MIX), bf16),
        compiler_params=_cparams(("arbitrary", "arbitrary")),
        name="attn_diff",
    )(qa, ka, va, t0, t1, dl, og)


def _attn_c_kernel(q_ref, k_ref, v_ref, mk_ref, o_ref):
    qi = pl.program_id(1)
    lane = lax.broadcasted_iota(jnp.int32, (TQ, LANES), 1)
    mk = mk_ref[...]
    for hp in range(HC // 2):
        vcols = slice(hp * LANES, (hp + 1) * LANES)
        outs = []
        for h in (2 * hp, 2 * hp + 1):
            cols = slice(h * LANES, (h + 1) * LANES)
            outs.append(_attend(q_ref[:, cols], k_ref, v_ref, cols, vcols, qi, mk, None))
        o_ref[:, vcols] = jnp.where(lane < VC, outs[0], outs[1]).astype(bf16)


def _attn_c(qc, kc, vc, mk, nb, seq):
    n = qc.shape[0]
    nq = seq // TQ
    return pl.pallas_call(
        _attn_c_kernel,
        grid=(nb, nq),
        in_specs=[pl.BlockSpec((TQ, HC * LANES), lambda b, i: (b * nq + i, 0)),
                  pl.BlockSpec((seq, HC * LANES), lambda b, i: (b, 0)),
                  pl.BlockSpec((seq, MIX), lambda b, i: (b, 0)),
                  _const_spec((TQ, TK))],
        out_specs=pl.BlockSpec((TQ, MIX), lambda b, i: (b * nq + i, 0)),
        out_shape=jax.ShapeDtypeStruct((n, MIX), bf16),
        compiler_params=_cparams(("arbitrary", "arbitrary")),
        name="attn_latent",
    )(qc, kc, vc, mk)


def _rope_apply(x, cosf, sinf, lane):
    w = x.shape[-1]
    partner = jnp.where((lane % LANES) < NOPE + ROPE // 2,
                        pltpu.roll(x, w - ROPE // 2, 1), pltpu.roll(x, ROPE // 2, 1))
    return x * cosf + partner * sinf


def _mla_prep_kernel(xc_ref, cos_ref, sin_ref, glq_ref, glkv_ref, wq_ref, wkv_ref, gq_ref, gk_ref, gr_ref,
                     sd_ref, ex_ref, q_ref, k_ref, v_ref):
    cq = _rms(xc_ref[:, 0:Q_LORA], glq_ref[...]).astype(bf16)
    ckv = _rms(xc_ref[:, Q_LORA:Q_LORA + KV_LORA], glkv_ref[...]).astype(bf16)
    cos8 = jnp.concatenate([cos_ref[...]] * HC, axis=-1)
    sin8 = jnp.concatenate([sin_ref[...]] * HC, axis=-1)
    lane8 = lax.broadcasted_iota(jnp.int32, (TM, HC * LANES), 1)

    q = jnp.dot(cq, wq_ref[...], preferred_element_type=f32)
    q = q * _seg_rinv(q, sd_ref, ex_ref) * gq_ref[...]
    q_ref[...] = _rope_apply(q, cos8, sin8, lane8).astype(bf16)

    kv = jnp.dot(ckv, wkv_ref[...], preferred_element_type=f32)
    v_ref[...] = kv[:, HC * LANES:].astype(bf16)
    xr = xc_ref[:, Q_LORA + KV_LORA:Q_LORA + KV_LORA + LANES]
    kr = xr * lax.rsqrt(jnp.sum(xr * xr, axis=-1, keepdims=True) * (1.0 / ROPE) + EPS) * gr_ref[...]
    kr = pltpu.roll(kr, NOPE, 1)
    lane1 = lax.broadcasted_iota(jnp.int32, (TM, LANES), 1)
    kr = _rope_apply(kr, cos_ref[...], sin_ref[...], lane1)
    for h in range(HC):
        cols = slice(h * LANES, (h + 1) * LANES)
        kn = kv[:, cols]
        kn = kn * lax.rsqrt(jnp.sum(kn * kn, axis=-1, keepdims=True) * (1.0 / NOPE) + EPS) * gk_ref[...]
        k_ref[:, cols] = (kn + kr).astype(bf16)


def _mla_prep(xc, cosf, sinf, glq, glkv, wq, wkv, gq, gk, gr, sd, ex):
    n = xc.shape[0]
    row = lambda w_: pl.BlockSpec((TM, w_), lambda i: (i, 0))
    return pl.pallas_call(
        _mla_prep_kernel,
        grid=(n // TM,),
        in_specs=[row(MIX), row(LANES), row(LANES), _const_spec((1, Q_LORA)), _const_spec((1, KV_LORA)),
                  _const_spec((Q_LORA, HC * LANES)), _const_spec((KV_LORA, HC * LANES + MIX)),
                  _const_spec((1, HC * LANES)), _const_spec((1, LANES)), _const_spec((1, LANES)),
                  _const_spec((HC * LANES, LANES)), _const_spec((2 * LANES, HC * LANES))],
        out_specs=[row(HC * LANES), row(HC * LANES), row(MIX)],
        out_shape=[jax.ShapeDtypeStruct((n, HC * LANES), bf16), jax.ShapeDtypeStruct((n, HC * LANES), bf16),
                   jax.ShapeDtypeStruct((n, MIX), bf16)],
        compiler_params=_cparams(("arbitrary",)),
        name="mla_prep",
    )(xc, cosf, sinf, glq, glkv, wq, wkv, gq, gk, gr, sd, ex)


def _merge_kernel(oa_ref, ob_ref, oc_ref, gt_ref, x_ref, g1_ref, sh_ref, sc_ref, ng_ref, wb_ref, wo_ref,
                  wr_ref, br_ref, x1_ref, h2_ref, meta_ref, cnt_ref, run_ref):
    i = pl.program_id(0)

    @pl.when(i == 0)
    def _():
        run_ref[...] = jnp.zeros_like(run_ref)

    merged = jnp.zeros((TM, D), f32)
    for j, o_ref in enumerate((oa_ref, ob_ref, oc_ref)):
        pj = jnp.dot(o_ref[...], wb_ref[j], preferred_element_type=f32)
        merged = merged + gt_ref[:, j * D:(j + 1) * D].astype(f32) * pj
    y = jnp.dot(merged.astype(bf16), wo_ref[...], preferred_element_type=f32)
    x1 = x_ref[...] + g1_ref[0] * y
    x1_ref[...] = x1
    h2 = _rms(x1, ng_ref[...]) * (1.0 + sc_ref[0]) + sh_ref[0]
    h2_ref[:, 0:D] = h2

    lg = jnp.dot(h2, wr_ref[...], preferred_element_type=f32, precision=lax.Precision.HIGHEST) + br_ref[...]
    lane = lax.broadcasted_iota(jnp.int32, (TM, LANES), 1)
    n_e = N_GROUPS * N_EXP
    gmask = (lane >= n_e) & (lane < n_e + N_GROUPS)
    gl = jnp.where(gmask, lg, NEG)
    gmax = jnp.max(gl, axis=-1, keepdims=True)
    g_w = 1.0 / jnp.sum(jnp.where(gmask, jnp.exp(gl - gmax), 0.0), axis=-1, keepdims=True)
    gidx = jnp.min(jnp.where(gl == gmax, lane - n_e, LANES), axis=-1, keepdims=True)
    emask = (lane < n_e) & ((lane // N_EXP) == gidx)
    el = jnp.where(emask, lg, NEG)
    m1 = jnp.max(el, axis=-1, keepdims=True)
    i1 = jnp.min(jnp.where(el == m1, lane, LANES), axis=-1, keepdims=True)
    el2 = jnp.where(lane == i1, NEG, el)
    m2 = jnp.max(el2, axis=-1, keepdims=True)
    i2 = jnp.min(jnp.where(el2 == m2, lane, LANES), axis=-1, keepdims=True)
    t = jnp.exp(m2 - m1)
    w1 = g_w / (1.0 + t)
    w2 = g_w * t / (1.0 + t)
    lo = jnp.minimum(i1, i2)
    hi = jnp.maximum(i1, i2)
    w_lo = jnp.where(i1 < i2, w1, w2)
    w_hi = jnp.where(i1 < i2, w2, w1)
    bucket = gidx * (N_EXP * N_EXP) + (lo % N_EXP) * N_EXP + (hi % N_EXP)

    lane2 = lax.broadcasted_iota(jnp.int32, (TM, 2 * LANES), 1)
    onehot = (lane2 == bucket).astype(f32)
    rr = lax.broadcasted_iota(jnp.int32, (TM, TM), 0)
    cc = lax.broadcasted_iota(jnp.int32, (TM, TM), 1)
    tri = (cc < rr).astype(bf16)
    before = jnp.dot(tri, onehot.astype(bf16), preferred_element_type=f32) + run_ref[...]
    rank = jnp.sum(onehot * before, axis=-1, keepdims=True)
    run_ref[...] = run_ref[...] + jnp.sum(onehot, axis=0, keepdims=True)
    cnt_ref[...] = run_ref[...]

    meta = (jnp.where(lane == i1, w1, 0.0) + jnp.where(lane == i2, w2, 0.0)
            + jnp.where(lane == 32, bucket.astype(f32), 0.0) + jnp.where(lane == 33, rank, 0.0)
            + jnp.where(lane == 34, w_lo, 0.0) + jnp.where(lane == 35, w_hi, 0.0))
    meta_ref[...] = meta
    h2_ref[:, D:D + LANES] = meta


def _merge(oa, ob, oc, gt, x, g1, shift, scale, ng, wb, wo, wr, br, seq):
    n = x.shape[0]
    spb = seq // TM
    row = lambda w_: pl.BlockSpec((TM, w_), lambda i: (i, 0))
    modspec = pl.BlockSpec((1, 1, D), lambda i: (i // spb, 0, 0))
    return pl.pallas_call(
        _merge_kernel,
        grid=(n // TM,),
        in_specs=[row(MIX), row(MIX), row(MIX), row(3 * D), row(D), modspec, modspec, modspec,
                  _const_spec((1, D)), _const_spec((3, MIX, D)), _const_spec((D, D)),
                  _const_spec((D, LANES)), _const_spec((1, LANES))],
        out_specs=[row(D), row(D + LANES), row(LANES), _const_spec((1, 2 * LANES))],
        out_shape=[jax.ShapeDtypeStruct((n, D), f32), jax.ShapeDtypeStruct((n, D + LANES), f32),
                   jax.ShapeDtypeStruct((n, LANES), f32), jax.ShapeDtypeStruct((1, 2 * LANES), f32)],
        scratch_shapes=[pltpu.VMEM((1, 2 * LANES), f32)],
        compiler_params=_cparams(("arbitrary",)),
        name="merge_route",
    )(oa, ob, oc, gt, x, g1, shift, scale, ng, wb, wo, wr, br)


TE = 128
N_BUCKETS = N_GROUPS * N_EXP * N_EXP
N_PAIRS = N_GROUPS * (N_EXP * (N_EXP - 1) // 2)


def _num_tiles(n):
    return n // TE + N_PAIRS


def _plan_kernel(bucket_ref, rank_ref, cnt_ref, tok_ref, tea_ref, teb_ref, nt_ref, base_ref, *, n):
    ntmax = _num_tiles(n)

    def per_bucket(b, carry):
        base, tile = carry
        c = cnt_ref[b]
        ntb = lax.shift_right_logical(c + (TE - 1), 7)
        base_ref[b] = base
        grp = lax.shift_right_logical(b, 6) * N_EXP
        ea = grp + (lax.shift_right_logical(b, 3) & 7)
        eb = grp + (b & 7)

        def set_tile(k, _):
            tea_ref[tile + k] = ea
            teb_ref[tile + k] = eb
            return 0

        lax.fori_loop(0, ntb, set_tile, 0)

        def set_pad(s, _):
            tok_ref[s] = -1
            return 0

        lax.fori_loop(base + c, base + ntb * TE, set_pad, 0)
        return base + ntb * TE, tile + ntb

    base, tile = lax.fori_loop(0, N_BUCKETS, per_bucket, (jnp.int32(0), jnp.int32(0)))
    nt_ref[0] = tile
    last_a = tea_ref[tile - 1]
    last_b = teb_ref[tile - 1]

    def set_tail(t, _):
        tea_ref[t] = last_a
        teb_ref[t] = last_b
        return 0

    lax.fori_loop(tile, ntmax, set_tail, 0)

    def set_pad_tail(s, _):
        tok_ref[s] = -1
        return 0

    lax.fori_loop(base, (ntmax + 1) * TE, set_pad_tail, 0)

    def place(t, _):
        tok_ref[base_ref[bucket_ref[t]] + rank_ref[t]] = t
        return 0

    lax.fori_loop(0, n, place, 0, unroll=8)


def _plan(bucket, rank, cnt):
    n = bucket.shape[0]
    ntmax = _num_tiles(n)
    smem = pl.BlockSpec(memory_space=pltpu.SMEM)
    return pl.pallas_call(
        functools.partial(_plan_kernel, n=n),
        in_specs=[smem] * 3,
        out_specs=[smem] * 4,
        out_shape=[jax.ShapeDtypeStruct(((ntmax + 1) * TE,), jnp.int32), jax.ShapeDtypeStruct((ntmax,), jnp.int32),
                   jax.ShapeDtypeStruct((ntmax,), jnp.int32), jax.ShapeDtypeStruct((1,), jnp.int32)],
        scratch_shapes=[pltpu.SMEM((N_BUCKETS,), jnp.int32)],
        name="moe_plan",
    )(bucket, rank, cnt)


def _moe_kernel(tea_ref, teb_ref, nt_ref, tok_ref, h_hbm, wg_ref, wu_ref, wd_ref, y_hbm,
                xbuf, obuf, gsem, ssem, *, n):
    i = pl.program_id(0)
    nt = nt_ref[0]
    slot = lax.rem(i, 2)

    def gather_copy(src_row, sl, r):
        return pltpu.make_async_copy(h_hbm.at[pl.ds(src_row, 1), :], xbuf.at[sl, pl.ds(r, 1), :], gsem.at[sl])

    def scatter_copy(dst_row, sl, r):
        return pltpu.make_async_copy(obuf.at[sl, pl.ds(r, 1), :], y_hbm.at[pl.ds(dst_row, 1), :], ssem.at[sl])

    def start_gathers(tile, sl):
        def body(r, _):
            gather_copy(jnp.maximum(tok_ref[tile * TE + r], 0), sl, r).start()
            return 0
        lax.fori_loop(0, TE, body, 0, unroll=8)

    def wait_gathers(sl):
        def body(r, _):
            gather_copy(0, sl, r).wait()
            return 0
        lax.fori_loop(0, TE, body, 0, unroll=8)

    def start_scatters(tile, sl):
        def body(r, _):
            t = tok_ref[tile * TE + r]
            scatter_copy(jnp.where(t < 0, n + sl * TE + r, t), sl, r).start()
            return 0
        lax.fori_loop(0, TE, body, 0, unroll=8)

    def wait_scatters(sl):
        def body(r, _):
            scatter_copy(n, sl, r).wait()
            return 0
        lax.fori_loop(0, TE, body, 0, unroll=8)

    @pl.when(i == 0)
    def _():
        start_gathers(0, 0)
        obuf[...] = jnp.zeros_like(obuf)
        for sl in range(2):
            spare = pltpu.make_async_copy(obuf.at[sl], y_hbm.at[pl.ds(n + sl * TE, TE), :], ssem.at[sl])
            spare.start()
            spare.wait()

    @pl.when(i < nt)
    def _():
        start_gathers(i + 1, 1 - slot)
        wait_gathers(slot)
        xs = xbuf[slot]
        hb = xs[:, 0:D].astype(bf16)
        meta = xs[:, D:D + LANES]
        lane = lax.broadcasted_iota(jnp.int32, (TE, LANES), 1)
        w_lo = jnp.sum(jnp.where(lane == 34, meta, 0.0), axis=-1, keepdims=True)
        w_hi = jnp.sum(jnp.where(lane == 35, meta, 0.0), axis=-1, keepdims=True)

        def ffn(e, wcol):
            a = jnp.dot(hb, wg_ref[0, e], preferred_element_type=f32)
            u = jnp.dot(hb, wu_ref[0, e], preferred_element_type=f32)
            act = (a * jax.nn.sigmoid(a)) * u * wcol
            return jnp.dot(act.astype(bf16), wd_ref[0, e], preferred_element_type=f32)

        obuf[slot] = ffn(lax.rem(tea_ref[i], N_EXP), w_lo) + ffn(lax.rem(teb_ref[i], N_EXP), w_hi)
        start_scatters(i, slot)

        @pl.when(i >= 1)
        def _():
            wait_scatters(1 - slot)

        @pl.when(i == nt - 1)
        def _():
            wait_gathers(1 - slot)
            wait_scatters(slot)


def _moe(h2ext, tok, tea, teb, nt, wg, wu, wd):
    n = h2ext.shape[0]
    ntmax = _num_tiles(n)
    wspec_in = pl.BlockSpec((1, N_EXP, D, F_EXP), lambda i, tea, teb, nt, tok: (tea[i] // N_EXP, 0, 0, 0))
    wspec_out = pl.BlockSpec((1, N_EXP, F_EXP, D), lambda i, tea, teb, nt, tok: (tea[i] // N_EXP, 0, 0, 0))
    return pl.pallas_call(
        functools.partial(_moe_kernel, n=n),
        grid_spec=pltpu.PrefetchScalarGridSpec(
            num_scalar_prefetch=4,
            grid=(ntmax,),
            in_specs=[pl.BlockSpec(memory_space=pl.ANY), wspec_in, wspec_in, wspec_out],
            out_specs=pl.BlockSpec(memory_space=pl.ANY),
            scratch_shapes=[pltpu.VMEM((2, TE, D + LANES), f32), pltpu.VMEM((2, TE, D), f32),
                            pltpu.SemaphoreType.DMA((2,)), pltpu.SemaphoreType.DMA((2,))]),
        out_shape=jax.ShapeDtypeStruct((n + 2 * TE, D), f32),
        compiler_params=_cparams(("arbitrary",)),
        name="moe_sparse",
    )(tea, teb, nt, tok, h2ext, wg, wu, wd)


def _final_kernel(x_ref, y_ref, g_ref, o_ref):
    o_ref[...] = x_ref[...] + g_ref[0] * y_ref[...]


def _final_combine(x1, y, g2, seq):
    n = x1.shape[0]
    t = 512
    spb = seq // t
    row = pl.BlockSpec((t, D), lambda i: (i, 0))
    return pl.pallas_call(
        _final_kernel,
        grid=(n // t,),
        in_specs=[row, row, pl.BlockSpec((1, 1, D), lambda i: (i // spb, 0, 0))],
        out_specs=row,
        out_shape=jax.ShapeDtypeStruct((n, D), f32),
        compiler_params=_cparams(("arbitrary",)),
        name="final_combine",
    )(x1, y, g2)


def _seg_matrices(width, segs):
    sd = np.zeros((width, LANES), np.float32)
    ex = np.zeros((LANES, width), np.float32)
    for j, (s, ln) in enumerate(segs):
        sd[s:s + ln, j] = 1.0 / ln
        ex[j, s:s + ln] = 1.0
    return jnp.asarray(sd, bf16), jnp.asarray(np.concatenate([ex, ex], axis=0), bf16)


def _head_pad(w, heads, per_head, keep):
    k = w.shape[0]
    w3 = w.reshape(k, heads, per_head)[:, :, :keep]
    return jnp.pad(w3, ((0, 0), (0, 0), (0, LANES - keep))).reshape(k, heads * LANES)


def kernel(x, c, positions, w_ada, b_ada, norm_g, w_in, diff_qk_g, diff_lambda, diff_out_g, rel_bias, sgu_v_g, sgu_w, sgu_b, mla_lat_g, mla_w_uq, mla_w_ukv, mla_qk_g, w_branch, w_out, router_g_w, router_g_b, router_e_w, router_e_b, w_e_gate, w_e_up, w_e_down):
    nb, seq, _ = x.shape
    n = nb * seq
    assert seq % TQ == 0 and seq % TM == 0 and x.shape[2] == D

    mod = _ada(c, w_ada, b_ada)
    cosf, sinf = _rope_tables(positions)
    t0, t1, mk = _bias_tiles(rel_bias)

    sd_a, ex_a = _seg_matrices(MIX, [(s * 64, 64) for s in range(8)])
    segs_q = []
    for h in range(HC):
        segs_q += [(h * LANES, NOPE), (h * LANES + NOPE, ROPE)]
    sd_q, ex_q = _seg_matrices(HC * LANES, segs_q)
    ne = N_GROUPS * N_EXP
    zpad = lambda a, w_: jnp.pad(a, ((0, 0), (0, w_ - a.shape[1])))

    xcur = x.reshape(n, D)
    y_prev, g2_prev = None, None
    for l in range(DEPTH):
        m3 = mod[l].reshape(nb, 1, 6 * D)
        shift1, scale1, gate1, shift2, scale2, gate2 = [m3[:, :, k * D:(k + 1) * D] for k in range(6)]
        lambda_init = LAMBDA_INIT_BASE - LAMBDA_INIT_SCALE * math.exp(-LAMBDA_INIT_DECAY * l)

        w = w_in[l]
        w_pad = jnp.concatenate([w[:, :XC_OFF + 416], jnp.zeros((D, IN_PAD - IN_REAL), f32), w[:, XC_OFF + 416:]],
                                axis=1).astype(bf16)
        gq = (jnp.tile(diff_qk_g[l, 0], 8) * (64 ** -0.5)).reshape(1, MIX)
        gk = jnp.tile(diff_qk_g[l, 1], 8).reshape(1, MIX)
        outs = _inproj(xcur, y_prev, g2_prev, shift1, scale1, norm_g[l, 0].reshape(1, D), w_pad, gq, gk,
                       sd_a, ex_a, sgu_v_g[l].reshape(1, MIX), sgu_w[l], jnp.transpose(sgu_b[l]), seq)
        if y_prev is not None:
            xcur = outs[0]
            outs = outs[1:]
        qa, ka, va, ob, xc, gt = outs

        oa = _attn_a(qa, ka, va, t0, t1, diff_lambda[l], diff_out_g[l].reshape(1, LANES), nb, seq, lambda_init)

        qkg = mla_qk_g[l]
        wq = _head_pad(mla_w_uq[l], HC, NOPE + ROPE, NOPE + ROPE).astype(bf16)
        wkv3 = mla_w_ukv[l].reshape(KV_LORA, HC, NOPE + VC)
        wk = jnp.pad(wkv3[:, :, :NOPE], ((0, 0), (0, 0), (0, LANES - NOPE))).reshape(KV_LORA, HC * LANES)
        wv = wkv3[:, :, NOPE:].reshape(KV_LORA, HC * VC)
        wkv = jnp.concatenate([wk, wv], axis=1).astype(bf16)
        gq_c = jnp.tile(jnp.pad(qkg[0], (0, LANES - NOPE - ROPE)), HC).reshape(1, HC * LANES) * ((NOPE + ROPE) ** -0.5)
        gk_c = jnp.pad(qkg[1, :NOPE], (0, LANES - NOPE)).reshape(1, LANES)
        gr_c = jnp.pad(qkg[1, NOPE:], (0, LANES - ROPE)).reshape(1, LANES)
        qc, kc, vc = _mla_prep(xc, cosf, sinf, mla_lat_g[l, :Q_LORA].reshape(1, Q_LORA),
                               mla_lat_g[l, Q_LORA:].reshape(1, KV_LORA), wq, wkv, gq_c, gk_c, gr_c, sd_q, ex_q)
        oc = _attn_c(qc, kc, vc, mk, nb, seq)

        wr = zpad(jnp.concatenate([router_e_w[l], router_g_w[l]], axis=1), LANES)
        br = zpad(jnp.concatenate([router_e_b[l], router_g_b[l]]).reshape(1, ne + N_GROUPS), LANES)
        x1, h2ext, meta, cnt = _merge(oa, ob, oc, gt, xcur, gate1, shift2, scale2, norm_g[l, 1].reshape(1, D),
                                      w_branch[l].astype(bf16), w_out[l].astype(bf16), wr, br, seq)

        tok, tea, teb, nt = _plan(meta[:, 32].astype(jnp.int32), meta[:, 33].astype(jnp.int32),
                                  cnt[0].astype(jnp.int32))
        y_prev = _moe(h2ext, tok, tea, teb, nt, w_e_gate[l].astype(bf16), w_e_up[l].astype(bf16),
                      w_e_down[l].astype(bf16))
        g2_prev = gate2
        xcur = x1

    out = _final_combine(xcur, y_prev, g2_prev, seq)
    return out.reshape(nb, seq, D)
```

```python
import functools
import math

import jax
import jax.numpy as jnp
import numpy as np
from jax import lax
from jax.experimental import pallas as pl
from jax.experimental.pallas import tpu as pltpu

f32 = jnp.float32
bf16 = jnp.bfloat16

D = 1024
DEPTH = 4
CHUNK = 64
MIX = 512
HA = 4
HC = 8
Q_LORA, KV_LORA, ROPE = 256, 128, 32
NOPE, VC = 64, 64
N_GROUPS, N_EXP, F_EXP = 4, 8, 256
N_REL_BUCKETS = 32
ROPE_THETA = 10000.0
LAMBDA_INIT_BASE, LAMBDA_INIT_SCALE, LAMBDA_INIT_DECAY = 0.8, 0.6, 0.3
EPS = 1e-6
NEG = -1e30

IN_REAL = 6048
IN_PAD = 6144
XC_OFF = 2560
GATE_OFF = 3072

TM = 256
TQ = 256
TK = 256
LANES = 128
VMEM_LIMIT = 56 * 1024 * 1024


def _cparams(sem, vmem=VMEM_LIMIT):
    return pltpu.CompilerParams(dimension_semantics=sem, vmem_limit_bytes=vmem)


def _const_spec(shape):
    nd = len(shape)
    return pl.BlockSpec(shape, lambda *_: (0,) * nd)


def _rms(x, g_row):
    return x * lax.rsqrt(jnp.mean(x * x, axis=-1, keepdims=True) + EPS) * g_row


def _seg_rinv(x, sd_ref, ex_ref):
    ss = jnp.dot((x * x).astype(bf16), sd_ref[...], preferred_element_type=f32)
    hi = ss.astype(bf16)
    lo = (ss - hi.astype(f32)).astype(bf16)
    full = jnp.dot(jnp.concatenate([hi, lo], axis=-1), ex_ref[...], preferred_element_type=f32)
    return lax.rsqrt(full + EPS)


def _ada_kernel(c_ref, w_ref, b_ref, o_ref):
    c = c_ref[...]
    ca = (c * jax.nn.sigmoid(c)).astype(bf16)
    o_ref[0] = jnp.dot(ca, w_ref[0].astype(bf16), preferred_element_type=f32) + b_ref[0]


def _ada(c, w_ada, b_ada):
    nb = c.shape[0]
    tn = 1024
    return pl.pallas_call(
        _ada_kernel,
        grid=(DEPTH, 6 * D // tn),
        in_specs=[pl.BlockSpec((nb, D), lambda l, j: (0, 0)),
                  pl.BlockSpec((1, D, tn), lambda l, j: (l, 0, j)),
                  pl.BlockSpec((1, 1, tn), lambda l, j: (l, 0, j))],
        out_specs=pl.BlockSpec((1, nb, tn), lambda l, j: (l, 0, j)),
        out_shape=jax.ShapeDtypeStruct((DEPTH, nb, 6 * D), f32),
        compiler_params=_cparams(("arbitrary", "arbitrary")),
        name="ada_mod",
    )(c, w_ada, b_ada.reshape(DEPTH, 1, 6 * D))


def _rope_kernel(pos_ref, invf_ref, cos_ref, sin_ref):
    ang = pos_ref[...].astype(f32) * invf_ref[...]
    lane = lax.broadcasted_iota(jnp.int32, ang.shape, 1)
    rot = (lane >= NOPE) & (lane < NOPE + ROPE)
    first = lane < NOPE + ROPE // 2
    cos_ref[...] = jnp.where(rot, jnp.cos(ang), 1.0)
    s = jnp.sin(ang)
    sin_ref[...] = jnp.where(rot, jnp.where(first, -s, s), 0.0)


def _rope_tables(positions):
    n = positions.size
    inv_freq = ROPE_THETA ** (-jnp.arange(0, ROPE, 2, dtype=f32) / ROPE)
    invf = jnp.zeros((1, LANES), f32).at[0, NOPE:NOPE + ROPE].set(jnp.tile(inv_freq, 2))
    t = math.gcd(n, 2048)
    return pl.pallas_call(
        _rope_kernel,
        grid=(n // t,),
        in_specs=[pl.BlockSpec((t, 1), lambda i: (i, 0)), _const_spec((1, LANES))],
        out_specs=[pl.BlockSpec((t, LANES), lambda i: (i, 0))] * 2,
        out_shape=[jax.ShapeDtypeStruct((n, LANES), f32)] * 2,
        compiler_params=_cparams(("arbitrary",)),
        name="rope_tables",
    )(positions.reshape(n, 1), invf)


def _t5_bucket(rel):
    n = jnp.abs(rel)
    large = 8 + sum((n >= t).astype(jnp.int32) for t in (12, 16, 23, 32, 46, 64, 91))
    return jnp.where(rel > 0, 16, 0) + jnp.where(n < 8, n, large)


def _bias_kernel(tbl_ref, t0_ref, t1_ref, mk_ref):
    r = lax.broadcasted_iota(jnp.int32, (TQ, TK), 0)
    c = lax.broadcasted_iota(jnp.int32, (TQ, TK), 1)
    allowed = (c // CHUNK) <= (r // CHUNK)
    mk = jnp.where(allowed, 0.0, NEG).astype(f32)
    mk_ref[...] = mk
    b0 = _t5_bucket(c - r)
    b1 = _t5_bucket(c - r - TK)
    for h in range(HA):
        far = tbl_ref[N_REL_BUCKETS // 2 - 1, h]
        t0 = jnp.zeros((TQ, TK), f32)
        t1 = jnp.zeros((TQ, TK), f32)
        for b in range(N_REL_BUCKETS):
            v = (tbl_ref[b, h] - far) * LOG2E
            t0 = jnp.where(b0 == b, v, t0)
            t1 = jnp.where(b1 == b, v, t1)
        t0 = t0 + mk
        t0_ref[h, 0:TQ, :] = t0
        t0_ref[h, TQ:2 * TQ, :] = t0
        t1_ref[h, 0:TQ, :] = t1
        t1_ref[h, TQ:2 * TQ, :] = t1


def _bias_tiles(rel_bias):
    return pl.pallas_call(
        _bias_kernel,
        in_specs=[pl.BlockSpec(memory_space=pltpu.SMEM)],
        out_specs=[pl.BlockSpec(memory_space=pltpu.VMEM)] * 3,
        out_shape=[jax.ShapeDtypeStruct((HA, 2 * TQ, TK), f32),
                   jax.ShapeDtypeStruct((HA, 2 * TQ, TK), f32),
                   jax.ShapeDtypeStruct((TQ, TK), f32)],
        name="bias_tiles",
    )(rel_bias)


def _inproj_kernel(*refs, combine):
    if combine:
        (xa_ref, y_ref, g2_ref, sh_ref, sc_ref, ng_ref, w_ref, gq_ref, gk_ref, sd_ref, ex_ref,
         vg_ref, ws_ref, bs_ref,
         xo_ref, qa_ref, ka_ref, va_ref, ob_ref, xc_ref, gt_ref) = refs
        x = xa_ref[...] + g2_ref[0] * y_ref[...]
        xo_ref[...] = x
    else:
        (xa_ref, sh_ref, sc_ref, ng_ref, w_ref, gq_ref, gk_ref, sd_ref, ex_ref,
         vg_ref, ws_ref, bs_ref,
         qa_ref, ka_ref, va_ref, ob_ref, xc_ref, gt_ref) = refs
        x = xa_ref[...]
    h = _rms(x, ng_ref[...]) * (1.0 + sc_ref[0]) + sh_ref[0]
    hb = h.astype(bf16)

    q = jnp.dot(hb, w_ref[:, 0:MIX], preferred_element_type=f32)
    qa_ref[...] = (q * _seg_rinv(q, sd_ref, ex_ref) * gq_ref[...]).astype(bf16)
    k = jnp.dot(hb, w_ref[:, MIX:2 * MIX], preferred_element_type=f32)
    ka_ref[...] = (k * _seg_rinv(k, sd_ref, ex_ref) * gk_ref[...]).astype(bf16)
    va_ref[...] = jnp.dot(hb, w_ref[:, 2 * MIX:3 * MIX], preferred_element_type=f32).astype(bf16)

    z = jnp.dot(hb, w_ref[:, 3 * MIX:5 * MIX], preferred_element_type=f32)
    z = 0.5 * z * (1.0 + jnp.tanh(math.sqrt(2.0 / math.pi) * (z + 0.044715 * (z * z * z))))
    u = z[:, :MIX]
    v = _rms(z[:, MIX:], vg_ref[...]).astype(bf16)
    ri = lax.broadcasted_iota(jnp.int32, (LANES, LANES), 0)
    ci = lax.broadcasted_iota(jnp.int32, (LANES, LANES), 1)
    allowed = (ci // CHUNK) <= (ri // CHUNK)
    for g in range(4):
        wm = jnp.where(allowed, ws_ref[g], 0.0).astype(bf16)
        bcol = bs_ref[:, g:g + 1]
        for wd in range(TM // LANES):
            rows = slice(wd * LANES, (wd + 1) * LANES)
            cols = slice(g * LANES, (g + 1) * LANES)
            vs = jnp.dot(wm, v[rows, cols], preferred_element_type=f32) + bcol
            ob_ref[rows, cols] = (u[rows, cols] * vs).astype(bf16)

    xc_ref[...] = jnp.dot(hb, w_ref[:, XC_OFF:GATE_OFF], preferred_element_type=f32)

    for j in range(3):
        gsl = slice(GATE_OFF + j * D, GATE_OFF + (j + 1) * D)
        gl = jnp.dot(hb, w_ref[:, gsl], preferred_element_type=f32)
        gt_ref[:, j * D:(j + 1) * D] = jax.nn.sigmoid(gl).astype(bf16)


def _inproj(xa, y, g2, shift, scale, ng, w, gq, gk, sd, ex, vg, ws, bs, seq):
    n = xa.shape[0]
    spb = seq // TM
    combine = y is not None
    row = lambda w_: pl.BlockSpec((TM, w_), lambda i: (i, 0))
    modspec = pl.BlockSpec((1, 1, D), lambda i: (i // spb, 0, 0))
    in_specs = [row(D)]
    args = [xa]
    if combine:
        in_specs += [row(D), modspec]
        args += [y, g2]
    in_specs += [modspec, modspec, _const_spec((1, D)),
                 pl.BlockSpec((D, IN_PAD), lambda i: (0, 0), pipeline_mode=pl.Buffered(1)),
                 _const_spec((1, MIX)), _const_spec((1, MIX)),
                 _const_spec((MIX, LANES)), _const_spec((2 * LANES, MIX)),
                 _const_spec((1, MIX)), _const_spec((4, LANES, LANES)), _const_spec((LANES, 4))]
    args += [shift, scale, ng, w, gq, gk, sd, ex, vg, ws, bs]
    out_specs = [row(MIX)] * 4 + [row(MIX), row(3 * D)]
    out_shape = [jax.ShapeDtypeStruct((n, MIX), bf16)] * 4 + [
        jax.ShapeDtypeStruct((n, MIX), f32), jax.ShapeDtypeStruct((n, 3 * D), bf16)]
    if combine:
        out_specs = [row(D)] + out_specs
        out_shape = [jax.ShapeDtypeStruct((n, D), f32)] + out_shape
    return pl.pallas_call(
        functools.partial(_inproj_kernel, combine=combine),
        grid=(n // TM,),
        in_specs=in_specs, out_specs=out_specs, out_shape=out_shape,
        compiler_params=_cparams(("arbitrary",)),
        name="inproj",
    )(*args)


LOG2E = math.log2(math.e)


def _two_pass_attention(nunits, q_of, kcols_of, vcols_of, k_ref, v_ref, qi, diag_bias_of, sub_bias_of,
                        sbuf, mbuf, lbuf, abuf):
    def score_block(j, bias_of):
        rows_k = pl.ds(pl.multiple_of(j * TK, TK), TK)
        for u in range(nunits):
            s = lax.dot_general(q_of(u), k_ref[rows_k, kcols_of(u)], (((1,), (1,)), ((), ())),
                                preferred_element_type=f32)
            if bias_of is not None:
                s = s + bias_of(u)
            sbuf[u, j] = s
            mbuf[u] = jnp.maximum(mbuf[u], jnp.maximum(s[:, :LANES], s[:, LANES:]))

    for u in range(nunits):
        mbuf[u] = jnp.full(mbuf.shape[1:], NEG, f32)
    if sub_bias_of is None:
        nfar = qi
    else:
        nfar = jnp.maximum(qi - 1, 0)

        @pl.when(qi >= 1)
        def _():
            score_block(qi - 1, sub_bias_of)

    def far(j, _):
        score_block(j, None)
        return 0

    lax.fori_loop(0, nfar, far, 0)
    score_block(qi, diag_bias_of)

    for u in range(nunits):
        mbuf[u] = jnp.broadcast_to(jnp.max(mbuf[u], axis=-1, keepdims=True), mbuf.shape[1:])
        lbuf[u] = jnp.zeros(lbuf.shape[1:], f32)
        abuf[u] = jnp.zeros(abuf.shape[1:], f32)

    def accumulate(j, _):
        rows_k = pl.ds(pl.multiple_of(j * TK, TK), TK)
        for u in range(nunits):
            s = sbuf[u, j]
            mb = mbuf[u]
            p0 = jnp.exp2(s[:, :LANES] - mb)
            p1 = jnp.exp2(s[:, LANES:] - mb)
            lbuf[u] = lbuf[u] + (p0 + p1)
            p = jnp.concatenate([p0, p1], axis=-1).astype(bf16)
            abuf[u] = abuf[u] + jnp.dot(p, v_ref[rows_k, vcols_of(u)], preferred_element_type=f32)
        return 0

    lax.fori_loop(0, qi + 1, accumulate, 0)


def _normalised(u, lbuf, abuf):
    return abuf[u] / jnp.sum(lbuf[u], axis=-1, keepdims=True)


GA = 2
GC = 4


def _attn_a_kernel(q_ref, k_ref, v_ref, t0_ref, t1_ref, dl_ref, og_ref, o_ref,
                   qstk, sbuf, mbuf, lbuf, abuf, *, lambda_init):
    qi = pl.program_id(2)
    dl = dl_ref[...]
    lam = (jnp.exp(jnp.sum(dl[0:1] * dl[1:2], axis=-1, keepdims=True))
           - jnp.exp(jnp.sum(dl[2:3] * dl[3:4], axis=-1, keepdims=True)) + lambda_init)
    lane = lax.broadcasted_iota(jnp.int32, (TQ, LANES), 1)
    cols = lambda u: slice(u * LANES, (u + 1) * LANES)
    for u in range(GA):
        qh = q_ref[:, cols(u)]
        zero = jnp.zeros_like(qh)
        qstk[u, 0:TQ, :] = jnp.where(lane < 64, qh, zero)
        qstk[u, TQ:2 * TQ, :] = jnp.where(lane >= 64, qh, zero)
    _two_pass_attention(GA, lambda u: qstk[u], cols, cols, k_ref, v_ref, qi,
                        lambda u: t0_ref[u], lambda u: t1_ref[u], sbuf, mbuf, lbuf, abuf)
    for u in range(GA):
        o = _normalised(u, lbuf, abuf)
        oh = o[:TQ] - lam * o[TQ:]
        oh = _rms(oh, og_ref[...]) * (1.0 - lambda_init)
        o_ref[:, cols(u)] = oh.astype(bf16)


def _attn_a(qa, ka, va, t0, t1, dl, og, nb, seq, lambda_init):
    n = qa.shape[0]
    nq = seq // TQ
    w = GA * LANES
    return pl.pallas_call(
        functools.partial(_attn_a_kernel, lambda_init=lambda_init),
        grid=(nb, HA // GA, nq),
        in_specs=[pl.BlockSpec((TQ, w), lambda b, g, i: (b * nq + i, g)),
                  pl.BlockSpec((seq, w), lambda b, g, i: (b, g)),
                  pl.BlockSpec((seq, w), lambda b, g, i: (b, g)),
                  pl.BlockSpec((GA, 2 * TQ, TK), lambda b, g, i: (g, 0, 0)),
                  pl.BlockSpec((GA, 2 * TQ, TK), lambda b, g, i: (g, 0, 0)),
                  _const_spec((4, 64)), _const_spec((1, LANES))],
        out_specs=pl.BlockSpec((TQ, w), lambda b, g, i: (b * nq + i, g)),
        out_shape=jax.ShapeDtypeStruct((n, MIX), bf16),
        scratch_shapes=[pltpu.VMEM((GA, 2 * TQ, LANES), bf16),
                        pltpu.VMEM((GA, seq // TK, 2 * TQ, TK), f32),
                        pltpu.VMEM((GA, 2 * TQ, LANES), f32), pltpu.VMEM((GA, 2 * TQ, LANES), f32),
                        pltpu.VMEM((GA, 2 * TQ, LANES), f32)],
        compiler_params=_cparams(("arbitrary", "arbitrary", "arbitrary")),
        name="attn_diff",
    )(qa, ka, va, t0, t1, dl, og)


def _attn_c_kernel(q_ref, k_ref, v_ref, mk_ref, o_ref, sbuf, mbuf, lbuf, abuf):
    qi = pl.program_id(2)
    lane = lax.broadcasted_iota(jnp.int32, (TQ, LANES), 1)
    cols = lambda u: slice(u * LANES, (u + 1) * LANES)
    _two_pass_attention(GC, lambda u: q_ref[:, cols(u)], cols, lambda u: cols(u // 2), k_ref, v_ref, qi,
                        lambda u: mk_ref[...], None, sbuf, mbuf, lbuf, abuf)
    for hp in range(GC // 2):
        o_ref[:, cols(hp)] = jnp.where(lane < VC, _normalised(2 * hp, lbuf, abuf),
                                       _normalised(2 * hp + 1, lbuf, abuf)).astype(bf16)


def _attn_c(qc, kc, vc, mk, nb, seq):
    n = qc.shape[0]
    nq = seq // TQ
    w = GC * LANES
    wv = GC * VC
    return pl.pallas_call(
        _attn_c_kernel,
        grid=(nb, HC // GC, nq),
        in_specs=[pl.BlockSpec((TQ, w), lambda b, g, i: (b * nq + i, g)),
                  pl.BlockSpec((seq, w), lambda b, g, i: (b, g)),
                  pl.BlockSpec((seq, wv), lambda b, g, i: (b, g)),
                  _const_spec((TQ, TK))],
        out_specs=pl.BlockSpec((TQ, wv), lambda b, g, i: (b * nq + i, g)),
        out_shape=jax.ShapeDtypeStruct((n, MIX), bf16),
        scratch_shapes=[pltpu.VMEM((GC, seq // TK, TQ, TK), f32),
                        pltpu.VMEM((GC, TQ, LANES), f32), pltpu.VMEM((GC, TQ, LANES), f32),
                        pltpu.VMEM((GC, TQ, LANES), f32)],
        compiler_params=_cparams(("arbitrary", "arbitrary", "arbitrary")),
        name="attn_latent",
    )(qc, kc, vc, mk)


def _rope_apply(x, cosf, sinf, lane):
    w = x.shape[-1]
    partner = jnp.where((lane % LANES) < NOPE + ROPE // 2,
                        pltpu.roll(x, w - ROPE // 2, 1), pltpu.roll(x, ROPE // 2, 1))
    return x * cosf + partner * sinf


def _mla_prep_kernel(xc_ref, cos_ref, sin_ref, glq_ref, glkv_ref, wq_ref, wkv_ref, gq_ref, gk_ref, gr_ref,
                     sd_ref, ex_ref, q_ref, k_ref, v_ref):
    cq = _rms(xc_ref[:, 0:Q_LORA], glq_ref[...]).astype(bf16)
    ckv = _rms(xc_ref[:, Q_LORA:Q_LORA + KV_LORA], glkv_ref[...]).astype(bf16)
    cos8 = jnp.concatenate([cos_ref[...]] * HC, axis=-1)
    sin8 = jnp.concatenate([sin_ref[...]] * HC, axis=-1)
    lane8 = lax.broadcasted_iota(jnp.int32, (TM, HC * LANES), 1)

    q = jnp.dot(cq, wq_ref[...], preferred_element_type=f32)
    q = q * _seg_rinv(q, sd_ref, ex_ref) * gq_ref[...]
    q_ref[...] = _rope_apply(q, cos8, sin8, lane8).astype(bf16)

    kv = jnp.dot(ckv, wkv_ref[...], preferred_element_type=f32)
    v_ref[...] = kv[:, HC * LANES:].astype(bf16)
    xr = xc_ref[:, Q_LORA + KV_LORA:Q_LORA + KV_LORA + LANES]
    kr = xr * lax.rsqrt(jnp.sum(xr * xr, axis=-1, keepdims=True) * (1.0 / ROPE) + EPS) * gr_ref[...]
    kr = pltpu.roll(kr, NOPE, 1)
    lane1 = lax.broadcasted_iota(jnp.int32, (TM, LANES), 1)
    kr = _rope_apply(kr, cos_ref[...], sin_ref[...], lane1)
    for h in range(HC):
        cols = slice(h * LANES, (h + 1) * LANES)
        kn = kv[:, cols]
        kn = kn * lax.rsqrt(jnp.sum(kn * kn, axis=-1, keepdims=True) * (1.0 / NOPE) + EPS) * gk_ref[...]
        k_ref[:, cols] = (kn + kr).astype(bf16)


def _mla_prep(xc, cosf, sinf, glq, glkv, wq, wkv, gq, gk, gr, sd, ex):
    n = xc.shape[0]
    row = lambda w_: pl.BlockSpec((TM, w_), lambda i: (i, 0))
    return pl.pallas_call(
        _mla_prep_kernel,
        grid=(n // TM,),
        in_specs=[row(MIX), row(LANES), row(LANES), _const_spec((1, Q_LORA)), _const_spec((1, KV_LORA)),
                  _const_spec((Q_LORA, HC * LANES)), _const_spec((KV_LORA, HC * LANES + MIX)),
                  _const_spec((1, HC * LANES)), _const_spec((1, LANES)), _const_spec((1, LANES)),
                  _const_spec((HC * LANES, LANES)), _const_spec((2 * LANES, HC * LANES))],
        out_specs=[row(HC * LANES), row(HC * LANES), row(MIX)],
        out_shape=[jax.ShapeDtypeStruct((n, HC * LANES), bf16), jax.ShapeDtypeStruct((n, HC * LANES), bf16),
                   jax.ShapeDtypeStruct((n, MIX), bf16)],
        compiler_params=_cparams(("arbitrary",)),
        name="mla_prep",
    )(xc, cosf, sinf, glq, glkv, wq, wkv, gq, gk, gr, sd, ex)


def _merge_kernel(oa_ref, ob_ref, oc_ref, gt_ref, x_ref, g1_ref, sh_ref, sc_ref, ng_ref, wb_ref, wo_ref,
                  wr_ref, br_ref, x1_ref, h2_ref, meta_ref, cnt_ref, run_ref):
    i = pl.program_id(0)

    @pl.when(i == 0)
    def _():
        run_ref[...] = jnp.zeros_like(run_ref)

    merged = jnp.zeros((TM, D), f32)
    for j, o_ref in enumerate((oa_ref, ob_ref, oc_ref)):
        pj = jnp.dot(o_ref[...], wb_ref[j], preferred_element_type=f32)
        merged = merged + gt_ref[:, j * D:(j + 1) * D].astype(f32) * pj
    y = jnp.dot(merged.astype(bf16), wo_ref[...], preferred_element_type=f32)
    x1 = x_ref[...] + g1_ref[0] * y
    x1_ref[...] = x1
    h2 = _rms(x1, ng_ref[...]) * (1.0 + sc_ref[0]) + sh_ref[0]
    h2_ref[:, 0:D] = h2

    lg = jnp.dot(h2, wr_ref[...], preferred_element_type=f32, precision=lax.Precision.HIGHEST) + br_ref[...]
    lane = lax.broadcasted_iota(jnp.int32, (TM, LANES), 1)
    n_e = N_GROUPS * N_EXP
    gmask = (lane >= n_e) & (lane < n_e + N_GROUPS)
    gl = jnp.where(gmask, lg, NEG)
    gmax = jnp.max(gl, axis=-1, keepdims=True)
    g_w = 1.0 / jnp.sum(jnp.where(gmask, jnp.exp(gl - gmax), 0.0), axis=-1, keepdims=True)
    gidx = jnp.min(jnp.where(gl == gmax, lane - n_e, LANES), axis=-1, keepdims=True)
    emask = (lane < n_e) & ((lane // N_EXP) == gidx)
    el = jnp.where(emask, lg, NEG)
    m1 = jnp.max(el, axis=-1, keepdims=True)
    i1 = jnp.min(jnp.where(el == m1, lane, LANES), axis=-1, keepdims=True)
    el2 = jnp.where(lane == i1, NEG, el)
    m2 = jnp.max(el2, axis=-1, keepdims=True)
    i2 = jnp.min(jnp.where(el2 == m2, lane, LANES), axis=-1, keepdims=True)
    t = jnp.exp(m2 - m1)
    w1 = g_w / (1.0 + t)
    w2 = g_w * t / (1.0 + t)
    lo = jnp.minimum(i1, i2)
    hi = jnp.maximum(i1, i2)
    w_lo = jnp.where(i1 < i2, w1, w2)
    w_hi = jnp.where(i1 < i2, w2, w1)
    bucket = gidx * (N_EXP * N_EXP) + (lo % N_EXP) * N_EXP + (hi % N_EXP)

    lane2 = lax.broadcasted_iota(jnp.int32, (TM, 2 * LANES), 1)
    onehot = (lane2 == bucket).astype(f32)
    rr = lax.broadcasted_iota(jnp.int32, (TM, TM), 0)
    cc = lax.broadcasted_iota(jnp.int32, (TM, TM), 1)
    tri = (cc < rr).astype(bf16)
    before = jnp.dot(tri, onehot.astype(bf16), preferred_element_type=f32) + run_ref[...]
    rank = jnp.sum(onehot * before, axis=-1, keepdims=True)
    run_ref[...] = run_ref[...] + jnp.sum(onehot, axis=0, keepdims=True)
    cnt_ref[...] = run_ref[...]

    meta = (jnp.where(lane == i1, w1, 0.0) + jnp.where(lane == i2, w2, 0.0)
            + jnp.where(lane == 32, bucket.astype(f32), 0.0) + jnp.where(lane == 33, rank, 0.0)
            + jnp.where(lane == 34, w_lo, 0.0) + jnp.where(lane == 35, w_hi, 0.0))
    meta_ref[...] = meta
    h2_ref[:, D:D + LANES] = meta


def _merge(oa, ob, oc, gt, x, g1, shift, scale, ng, wb, wo, wr, br, seq):
    n = x.shape[0]
    spb = seq // TM
    row = lambda w_: pl.BlockSpec((TM, w_), lambda i: (i, 0))
    modspec = pl.BlockSpec((1, 1, D), lambda i: (i // spb, 0, 0))
    return pl.pallas_call(
        _merge_kernel,
        grid=(n // TM,),
        in_specs=[row(MIX), row(MIX), row(MIX), row(3 * D), row(D), modspec, modspec, modspec,
                  _const_spec((1, D)), _const_spec((3, MIX, D)), _const_spec((D, D)),
                  _const_spec((D, LANES)), _const_spec((1, LANES))],
        out_specs=[row(D), row(D + LANES), row(LANES), _const_spec((1, 2 * LANES))],
        out_shape=[jax.ShapeDtypeStruct((n, D), f32), jax.ShapeDtypeStruct((n, D + LANES), f32),
                   jax.ShapeDtypeStruct((n, LANES), f32), jax.ShapeDtypeStruct((1, 2 * LANES), f32)],
        scratch_shapes=[pltpu.VMEM((1, 2 * LANES), f32)],
        compiler_params=_cparams(("arbitrary",)),
        name="merge_route",
    )(oa, ob, oc, gt, x, g1, shift, scale, ng, wb, wo, wr, br)


TE = 128
N_BUCKETS = N_GROUPS * N_EXP * N_EXP
N_PAIRS = N_GROUPS * (N_EXP * (N_EXP - 1) // 2)


def _num_tiles(n):
    return n // TE + N_PAIRS


def _plan_kernel(bucket_ref, rank_ref, cnt_ref, tok_ref, tea_ref, teb_ref, nt_ref, base_ref, *, n):
    ntmax = _num_tiles(n)

    def per_bucket(b, carry):
        base, tile = carry
        c = cnt_ref[b]
        ntb = lax.shift_right_logical(c + (TE - 1), 7)
        base_ref[b] = base
        grp = lax.shift_right_logical(b, 6) * N_EXP
        ea = grp + (lax.shift_right_logical(b, 3) & 7)
        eb = grp + (b & 7)

        def set_tile(k, _):
            tea_ref[tile + k] = ea
            teb_ref[tile + k] = eb
            return 0

        lax.fori_loop(0, ntb, set_tile, 0)

        def set_pad(s, _):
            tok_ref[s] = -1
            return 0

        lax.fori_loop(base + c, base + ntb * TE, set_pad, 0)
        return base + ntb * TE, tile + ntb

    base, tile = lax.fori_loop(0, N_BUCKETS, per_bucket, (jnp.int32(0), jnp.int32(0)))
    nt_ref[0] = tile
    last_a = tea_ref[tile - 1]
    last_b = teb_ref[tile - 1]

    def set_tail(t, _):
        tea_ref[t] = last_a
        teb_ref[t] = last_b
        return 0

    lax.fori_loop(tile, ntmax, set_tail, 0)

    def set_pad_tail(s, _):
        tok_ref[s] = -1
        return 0

    lax.fori_loop(base, (ntmax + 1) * TE, set_pad_tail, 0)

    def place(t, _):
        tok_ref[base_ref[bucket_ref[t]] + rank_ref[t]] = t
        return 0

    lax.fori_loop(0, n, place, 0, unroll=8)


def _plan(bucket, rank, cnt):
    n = bucket.shape[0]
    ntmax = _num_tiles(n)
    smem = pl.BlockSpec(memory_space=pltpu.SMEM)
    return pl.pallas_call(
        functools.partial(_plan_kernel, n=n),
        in_specs=[smem] * 3,
        out_specs=[smem] * 4,
        out_shape=[jax.ShapeDtypeStruct(((ntmax + 1) * TE,), jnp.int32), jax.ShapeDtypeStruct((ntmax,), jnp.int32),
                   jax.ShapeDtypeStruct((ntmax,), jnp.int32), jax.ShapeDtypeStruct((1,), jnp.int32)],
        scratch_shapes=[pltpu.SMEM((N_BUCKETS,), jnp.int32)],
        name="moe_plan",
    )(bucket, rank, cnt)


def _moe_kernel(tea_ref, teb_ref, nt_ref, tok_ref, h_hbm, wg_ref, wu_ref, wd_ref, y_hbm,
                xbuf, obuf, gsem, ssem, *, n):
    i = pl.program_id(0)
    nt = nt_ref[0]
    slot = lax.rem(i, 2)

    def gather_copy(src_row, sl, r):
        return pltpu.make_async_copy(h_hbm.at[pl.ds(src_row, 1), :], xbuf.at[sl, pl.ds(r, 1), :], gsem.at[sl])

    def scatter_copy(dst_row, sl, r):
        return pltpu.make_async_copy(obuf.at[sl, pl.ds(r, 1), :], y_hbm.at[pl.ds(dst_row, 1), :], ssem.at[sl])

    def start_gathers(tile, sl):
        def body(r, _):
            gather_copy(jnp.maximum(tok_ref[tile * TE + r], 0), sl, r).start()
            return 0
        lax.fori_loop(0, TE, body, 0, unroll=8)

    def wait_gathers(sl):
        def body(r, _):
            gather_copy(0, sl, r).wait()
            return 0
        lax.fori_loop(0, TE, body, 0, unroll=8)

    def start_scatters(tile, sl):
        def body(r, _):
            t = tok_ref[tile * TE + r]
            scatter_copy(jnp.where(t < 0, n + sl * TE + r, t), sl, r).start()
            return 0
        lax.fori_loop(0, TE, body, 0, unroll=8)

    def wait_scatters(sl):
        def body(r, _):
            scatter_copy(n, sl, r).wait()
            return 0
        lax.fori_loop(0, TE, body, 0, unroll=8)

    @pl.when(i == 0)
    def _():
        start_gathers(0, 0)
        obuf[...] = jnp.zeros_like(obuf)
        for sl in range(2):
            spare = pltpu.make_async_copy(obuf.at[sl], y_hbm.at[pl.ds(n + sl * TE, TE), :], ssem.at[sl])
            spare.start()
            spare.wait()

    @pl.when(i < nt)
    def _():
        start_gathers(i + 1, 1 - slot)
        wait_gathers(slot)
        xs = xbuf[slot]
        hb = xs[:, 0:D].astype(bf16)
        meta = xs[:, D:D + LANES]
        lane = lax.broadcasted_iota(jnp.int32, (TE, LANES), 1)
        w_lo = jnp.sum(jnp.where(lane == 34, meta, 0.0), axis=-1, keepdims=True)
        w_hi = jnp.sum(jnp.where(lane == 35, meta, 0.0), axis=-1, keepdims=True)

        def ffn(e, wcol):
            a = jnp.dot(hb, wg_ref[0, e], preferred_element_type=f32)
            u = jnp.dot(hb, wu_ref[0, e], preferred_element_type=f32)
            act = (a * jax.nn.sigmoid(a)) * u * wcol
            return jnp.dot(act.astype(bf16), wd_ref[0, e], preferred_element_type=f32)

        obuf[slot] = ffn(lax.rem(tea_ref[i], N_EXP), w_lo) + ffn(lax.rem(teb_ref[i], N_EXP), w_hi)
        start_scatters(i, slot)

        @pl.when(i >= 1)
        def _():
            wait_scatters(1 - slot)

        @pl.when(i == nt - 1)
        def _():
            wait_gathers(1 - slot)
            wait_scatters(slot)


def _moe(h2ext, tok, tea, teb, nt, wg, wu, wd):
    n = h2ext.shape[0]
    ntmax = _num_tiles(n)
    wspec_in = pl.BlockSpec((1, N_EXP, D, F_EXP), lambda i, tea, teb, nt, tok: (tea[i] // N_EXP, 0, 0, 0))
    wspec_out = pl.BlockSpec((1, N_EXP, F_EXP, D), lambda i, tea, teb, nt, tok: (tea[i] // N_EXP, 0, 0, 0))
    return pl.pallas_call(
        functools.partial(_moe_kernel, n=n),
        grid_spec=pltpu.PrefetchScalarGridSpec(
            num_scalar_prefetch=4,
            grid=(ntmax,),
            in_specs=[pl.BlockSpec(memory_space=pl.ANY), wspec_in, wspec_in, wspec_out],
            out_specs=pl.BlockSpec(memory_space=pl.ANY),
            scratch_shapes=[pltpu.VMEM((2, TE, D + LANES), f32), pltpu.VMEM((2, TE, D), f32),
                            pltpu.SemaphoreType.DMA((2,)), pltpu.SemaphoreType.DMA((2,))]),
        out_shape=jax.ShapeDtypeStruct((n + 2 * TE, D), f32),
        compiler_params=_cparams(("arbitrary",)),
        name="moe_sparse",
    )(tea, teb, nt, tok, h2ext, wg, wu, wd)


def _final_kernel(x_ref, y_ref, g_ref, o_ref):
    o_ref[...] = x_ref[...] + g_ref[0] * y_ref[...]


def _final_combine(x1, y, g2, seq):
    n = x1.shape[0]
    t = 512
    spb = seq // t
    row = pl.BlockSpec((t, D), lambda i: (i, 0))
    return pl.pallas_call(
        _final_kernel,
        grid=(n // t,),
        in_specs=[row, row, pl.BlockSpec((1, 1, D), lambda i: (i // spb, 0, 0))],
        out_specs=row,
        out_shape=jax.ShapeDtypeStruct((n, D), f32),
        compiler_params=_cparams(("arbitrary",)),
        name="final_combine",
    )(x1, y, g2)


def _seg_matrices(width, segs):
    sd = np.zeros((width, LANES), np.float32)
    ex = np.zeros((LANES, width), np.float32)
    for j, (s, ln) in enumerate(segs):
        sd[s:s + ln, j] = 1.0 / ln
        ex[j, s:s + ln] = 1.0
    return jnp.asarray(sd, bf16), jnp.asarray(np.concatenate([ex, ex], axis=0), bf16)


def _head_pad(w, heads, per_head, keep):
    k = w.shape[0]
    w3 = w.reshape(k, heads, per_head)[:, :, :keep]
    return jnp.pad(w3, ((0, 0), (0, 0), (0, LANES - keep))).reshape(k, heads * LANES)


def kernel(x, c, positions, w_ada, b_ada, norm_g, w_in, diff_qk_g, diff_lambda, diff_out_g, rel_bias, sgu_v_g, sgu_w, sgu_b, mla_lat_g, mla_w_uq, mla_w_ukv, mla_qk_g, w_branch, w_out, router_g_w, router_g_b, router_e_w, router_e_b, w_e_gate, w_e_up, w_e_down):
    nb, seq, _ = x.shape
    n = nb * seq
    assert seq % TQ == 0 and seq % TM == 0 and x.shape[2] == D

    mod = _ada(c, w_ada, b_ada)
    cosf, sinf = _rope_tables(positions)
    t0, t1, mk = _bias_tiles(rel_bias)

    sd_a, ex_a = _seg_matrices(MIX, [(s * 64, 64) for s in range(8)])
    segs_q = []
    for h in range(HC):
        segs_q += [(h * LANES, NOPE), (h * LANES + NOPE, ROPE)]
    sd_q, ex_q = _seg_matrices(HC * LANES, segs_q)
    ne = N_GROUPS * N_EXP
    zpad = lambda a, w_: jnp.pad(a, ((0, 0), (0, w_ - a.shape[1])))

    xcur = x.reshape(n, D)
    y_prev, g2_prev = None, None
    for l in range(DEPTH):
        m3 = mod[l].reshape(nb, 1, 6 * D)
        shift1, scale1, gate1, shift2, scale2, gate2 = [m3[:, :, k * D:(k + 1) * D] for k in range(6)]
        lambda_init = LAMBDA_INIT_BASE - LAMBDA_INIT_SCALE * math.exp(-LAMBDA_INIT_DECAY * l)

        w = w_in[l]
        w_pad = jnp.concatenate([w[:, :XC_OFF + 416], jnp.zeros((D, IN_PAD - IN_REAL), f32), w[:, XC_OFF + 416:]],
                                axis=1).astype(bf16)
        gq = (jnp.tile(diff_qk_g[l, 0], 8) * (64 ** -0.5 * LOG2E)).reshape(1, MIX)
        gk = jnp.tile(diff_qk_g[l, 1], 8).reshape(1, MIX)
        outs = _inproj(xcur, y_prev, g2_prev, shift1, scale1, norm_g[l, 0].reshape(1, D), w_pad, gq, gk,
                       sd_a, ex_a, sgu_v_g[l].reshape(1, MIX), sgu_w[l], jnp.transpose(sgu_b[l]), seq)
        if y_prev is not None:
            xcur = outs[0]
            outs = outs[1:]
        qa, ka, va, ob, xc, gt = outs

        oa = _attn_a(qa, ka, va, t0, t1, diff_lambda[l], diff_out_g[l].reshape(1, LANES), nb, seq, lambda_init)

        qkg = mla_qk_g[l]
        wq = _head_pad(mla_w_uq[l], HC, NOPE + ROPE, NOPE + ROPE).astype(bf16)
        wkv3 = mla_w_ukv[l].reshape(KV_LORA, HC, NOPE + VC)
        wk = jnp.pad(wkv3[:, :, :NOPE], ((0, 0), (0, 0), (0, LANES - NOPE))).reshape(KV_LORA, HC * LANES)
        wv = wkv3[:, :, NOPE:].reshape(KV_LORA, HC * VC)
        wkv = jnp.concatenate([wk, wv], axis=1).astype(bf16)
        gq_c = jnp.tile(jnp.pad(qkg[0], (0, LANES - NOPE - ROPE)), HC).reshape(1, HC * LANES) * ((NOPE + ROPE) ** -0.5 * LOG2E)
        gk_c = jnp.pad(qkg[1, :NOPE], (0, LANES - NOPE)).reshape(1, LANES)
        gr_c = jnp.pad(qkg[1, NOPE:], (0, LANES - ROPE)).reshape(1, LANES)
        qc, kc, vc = _mla_prep(xc, cosf, sinf, mla_lat_g[l, :Q_LORA].reshape(1, Q_LORA),
                               mla_lat_g[l, Q_LORA:].reshape(1, KV_LORA), wq, wkv, gq_c, gk_c, gr_c, sd_q, ex_q)
        oc = _attn_c(qc, kc, vc, mk, nb, seq)

        wr = zpad(jnp.concatenate([router_e_w[l], router_g_w[l]], axis=1), LANES)
        br = zpad(jnp.concatenate([router_e_b[l], router_g_b[l]]).reshape(1, ne + N_GROUPS), LANES)
        x1, h2ext, meta, cnt = _merge(oa, ob, oc, gt, xcur, gate1, shift2, scale2, norm_g[l, 1].reshape(1, D),
                                      w_branch[l].astype(bf16), w_out[l].astype(bf16), wr, br, seq)

        tok, tea, teb, nt = _plan(meta[:, 32].astype(jnp.int32), meta[:, 33].astype(jnp.int32),
                                  cnt[0].astype(jnp.int32))
        y_prev = _moe(h2ext, tok, tea, teb, nt, w_e_gate[l].astype(bf16), w_e_up[l].astype(bf16),
                      w_e_down[l].astype(bf16))
        g2_prev = gate2
        xcur = x1

    out = _final_combine(xcur, y_prev, g2_prev, seq)
    return out.reshape(nb, seq, D)
```

```python
import functools
import math

import jax
import jax.numpy as jnp
import numpy as np
from jax import lax
from jax.experimental import pallas as pl
from jax.experimental.pallas import tpu as pltpu

f32 = jnp.float32
bf16 = jnp.bfloat16

D = 1024
DEPTH = 4
CHUNK = 64
MIX = 512
HA = 4
HC = 8
Q_LORA, KV_LORA, ROPE = 256, 128, 32
NOPE, VC = 64, 64
N_GROUPS, N_EXP, F_EXP = 4, 8, 256
N_REL_BUCKETS = 32
ROPE_THETA = 10000.0
LAMBDA_INIT_BASE, LAMBDA_INIT_SCALE, LAMBDA_INIT_DECAY = 0.8, 0.6, 0.3
EPS = 1e-6
NEG = -1e30

IN_REAL = 6048
IN_PAD = 6144
XC_OFF = 2560
GATE_OFF = 3072

TM = 256
TQ = 256
TK = 256
LANES = 128
VMEM_LIMIT = 56 * 1024 * 1024


def _cparams(sem, vmem=VMEM_LIMIT):
    return pltpu.CompilerParams(dimension_semantics=sem, vmem_limit_bytes=vmem)


def _const_spec(shape):
    nd = len(shape)
    return pl.BlockSpec(shape, lambda *_: (0,) * nd)


def _rms(x, g_row):
    return x * lax.rsqrt(jnp.mean(x * x, axis=-1, keepdims=True) + EPS) * g_row


def _seg_rinv(x, sd_ref, ex_ref):
    ss = jnp.dot((x * x).astype(bf16), sd_ref[...], preferred_element_type=f32)
    hi = ss.astype(bf16)
    lo = (ss - hi.astype(f32)).astype(bf16)
    full = jnp.dot(jnp.concatenate([hi, lo], axis=-1), ex_ref[...], preferred_element_type=f32)
    return lax.rsqrt(full + EPS)


def _ada_kernel(c_ref, w_ref, b_ref, o_ref):
    c = c_ref[...]
    ca = (c * jax.nn.sigmoid(c)).astype(bf16)
    o_ref[0] = jnp.dot(ca, w_ref[0].astype(bf16), preferred_element_type=f32) + b_ref[0]


def _ada(c, w_ada, b_ada):
    nb = c.shape[0]
    tn = 1024
    return pl.pallas_call(
        _ada_kernel,
        grid=(DEPTH, 6 * D // tn),
        in_specs=[pl.BlockSpec((nb, D), lambda l, j: (0, 0)),
                  pl.BlockSpec((1, D, tn), lambda l, j: (l, 0, j)),
                  pl.BlockSpec((1, 1, tn), lambda l, j: (l, 0, j))],
        out_specs=pl.BlockSpec((1, nb, tn), lambda l, j: (l, 0, j)),
        out_shape=jax.ShapeDtypeStruct((DEPTH, nb, 6 * D), f32),
        compiler_params=_cparams(("arbitrary", "arbitrary")),
        name="ada_mod",
    )(c, w_ada, b_ada.reshape(DEPTH, 1, 6 * D))


def _rope_kernel(pos_ref, invf_ref, cos_ref, sin_ref):
    ang = pos_ref[...].astype(f32) * invf_ref[...]
    lane = lax.broadcasted_iota(jnp.int32, ang.shape, 1)
    rot = (lane >= NOPE) & (lane < NOPE + ROPE)
    first = lane < NOPE + ROPE // 2
    cos_ref[...] = jnp.where(rot, jnp.cos(ang), 1.0)
    s = jnp.sin(ang)
    sin_ref[...] = jnp.where(rot, jnp.where(first, -s, s), 0.0)


def _rope_tables(positions):
    n = positions.size
    inv_freq = ROPE_THETA ** (-jnp.arange(0, ROPE, 2, dtype=f32) / ROPE)
    invf = jnp.zeros((1, LANES), f32).at[0, NOPE:NOPE + ROPE].set(jnp.tile(inv_freq, 2))
    t = math.gcd(n, 2048)
    return pl.pallas_call(
        _rope_kernel,
        grid=(n // t,),
        in_specs=[pl.BlockSpec((t, 1), lambda i: (i, 0)), _const_spec((1, LANES))],
        out_specs=[pl.BlockSpec((t, LANES), lambda i: (i, 0))] * 2,
        out_shape=[jax.ShapeDtypeStruct((n, LANES), f32)] * 2,
        compiler_params=_cparams(("arbitrary",)),
        name="rope_tables",
    )(positions.reshape(n, 1), invf)


def _t5_bucket(rel):
    n = jnp.abs(rel)
    large = 8 + sum((n >= t).astype(jnp.int32) for t in (12, 16, 23, 32, 46, 64, 91))
    return jnp.where(rel > 0, 16, 0) + jnp.where(n < 8, n, large)


def _bias_kernel(tbl_ref, t0_ref, t1_ref, mk_ref):
    r = lax.broadcasted_iota(jnp.int32, (TQ, TK), 0)
    c = lax.broadcasted_iota(jnp.int32, (TQ, TK), 1)
    allowed = (c // CHUNK) <= (r // CHUNK)
    mk = jnp.where(allowed, 0.0, NEG).astype(f32)
    mk_ref[...] = mk
    b0 = _t5_bucket(c - r)
    b1 = _t5_bucket(c - r - TK)
    for h in range(HA):
        far = tbl_ref[N_REL_BUCKETS // 2 - 1, h]
        t0 = jnp.zeros((TQ, TK), f32)
        t1 = jnp.zeros((TQ, TK), f32)
        for b in range(N_REL_BUCKETS):
            v = (tbl_ref[b, h] - far) * LOG2E
            t0 = jnp.where(b0 == b, v, t0)
            t1 = jnp.where(b1 == b, v, t1)
        t0 = t0 + mk
        t0_ref[h, 0:TQ, :] = t0
        t0_ref[h, TQ:2 * TQ, :] = t0
        t1_ref[h, 0:TQ, :] = t1
        t1_ref[h, TQ:2 * TQ, :] = t1


def _bias_tiles(rel_bias):
    return pl.pallas_call(
        _bias_kernel,
        in_specs=[pl.BlockSpec(memory_space=pltpu.SMEM)],
        out_specs=[pl.BlockSpec(memory_space=pltpu.VMEM)] * 3,
        out_shape=[jax.ShapeDtypeStruct((HA, 2 * TQ, TK), f32),
                   jax.ShapeDtypeStruct((HA, 2 * TQ, TK), f32),
                   jax.ShapeDtypeStruct((TQ, TK), f32)],
        name="bias_tiles",
    )(rel_bias)


def _inproj_kernel(*refs, combine):
    if combine:
        (xa_ref, y_ref, g2_ref, sh_ref, sc_ref, ng_ref, w_ref, gq_ref, gk_ref, sd_ref, ex_ref,
         vg_ref, ws_ref, bs_ref,
         xo_ref, qa_ref, ka_ref, va_ref, ob_ref, xc_ref, gt_ref) = refs
        x = xa_ref[...] + g2_ref[0] * y_ref[...]
        xo_ref[...] = x
    else:
        (xa_ref, sh_ref, sc_ref, ng_ref, w_ref, gq_ref, gk_ref, sd_ref, ex_ref,
         vg_ref, ws_ref, bs_ref,
         qa_ref, ka_ref, va_ref, ob_ref, xc_ref, gt_ref) = refs
        x = xa_ref[...]
    h = _rms(x, ng_ref[...]) * (1.0 + sc_ref[0]) + sh_ref[0]
    hb = h.astype(bf16)

    q = jnp.dot(hb, w_ref[:, 0:MIX], preferred_element_type=f32)
    qa_ref[...] = (q * _seg_rinv(q, sd_ref, ex_ref) * gq_ref[...]).astype(bf16)
    k = jnp.dot(hb, w_ref[:, MIX:2 * MIX], preferred_element_type=f32)
    ka_ref[...] = (k * _seg_rinv(k, sd_ref, ex_ref) * gk_ref[...]).astype(bf16)
    va_ref[...] = jnp.dot(hb, w_ref[:, 2 * MIX:3 * MIX], preferred_element_type=f32).astype(bf16)

    z = jnp.dot(hb, w_ref[:, 3 * MIX:5 * MIX], preferred_element_type=f32)
    z = 0.5 * z * (1.0 + jnp.tanh(math.sqrt(2.0 / math.pi) * (z + 0.044715 * (z * z * z))))
    u = z[:, :MIX]
    v = _rms(z[:, MIX:], vg_ref[...]).astype(bf16)
    ri = lax.broadcasted_iota(jnp.int32, (LANES, LANES), 0)
    ci = lax.broadcasted_iota(jnp.int32, (LANES, LANES), 1)
    allowed = (ci // CHUNK) <= (ri // CHUNK)
    for g in range(4):
        wm = jnp.where(allowed, ws_ref[g], 0.0).astype(bf16)
        bcol = bs_ref[:, g:g + 1]
        for wd in range(TM // LANES):
            rows = slice(wd * LANES, (wd + 1) * LANES)
            cols = slice(g * LANES, (g + 1) * LANES)
            vs = jnp.dot(wm, v[rows, cols], preferred_element_type=f32) + bcol
            ob_ref[rows, cols] = (u[rows, cols] * vs).astype(bf16)

    xc_ref[...] = jnp.dot(hb, w_ref[:, XC_OFF:GATE_OFF], preferred_element_type=f32)

    for j in range(3):
        gsl = slice(GATE_OFF + j * D, GATE_OFF + (j + 1) * D)
        gl = jnp.dot(hb, w_ref[:, gsl], preferred_element_type=f32)
        gt_ref[:, j * D:(j + 1) * D] = jax.nn.sigmoid(gl).astype(bf16)


def _inproj(xa, y, g2, shift, scale, ng, w, gq, gk, sd, ex, vg, ws, bs, seq):
    n = xa.shape[0]
    spb = seq // TM
    combine = y is not None
    row = lambda w_: pl.BlockSpec((TM, w_), lambda i: (i, 0))
    modspec = pl.BlockSpec((1, 1, D), lambda i: (i // spb, 0, 0))
    in_specs = [row(D)]
    args = [xa]
    if combine:
        in_specs += [row(D), modspec]
        args += [y, g2]
    in_specs += [modspec, modspec, _const_spec((1, D)),
                 pl.BlockSpec((D, IN_PAD), lambda i: (0, 0), pipeline_mode=pl.Buffered(1)),
                 _const_spec((1, MIX)), _const_spec((1, MIX)),
                 _const_spec((MIX, LANES)), _const_spec((2 * LANES, MIX)),
                 _const_spec((1, MIX)), _const_spec((4, LANES, LANES)), _const_spec((LANES, 4))]
    args += [shift, scale, ng, w, gq, gk, sd, ex, vg, ws, bs]
    out_specs = [row(MIX)] * 4 + [row(MIX), row(3 * D)]
    out_shape = [jax.ShapeDtypeStruct((n, MIX), bf16)] * 4 + [
        jax.ShapeDtypeStruct((n, MIX), f32), jax.ShapeDtypeStruct((n, 3 * D), bf16)]
    if combine:
        out_specs = [row(D)] + out_specs
        out_shape = [jax.ShapeDtypeStruct((n, D), f32)] + out_shape
    return pl.pallas_call(
        functools.partial(_inproj_kernel, combine=combine),
        grid=(n // TM,),
        in_specs=in_specs, out_specs=out_specs, out_shape=out_shape,
        compiler_params=_cparams(("arbitrary",)),
        name="inproj",
    )(*args)


LOG2E = math.log2(math.e)


def _two_pass_attention(nunits, q_of, kcols_of, vcols_of, k_ref, v_ref, qi, diag_bias_of, sub_bias_of,
                        sbuf, mbuf, lbuf, abuf):
    def score_block(j, bias_of):
        rows_k = pl.ds(pl.multiple_of(j * TK, TK), TK)
        for u in range(nunits):
            s = lax.dot_general(q_of(u), k_ref[rows_k, kcols_of(u)], (((1,), (1,)), ((), ())),
                                preferred_element_type=f32)
            if bias_of is not None:
                s = s + bias_of(u)
            sbuf[u, j] = s
            mbuf[u] = jnp.maximum(mbuf[u], jnp.maximum(s[:, :LANES], s[:, LANES:]))

    for u in range(nunits):
        mbuf[u] = jnp.full(mbuf.shape[1:], NEG, f32)
    if sub_bias_of is None:
        nfar = qi
    else:
        nfar = jnp.maximum(qi - 1, 0)

        @pl.when(qi >= 1)
        def _():
            score_block(qi - 1, sub_bias_of)

    def far(j, _):
        score_block(j, None)
        return 0

    lax.fori_loop(0, nfar, far, 0)
    score_block(qi, diag_bias_of)

    for u in range(nunits):
        mbuf[u] = jnp.broadcast_to(jnp.max(mbuf[u], axis=-1, keepdims=True), mbuf.shape[1:])
        lbuf[u] = jnp.zeros(lbuf.shape[1:], f32)
        abuf[u] = jnp.zeros(abuf.shape[1:], f32)

    def accumulate(j, _):
        rows_k = pl.ds(pl.multiple_of(j * TK, TK), TK)
        for u in range(nunits):
            s = sbuf[u, j]
            mb = mbuf[u]
            p0 = jnp.exp2(s[:, :LANES] - mb)
            p1 = jnp.exp2(s[:, LANES:] - mb)
            lbuf[u] = lbuf[u] + (p0 + p1)
            p = jnp.concatenate([p0, p1], axis=-1).astype(bf16)
            abuf[u] = abuf[u] + jnp.dot(p, v_ref[rows_k, vcols_of(u)], preferred_element_type=f32)
        return 0

    lax.fori_loop(0, qi + 1, accumulate, 0)


def _normalised(u, lbuf, abuf):
    return abuf[u] / jnp.sum(lbuf[u], axis=-1, keepdims=True)


GA = 2
GC = 4


def _attn_a_kernel(q_ref, k_ref, v_ref, t0_ref, t1_ref, dl_ref, og_ref, o_ref,
                   qstk, sbuf, mbuf, lbuf, abuf, *, lambda_init):
    qi = pl.program_id(2)
    dl = dl_ref[...]
    lam = (jnp.exp(jnp.sum(dl[0:1] * dl[1:2], axis=-1, keepdims=True))
           - jnp.exp(jnp.sum(dl[2:3] * dl[3:4], axis=-1, keepdims=True)) + lambda_init)
    lane = lax.broadcasted_iota(jnp.int32, (TQ, LANES), 1)
    cols = lambda u: slice(u * LANES, (u + 1) * LANES)
    for u in range(GA):
        qh = q_ref[:, cols(u)]
        zero = jnp.zeros_like(qh)
        qstk[u, 0:TQ, :] = jnp.where(lane < 64, qh, zero)
        qstk[u, TQ:2 * TQ, :] = jnp.where(lane >= 64, qh, zero)
    _two_pass_attention(GA, lambda u: qstk[u], cols, cols, k_ref, v_ref, qi,
                        lambda u: t0_ref[u], lambda u: t1_ref[u], sbuf, mbuf, lbuf, abuf)
    for u in range(GA):
        o = _normalised(u, lbuf, abuf)
        oh = o[:TQ] - lam * o[TQ:]
        oh = _rms(oh, og_ref[...]) * (1.0 - lambda_init)
        o_ref[:, cols(u)] = oh.astype(bf16)


def _attn_a(qa, ka, va, t0, t1, dl, og, nb, seq, lambda_init):
    n = qa.shape[0]
    nq = seq // TQ
    w = GA * LANES
    return pl.pallas_call(
        functools.partial(_attn_a_kernel, lambda_init=lambda_init),
        grid=(nb, HA // GA, nq),
        in_specs=[pl.BlockSpec((TQ, w), lambda b, g, i: (b * nq + i, g)),
                  pl.BlockSpec((seq, w), lambda b, g, i: (b, g)),
                  pl.BlockSpec((seq, w), lambda b, g, i: (b, g)),
                  pl.BlockSpec((GA, 2 * TQ, TK), lambda b, g, i: (g, 0, 0)),
                  pl.BlockSpec((GA, 2 * TQ, TK), lambda b, g, i: (g, 0, 0)),
                  _const_spec((4, 64)), _const_spec((1, LANES))],
        out_specs=pl.BlockSpec((TQ, w), lambda b, g, i: (b * nq + i, g)),
        out_shape=jax.ShapeDtypeStruct((n, MIX), bf16),
        scratch_shapes=[pltpu.VMEM((GA, 2 * TQ, LANES), bf16),
                        pltpu.VMEM((GA, seq // TK, 2 * TQ, TK), f32),
                        pltpu.VMEM((GA, 2 * TQ, LANES), f32), pltpu.VMEM((GA, 2 * TQ, LANES), f32),
                        pltpu.VMEM((GA, 2 * TQ, LANES), f32)],
        compiler_params=_cparams(("arbitrary", "arbitrary", "arbitrary")),
        name="attn_diff",
    )(qa, ka, va, t0, t1, dl, og)


def _attn_c_kernel(q_ref, k_ref, v_ref, mk_ref, o_ref, sbuf, mbuf, lbuf, abuf):
    qi = pl.program_id(2)
    lane = lax.broadcasted_iota(jnp.int32, (TQ, LANES), 1)
    cols = lambda u: slice(u * LANES, (u + 1) * LANES)
    _two_pass_attention(GC, lambda u: q_ref[:, cols(u)], cols, lambda u: cols(u // 2), k_ref, v_ref, qi,
                        lambda u: mk_ref[...], None, sbuf, mbuf, lbuf, abuf)
    for hp in range(GC // 2):
        o_ref[:, cols(hp)] = jnp.where(lane < VC, _normalised(2 * hp, lbuf, abuf),
                                       _normalised(2 * hp + 1, lbuf, abuf)).astype(bf16)


def _attn_c(qc, kc, vc, mk, nb, seq):
    n = qc.shape[0]
    nq = seq // TQ
    w = GC * LANES
    wv = GC * VC
    return pl.pallas_call(
        _attn_c_kernel,
        grid=(nb, HC // GC, nq),
        in_specs=[pl.BlockSpec((TQ, w), lambda b, g, i: (b * nq + i, g)),
                  pl.BlockSpec((seq, w), lambda b, g, i: (b, g)),
                  pl.BlockSpec((seq, wv), lambda b, g, i: (b, g)),
                  _const_spec((TQ, TK))],
        out_specs=pl.BlockSpec((TQ, wv), lambda b, g, i: (b * nq + i, g)),
        out_shape=jax.ShapeDtypeStruct((n, MIX), bf16),
        scratch_shapes=[pltpu.VMEM((GC, seq // TK, TQ, TK), f32),
                        pltpu.VMEM((GC, TQ, LANES), f32), pltpu.VMEM((GC, TQ, LANES), f32),
                        pltpu.VMEM((GC, TQ, LANES), f32)],
        compiler_params=_cparams(("arbitrary", "arbitrary", "arbitrary")),
        name="attn_latent",
    )(qc, kc, vc, mk)


def _rope_apply(x, cosf, sinf, lane):
    w = x.shape[-1]
    partner = jnp.where((lane % LANES) < NOPE + ROPE // 2,
                        pltpu.roll(x, w - ROPE // 2, 1), pltpu.roll(x, ROPE // 2, 1))
    return x * cosf + partner * sinf


def _mla_prep_kernel(xc_ref, cos_ref, sin_ref, glq_ref, glkv_ref, wq_ref, wkv_ref, gq_ref, gk_ref, gr_ref,
                     sd_ref, ex_ref, q_ref, k_ref, v_ref):
    cq = _rms(xc_ref[:, 0:Q_LORA], glq_ref[...]).astype(bf16)
    ckv = _rms(xc_ref[:, Q_LORA:Q_LORA + KV_LORA], glkv_ref[...]).astype(bf16)
    cos8 = jnp.concatenate([cos_ref[...]] * HC, axis=-1)
    sin8 = jnp.concatenate([sin_ref[...]] * HC, axis=-1)
    lane8 = lax.broadcasted_iota(jnp.int32, (TM, HC * LANES), 1)

    q = jnp.dot(cq, wq_ref[...], preferred_element_type=f32)
    q = q * _seg_rinv(q, sd_ref, ex_ref) * gq_ref[...]
    q_ref[...] = _rope_apply(q, cos8, sin8, lane8).astype(bf16)

    kv = jnp.dot(ckv, wkv_ref[...], preferred_element_type=f32)
    v_ref[...] = kv[:, HC * LANES:].astype(bf16)
    xr = xc_ref[:, Q_LORA + KV_LORA:Q_LORA + KV_LORA + LANES]
    kr = xr * lax.rsqrt(jnp.sum(xr * xr, axis=-1, keepdims=True) * (1.0 / ROPE) + EPS) * gr_ref[...]
    kr = pltpu.roll(kr, NOPE, 1)
    lane1 = lax.broadcasted_iota(jnp.int32, (TM, LANES), 1)
    kr = _rope_apply(kr, cos_ref[...], sin_ref[...], lane1)
    for h in range(HC):
        cols = slice(h * LANES, (h + 1) * LANES)
        kn = kv[:, cols]
        kn = kn * lax.rsqrt(jnp.sum(kn * kn, axis=-1, keepdims=True) * (1.0 / NOPE) + EPS) * gk_ref[...]
        k_ref[:, cols] = (kn + kr).astype(bf16)


def _mla_prep(xc, cosf, sinf, glq, glkv, wq, wkv, gq, gk, gr, sd, ex):
    n = xc.shape[0]
    row = lambda w_: pl.BlockSpec((TM, w_), lambda i: (i, 0))
    return pl.pallas_call(
        _mla_prep_kernel,
        grid=(n // TM,),
        in_specs=[row(MIX), row(LANES), row(LANES), _const_spec((1, Q_LORA)), _const_spec((1, KV_LORA)),
                  _const_spec((Q_LORA, HC * LANES)), _const_spec((KV_LORA, HC * LANES + MIX)),
                  _const_spec((1, HC * LANES)), _const_spec((1, LANES)), _const_spec((1, LANES)),
                  _const_spec((HC * LANES, LANES)), _const_spec((2 * LANES, HC * LANES))],
        out_specs=[row(HC * LANES), row(HC * LANES), row(MIX)],
        out_shape=[jax.ShapeDtypeStruct((n, HC * LANES), bf16), jax.ShapeDtypeStruct((n, HC * LANES), bf16),
                   jax.ShapeDtypeStruct((n, MIX), bf16)],
        compiler_params=_cparams(("arbitrary",)),
        name="mla_prep",
    )(xc, cosf, sinf, glq, glkv, wq, wkv, gq, gk, gr, sd, ex)


def _merge_kernel(oa_ref, ob_ref, oc_ref, gt_ref, x_ref, g1_ref, sh_ref, sc_ref, ng_ref, wb_ref, wo_ref,
                  wr_ref, br_ref, x1_ref, h2_ref, meta_ref, cnt_ref, run_ref):
    i = pl.program_id(0)

    @pl.when(i == 0)
    def _():
        run_ref[...] = jnp.zeros_like(run_ref)

    merged = jnp.zeros((TM, D), f32)
    for j, o_ref in enumerate((oa_ref, ob_ref, oc_ref)):
        pj = jnp.dot(o_ref[...], wb_ref[j], preferred_element_type=f32)
        merged = merged + gt_ref[:, j * D:(j + 1) * D].astype(f32) * pj
    y = jnp.dot(merged.astype(bf16), wo_ref[...], preferred_element_type=f32)
    x1 = x_ref[...] + g1_ref[0] * y
    x1_ref[...] = x1
    h2 = _rms(x1, ng_ref[...]) * (1.0 + sc_ref[0]) + sh_ref[0]
    h2_ref[:, 0:D] = h2

    lg = jnp.dot(h2, wr_ref[...], preferred_element_type=f32, precision=lax.Precision.HIGHEST) + br_ref[...]
    lane = lax.broadcasted_iota(jnp.int32, (TM, LANES), 1)
    n_e = N_GROUPS * N_EXP
    gmask = (lane >= n_e) & (lane < n_e + N_GROUPS)
    gl = jnp.where(gmask, lg, NEG)
    gmax = jnp.max(gl, axis=-1, keepdims=True)
    g_w = 1.0 / jnp.sum(jnp.where(gmask, jnp.exp(gl - gmax), 0.0), axis=-1, keepdims=True)
    gidx = jnp.min(jnp.where(gl == gmax, lane - n_e, LANES), axis=-1, keepdims=True)
    emask = (lane < n_e) & ((lane // N_EXP) == gidx)
    el = jnp.where(emask, lg, NEG)
    m1 = jnp.max(el, axis=-1, keepdims=True)
    i1 = jnp.min(jnp.where(el == m1, lane, LANES), axis=-1, keepdims=True)
    el2 = jnp.where(lane == i1, NEG, el)
    m2 = jnp.max(el2, axis=-1, keepdims=True)
    i2 = jnp.min(jnp.where(el2 == m2, lane, LANES), axis=-1, keepdims=True)
    t = jnp.exp(m2 - m1)
    w1 = g_w / (1.0 + t)
    w2 = g_w * t / (1.0 + t)
    lo = jnp.minimum(i1, i2)
    hi = jnp.maximum(i1, i2)
    w_lo = jnp.where(i1 < i2, w1, w2)
    w_hi = jnp.where(i1 < i2, w2, w1)
    bucket = gidx * (N_EXP * N_EXP) + (lo % N_EXP) * N_EXP + (hi % N_EXP)

    lane2 = lax.broadcasted_iota(jnp.int32, (TM, 2 * LANES), 1)
    onehot = (lane2 == bucket).astype(f32)
    rr = lax.broadcasted_iota(jnp.int32, (TM, TM), 0)
    cc = lax.broadcasted_iota(jnp.int32, (TM, TM), 1)
    tri = (cc < rr).astype(bf16)
    before = jnp.dot(tri, onehot.astype(bf16), preferred_element_type=f32) + run_ref[...]
    rank = jnp.sum(onehot * before, axis=-1, keepdims=True)
    run_ref[...] = run_ref[...] + jnp.sum(onehot, axis=0, keepdims=True)
    cnt_ref[...] = run_ref[...]

    meta = (jnp.where(lane == i1, w1, 0.0) + jnp.where(lane == i2, w2, 0.0)
            + jnp.where(lane == 32, bucket.astype(f32), 0.0) + jnp.where(lane == 33, rank, 0.0)
            + jnp.where(lane == 34, w_lo, 0.0) + jnp.where(lane == 35, w_hi, 0.0))
    meta_ref[...] = meta
    h2_ref[:, D:D + LANES] = meta


def _merge(oa, ob, oc, gt, x, g1, shift, scale, ng, wb, wo, wr, br, seq):
    n = x.shape[0]
    spb = seq // TM
    row = lambda w_: pl.BlockSpec((TM, w_), lambda i: (i, 0))
    modspec = pl.BlockSpec((1, 1, D), lambda i: (i // spb, 0, 0))
    return pl.pallas_call(
        _merge_kernel,
        grid=(n // TM,),
        in_specs=[row(MIX), row(MIX), row(MIX), row(3 * D), row(D), modspec, modspec, modspec,
                  _const_spec((1, D)), _const_spec((3, MIX, D)), _const_spec((D, D)),
                  _const_spec((D, LANES)), _const_spec((1, LANES))],
        out_specs=[row(D), row(D + LANES), row(LANES), _const_spec((1, 2 * LANES))],
        out_shape=[jax.ShapeDtypeStruct((n, D), f32), jax.ShapeDtypeStruct((n, D + LANES), f32),
                   jax.ShapeDtypeStruct((n, LANES), f32), jax.ShapeDtypeStruct((1, 2 * LANES), f32)],
        scratch_shapes=[pltpu.VMEM((1, 2 * LANES), f32)],
        compiler_params=_cparams(("arbitrary",)),
        name="merge_route",
    )(oa, ob, oc, gt, x, g1, shift, scale, ng, wb, wo, wr, br)


TE = 128
N_BUCKETS = N_GROUPS * N_EXP * N_EXP
N_PAIRS = N_GROUPS * (N_EXP * (N_EXP - 1) // 2)


def _num_tiles(n):
    return n // TE + N_PAIRS


TP = 2048


def _tables_kernel(meta_ref, cnt_ref, pos_ref, tt_ref):
    cnt = cnt_ref[...]
    ntile = jnp.floor((cnt + (TE - 1)) * (1.0 / TE))
    bi = lax.broadcasted_iota(jnp.int32, (N_BUCKETS, N_BUCKETS), 0)
    bj = lax.broadcasted_iota(jnp.int32, (N_BUCKETS, N_BUCKETS), 1)
    upper = (bi <= bj).astype(bf16)
    incl = jnp.dot(jnp.broadcast_to(ntile, (8, N_BUCKETS)).astype(bf16), upper, preferred_element_type=f32)[0:1]
    excl = incl - ntile

    meta = meta_ref[...]
    tp = meta.shape[0]
    lane = lax.broadcasted_iota(jnp.int32, (tp, LANES), 1)
    bucket = jnp.sum(jnp.where(lane == 32, meta, 0.0), axis=-1, keepdims=True).astype(jnp.int32)
    rank = jnp.sum(jnp.where(lane == 33, meta, 0.0), axis=-1, keepdims=True)
    lane2 = lax.broadcasted_iota(jnp.int32, (tp, N_BUCKETS), 1)
    first_tile = jnp.sum(jnp.where(lane2 == bucket, excl, 0.0), axis=-1, keepdims=True)
    pos_ref[...] = (first_tile * TE + rank).astype(jnp.int32)

    @pl.when(pl.program_id(0) == 0)
    def _():
        tau = lax.broadcasted_iota(jnp.int32, (N_BUCKETS, 1), 0).astype(f32)
        lane_b = lax.broadcasted_iota(jnp.int32, (1, N_BUCKETS), 1).astype(f32)
        tb = jnp.sum((incl <= tau).astype(f32), axis=-1, keepdims=True)
        last_b = jnp.max(jnp.where(cnt > 0, lane_b, 0.0), axis=-1, keepdims=True)
        tbi = jnp.minimum(tb, last_b).astype(jnp.int32)
        grp = lax.shift_right_logical(tbi, 6) * N_EXP
        ea = grp + (lax.shift_right_logical(tbi, 3) & 7)
        eb = grp + (tbi & 7)
        total = jnp.sum(ntile, axis=-1, keepdims=True).astype(jnp.int32)
        lane_t = lax.broadcasted_iota(jnp.int32, (N_BUCKETS, LANES), 1)
        tt_ref[...] = jnp.where(lane_t == 0, ea, jnp.where(lane_t == 1, eb, jnp.where(lane_t == 2, total, 0)))


def _tables(meta, cnt):
    n = meta.shape[0]
    tp = math.gcd(n, TP)
    assert _num_tiles(n) <= N_BUCKETS
    return pl.pallas_call(
        _tables_kernel,
        grid=(n // tp,),
        in_specs=[pl.BlockSpec((tp, LANES), lambda i: (i, 0)), _const_spec((1, N_BUCKETS))],
        out_specs=[pl.BlockSpec((tp, 1), lambda i: (i, 0)), _const_spec((N_BUCKETS, LANES))],
        out_shape=[jax.ShapeDtypeStruct((n, 1), jnp.int32), jax.ShapeDtypeStruct((N_BUCKETS, LANES), jnp.int32)],
        compiler_params=_cparams(("arbitrary",)),
        name="moe_tables",
    )(meta, cnt)


def _plan_kernel(pos_ref, tok_ref, *, n, nslot):
    def fill(s, _):
        tok_ref[s] = -1
        return 0

    lax.fori_loop(0, nslot, fill, 0, unroll=8)

    def place(t, _):
        tok_ref[pos_ref[t]] = t
        return 0

    lax.fori_loop(0, n, place, 0, unroll=8)


def _plan(pos):
    n = pos.shape[0]
    nslot = (_num_tiles(n) + 1) * TE
    smem = pl.BlockSpec(memory_space=pltpu.SMEM)
    return pl.pallas_call(
        functools.partial(_plan_kernel, n=n, nslot=nslot),
        in_specs=[smem], out_specs=smem,
        out_shape=jax.ShapeDtypeStruct((nslot,), jnp.int32),
        name="moe_plan",
    )(pos)


def _moe_kernel(tea_ref, teb_ref, nt_ref, tok_ref, h_hbm, wg_ref, wu_ref, wd_ref, y_hbm,
                xb0, xb1, ob0, ob1, gsem, ssem, *, n):
    i = pl.program_id(0)
    nt = nt_ref[0]
    xbufs = (xb0, xb1)
    obufs = (ob0, ob1)

    def gather_copy(src_row, sl, r):
        return pltpu.make_async_copy(h_hbm.at[pl.ds(src_row, 1), :], xbufs[sl].at[pl.ds(r, 1), :], gsem.at[sl])

    def scatter_copy(dst_row, sl, r):
        return pltpu.make_async_copy(obufs[sl].at[pl.ds(r, 1), :], y_hbm.at[pl.ds(dst_row, 1), :], ssem.at[sl])

    def start_gathers(tile, sl):
        for r in range(TE):
            gather_copy(jnp.maximum(tok_ref[tile * TE + r], 0), sl, r).start()

    def wait_gathers(sl):
        for r in range(TE):
            gather_copy(0, sl, r).wait()

    def start_scatters(tile, sl, to_spare):
        for r in range(TE):
            t = tok_ref[tile * TE + r]
            scatter_copy(jnp.where((t < 0) | to_spare, n + sl * TE + r, t), sl, r).start()

    def wait_scatters(sl):
        for r in range(TE):
            scatter_copy(n, sl, r).wait()

    @pl.when(i == 0)
    def _():
        start_gathers(0, 0)
        for sl in range(2):
            obufs[sl][...] = jnp.zeros_like(obufs[sl])
            spare = pltpu.make_async_copy(obufs[sl], y_hbm.at[pl.ds(n + sl * TE, TE), :], ssem.at[sl])
            spare.start()
            spare.wait()

    def step(s):
        o = 1 - s
        wait_gathers(s)

        @pl.when(i >= 1)
        def _():
            wait_scatters(s)

        start_gathers(i + 1, o)
        start_scatters(jnp.maximum(i - 1, 0), o, i == 0)

        xs = xbufs[s][...]
        hb = xs[:, 0:D].astype(bf16)
        meta = xs[:, D:D + LANES]
        lane = lax.broadcasted_iota(jnp.int32, (TE, LANES), 1)
        w_lo = jnp.sum(jnp.where(lane == 34, meta, 0.0), axis=-1, keepdims=True)
        w_hi = jnp.sum(jnp.where(lane == 35, meta, 0.0), axis=-1, keepdims=True)

        def ffn(e, wcol):
            a = jnp.dot(hb, wg_ref[0, e], preferred_element_type=f32)
            u = jnp.dot(hb, wu_ref[0, e], preferred_element_type=f32)
            act = (a * jax.nn.sigmoid(a)) * u * wcol
            return jnp.dot(act.astype(bf16), wd_ref[0, e], preferred_element_type=f32)

        obufs[s][...] = ffn(lax.rem(tea_ref[i], N_EXP), w_lo) + ffn(lax.rem(teb_ref[i], N_EXP), w_hi)

        @pl.when(i == nt - 1)
        def _():
            start_scatters(i, s, False)
            wait_scatters(o)
            wait_scatters(s)
            wait_gathers(o)

    for s in range(2):
        @pl.when((i < nt) & (lax.rem(i, 2) == s))
        def _(s=s):
            step(s)


def _moe(h2ext, tok, tea, teb, nt, wg, wu, wd):
    n = h2ext.shape[0]
    ntmax = _num_tiles(n)
    wspec_in = pl.BlockSpec((1, N_EXP, D, F_EXP), lambda i, tea, teb, nt, tok: (tea[i] // N_EXP, 0, 0, 0))
    wspec_out = pl.BlockSpec((1, N_EXP, F_EXP, D), lambda i, tea, teb, nt, tok: (tea[i] // N_EXP, 0, 0, 0))
    return pl.pallas_call(
        functools.partial(_moe_kernel, n=n),
        grid_spec=pltpu.PrefetchScalarGridSpec(
            num_scalar_prefetch=4,
            grid=(ntmax,),
            in_specs=[pl.BlockSpec(memory_space=pl.ANY), wspec_in, wspec_in, wspec_out],
            out_specs=pl.BlockSpec(memory_space=pl.ANY),
            scratch_shapes=[pltpu.VMEM((TE, D + LANES), f32), pltpu.VMEM((TE, D + LANES), f32),
                            pltpu.VMEM((TE, D), f32), pltpu.VMEM((TE, D), f32),
                            pltpu.SemaphoreType.DMA((2,)), pltpu.SemaphoreType.DMA((2,))]),
        out_shape=jax.ShapeDtypeStruct((n + 2 * TE, D), f32),
        compiler_params=_cparams(("arbitrary",)),
        name="moe_sparse",
    )(tea, teb, nt, tok, h2ext, wg, wu, wd)


def _final_kernel(x_ref, y_ref, g_ref, o_ref):
    o_ref[...] = x_ref[...] + g_ref[0] * y_ref[...]


def _final_combine(x1, y, g2, seq):
    n = x1.shape[0]
    t = 512
    spb = seq // t
    row = pl.BlockSpec((t, D), lambda i: (i, 0))
    return pl.pallas_call(
        _final_kernel,
        grid=(n // t,),
        in_specs=[row, row, pl.BlockSpec((1, 1, D), lambda i: (i // spb, 0, 0))],
        out_specs=row,
        out_shape=jax.ShapeDtypeStruct((n, D), f32),
        compiler_params=_cparams(("arbitrary",)),
        name="final_combine",
    )(x1, y, g2)


def _seg_matrices(width, segs):
    sd = np.zeros((width, LANES), np.float32)
    ex = np.zeros((LANES, width), np.float32)
    for j, (s, ln) in enumerate(segs):
        sd[s:s + ln, j] = 1.0 / ln
        ex[j, s:s + ln] = 1.0
    return jnp.asarray(sd, bf16), jnp.asarray(np.concatenate([ex, ex], axis=0), bf16)


def _head_pad(w, heads, per_head, keep):
    k = w.shape[0]
    w3 = w.reshape(k, heads, per_head)[:, :, :keep]
    return jnp.pad(w3, ((0, 0), (0, 0), (0, LANES - keep))).reshape(k, heads * LANES)


def kernel(x, c, positions, w_ada, b_ada, norm_g, w_in, diff_qk_g, diff_lambda, diff_out_g, rel_bias, sgu_v_g, sgu_w, sgu_b, mla_lat_g, mla_w_uq, mla_w_ukv, mla_qk_g, w_branch, w_out, router_g_w, router_g_b, router_e_w, router_e_b, w_e_gate, w_e_up, w_e_down):
    nb, seq, _ = x.shape
    n = nb * seq
    assert seq % TQ == 0 and seq % TM == 0 and x.shape[2] == D

    mod = _ada(c, w_ada, b_ada)
    cosf, sinf = _rope_tables(positions)
    t0, t1, mk = _bias_tiles(rel_bias)

    sd_a, ex_a = _seg_matrices(MIX, [(s * 64, 64) for s in range(8)])
    segs_q = []
    for h in range(HC):
        segs_q += [(h * LANES, NOPE), (h * LANES + NOPE, ROPE)]
    sd_q, ex_q = _seg_matrices(HC * LANES, segs_q)
    ne = N_GROUPS * N_EXP
    zpad = lambda a, w_: jnp.pad(a, ((0, 0), (0, w_ - a.shape[1])))

    xcur = x.reshape(n, D)
    y_prev, g2_prev = None, None
    for l in range(DEPTH):
        m3 = mod[l].reshape(nb, 1, 6 * D)
        shift1, scale1, gate1, shift2, scale2, gate2 = [m3[:, :, k * D:(k + 1) * D] for k in range(6)]
        lambda_init = LAMBDA_INIT_BASE - LAMBDA_INIT_SCALE * math.exp(-LAMBDA_INIT_DECAY * l)

        w = w_in[l]
        w_pad = jnp.concatenate([w[:, :XC_OFF + 416], jnp.zeros((D, IN_PAD - IN_REAL), f32), w[:, XC_OFF + 416:]],
                                axis=1).astype(bf16)
        gq = (jnp.tile(diff_qk_g[l, 0], 8) * (64 ** -0.5 * LOG2E)).reshape(1, MIX)
        gk = jnp.tile(diff_qk_g[l, 1], 8).reshape(1, MIX)
        outs = _inproj(xcur, y_prev, g2_prev, shift1, scale1, norm_g[l, 0].reshape(1, D), w_pad, gq, gk,
                       sd_a, ex_a, sgu_v_g[l].reshape(1, MIX), sgu_w[l], jnp.transpose(sgu_b[l]), seq)
        if y_prev is not None:
            xcur = outs[0]
            outs = outs[1:]
        qa, ka, va, ob, xc, gt = outs

        oa = _attn_a(qa, ka, va, t0, t1, diff_lambda[l], diff_out_g[l].reshape(1, LANES), nb, seq, lambda_init)

        qkg = mla_qk_g[l]
        wq = _head_pad(mla_w_uq[l], HC, NOPE + ROPE, NOPE + ROPE).astype(bf16)
        wkv3 = mla_w_ukv[l].reshape(KV_LORA, HC, NOPE + VC)
        wk = jnp.pad(wkv3[:, :, :NOPE], ((0, 0), (0, 0), (0, LANES - NOPE))).reshape(KV_LORA, HC * LANES)
        wv = wkv3[:, :, NOPE:].reshape(KV_LORA, HC * VC)
        wkv = jnp.concatenate([wk, wv], axis=1).astype(bf16)
        gq_c = jnp.tile(jnp.pad(qkg[0], (0, LANES - NOPE - ROPE)), HC).reshape(1, HC * LANES) * ((NOPE + ROPE) ** -0.5 * LOG2E)
        gk_c = jnp.pad(qkg[1, :NOPE], (0, LANES - NOPE)).reshape(1, LANES)
        gr_c = jnp.pad(qkg[1, NOPE:], (0, LANES - ROPE)).reshape(1, LANES)
        qc, kc, vc = _mla_prep(xc, cosf, sinf, mla_lat_g[l, :Q_LORA].reshape(1, Q_LORA),
                               mla_lat_g[l, Q_LORA:].reshape(1, KV_LORA), wq, wkv, gq_c, gk_c, gr_c, sd_q, ex_q)
        oc = _attn_c(qc, kc, vc, mk, nb, seq)

        wr = zpad(jnp.concatenate([router_e_w[l], router_g_w[l]], axis=1), LANES)
        br = zpad(jnp.concatenate([router_e_b[l], router_g_b[l]]).reshape(1, ne + N_GROUPS), LANES)
        x1, h2ext, meta, cnt = _merge(oa, ob, oc, gt, xcur, gate1, shift2, scale2, norm_g[l, 1].reshape(1, D),
                                      w_branch[l].astype(bf16), w_out[l].astype(bf16), wr, br, seq)

        pos, tt = _tables(meta, cnt)
        tok = _plan(pos.reshape(n))
        ntmax = _num_tiles(n)
        y_prev = _moe(h2ext, tok, tt[:ntmax, 0], tt[:ntmax, 1], tt[0:1, 2], w_e_gate[l].astype(bf16),
                      w_e_up[l].astype(bf16), w_e_down[l].astype(bf16))
        g2_prev = gate2
        xcur = x1

    out = _final_combine(xcur, y_prev, g2_prev, seq)
    return out.reshape(nb, seq, D)
```

```python
import functools
import math

import jax
import jax.numpy as jnp
import numpy as np
from jax import lax
from jax.experimental import pallas as pl
from jax.experimental.pallas import tpu as pltpu

f32 = jnp.float32
bf16 = jnp.bfloat16

D = 1024
DEPTH = 4
CHUNK = 64
MIX = 512
HA = 4
HC = 8
Q_LORA, KV_LORA, ROPE = 256, 128, 32
NOPE, VC = 64, 64
N_GROUPS, N_EXP, F_EXP = 4, 8, 256
N_REL_BUCKETS = 32
ROPE_THETA = 10000.0
LAMBDA_INIT_BASE, LAMBDA_INIT_SCALE, LAMBDA_INIT_DECAY = 0.8, 0.6, 0.3
EPS = 1e-6
NEG = -1e30

IN_REAL = 6048
IN_PAD = 6144
XC_OFF = 2560
GATE_OFF = 3072

TM = 512
TQ = 256
TK = 256
LANES = 128
VMEM_LIMIT = 56 * 1024 * 1024


def _cparams(sem, vmem=VMEM_LIMIT):
    return pltpu.CompilerParams(dimension_semantics=sem, vmem_limit_bytes=vmem)


def _const_spec(shape):
    nd = len(shape)
    return pl.BlockSpec(shape, lambda *_: (0,) * nd)


def _rms(x, g_row):
    return x * lax.rsqrt(jnp.mean(x * x, axis=-1, keepdims=True) + EPS) * g_row


def _seg_rinv(x, sd_ref, ex_ref):
    ss = jnp.dot((x * x).astype(bf16), sd_ref[...], preferred_element_type=f32)
    hi = ss.astype(bf16)
    lo = (ss - hi.astype(f32)).astype(bf16)
    full = jnp.dot(jnp.concatenate([hi, lo], axis=-1), ex_ref[...], preferred_element_type=f32)
    return lax.rsqrt(full + EPS)


def _ada_kernel(c_ref, w_ref, b_ref, o_ref):
    c = c_ref[...]
    ca = (c * jax.nn.sigmoid(c)).astype(bf16)
    o_ref[0] = jnp.dot(ca, w_ref[0].astype(bf16), preferred_element_type=f32) + b_ref[0]


def _ada(c, w_ada, b_ada):
    nb = c.shape[0]
    tn = 1024
    return pl.pallas_call(
        _ada_kernel,
        grid=(DEPTH, 6 * D // tn),
        in_specs=[pl.BlockSpec((nb, D), lambda l, j: (0, 0)),
                  pl.BlockSpec((1, D, tn), lambda l, j: (l, 0, j)),
                  pl.BlockSpec((1, 1, tn), lambda l, j: (l, 0, j))],
        out_specs=pl.BlockSpec((1, nb, tn), lambda l, j: (l, 0, j)),
        out_shape=jax.ShapeDtypeStruct((DEPTH, nb, 6 * D), f32),
        compiler_params=_cparams(("arbitrary", "arbitrary")),
        name="ada_mod",
    )(c, w_ada, b_ada.reshape(DEPTH, 1, 6 * D))


def _rope_kernel(pos_ref, invf_ref, cos_ref, sin_ref):
    ang = pos_ref[...].astype(f32) * invf_ref[...]
    lane = lax.broadcasted_iota(jnp.int32, ang.shape, 1)
    rot = (lane >= NOPE) & (lane < NOPE + ROPE)
    first = lane < NOPE + ROPE // 2
    cos_ref[...] = jnp.where(rot, jnp.cos(ang), 1.0)
    s = jnp.sin(ang)
    sin_ref[...] = jnp.where(rot, jnp.where(first, -s, s), 0.0)


def _rope_tables(positions):
    n = positions.size
    inv_freq = ROPE_THETA ** (-jnp.arange(0, ROPE, 2, dtype=f32) / ROPE)
    invf = jnp.zeros((1, LANES), f32).at[0, NOPE:NOPE + ROPE].set(jnp.tile(inv_freq, 2))
    t = math.gcd(n, 2048)
    return pl.pallas_call(
        _rope_kernel,
        grid=(n // t,),
        in_specs=[pl.BlockSpec((t, 1), lambda i: (i, 0)), _const_spec((1, LANES))],
        out_specs=[pl.BlockSpec((t, LANES), lambda i: (i, 0))] * 2,
        out_shape=[jax.ShapeDtypeStruct((n, LANES), f32)] * 2,
        compiler_params=_cparams(("arbitrary",)),
        name="rope_tables",
    )(positions.reshape(n, 1), invf)


def _t5_bucket(rel):
    n = jnp.abs(rel)
    large = 8 + sum((n >= t).astype(jnp.int32) for t in (12, 16, 23, 32, 46, 64, 91))
    return jnp.where(rel > 0, 16, 0) + jnp.where(n < 8, n, large)


def _bias_kernel(tbl_ref, t0_ref, t1_ref, mk_ref):
    r = lax.broadcasted_iota(jnp.int32, (TQ, TK), 0)
    c = lax.broadcasted_iota(jnp.int32, (TQ, TK), 1)
    allowed = (c // CHUNK) <= (r // CHUNK)
    mk = jnp.where(allowed, 0.0, NEG).astype(f32)
    mk_ref[...] = mk
    b0 = _t5_bucket(c - r)
    b1 = _t5_bucket(c - r - TK)
    for h in range(HA):
        far = tbl_ref[N_REL_BUCKETS // 2 - 1, h]
        t0 = jnp.zeros((TQ, TK), f32)
        t1 = jnp.zeros((TQ, TK), f32)
        for b in range(N_REL_BUCKETS):
            v = (tbl_ref[b, h] - far) * LOG2E
            t0 = jnp.where(b0 == b, v, t0)
            t1 = jnp.where(b1 == b, v, t1)
        t0 = t0 + mk
        t0_ref[h, 0:TQ, :] = t0
        t0_ref[h, TQ:2 * TQ, :] = t0
        t1_ref[h, 0:TQ, :] = t1
        t1_ref[h, TQ:2 * TQ, :] = t1


def _bias_tiles(rel_bias):
    return pl.pallas_call(
        _bias_kernel,
        in_specs=[pl.BlockSpec(memory_space=pltpu.SMEM)],
        out_specs=[pl.BlockSpec(memory_space=pltpu.VMEM)] * 3,
        out_shape=[jax.ShapeDtypeStruct((HA, 2 * TQ, TK), f32),
                   jax.ShapeDtypeStruct((HA, 2 * TQ, TK), f32),
                   jax.ShapeDtypeStruct((TQ, TK), f32)],
        name="bias_tiles",
    )(rel_bias)


def _inproj_kernel(*refs, combine):
    if combine:
        (xa_ref, y_ref, g2_ref, sh_ref, sc_ref, ng_ref, w_ref, gq_ref, gk_ref, sd_ref, ex_ref,
         vg_ref, ws_ref, bs_ref,
         xo_ref, qa_ref, ka_ref, va_ref, ob_ref, xc_ref, gt_ref) = refs
        x = xa_ref[...] + g2_ref[0] * y_ref[...]
        xo_ref[...] = x
    else:
        (xa_ref, sh_ref, sc_ref, ng_ref, w_ref, gq_ref, gk_ref, sd_ref, ex_ref,
         vg_ref, ws_ref, bs_ref,
         qa_ref, ka_ref, va_ref, ob_ref, xc_ref, gt_ref) = refs
        x = xa_ref[...]
    h = _rms(x, ng_ref[...]) * (1.0 + sc_ref[0]) + sh_ref[0]
    hb = h.astype(bf16)

    q = jnp.dot(hb, w_ref[:, 0:MIX], preferred_element_type=f32)
    qa_ref[...] = (q * _seg_rinv(q, sd_ref, ex_ref) * gq_ref[...]).astype(bf16)
    k = jnp.dot(hb, w_ref[:, MIX:2 * MIX], preferred_element_type=f32)
    ka_ref[...] = (k * _seg_rinv(k, sd_ref, ex_ref) * gk_ref[...]).astype(bf16)
    va_ref[...] = jnp.dot(hb, w_ref[:, 2 * MIX:3 * MIX], preferred_element_type=f32).astype(bf16)

    z = jnp.dot(hb, w_ref[:, 3 * MIX:5 * MIX], preferred_element_type=f32)
    z = 0.5 * z * (1.0 + jnp.tanh(math.sqrt(2.0 / math.pi) * (z + 0.044715 * (z * z * z))))
    u = z[:, :MIX]
    v = _rms(z[:, MIX:], vg_ref[...]).astype(bf16)
    ri = lax.broadcasted_iota(jnp.int32, (LANES, LANES), 0)
    ci = lax.broadcasted_iota(jnp.int32, (LANES, LANES), 1)
    allowed = (ci // CHUNK) <= (ri // CHUNK)
    for g in range(4):
        wm = jnp.where(allowed, ws_ref[g], 0.0).astype(bf16)
        bcol = bs_ref[:, g:g + 1]
        for wd in range(TM // LANES):
            rows = slice(wd * LANES, (wd + 1) * LANES)
            cols = slice(g * LANES, (g + 1) * LANES)
            vs = jnp.dot(wm, v[rows, cols], preferred_element_type=f32) + bcol
            ob_ref[rows, cols] = (u[rows, cols] * vs).astype(bf16)

    xc_ref[...] = jnp.dot(hb, w_ref[:, XC_OFF:GATE_OFF], preferred_element_type=f32)

    for j in range(3):
        gsl = slice(GATE_OFF + j * D, GATE_OFF + (j + 1) * D)
        gl = jnp.dot(hb, w_ref[:, gsl], preferred_element_type=f32)
        gt_ref[:, j * D:(j + 1) * D] = jax.nn.sigmoid(gl).astype(bf16)


def _inproj(xa, y, g2, shift, scale, ng, w, gq, gk, sd, ex, vg, ws, bs, seq):
    n = xa.shape[0]
    spb = seq // TM
    combine = y is not None
    row = lambda w_: pl.BlockSpec((TM, w_), lambda i: (i, 0))
    modspec = pl.BlockSpec((1, 1, D), lambda i: (i // spb, 0, 0))
    in_specs = [row(D)]
    args = [xa]
    if combine:
        in_specs += [row(D), modspec]
        args += [y, g2]
    in_specs += [modspec, modspec, _const_spec((1, D)),
                 pl.BlockSpec((D, IN_PAD), lambda i: (0, 0), pipeline_mode=pl.Buffered(1)),
                 _const_spec((1, MIX)), _const_spec((1, MIX)),
                 _const_spec((MIX, LANES)), _const_spec((2 * LANES, MIX)),
                 _const_spec((1, MIX)), _const_spec((4, LANES, LANES)), _const_spec((LANES, 4))]
    args += [shift, scale, ng, w, gq, gk, sd, ex, vg, ws, bs]
    out_specs = [row(MIX)] * 4 + [row(MIX), row(3 * D)]
    out_shape = [jax.ShapeDtypeStruct((n, MIX), bf16)] * 4 + [
        jax.ShapeDtypeStruct((n, MIX), f32), jax.ShapeDtypeStruct((n, 3 * D), bf16)]
    if combine:
        out_specs = [row(D)] + out_specs
        out_shape = [jax.ShapeDtypeStruct((n, D), f32)] + out_shape
    return pl.pallas_call(
        functools.partial(_inproj_kernel, combine=combine),
        grid=(n // TM,),
        in_specs=in_specs, out_specs=out_specs, out_shape=out_shape,
        compiler_params=_cparams(("arbitrary",)),
        name="inproj",
    )(*args)


LOG2E = math.log2(math.e)


def _two_pass_attention(nunits, q_of, kcols_of, vcols_of, k_ref, v_ref, qi, diag_bias_of, sub_bias_of,
                        sbuf, mbuf, lbuf, abuf):
    def score_block(j, bias_of):
        rows_k = pl.ds(pl.multiple_of(j * TK, TK), TK)
        for u in range(nunits):
            s = lax.dot_general(q_of(u), k_ref[rows_k, kcols_of(u)], (((1,), (1,)), ((), ())),
                                preferred_element_type=f32)
            if bias_of is not None:
                s = s + bias_of(u)
            sbuf[u, j] = s
            mbuf[u] = jnp.maximum(mbuf[u], jnp.maximum(s[:, :LANES], s[:, LANES:]))

    for u in range(nunits):
        mbuf[u] = jnp.full(mbuf.shape[1:], NEG, f32)
    if sub_bias_of is None:
        nfar = qi
    else:
        nfar = jnp.maximum(qi - 1, 0)

        @pl.when(qi >= 1)
        def _():
            score_block(qi - 1, sub_bias_of)

    def far(j, _):
        score_block(j, None)
        return 0

    lax.fori_loop(0, nfar, far, 0)
    score_block(qi, diag_bias_of)

    for u in range(nunits):
        mbuf[u] = jnp.broadcast_to(jnp.max(mbuf[u], axis=-1, keepdims=True), mbuf.shape[1:])
        lbuf[u] = jnp.zeros(lbuf.shape[1:], f32)
        abuf[u] = jnp.zeros(abuf.shape[1:], f32)

    def accumulate(j, _):
        rows_k = pl.ds(pl.multiple_of(j * TK, TK), TK)
        for u in range(nunits):
            s = sbuf[u, j]
            mb = mbuf[u]
            p0 = jnp.exp2(s[:, :LANES] - mb)
            p1 = jnp.exp2(s[:, LANES:] - mb)
            lbuf[u] = lbuf[u] + (p0 + p1)
            p = jnp.concatenate([p0, p1], axis=-1).astype(bf16)
            abuf[u] = abuf[u] + jnp.dot(p, v_ref[rows_k, vcols_of(u)], preferred_element_type=f32)
        return 0

    lax.fori_loop(0, qi + 1, accumulate, 0)


def _normalised(u, lbuf, abuf):
    return abuf[u] / jnp.sum(lbuf[u], axis=-1, keepdims=True)


GA = 2
GC = 4


def _attn_a_kernel(q_ref, k_ref, v_ref, t0_ref, t1_ref, dl_ref, og_ref, o_ref,
                   qstk, sbuf, mbuf, lbuf, abuf, *, lambda_init):
    qi = pl.program_id(2)
    dl = dl_ref[...]
    lam = (jnp.exp(jnp.sum(dl[0:1] * dl[1:2], axis=-1, keepdims=True))
           - jnp.exp(jnp.sum(dl[2:3] * dl[3:4], axis=-1, keepdims=True)) + lambda_init)
    lane = lax.broadcasted_iota(jnp.int32, (TQ, LANES), 1)
    cols = lambda u: slice(u * LANES, (u + 1) * LANES)
    for u in range(GA):
        qh = q_ref[:, cols(u)]
        zero = jnp.zeros_like(qh)
        qstk[u, 0:TQ, :] = jnp.where(lane < 64, qh, zero)
        qstk[u, TQ:2 * TQ, :] = jnp.where(lane >= 64, qh, zero)
    _two_pass_attention(GA, lambda u: qstk[u], cols, cols, k_ref, v_ref, qi,
                        lambda u: t0_ref[u], lambda u: t1_ref[u], sbuf, mbuf, lbuf, abuf)
    for u in range(GA):
        o = _normalised(u, lbuf, abuf)
        oh = o[:TQ] - lam * o[TQ:]
        oh = _rms(oh, og_ref[...]) * (1.0 - lambda_init)
        o_ref[:, cols(u)] = oh.astype(bf16)


def _attn_a(qa, ka, va, t0, t1, dl, og, nb, seq, lambda_init):
    n = qa.shape[0]
    nq = seq // TQ
    w = GA * LANES
    return pl.pallas_call(
        functools.partial(_attn_a_kernel, lambda_init=lambda_init),
        grid=(nb, HA // GA, nq),
        in_specs=[pl.BlockSpec((TQ, w), lambda b, g, i: (b * nq + i, g)),
                  pl.BlockSpec((seq, w), lambda b, g, i: (b, g)),
                  pl.BlockSpec((seq, w), lambda b, g, i: (b, g)),
                  pl.BlockSpec((GA, 2 * TQ, TK), lambda b, g, i: (g, 0, 0)),
                  pl.BlockSpec((GA, 2 * TQ, TK), lambda b, g, i: (g, 0, 0)),
                  _const_spec((4, 64)), _const_spec((1, LANES))],
        out_specs=pl.BlockSpec((TQ, w), lambda b, g, i: (b * nq + i, g)),
        out_shape=jax.ShapeDtypeStruct((n, MIX), bf16),
        scratch_shapes=[pltpu.VMEM((GA, 2 * TQ, LANES), bf16),
                        pltpu.VMEM((GA, seq // TK, 2 * TQ, TK), f32),
                        pltpu.VMEM((GA, 2 * TQ, LANES), f32), pltpu.VMEM((GA, 2 * TQ, LANES), f32),
                        pltpu.VMEM((GA, 2 * TQ, LANES), f32)],
        compiler_params=_cparams(("arbitrary", "arbitrary", "arbitrary")),
        name="attn_diff",
    )(qa, ka, va, t0, t1, dl, og)


def _attn_c_kernel(q_ref, k_ref, v_ref, mk_ref, o_ref, sbuf, mbuf, lbuf, abuf):
    qi = pl.program_id(2)
    lane = lax.broadcasted_iota(jnp.int32, (TQ, LANES), 1)
    cols = lambda u: slice(u * LANES, (u + 1) * LANES)
    _two_pass_attention(GC, lambda u: q_ref[:, cols(u)], cols, lambda u: cols(u // 2), k_ref, v_ref, qi,
                        lambda u: mk_ref[...], None, sbuf, mbuf, lbuf, abuf)
    for hp in range(GC // 2):
        o_ref[:, cols(hp)] = jnp.where(lane < VC, _normalised(2 * hp, lbuf, abuf),
                                       _normalised(2 * hp + 1, lbuf, abuf)).astype(bf16)


def _attn_c(qc, kc, vc, mk, nb, seq):
    n = qc.shape[0]
    nq = seq // TQ
    w = GC * LANES
    wv = GC * VC
    return pl.pallas_call(
        _attn_c_kernel,
        grid=(nb, HC // GC, nq),
        in_specs=[pl.BlockSpec((TQ, w), lambda b, g, i: (b * nq + i, g)),
                  pl.BlockSpec((seq, w), lambda b, g, i: (b, g)),
                  pl.BlockSpec((seq, wv), lambda b, g, i: (b, g)),
                  _const_spec((TQ, TK))],
        out_specs=pl.BlockSpec((TQ, wv), lambda b, g, i: (b * nq + i, g)),
        out_shape=jax.ShapeDtypeStruct((n, MIX), bf16),
        scratch_shapes=[pltpu.VMEM((GC, seq // TK, TQ, TK), f32),
                        pltpu.VMEM((GC, TQ, LANES), f32), pltpu.VMEM((GC, TQ, LANES), f32),
                        pltpu.VMEM((GC, TQ, LANES), f32)],
        compiler_params=_cparams(("arbitrary", "arbitrary", "arbitrary")),
        name="attn_latent",
    )(qc, kc, vc, mk)


def _rope_apply(x, cosf, sinf, lane):
    w = x.shape[-1]
    partner = jnp.where((lane % LANES) < NOPE + ROPE // 2,
                        pltpu.roll(x, w - ROPE // 2, 1), pltpu.roll(x, ROPE // 2, 1))
    return x * cosf + partner * sinf


def _mla_prep_kernel(xc_ref, cos_ref, sin_ref, glq_ref, glkv_ref, wq_ref, wkv_ref, gq_ref, gk_ref, gr_ref,
                     sd_ref, ex_ref, q_ref, k_ref, v_ref):
    cq = _rms(xc_ref[:, 0:Q_LORA], glq_ref[...]).astype(bf16)
    ckv = _rms(xc_ref[:, Q_LORA:Q_LORA + KV_LORA], glkv_ref[...]).astype(bf16)
    cos8 = jnp.concatenate([cos_ref[...]] * HC, axis=-1)
    sin8 = jnp.concatenate([sin_ref[...]] * HC, axis=-1)
    lane8 = lax.broadcasted_iota(jnp.int32, (TM, HC * LANES), 1)

    q = jnp.dot(cq, wq_ref[...], preferred_element_type=f32)
    q = q * _seg_rinv(q, sd_ref, ex_ref) * gq_ref[...]
    q_ref[...] = _rope_apply(q, cos8, sin8, lane8).astype(bf16)

    kv = jnp.dot(ckv, wkv_ref[...], preferred_element_type=f32)
    v_ref[...] = kv[:, HC * LANES:].astype(bf16)
    xr = xc_ref[:, Q_LORA + KV_LORA:Q_LORA + KV_LORA + LANES]
    kr = xr * lax.rsqrt(jnp.sum(xr * xr, axis=-1, keepdims=True) * (1.0 / ROPE) + EPS) * gr_ref[...]
    kr = pltpu.roll(kr, NOPE, 1)
    lane1 = lax.broadcasted_iota(jnp.int32, (TM, LANES), 1)
    kr = _rope_apply(kr, cos_ref[...], sin_ref[...], lane1)
    for h in range(HC):
        cols = slice(h * LANES, (h + 1) * LANES)
        kn = kv[:, cols]
        kn = kn * lax.rsqrt(jnp.sum(kn * kn, axis=-1, keepdims=True) * (1.0 / NOPE) + EPS) * gk_ref[...]
        k_ref[:, cols] = (kn + kr).astype(bf16)


def _mla_prep(xc, cosf, sinf, glq, glkv, wq, wkv, gq, gk, gr, sd, ex):
    n = xc.shape[0]
    row = lambda w_: pl.BlockSpec((TM, w_), lambda i: (i, 0))
    return pl.pallas_call(
        _mla_prep_kernel,
        grid=(n // TM,),
        in_specs=[row(MIX), row(LANES), row(LANES), _const_spec((1, Q_LORA)), _const_spec((1, KV_LORA)),
                  _const_spec((Q_LORA, HC * LANES)), _const_spec((KV_LORA, HC * LANES + MIX)),
                  _const_spec((1, HC * LANES)), _const_spec((1, LANES)), _const_spec((1, LANES)),
                  _const_spec((HC * LANES, LANES)), _const_spec((2 * LANES, HC * LANES))],
        out_specs=[row(HC * LANES), row(HC * LANES), row(MIX)],
        out_shape=[jax.ShapeDtypeStruct((n, HC * LANES), bf16), jax.ShapeDtypeStruct((n, HC * LANES), bf16),
                   jax.ShapeDtypeStruct((n, MIX), bf16)],
        compiler_params=_cparams(("arbitrary",)),
        name="mla_prep",
    )(xc, cosf, sinf, glq, glkv, wq, wkv, gq, gk, gr, sd, ex)


def _merge_kernel(oa_ref, ob_ref, oc_ref, gt_ref, x_ref, g1_ref, sh_ref, sc_ref, ng_ref, wb_ref, wo_ref,
                  wr_ref, br_ref, x1_ref, h2_ref, meta_ref, cnt_ref, run_ref):
    i = pl.program_id(0)

    @pl.when(i == 0)
    def _():
        run_ref[...] = jnp.zeros_like(run_ref)

    merged = jnp.zeros((TM, D), f32)
    for j, o_ref in enumerate((oa_ref, ob_ref, oc_ref)):
        pj = jnp.dot(o_ref[...], wb_ref[j], preferred_element_type=f32)
        merged = merged + gt_ref[:, j * D:(j + 1) * D].astype(f32) * pj
    y = jnp.dot(merged.astype(bf16), wo_ref[...], preferred_element_type=f32)
    x1 = x_ref[...] + g1_ref[0] * y
    x1_ref[...] = x1
    h2 = _rms(x1, ng_ref[...]) * (1.0 + sc_ref[0]) + sh_ref[0]
    h2_ref[:, 0:D] = h2

    lg = jnp.dot(h2, wr_ref[...], preferred_element_type=f32, precision=lax.Precision.HIGHEST) + br_ref[...]
    lane = lax.broadcasted_iota(jnp.int32, (TM, LANES), 1)
    n_e = N_GROUPS * N_EXP
    gmask = (lane >= n_e) & (lane < n_e + N_GROUPS)
    gl = jnp.where(gmask, lg, NEG)
    gmax = jnp.max(gl, axis=-1, keepdims=True)
    g_w = 1.0 / jnp.sum(jnp.where(gmask, jnp.exp(gl - gmax), 0.0), axis=-1, keepdims=True)
    gidx = jnp.min(jnp.where(gl == gmax, lane - n_e, LANES), axis=-1, keepdims=True)
    emask = (lane < n_e) & ((lane // N_EXP) == gidx)
    el = jnp.where(emask, lg, NEG)
    m1 = jnp.max(el, axis=-1, keepdims=True)
    i1 = jnp.min(jnp.where(el == m1, lane, LANES), axis=-1, keepdims=True)
    el2 = jnp.where(lane == i1, NEG, el)
    m2 = jnp.max(el2, axis=-1, keepdims=True)
    i2 = jnp.min(jnp.where(el2 == m2, lane, LANES), axis=-1, keepdims=True)
    t = jnp.exp(m2 - m1)
    w1 = g_w / (1.0 + t)
    w2 = g_w * t / (1.0 + t)
    lo = jnp.minimum(i1, i2)
    hi = jnp.maximum(i1, i2)
    w_lo = jnp.where(i1 < i2, w1, w2)
    w_hi = jnp.where(i1 < i2, w2, w1)
    bucket = gidx * (N_EXP * N_EXP) + (lo % N_EXP) * N_EXP + (hi % N_EXP)

    lane2 = lax.broadcasted_iota(jnp.int32, (TM, 2 * LANES), 1)
    onehot = (lane2 == bucket).astype(f32)
    rr = lax.broadcasted_iota(jnp.int32, (TM, TM), 0)
    cc = lax.broadcasted_iota(jnp.int32, (TM, TM), 1)
    tri = (cc < rr).astype(bf16)
    before = jnp.dot(tri, onehot.astype(bf16), preferred_element_type=f32) + run_ref[...]
    rank = jnp.sum(onehot * before, axis=-1, keepdims=True)
    run_ref[...] = run_ref[...] + jnp.sum(onehot, axis=0, keepdims=True)
    cnt_ref[...] = run_ref[...]

    meta = (jnp.where(lane == i1, w1, 0.0) + jnp.where(lane == i2, w2, 0.0)
            + jnp.where(lane == 32, bucket.astype(f32), 0.0) + jnp.where(lane == 33, rank, 0.0)
            + jnp.where(lane == 34, w_lo, 0.0) + jnp.where(lane == 35, w_hi, 0.0))
    meta_ref[...] = meta
    h2_ref[:, D:D + LANES] = meta


def _merge(oa, ob, oc, gt, x, g1, shift, scale, ng, wb, wo, wr, br, seq):
    n = x.shape[0]
    spb = seq // TM
    row = lambda w_: pl.BlockSpec((TM, w_), lambda i: (i, 0))
    modspec = pl.BlockSpec((1, 1, D), lambda i: (i // spb, 0, 0))
    return pl.pallas_call(
        _merge_kernel,
        grid=(n // TM,),
        in_specs=[row(MIX), row(MIX), row(MIX), row(3 * D), row(D), modspec, modspec, modspec,
                  _const_spec((1, D)), _const_spec((3, MIX, D)), _const_spec((D, D)),
                  _const_spec((D, LANES)), _const_spec((1, LANES))],
        out_specs=[row(D), row(D + LANES), row(LANES), _const_spec((1, 2 * LANES))],
        out_shape=[jax.ShapeDtypeStruct((n, D), f32), jax.ShapeDtypeStruct((n, D + LANES), f32),
                   jax.ShapeDtypeStruct((n, LANES), f32), jax.ShapeDtypeStruct((1, 2 * LANES), f32)],
        scratch_shapes=[pltpu.VMEM((1, 2 * LANES), f32)],
        compiler_params=_cparams(("arbitrary",)),
        name="merge_route",
    )(oa, ob, oc, gt, x, g1, shift, scale, ng, wb, wo, wr, br)


TE = 128
N_BUCKETS = N_GROUPS * N_EXP * N_EXP
N_PAIRS = N_GROUPS * (N_EXP * (N_EXP - 1) // 2)


def _num_tiles(n):
    return n // TE + N_PAIRS


TP = 2048


def _tables_kernel(meta_ref, cnt_ref, pos_ref, tt_ref):
    cnt = cnt_ref[...]
    ntile = jnp.floor((cnt + (TE - 1)) * (1.0 / TE))
    bi = lax.broadcasted_iota(jnp.int32, (N_BUCKETS, N_BUCKETS), 0)
    bj = lax.broadcasted_iota(jnp.int32, (N_BUCKETS, N_BUCKETS), 1)
    upper = (bi <= bj).astype(bf16)
    incl = jnp.dot(jnp.broadcast_to(ntile, (8, N_BUCKETS)).astype(bf16), upper, preferred_element_type=f32)[0:1]
    excl = incl - ntile

    meta = meta_ref[...]
    tp = meta.shape[0]
    lane = lax.broadcasted_iota(jnp.int32, (tp, LANES), 1)
    bucket = jnp.sum(jnp.where(lane == 32, meta, 0.0), axis=-1, keepdims=True).astype(jnp.int32)
    rank = jnp.sum(jnp.where(lane == 33, meta, 0.0), axis=-1, keepdims=True)
    lane2 = lax.broadcasted_iota(jnp.int32, (tp, N_BUCKETS), 1)
    first_tile = jnp.sum(jnp.where(lane2 == bucket, excl, 0.0), axis=-1, keepdims=True)
    pos_ref[...] = (first_tile * TE + rank).astype(jnp.int32)

    @pl.when(pl.program_id(0) == 0)
    def _():
        tau = lax.broadcasted_iota(jnp.int32, (N_BUCKETS, 1), 0).astype(f32)
        lane_b = lax.broadcasted_iota(jnp.int32, (1, N_BUCKETS), 1).astype(f32)
        tb = jnp.sum((incl <= tau).astype(f32), axis=-1, keepdims=True)
        last_b = jnp.max(jnp.where(cnt > 0, lane_b, 0.0), axis=-1, keepdims=True)
        tbi = jnp.minimum(tb, last_b).astype(jnp.int32)
        grp = lax.shift_right_logical(tbi, 6) * N_EXP
        ea = grp + (lax.shift_right_logical(tbi, 3) & 7)
        eb = grp + (tbi & 7)
        total = jnp.sum(ntile, axis=-1, keepdims=True).astype(jnp.int32)
        lane_t = lax.broadcasted_iota(jnp.int32, (N_BUCKETS, LANES), 1)
        tt_ref[...] = jnp.where(lane_t == 0, ea, jnp.where(lane_t == 1, eb, jnp.where(lane_t == 2, total, 0)))


def _tables(meta, cnt):
    n = meta.shape[0]
    tp = math.gcd(n, TP)
    assert _num_tiles(n) <= N_BUCKETS
    return pl.pallas_call(
        _tables_kernel,
        grid=(n // tp,),
        in_specs=[pl.BlockSpec((tp, LANES), lambda i: (i, 0)), _const_spec((1, N_BUCKETS))],
        out_specs=[pl.BlockSpec((tp, 1), lambda i: (i, 0)), _const_spec((N_BUCKETS, LANES))],
        out_shape=[jax.ShapeDtypeStruct((n, 1), jnp.int32), jax.ShapeDtypeStruct((N_BUCKETS, LANES), jnp.int32)],
        compiler_params=_cparams(("arbitrary",)),
        name="moe_tables",
    )(meta, cnt)


def _plan_kernel(pos_ref, tok_ref, *, n, nslot):
    def fill(s, _):
        tok_ref[s] = -1
        return 0

    lax.fori_loop(0, nslot, fill, 0, unroll=8)

    def place(t, _):
        tok_ref[pos_ref[t]] = t
        return 0

    lax.fori_loop(0, n, place, 0, unroll=8)


def _plan(pos):
    n = pos.shape[0]
    nslot = (_num_tiles(n) + 1) * TE
    smem = pl.BlockSpec(memory_space=pltpu.SMEM)
    return pl.pallas_call(
        functools.partial(_plan_kernel, n=n, nslot=nslot),
        in_specs=[smem], out_specs=smem,
        out_shape=jax.ShapeDtypeStruct((nslot,), jnp.int32),
        name="moe_plan",
    )(pos)


def _moe_kernel(tea_ref, teb_ref, nt_ref, tok_ref, h_hbm, wg_ref, wu_ref, wd_ref, y_hbm,
                xb0, xb1, ob0, ob1, gsem, ssem, *, n):
    i = pl.program_id(0)
    nt = nt_ref[0]
    xbufs = (xb0, xb1)
    obufs = (ob0, ob1)

    def gather_copy(src_row, sl, r):
        return pltpu.make_async_copy(h_hbm.at[pl.ds(src_row, 1), :], xbufs[sl].at[pl.ds(r, 1), :], gsem.at[sl])

    def scatter_copy(dst_row, sl, r):
        return pltpu.make_async_copy(obufs[sl].at[pl.ds(r, 1), :], y_hbm.at[pl.ds(dst_row, 1), :], ssem.at[sl])

    def start_gathers(tile, sl):
        for r in range(TE):
            gather_copy(jnp.maximum(tok_ref[tile * TE + r], 0), sl, r).start()

    def wait_gathers(sl):
        for r in range(TE):
            gather_copy(0, sl, r).wait()

    def start_scatters(tile, sl, to_spare):
        for r in range(TE):
            t = tok_ref[tile * TE + r]
            scatter_copy(jnp.where((t < 0) | to_spare, n + sl * TE + r, t), sl, r).start(priority=1)

    def wait_scatters(sl):
        for r in range(TE):
            scatter_copy(n, sl, r).wait()

    @pl.when(i == 0)
    def _():
        start_gathers(0, 0)
        for sl in range(2):
            obufs[sl][...] = jnp.zeros_like(obufs[sl])
            spare = pltpu.make_async_copy(obufs[sl], y_hbm.at[pl.ds(n + sl * TE, TE), :], ssem.at[sl])
            spare.start()
            spare.wait()

    def step(s):
        o = 1 - s
        wait_gathers(s)

        @pl.when(i >= 1)
        def _():
            wait_scatters(s)

        start_gathers(i + 1, o)
        start_scatters(jnp.maximum(i - 1, 0), o, i == 0)

        xs = xbufs[s][...]
        hb = xs[:, 0:D].astype(bf16)
        meta = xs[:, D:D + LANES]
        lane = lax.broadcasted_iota(jnp.int32, (TE, LANES), 1)
        w_lo = jnp.sum(jnp.where(lane == 34, meta, 0.0), axis=-1, keepdims=True)
        w_hi = jnp.sum(jnp.where(lane == 35, meta, 0.0), axis=-1, keepdims=True)

        def ffn(e, wcol):
            a = jnp.dot(hb, wg_ref[0, e], preferred_element_type=f32)
            u = jnp.dot(hb, wu_ref[0, e], preferred_element_type=f32)
            act = (a * jax.nn.sigmoid(a)) * u * wcol
            return jnp.dot(act.astype(bf16), wd_ref[0, e], preferred_element_type=f32)

        obufs[s][...] = ffn(lax.rem(tea_ref[i], N_EXP), w_lo) + ffn(lax.rem(teb_ref[i], N_EXP), w_hi)

        @pl.when(i == nt - 1)
        def _():
            start_scatters(i, s, False)
            wait_scatters(o)
            wait_scatters(s)
            wait_gathers(o)

    for s in range(2):
        @pl.when((i < nt) & (lax.rem(i, 2) == s))
        def _(s=s):
            step(s)


def _moe(h2ext, tok, tea, teb, nt, wg, wu, wd):
    n = h2ext.shape[0]
    ntmax = _num_tiles(n)
    wspec_in = pl.BlockSpec((1, N_EXP, D, F_EXP), lambda i, tea, teb, nt, tok: (tea[i] // N_EXP, 0, 0, 0))
    wspec_out = pl.BlockSpec((1, N_EXP, F_EXP, D), lambda i, tea, teb, nt, tok: (tea[i] // N_EXP, 0, 0, 0))
    return pl.pallas_call(
        functools.partial(_moe_kernel, n=n),
        grid_spec=pltpu.PrefetchScalarGridSpec(
            num_scalar_prefetch=4,
            grid=(ntmax,),
            in_specs=[pl.BlockSpec(memory_space=pl.ANY), wspec_in, wspec_in, wspec_out],
            out_specs=pl.BlockSpec(memory_space=pl.ANY),
            scratch_shapes=[pltpu.VMEM((TE, D + LANES), f32), pltpu.VMEM((TE, D + LANES), f32),
                            pltpu.VMEM((TE, D), f32), pltpu.VMEM((TE, D), f32),
                            pltpu.SemaphoreType.DMA((2,)), pltpu.SemaphoreType.DMA((2,))]),
        out_shape=jax.ShapeDtypeStruct((n + 2 * TE, D), f32),
        compiler_params=_cparams(("arbitrary",)),
        name="moe_sparse",
    )(tea, teb, nt, tok, h2ext, wg, wu, wd)


def _final_kernel(x_ref, y_ref, g_ref, o_ref):
    o_ref[...] = x_ref[...] + g_ref[0] * y_ref[...]


def _final_combine(x1, y, g2, seq):
    n = x1.shape[0]
    t = 512
    spb = seq // t
    row = pl.BlockSpec((t, D), lambda i: (i, 0))
    return pl.pallas_call(
        _final_kernel,
        grid=(n // t,),
        in_specs=[row, row, pl.BlockSpec((1, 1, D), lambda i: (i // spb, 0, 0))],
        out_specs=row,
        out_shape=jax.ShapeDtypeStruct((n, D), f32),
        compiler_params=_cparams(("arbitrary",)),
        name="final_combine",
    )(x1, y, g2)


def _seg_matrices(width, segs):
    sd = np.zeros((width, LANES), np.float32)
    ex = np.zeros((LANES, width), np.float32)
    for j, (s, ln) in enumerate(segs):
        sd[s:s + ln, j] = 1.0 / ln
        ex[j, s:s + ln] = 1.0
    return jnp.asarray(sd, bf16), jnp.asarray(np.concatenate([ex, ex], axis=0), bf16)


def _head_pad(w, heads, per_head, keep):
    k = w.shape[0]
    w3 = w.reshape(k, heads, per_head)[:, :, :keep]
    return jnp.pad(w3, ((0, 0), (0, 0), (0, LANES - keep))).reshape(k, heads * LANES)


def kernel(x, c, positions, w_ada, b_ada, norm_g, w_in, diff_qk_g, diff_lambda, diff_out_g, rel_bias, sgu_v_g, sgu_w, sgu_b, mla_lat_g, mla_w_uq, mla_w_ukv, mla_qk_g, w_branch, w_out, router_g_w, router_g_b, router_e_w, router_e_b, w_e_gate, w_e_up, w_e_down):
    nb, seq, _ = x.shape
    n = nb * seq
    assert seq % TQ == 0 and seq % TM == 0 and x.shape[2] == D

    mod = _ada(c, w_ada, b_ada)
    cosf, sinf = _rope_tables(positions)
    t0, t1, mk = _bias_tiles(rel_bias)

    sd_a, ex_a = _seg_matrices(MIX, [(s * 64, 64) for s in range(8)])
    segs_q = []
    for h in range(HC):
        segs_q += [(h * LANES, NOPE), (h * LANES + NOPE, ROPE)]
    sd_q, ex_q = _seg_matrices(HC * LANES, segs_q)
    ne = N_GROUPS * N_EXP
    zpad = lambda a, w_: jnp.pad(a, ((0, 0), (0, w_ - a.shape[1])))

    xcur = x.reshape(n, D)
    y_prev, g2_prev = None, None
    for l in range(DEPTH):
        m3 = mod[l].reshape(nb, 1, 6 * D)
        shift1, scale1, gate1, shift2, scale2, gate2 = [m3[:, :, k * D:(k + 1) * D] for k in range(6)]
        lambda_init = LAMBDA_INIT_BASE - LAMBDA_INIT_SCALE * math.exp(-LAMBDA_INIT_DECAY * l)

        w = w_in[l]
        w_pad = jnp.concatenate([w[:, :XC_OFF + 416], jnp.zeros((D, IN_PAD - IN_REAL), f32), w[:, XC_OFF + 416:]],
                                axis=1).astype(bf16)
        gq = (jnp.tile(diff_qk_g[l, 0], 8) * (64 ** -0.5 * LOG2E)).reshape(1, MIX)
        gk = jnp.tile(diff_qk_g[l, 1], 8).reshape(1, MIX)
        outs = _inproj(xcur, y_prev, g2_prev, shift1, scale1, norm_g[l, 0].reshape(1, D), w_pad, gq, gk,
                       sd_a, ex_a, sgu_v_g[l].reshape(1, MIX), sgu_w[l], jnp.transpose(sgu_b[l]), seq)
        if y_prev is not None:
            xcur = outs[0]
            outs = outs[1:]
        qa, ka, va, ob, xc, gt = outs

        oa = _attn_a(qa, ka, va, t0, t1, diff_lambda[l], diff_out_g[l].reshape(1, LANES), nb, seq, lambda_init)

        qkg = mla_qk_g[l]
        wq = _head_pad(mla_w_uq[l], HC, NOPE + ROPE, NOPE + ROPE).astype(bf16)
        wkv3 = mla_w_ukv[l].reshape(KV_LORA, HC, NOPE + VC)
        wk = jnp.pad(wkv3[:, :, :NOPE], ((0, 0), (0, 0), (0, LANES - NOPE))).reshape(KV_LORA, HC * LANES)
        wv = wkv3[:, :, NOPE:].reshape(KV_LORA, HC * VC)
        wkv = jnp.concatenate([wk, wv], axis=1).astype(bf16)
        gq_c = jnp.tile(jnp.pad(qkg[0], (0, LANES - NOPE - ROPE)), HC).reshape(1, HC * LANES) * ((NOPE + ROPE) ** -0.5 * LOG2E)
        gk_c = jnp.pad(qkg[1, :NOPE], (0, LANES - NOPE)).reshape(1, LANES)
        gr_c = jnp.pad(qkg[1, NOPE:], (0, LANES - ROPE)).reshape(1, LANES)
        qc, kc, vc = _mla_prep(xc, cosf, sinf, mla_lat_g[l, :Q_LORA].reshape(1, Q_LORA),
                               mla_lat_g[l, Q_LORA:].reshape(1, KV_LORA), wq, wkv, gq_c, gk_c, gr_c, sd_q, ex_q)
        oc = _attn_c(qc, kc, vc, mk, nb, seq)

        wr = zpad(jnp.concatenate([router_e_w[l], router_g_w[l]], axis=1), LANES)
        br = zpad(jnp.concatenate([router_e_b[l], router_g_b[l]]).reshape(1, ne + N_GROUPS), LANES)
        x1, h2ext, meta, cnt = _merge(oa, ob, oc, gt, xcur, gate1, shift2, scale2, norm_g[l, 1].reshape(1, D),
                                      w_branch[l].astype(bf16), w_out[l].astype(bf16), wr, br, seq)

        pos, tt = _tables(meta, cnt)
        tok = _plan(pos.reshape(n))
        ntmax = _num_tiles(n)
        y_prev = _moe(h2ext, tok, tt[:ntmax, 0], tt[:ntmax, 1], tt[0:1, 2], w_e_gate[l].astype(bf16),
                      w_e_up[l].astype(bf16), w_e_down[l].astype(bf16))
        g2_prev = gate2
        xcur = x1

    out = _final_combine(xcur, y_prev, g2_prev, seq)
    return out.reshape(nb, seq, D)
```

```python
import functools
import math

import jax
import jax.numpy as jnp
import numpy as np
from jax import lax
from jax.experimental import pallas as pl
from jax.experimental.pallas import tpu as pltpu

f32 = jnp.float32
bf16 = jnp.bfloat16

D = 1024
DEPTH = 4
CHUNK = 64
MIX = 512
HA = 4
HC = 8
Q_LORA, KV_LORA, ROPE = 256, 128, 32
NOPE, VC = 64, 64
N_GROUPS, N_EXP, F_EXP = 4, 8, 256
N_REL_BUCKETS = 32
ROPE_THETA = 10000.0
LAMBDA_INIT_BASE, LAMBDA_INIT_SCALE, LAMBDA_INIT_DECAY = 0.8, 0.6, 0.3
EPS = 1e-6
NEG = -1e30

IN_REAL = 6048
IN_PAD = 6144
XC_OFF = 2560
GATE_OFF = 3072

TM = 512
TQ = 256
TK = 256
LANES = 128
VMEM_LIMIT = 56 * 1024 * 1024
SLABS = D // LANES
TOKEN_PITCH = 16
Y_PITCH = 8
VMEM_PITCH = 9


def _cparams(sem, vmem=VMEM_LIMIT):
    return pltpu.CompilerParams(dimension_semantics=sem, vmem_limit_bytes=vmem)


def _const_spec(shape):
    nd = len(shape)
    return pl.BlockSpec(shape, lambda *_: (0,) * nd)


def _rms(x, g_row):
    return x * lax.rsqrt(jnp.mean(x * x, axis=-1, keepdims=True) + EPS) * g_row


def _rows_from_token_major(y_ref, rows):
    return jnp.concatenate([y_ref[pl.ds(k, rows, stride=Y_PITCH), :] for k in range(SLABS)], axis=-1)


def _seg_rinv(x, sd_ref, ex_ref):
    ss = jnp.dot((x * x).astype(bf16), sd_ref[...], preferred_element_type=f32)
    hi = ss.astype(bf16)
    lo = (ss - hi.astype(f32)).astype(bf16)
    full = jnp.dot(jnp.concatenate([hi, lo], axis=-1), ex_ref[...], preferred_element_type=f32)
    return lax.rsqrt(full + EPS)


def _ada_kernel(c_ref, w_ref, b_ref, o_ref):
    c = c_ref[...]
    ca = (c * jax.nn.sigmoid(c)).astype(bf16)
    o_ref[0] = jnp.dot(ca, w_ref[0].astype(bf16), preferred_element_type=f32) + b_ref[0]


def _ada(c, w_ada, b_ada):
    nb = c.shape[0]
    tn = 1024
    return pl.pallas_call(
        _ada_kernel,
        grid=(DEPTH, 6 * D // tn),
        in_specs=[pl.BlockSpec((nb, D), lambda l, j: (0, 0)),
                  pl.BlockSpec((1, D, tn), lambda l, j: (l, 0, j)),
                  pl.BlockSpec((1, 1, tn), lambda l, j: (l, 0, j))],
        out_specs=pl.BlockSpec((1, nb, tn), lambda l, j: (l, 0, j)),
        out_shape=jax.ShapeDtypeStruct((DEPTH, nb, 6 * D), f32),
        compiler_params=_cparams(("arbitrary", "arbitrary")),
        name="ada_mod",
    )(c, w_ada, b_ada.reshape(DEPTH, 1, 6 * D))


def _rope_kernel(pos_ref, invf_ref, cos_ref, sin_ref):
    ang = pos_ref[...].astype(f32) * invf_ref[...]
    lane = lax.broadcasted_iota(jnp.int32, ang.shape, 1)
    rot = (lane >= NOPE) & (lane < NOPE + ROPE)
    first = lane < NOPE + ROPE // 2
    cos_ref[...] = jnp.where(rot, jnp.cos(ang), 1.0)
    s = jnp.sin(ang)
    sin_ref[...] = jnp.where(rot, jnp.where(first, -s, s), 0.0)


def _rope_tables(positions):
    n = positions.size
    inv_freq = ROPE_THETA ** (-jnp.arange(0, ROPE, 2, dtype=f32) / ROPE)
    invf = jnp.zeros((1, LANES), f32).at[0, NOPE:NOPE + ROPE].set(jnp.tile(inv_freq, 2))
    t = math.gcd(n, 2048)
    return pl.pallas_call(
        _rope_kernel,
        grid=(n // t,),
        in_specs=[pl.BlockSpec((t, 1), lambda i: (i, 0)), _const_spec((1, LANES))],
        out_specs=[pl.BlockSpec((t, LANES), lambda i: (i, 0))] * 2,
        out_shape=[jax.ShapeDtypeStruct((n, LANES), f32)] * 2,
        compiler_params=_cparams(("arbitrary",)),
        name="rope_tables",
    )(positions.reshape(n, 1), invf)


def _t5_bucket(rel):
    n = jnp.abs(rel)
    large = 8 + sum((n >= t).astype(jnp.int32) for t in (12, 16, 23, 32, 46, 64, 91))
    return jnp.where(rel > 0, 16, 0) + jnp.where(n < 8, n, large)


def _bias_kernel(tbl_ref, t0_ref, t1_ref, mk_ref):
    r = lax.broadcasted_iota(jnp.int32, (TQ, TK), 0)
    c = lax.broadcasted_iota(jnp.int32, (TQ, TK), 1)
    allowed = (c // CHUNK) <= (r // CHUNK)
    mk = jnp.where(allowed, 0.0, NEG).astype(f32)
    mk_ref[...] = mk
    b0 = _t5_bucket(c - r)
    b1 = _t5_bucket(c - r - TK)
    for h in range(HA):
        far = tbl_ref[N_REL_BUCKETS // 2 - 1, h]
        t0 = jnp.zeros((TQ, TK), f32)
        t1 = jnp.zeros((TQ, TK), f32)
        for b in range(N_REL_BUCKETS):
            v = (tbl_ref[b, h] - far) * LOG2E
            t0 = jnp.where(b0 == b, v, t0)
            t1 = jnp.where(b1 == b, v, t1)
        t0 = t0 + mk
        t0_ref[h, 0:TQ, :] = t0
        t0_ref[h, TQ:2 * TQ, :] = t0
        t1_ref[h, 0:TQ, :] = t1
        t1_ref[h, TQ:2 * TQ, :] = t1


def _bias_tiles(rel_bias):
    return pl.pallas_call(
        _bias_kernel,
        in_specs=[pl.BlockSpec(memory_space=pltpu.SMEM)],
        out_specs=[pl.BlockSpec(memory_space=pltpu.VMEM)] * 3,
        out_shape=[jax.ShapeDtypeStruct((HA, 2 * TQ, TK), f32),
                   jax.ShapeDtypeStruct((HA, 2 * TQ, TK), f32),
                   jax.ShapeDtypeStruct((TQ, TK), f32)],
        name="bias_tiles",
    )(rel_bias)


def _inproj_kernel(*refs, combine):
    if combine:
        (xa_ref, y_ref, g2_ref, sh_ref, sc_ref, ng_ref, w_ref, gq_ref, gk_ref, sd_ref, ex_ref,
         vg_ref, ws_ref, bs_ref,
         xo_ref, qa_ref, ka_ref, va_ref, ob_ref, xc_ref, gt_ref) = refs
        x = xa_ref[...] + g2_ref[0] * _rows_from_token_major(y_ref, TM)
        xo_ref[...] = x
    else:
        (xa_ref, sh_ref, sc_ref, ng_ref, w_ref, gq_ref, gk_ref, sd_ref, ex_ref,
         vg_ref, ws_ref, bs_ref,
         qa_ref, ka_ref, va_ref, ob_ref, xc_ref, gt_ref) = refs
        x = xa_ref[...]
    h = _rms(x, ng_ref[...]) * (1.0 + sc_ref[0]) + sh_ref[0]
    hb = h.astype(bf16)

    q = jnp.dot(hb, w_ref[:, 0:MIX], preferred_element_type=f32)
    qa_ref[...] = (q * _seg_rinv(q, sd_ref, ex_ref) * gq_ref[...]).astype(bf16)
    k = jnp.dot(hb, w_ref[:, MIX:2 * MIX], preferred_element_type=f32)
    ka_ref[...] = (k * _seg_rinv(k, sd_ref, ex_ref) * gk_ref[...]).astype(bf16)
    va_ref[...] = jnp.dot(hb, w_ref[:, 2 * MIX:3 * MIX], preferred_element_type=f32).astype(bf16)

    z = jnp.dot(hb, w_ref[:, 3 * MIX:5 * MIX], preferred_element_type=f32)
    z = 0.5 * z * (1.0 + jnp.tanh(math.sqrt(2.0 / math.pi) * (z + 0.044715 * (z * z * z))))
    u = z[:, :MIX]
    v = _rms(z[:, MIX:], vg_ref[...]).astype(bf16)
    ri = lax.broadcasted_iota(jnp.int32, (LANES, LANES), 0)
    ci = lax.broadcasted_iota(jnp.int32, (LANES, LANES), 1)
    allowed = (ci // CHUNK) <= (ri // CHUNK)
    for g in range(4):
        wm = jnp.where(allowed, ws_ref[g], 0.0).astype(bf16)
        bcol = bs_ref[:, g:g + 1]
        for wd in range(TM // LANES):
            rows = slice(wd * LANES, (wd + 1) * LANES)
            cols = slice(g * LANES, (g + 1) * LANES)
            vs = jnp.dot(wm, v[rows, cols], preferred_element_type=f32) + bcol
            ob_ref[rows, cols] = (u[rows, cols] * vs).astype(bf16)

    xc_ref[...] = jnp.dot(hb, w_ref[:, XC_OFF:GATE_OFF], preferred_element_type=f32)

    for j in range(3):
        gsl = slice(GATE_OFF + j * D, GATE_OFF + (j + 1) * D)
        gl = jnp.dot(hb, w_ref[:, gsl], preferred_element_type=f32)
        gt_ref[:, j * D:(j + 1) * D] = jax.nn.sigmoid(gl).astype(bf16)


def _inproj(xa, y, g2, shift, scale, ng, w, gq, gk, sd, ex, vg, ws, bs, seq):
    n = xa.shape[0]
    spb = seq // TM
    combine = y is not None
    row = lambda w_: pl.BlockSpec((TM, w_), lambda i: (i, 0))
    modspec = pl.BlockSpec((1, 1, D), lambda i: (i // spb, 0, 0))
    in_specs = [row(D)]
    args = [xa]
    if combine:
        in_specs += [pl.BlockSpec((TM * Y_PITCH, LANES), lambda i: (i, 0)), modspec]
        args += [y, g2]
    in_specs += [modspec, modspec, _const_spec((1, D)),
                 pl.BlockSpec((D, IN_PAD), lambda i: (0, 0), pipeline_mode=pl.Buffered(1)),
                 _const_spec((1, MIX)), _const_spec((1, MIX)),
                 _const_spec((MIX, LANES)), _const_spec((2 * LANES, MIX)),
                 _const_spec((1, MIX)), _const_spec((4, LANES, LANES)), _const_spec((LANES, 4))]
    args += [shift, scale, ng, w, gq, gk, sd, ex, vg, ws, bs]
    out_specs = [row(MIX)] * 4 + [row(MIX), row(3 * D)]
    out_shape = [jax.ShapeDtypeStruct((n, MIX), bf16)] * 4 + [
        jax.ShapeDtypeStruct((n, MIX), f32), jax.ShapeDtypeStruct((n, 3 * D), bf16)]
    if combine:
        out_specs = [row(D)] + out_specs
        out_shape = [jax.ShapeDtypeStruct((n, D), f32)] + out_shape
    return pl.pallas_call(
        functools.partial(_inproj_kernel, combine=combine),
        grid=(n // TM,),
        in_specs=in_specs, out_specs=out_specs, out_shape=out_shape,
        compiler_params=_cparams(("arbitrary",)),
        name="inproj",
    )(*args)


LOG2E = math.log2(math.e)


def _two_pass_attention(nunits, q_of, kcols_of, vcols_of, k_ref, v_ref, qi, diag_bias_of, sub_bias_of,
                        sbuf, mbuf, lbuf, abuf):
    def score_block(j, bias_of):
        rows_k = pl.ds(pl.multiple_of(j * TK, TK), TK)
        for u in range(nunits):
            s = lax.dot_general(q_of(u), k_ref[rows_k, kcols_of(u)], (((1,), (1,)), ((), ())),
                                preferred_element_type=f32)
            if bias_of is not None:
                s = s + bias_of(u)
            sbuf[u, j] = s
            mbuf[u] = jnp.maximum(mbuf[u], jnp.maximum(s[:, :LANES], s[:, LANES:]))

    for u in range(nunits):
        mbuf[u] = jnp.full(mbuf.shape[1:], NEG, f32)
    if sub_bias_of is None:
        nfar = qi
    else:
        nfar = jnp.maximum(qi - 1, 0)

        @pl.when(qi >= 1)
        def _():
            score_block(qi - 1, sub_bias_of)

    def far(j, _):
        score_block(j, None)
        return 0

    lax.fori_loop(0, nfar, far, 0)
    score_block(qi, diag_bias_of)

    for u in range(nunits):
        mbuf[u] = jnp.broadcast_to(jnp.max(mbuf[u], axis=-1, keepdims=True), mbuf.shape[1:])
        lbuf[u] = jnp.zeros(lbuf.shape[1:], f32)
        abuf[u] = jnp.zeros(abuf.shape[1:], f32)

    def accumulate(j, _):
        rows_k = pl.ds(pl.multiple_of(j * TK, TK), TK)
        for u in range(nunits):
            s = sbuf[u, j]
            mb = mbuf[u]
            p0 = jnp.exp2(s[:, :LANES] - mb)
            p1 = jnp.exp2(s[:, LANES:] - mb)
            lbuf[u] = lbuf[u] + (p0 + p1)
            p = jnp.concatenate([p0, p1], axis=-1).astype(bf16)
            abuf[u] = abuf[u] + jnp.dot(p, v_ref[rows_k, vcols_of(u)], preferred_element_type=f32)
        return 0

    lax.fori_loop(0, qi + 1, accumulate, 0)


def _normalised(u, lbuf, abuf):
    return abuf[u] / jnp.sum(lbuf[u], axis=-1, keepdims=True)


GA = 2
GC = 4


def _attn_a_kernel(q_ref, k_ref, v_ref, t0_ref, t1_ref, dl_ref, og_ref, o_ref,
                   qstk, sbuf, mbuf, lbuf, abuf, *, lambda_init):
    qi = pl.program_id(2)
    dl = dl_ref[...]
    lam = (jnp.exp(jnp.sum(dl[0:1] * dl[1:2], axis=-1, keepdims=True))
           - jnp.exp(jnp.sum(dl[2:3] * dl[3:4], axis=-1, keepdims=True)) + lambda_init)
    lane = lax.broadcasted_iota(jnp.int32, (TQ, LANES), 1)
    cols = lambda u: slice(u * LANES, (u + 1) * LANES)
    for u in range(GA):
        qh = q_ref[:, cols(u)]
        zero = jnp.zeros_like(qh)
        qstk[u, 0:TQ, :] = jnp.where(lane < 64, qh, zero)
        qstk[u, TQ:2 * TQ, :] = jnp.where(lane >= 64, qh, zero)
    _two_pass_attention(GA, lambda u: qstk[u], cols, cols, k_ref, v_ref, qi,
                        lambda u: t0_ref[u], lambda u: t1_ref[u], sbuf, mbuf, lbuf, abuf)
    for u in range(GA):
        o = _normalised(u, lbuf, abuf)
        oh = o[:TQ] - lam * o[TQ:]
        oh = _rms(oh, og_ref[...]) * (1.0 - lambda_init)
        o_ref[:, cols(u)] = oh.astype(bf16)


def _attn_a(qa, ka, va, t0, t1, dl, og, nb, seq, lambda_init):
    n = qa.shape[0]
    nq = seq // TQ
    w = GA * LANES
    return pl.pallas_call(
        functools.partial(_attn_a_kernel, lambda_init=lambda_init),
        grid=(nb, HA // GA, nq),
        in_specs=[pl.BlockSpec((TQ, w), lambda b, g, i: (b * nq + i, g)),
                  pl.BlockSpec((seq, w), lambda b, g, i: (b, g)),
                  pl.BlockSpec((seq, w), lambda b, g, i: (b, g)),
                  pl.BlockSpec((GA, 2 * TQ, TK), lambda b, g, i: (g, 0, 0)),
                  pl.BlockSpec((GA, 2 * TQ, TK), lambda b, g, i: (g, 0, 0)),
                  _const_spec((4, 64)), _const_spec((1, LANES))],
        out_specs=pl.BlockSpec((TQ, w), lambda b, g, i: (b * nq + i, g)),
        out_shape=jax.ShapeDtypeStruct((n, MIX), bf16),
        scratch_shapes=[pltpu.VMEM((GA, 2 * TQ, LANES), bf16),
                        pltpu.VMEM((GA, seq // TK, 2 * TQ, TK), f32),
                        pltpu.VMEM((GA, 2 * TQ, LANES), f32), pltpu.VMEM((GA, 2 * TQ, LANES), f32),
                        pltpu.VMEM((GA, 2 * TQ, LANES), f32)],
        compiler_params=_cparams(("arbitrary", "arbitrary", "arbitrary")),
        name="attn_diff",
    )(qa, ka, va, t0, t1, dl, og)


def _attn_c_kernel(q_ref, k_ref, v_ref, mk_ref, o_ref, sbuf, mbuf, lbuf, abuf):
    qi = pl.program_id(2)
    lane = lax.broadcasted_iota(jnp.int32, (TQ, LANES), 1)
    cols = lambda u: slice(u * LANES, (u + 1) * LANES)
    _two_pass_attention(GC, lambda u: q_ref[:, cols(u)], cols, lambda u: cols(u // 2), k_ref, v_ref, qi,
                        lambda u: mk_ref[...], None, sbuf, mbuf, lbuf, abuf)
    for hp in range(GC // 2):
        o_ref[:, cols(hp)] = jnp.where(lane < VC, _normalised(2 * hp, lbuf, abuf),
                                       _normalised(2 * hp + 1, lbuf, abuf)).astype(bf16)


def _attn_c(qc, kc, vc, mk, nb, seq):
    n = qc.shape[0]
    nq = seq // TQ
    w = GC * LANES
    wv = GC * VC
    return pl.pallas_call(
        _attn_c_kernel,
        grid=(nb, HC // GC, nq),
        in_specs=[pl.BlockSpec((TQ, w), lambda b, g, i: (b * nq + i, g)),
                  pl.BlockSpec((seq, w), lambda b, g, i: (b, g)),
                  pl.BlockSpec((seq, wv), lambda b, g, i: (b, g)),
                  _const_spec((TQ, TK))],
        out_specs=pl.BlockSpec((TQ, wv), lambda b, g, i: (b * nq + i, g)),
        out_shape=jax.ShapeDtypeStruct((n, MIX), bf16),
        scratch_shapes=[pltpu.VMEM((GC, seq // TK, TQ, TK), f32),
                        pltpu.VMEM((GC, TQ, LANES), f32), pltpu.VMEM((GC, TQ, LANES), f32),
                        pltpu.VMEM((GC, TQ, LANES), f32)],
        compiler_params=_cparams(("arbitrary", "arbitrary", "arbitrary")),
        name="attn_latent",
    )(qc, kc, vc, mk)


def _rope_apply(x, cosf, sinf, lane):
    w = x.shape[-1]
    partner = jnp.where((lane % LANES) < NOPE + ROPE // 2,
                        pltpu.roll(x, w - ROPE // 2, 1), pltpu.roll(x, ROPE // 2, 1))
    return x * cosf + partner * sinf


def _mla_prep_kernel(xc_ref, cos_ref, sin_ref, glq_ref, glkv_ref, wq_ref, wkv_ref, gq_ref, gk_ref, gr_ref,
                     sd_ref, ex_ref, q_ref, k_ref, v_ref):
    cq = _rms(xc_ref[:, 0:Q_LORA], glq_ref[...]).astype(bf16)
    ckv = _rms(xc_ref[:, Q_LORA:Q_LORA + KV_LORA], glkv_ref[...]).astype(bf16)
    cos8 = jnp.concatenate([cos_ref[...]] * HC, axis=-1)
    sin8 = jnp.concatenate([sin_ref[...]] * HC, axis=-1)
    lane8 = lax.broadcasted_iota(jnp.int32, (TM, HC * LANES), 1)

    q = jnp.dot(cq, wq_ref[...], preferred_element_type=f32)
    q = q * _seg_rinv(q, sd_ref, ex_ref) * gq_ref[...]
    q_ref[...] = _rope_apply(q, cos8, sin8, lane8).astype(bf16)

    kv = jnp.dot(ckv, wkv_ref[...], preferred_element_type=f32)
    v_ref[...] = kv[:, HC * LANES:].astype(bf16)
    xr = xc_ref[:, Q_LORA + KV_LORA:Q_LORA + KV_LORA + LANES]
    kr = xr * lax.rsqrt(jnp.sum(xr * xr, axis=-1, keepdims=True) * (1.0 / ROPE) + EPS) * gr_ref[...]
    kr = pltpu.roll(kr, NOPE, 1)
    lane1 = lax.broadcasted_iota(jnp.int32, (TM, LANES), 1)
    kr = _rope_apply(kr, cos_ref[...], sin_ref[...], lane1)
    for h in range(HC):
        cols = slice(h * LANES, (h + 1) * LANES)
        kn = kv[:, cols]
        kn = kn * lax.rsqrt(jnp.sum(kn * kn, axis=-1, keepdims=True) * (1.0 / NOPE) + EPS) * gk_ref[...]
        k_ref[:, cols] = (kn + kr).astype(bf16)


def _mla_prep(xc, cosf, sinf, glq, glkv, wq, wkv, gq, gk, gr, sd, ex):
    n = xc.shape[0]
    row = lambda w_: pl.BlockSpec((TM, w_), lambda i: (i, 0))
    return pl.pallas_call(
        _mla_prep_kernel,
        grid=(n // TM,),
        in_specs=[row(MIX), row(LANES), row(LANES), _const_spec((1, Q_LORA)), _const_spec((1, KV_LORA)),
                  _const_spec((Q_LORA, HC * LANES)), _const_spec((KV_LORA, HC * LANES + MIX)),
                  _const_spec((1, HC * LANES)), _const_spec((1, LANES)), _const_spec((1, LANES)),
                  _const_spec((HC * LANES, LANES)), _const_spec((2 * LANES, HC * LANES))],
        out_specs=[row(HC * LANES), row(HC * LANES), row(MIX)],
        out_shape=[jax.ShapeDtypeStruct((n, HC * LANES), bf16), jax.ShapeDtypeStruct((n, HC * LANES), bf16),
                   jax.ShapeDtypeStruct((n, MIX), bf16)],
        compiler_params=_cparams(("arbitrary",)),
        name="mla_prep",
    )(xc, cosf, sinf, glq, glkv, wq, wkv, gq, gk, gr, sd, ex)


def _merge_kernel(oa_ref, ob_ref, oc_ref, gt_ref, x_ref, g1_ref, sh_ref, sc_ref, ng_ref, wb_ref, wo_ref,
                  wr_ref, br_ref, x1_ref, h2_ref, meta_ref, cnt_ref, run_ref):
    i = pl.program_id(0)

    @pl.when(i == 0)
    def _():
        run_ref[...] = jnp.zeros_like(run_ref)

    merged = jnp.zeros((TM, D), f32)
    for j, o_ref in enumerate((oa_ref, ob_ref, oc_ref)):
        pj = jnp.dot(o_ref[...], wb_ref[j], preferred_element_type=f32)
        merged = merged + gt_ref[:, j * D:(j + 1) * D].astype(f32) * pj
    y = jnp.dot(merged.astype(bf16), wo_ref[...], preferred_element_type=f32)
    x1 = x_ref[...] + g1_ref[0] * y
    x1_ref[...] = x1
    h2 = _rms(x1, ng_ref[...]) * (1.0 + sc_ref[0]) + sh_ref[0]
    h2_ref[...] = jnp.zeros_like(h2_ref)
    for s in range(D // LANES):
        h2_ref[pl.ds(s, TM, stride=TOKEN_PITCH), :] = h2[:, s * LANES:(s + 1) * LANES]

    lg = jnp.dot(h2, wr_ref[...], preferred_element_type=f32, precision=lax.Precision.HIGHEST) + br_ref[...]
    lane = lax.broadcasted_iota(jnp.int32, (TM, LANES), 1)
    n_e = N_GROUPS * N_EXP
    gmask = (lane >= n_e) & (lane < n_e + N_GROUPS)
    gl = jnp.where(gmask, lg, NEG)
    gmax = jnp.max(gl, axis=-1, keepdims=True)
    g_w = 1.0 / jnp.sum(jnp.where(gmask, jnp.exp(gl - gmax), 0.0), axis=-1, keepdims=True)
    gidx = jnp.min(jnp.where(gl == gmax, lane - n_e, LANES), axis=-1, keepdims=True)
    emask = (lane < n_e) & ((lane // N_EXP) == gidx)
    el = jnp.where(emask, lg, NEG)
    m1 = jnp.max(el, axis=-1, keepdims=True)
    i1 = jnp.min(jnp.where(el == m1, lane, LANES), axis=-1, keepdims=True)
    el2 = jnp.where(lane == i1, NEG, el)
    m2 = jnp.max(el2, axis=-1, keepdims=True)
    i2 = jnp.min(jnp.where(el2 == m2, lane, LANES), axis=-1, keepdims=True)
    t = jnp.exp(m2 - m1)
    w1 = g_w / (1.0 + t)
    w2 = g_w * t / (1.0 + t)
    lo = jnp.minimum(i1, i2)
    hi = jnp.maximum(i1, i2)
    w_lo = jnp.where(i1 < i2, w1, w2)
    w_hi = jnp.where(i1 < i2, w2, w1)
    bucket = gidx * (N_EXP * N_EXP) + (lo % N_EXP) * N_EXP + (hi % N_EXP)

    lane2 = lax.broadcasted_iota(jnp.int32, (TM, 2 * LANES), 1)
    onehot = (lane2 == bucket).astype(f32)
    rr = lax.broadcasted_iota(jnp.int32, (TM, TM), 0)
    cc = lax.broadcasted_iota(jnp.int32, (TM, TM), 1)
    tri = (cc < rr).astype(bf16)
    before = jnp.dot(tri, onehot.astype(bf16), preferred_element_type=f32) + run_ref[...]
    rank = jnp.sum(onehot * before, axis=-1, keepdims=True)
    run_ref[...] = run_ref[...] + jnp.sum(onehot, axis=0, keepdims=True)
    cnt_ref[...] = run_ref[...]

    meta = (jnp.where(lane == i1, w1, 0.0) + jnp.where(lane == i2, w2, 0.0)
            + jnp.where(lane == 32, bucket.astype(f32), 0.0) + jnp.where(lane == 33, rank, 0.0)
            + jnp.where(lane == 34, w_lo, 0.0) + jnp.where(lane == 35, w_hi, 0.0))
    meta_ref[...] = meta
    h2_ref[pl.ds(D // LANES, TM, stride=TOKEN_PITCH), :] = meta


def _merge(oa, ob, oc, gt, x, g1, shift, scale, ng, wb, wo, wr, br, seq):
    n = x.shape[0]
    spb = seq // TM
    row = lambda w_: pl.BlockSpec((TM, w_), lambda i: (i, 0))
    modspec = pl.BlockSpec((1, 1, D), lambda i: (i // spb, 0, 0))
    return pl.pallas_call(
        _merge_kernel,
        grid=(n // TM,),
        in_specs=[row(MIX), row(MIX), row(MIX), row(3 * D), row(D), modspec, modspec, modspec,
                  _const_spec((1, D)), _const_spec((3, MIX, D)), _const_spec((D, D)),
                  _const_spec((D, LANES)), _const_spec((1, LANES))],
        out_specs=[row(D), pl.BlockSpec((TM * TOKEN_PITCH, LANES), lambda i: (i, 0)), row(LANES),
                   _const_spec((1, 2 * LANES))],
        out_shape=[jax.ShapeDtypeStruct((n, D), f32), jax.ShapeDtypeStruct((n * TOKEN_PITCH, LANES), f32),
                   jax.ShapeDtypeStruct((n, LANES), f32), jax.ShapeDtypeStruct((1, 2 * LANES), f32)],
        scratch_shapes=[pltpu.VMEM((1, 2 * LANES), f32)],
        compiler_params=_cparams(("arbitrary",)),
        name="merge_route",
    )(oa, ob, oc, gt, x, g1, shift, scale, ng, wb, wo, wr, br)


TE = 128
N_BUCKETS = N_GROUPS * N_EXP * N_EXP
N_PAIRS = N_GROUPS * (N_EXP * (N_EXP - 1) // 2)


def _num_tiles(n):
    return n // TE + N_PAIRS


TP = 2048


def _tables_kernel(meta_ref, cnt_ref, pos_ref, tt_ref):
    cnt = cnt_ref[...]
    ntile = jnp.floor((cnt + (TE - 1)) * (1.0 / TE))
    bi = lax.broadcasted_iota(jnp.int32, (N_BUCKETS, N_BUCKETS), 0)
    bj = lax.broadcasted_iota(jnp.int32, (N_BUCKETS, N_BUCKETS), 1)
    upper = (bi <= bj).astype(bf16)
    incl = jnp.dot(jnp.broadcast_to(ntile, (8, N_BUCKETS)).astype(bf16), upper, preferred_element_type=f32)[0:1]
    excl = incl - ntile

    meta = meta_ref[...]
    tp = meta.shape[0]
    lane = lax.broadcasted_iota(jnp.int32, (tp, LANES), 1)
    bucket = jnp.sum(jnp.where(lane == 32, meta, 0.0), axis=-1, keepdims=True).astype(jnp.int32)
    rank = jnp.sum(jnp.where(lane == 33, meta, 0.0), axis=-1, keepdims=True)
    lane2 = lax.broadcasted_iota(jnp.int32, (tp, N_BUCKETS), 1)
    first_tile = jnp.sum(jnp.where(lane2 == bucket, excl, 0.0), axis=-1, keepdims=True)
    pos_ref[...] = (first_tile * TE + rank).astype(jnp.int32)

    @pl.when(pl.program_id(0) == 0)
    def _():
        tau = lax.broadcasted_iota(jnp.int32, (N_BUCKETS, 1), 0).astype(f32)
        lane_b = lax.broadcasted_iota(jnp.int32, (1, N_BUCKETS), 1).astype(f32)
        tb = jnp.sum((incl <= tau).astype(f32), axis=-1, keepdims=True)
        last_b = jnp.max(jnp.where(cnt > 0, lane_b, 0.0), axis=-1, keepdims=True)
        tbi = jnp.minimum(tb, last_b).astype(jnp.int32)
        grp = lax.shift_right_logical(tbi, 6) * N_EXP
        ea = grp + (lax.shift_right_logical(tbi, 3) & 7)
        eb = grp + (tbi & 7)
        total = jnp.sum(ntile, axis=-1, keepdims=True).astype(jnp.int32)
        lane_t = lax.broadcasted_iota(jnp.int32, (N_BUCKETS, LANES), 1)
        tt_ref[...] = jnp.where(lane_t == 0, ea, jnp.where(lane_t == 1, eb, jnp.where(lane_t == 2, total, 0)))


def _tables(meta, cnt):
    n = meta.shape[0]
    tp = math.gcd(n, TP)
    assert _num_tiles(n) <= N_BUCKETS
    return pl.pallas_call(
        _tables_kernel,
        grid=(n // tp,),
        in_specs=[pl.BlockSpec((tp, LANES), lambda i: (i, 0)), _const_spec((1, N_BUCKETS))],
        out_specs=[pl.BlockSpec((tp, 1), lambda i: (i, 0)), _const_spec((N_BUCKETS, LANES))],
        out_shape=[jax.ShapeDtypeStruct((n, 1), jnp.int32), jax.ShapeDtypeStruct((N_BUCKETS, LANES), jnp.int32)],
        compiler_params=_cparams(("arbitrary",)),
        name="moe_tables",
    )(meta, cnt)


def _plan_kernel(pos_ref, tok_ref, *, n, nslot):
    def fill(s, _):
        tok_ref[s] = -1
        return 0

    lax.fori_loop(0, nslot, fill, 0, unroll=8)

    def place(t, _):
        tok_ref[pos_ref[t]] = t
        return 0

    lax.fori_loop(0, n, place, 0, unroll=8)


def _plan(pos):
    n = pos.shape[0]
    nslot = (_num_tiles(n) + 1) * TE
    smem = pl.BlockSpec(memory_space=pltpu.SMEM)
    return pl.pallas_call(
        functools.partial(_plan_kernel, n=n, nslot=nslot),
        in_specs=[smem], out_specs=smem,
        out_shape=jax.ShapeDtypeStruct((nslot,), jnp.int32),
        name="moe_plan",
    )(pos)


def _moe_kernel(tea_ref, teb_ref, nt_ref, tok_ref, h_hbm, wg_ref, wu_ref, wd_ref, y_hbm,
                xb0, xb1, ob0, ob1, gsem, ssem, *, n):
    i = pl.program_id(0)
    nt = nt_ref[0]
    xbufs = (xb0, xb1)
    obufs = (ob0, ob1)

    def gather_copy(src_tok, sl, r):
        src = h_hbm.at[pl.ds(pl.multiple_of(src_tok * TOKEN_PITCH, TOKEN_PITCH), SLABS + 1), :]
        return pltpu.make_async_copy(src, xbufs[sl].at[pl.ds(r * VMEM_PITCH, SLABS + 1), :], gsem.at[sl])

    def scatter_copy(dst_tok, sl, r):
        dst = y_hbm.at[pl.ds(pl.multiple_of(dst_tok * Y_PITCH, Y_PITCH), SLABS), :]
        return pltpu.make_async_copy(obufs[sl].at[pl.ds(r * VMEM_PITCH, SLABS), :], dst, ssem.at[sl])

    def start_gathers(tile, sl):
        for r in range(TE):
            gather_copy(jnp.maximum(tok_ref[tile * TE + r], 0), sl, r).start()

    def wait_gathers(sl):
        for r in range(TE):
            gather_copy(0, sl, r).wait()

    def start_scatters(tile, sl, to_spare):
        for r in range(TE):
            t = tok_ref[tile * TE + r]
            scatter_copy(jnp.where((t < 0) | to_spare, n + sl * TE + r, t), sl, r).start(priority=1)

    def wait_scatters(sl):
        for r in range(TE):
            scatter_copy(n, sl, r).wait()

    @pl.when(i == 0)
    def _():
        start_gathers(0, 0)
        for sl in range(2):
            obufs[sl][...] = jnp.zeros_like(obufs[sl])
            spare = pltpu.make_async_copy(obufs[sl].at[pl.ds(0, TE * Y_PITCH), :],
                                          y_hbm.at[pl.ds((n + sl * TE) * Y_PITCH, TE * Y_PITCH), :], ssem.at[sl])
            spare.start()
            spare.wait()

    def step(s):
        o = 1 - s
        wait_gathers(s)

        @pl.when(i >= 1)
        def _():
            wait_scatters(s)

        start_gathers(i + 1, o)
        start_scatters(jnp.maximum(i - 1, 0), o, i == 0)

        xb = xbufs[s]
        slab = lambda k: xb[pl.ds(k, TE, stride=VMEM_PITCH), :]
        hb = jnp.concatenate([slab(k) for k in range(SLABS)], axis=-1).astype(bf16)
        meta = slab(SLABS)
        lane = lax.broadcasted_iota(jnp.int32, (TE, LANES), 1)
        w_lo = jnp.sum(jnp.where(lane == 34, meta, 0.0), axis=-1, keepdims=True)
        w_hi = jnp.sum(jnp.where(lane == 35, meta, 0.0), axis=-1, keepdims=True)

        def ffn(e, wcol):
            a = jnp.dot(hb, wg_ref[0, e], preferred_element_type=f32)
            u = jnp.dot(hb, wu_ref[0, e], preferred_element_type=f32)
            act = (a * jax.nn.sigmoid(a)) * u * wcol
            return jnp.dot(act.astype(bf16), wd_ref[0, e], preferred_element_type=f32)

        out = ffn(lax.rem(tea_ref[i], N_EXP), w_lo) + ffn(lax.rem(teb_ref[i], N_EXP), w_hi)
        for k in range(SLABS):
            obufs[s][pl.ds(k, TE, stride=VMEM_PITCH), :] = out[:, k * LANES:(k + 1) * LANES]

        @pl.when(i == nt - 1)
        def _():
            start_scatters(i, s, False)
            wait_scatters(o)
            wait_scatters(s)
            wait_gathers(o)

    for s in range(2):
        @pl.when((i < nt) & (lax.rem(i, 2) == s))
        def _(s=s):
            step(s)


def _moe(h2ext, tok, tea, teb, nt, wg, wu, wd):
    n = h2ext.shape[0] // TOKEN_PITCH
    ntmax = _num_tiles(n)
    wspec_in = pl.BlockSpec((1, N_EXP, D, F_EXP), lambda i, tea, teb, nt, tok: (tea[i] // N_EXP, 0, 0, 0))
    wspec_out = pl.BlockSpec((1, N_EXP, F_EXP, D), lambda i, tea, teb, nt, tok: (tea[i] // N_EXP, 0, 0, 0))
    return pl.pallas_call(
        functools.partial(_moe_kernel, n=n),
        grid_spec=pltpu.PrefetchScalarGridSpec(
            num_scalar_prefetch=4,
            grid=(ntmax,),
            in_specs=[pl.BlockSpec(memory_space=pl.ANY), wspec_in, wspec_in, wspec_out],
            out_specs=pl.BlockSpec(memory_space=pl.ANY),
            scratch_shapes=[pltpu.VMEM((TE * VMEM_PITCH, LANES), f32)] * 4 + [
                pltpu.SemaphoreType.DMA((2,)), pltpu.SemaphoreType.DMA((2,))]),
        out_shape=jax.ShapeDtypeStruct(((n + 2 * TE) * Y_PITCH, LANES), f32),
        compiler_params=_cparams(("arbitrary",)),
        name="moe_sparse",
    )(tea, teb, nt, tok, h2ext, wg, wu, wd)


def _final_kernel(x_ref, y_ref, g_ref, o_ref):
    o_ref[...] = x_ref[...] + g_ref[0] * _rows_from_token_major(y_ref, x_ref.shape[0])


def _final_combine(x1, y, g2, seq):
    n = x1.shape[0]
    t = 512
    spb = seq // t
    row = pl.BlockSpec((t, D), lambda i: (i, 0))
    return pl.pallas_call(
        _final_kernel,
        grid=(n // t,),
        in_specs=[row, pl.BlockSpec((t * Y_PITCH, LANES), lambda i: (i, 0)),
                  pl.BlockSpec((1, 1, D), lambda i: (i // spb, 0, 0))],
        out_specs=row,
        out_shape=jax.ShapeDtypeStruct((n, D), f32),
        compiler_params=_cparams(("arbitrary",)),
        name="final_combine",
    )(x1, y, g2)


def _seg_matrices(width, segs):
    sd = np.zeros((width, LANES), np.float32)
    ex = np.zeros((LANES, width), np.float32)
    for j, (s, ln) in enumerate(segs):
        sd[s:s + ln, j] = 1.0 / ln
        ex[j, s:s + ln] = 1.0
    return jnp.asarray(sd, bf16), jnp.asarray(np.concatenate([ex, ex], axis=0), bf16)


def _head_pad(w, heads, per_head, keep):
    k = w.shape[0]
    w3 = w.reshape(k, heads, per_head)[:, :, :keep]
    return jnp.pad(w3, ((0, 0), (0, 0), (0, LANES - keep))).reshape(k, heads * LANES)


def kernel(x, c, positions, w_ada, b_ada, norm_g, w_in, diff_qk_g, diff_lambda, diff_out_g, rel_bias, sgu_v_g, sgu_w, sgu_b, mla_lat_g, mla_w_uq, mla_w_ukv, mla_qk_g, w_branch, w_out, router_g_w, router_g_b, router_e_w, router_e_b, w_e_gate, w_e_up, w_e_down):
    nb, seq, _ = x.shape
    n = nb * seq
    assert seq % TQ == 0 and seq % TM == 0 and x.shape[2] == D

    mod = _ada(c, w_ada, b_ada)
    cosf, sinf = _rope_tables(positions)
    t0, t1, mk = _bias_tiles(rel_bias)

    sd_a, ex_a = _seg_matrices(MIX, [(s * 64, 64) for s in range(8)])
    segs_q = []
    for h in range(HC):
        segs_q += [(h * LANES, NOPE), (h * LANES + NOPE, ROPE)]
    sd_q, ex_q = _seg_matrices(HC * LANES, segs_q)
    ne = N_GROUPS * N_EXP
    zpad = lambda a, w_: jnp.pad(a, ((0, 0), (0, w_ - a.shape[1])))

    xcur = x.reshape(n, D)
    y_prev, g2_prev = None, None
    for l in range(DEPTH):
        m3 = mod[l].reshape(nb, 1, 6 * D)
        shift1, scale1, gate1, shift2, scale2, gate2 = [m3[:, :, k * D:(k + 1) * D] for k in range(6)]
        lambda_init = LAMBDA_INIT_BASE - LAMBDA_INIT_SCALE * math.exp(-LAMBDA_INIT_DECAY * l)

        w = w_in[l]
        w_pad = jnp.concatenate([w[:, :XC_OFF + 416], jnp.zeros((D, IN_PAD - IN_REAL), f32), w[:, XC_OFF + 416:]],
                                axis=1).astype(bf16)
        gq = (jnp.tile(diff_qk_g[l, 0], 8) * (64 ** -0.5 * LOG2E)).reshape(1, MIX)
        gk = jnp.tile(diff_qk_g[l, 1], 8).reshape(1, MIX)
        outs = _inproj(xcur, y_prev, g2_prev, shift1, scale1, norm_g[l, 0].reshape(1, D), w_pad, gq, gk,
                       sd_a, ex_a, sgu_v_g[l].reshape(1, MIX), sgu_w[l], jnp.transpose(sgu_b[l]), seq)
        if y_prev is not None:
            xcur = outs[0]
            outs = outs[1:]
        qa, ka, va, ob, xc, gt = outs

        oa = _attn_a(qa, ka, va, t0, t1, diff_lambda[l], diff_out_g[l].reshape(1, LANES), nb, seq, lambda_init)

        qkg = mla_qk_g[l]
        wq = _head_pad(mla_w_uq[l], HC, NOPE + ROPE, NOPE + ROPE).astype(bf16)
        wkv3 = mla_w_ukv[l].reshape(KV_LORA, HC, NOPE + VC)
        wk = jnp.pad(wkv3[:, :, :NOPE], ((0, 0), (0, 0), (0, LANES - NOPE))).reshape(KV_LORA, HC * LANES)
        wv = wkv3[:, :, NOPE:].reshape(KV_LORA, HC * VC)
        wkv = jnp.concatenate([wk, wv], axis=1).astype(bf16)
        gq_c = jnp.tile(jnp.pad(qkg[0], (0, LANES - NOPE - ROPE)), HC).reshape(1, HC * LANES) * ((NOPE + ROPE) ** -0.5 * LOG2E)
        gk_c = jnp.pad(qkg[1, :NOPE], (0, LANES - NOPE)).reshape(1, LANES)
        gr_c = jnp.pad(qkg[1, NOPE:], (0, LANES - ROPE)).reshape(1, LANES)
        qc, kc, vc = _mla_prep(xc, cosf, sinf, mla_lat_g[l, :Q_LORA].reshape(1, Q_LORA),
                               mla_lat_g[l, Q_LORA:].reshape(1, KV_LORA), wq, wkv, gq_c, gk_c, gr_c, sd_q, ex_q)
        oc = _attn_c(qc, kc, vc, mk, nb, seq)

        wr = zpad(jnp.concatenate([router_e_w[l], router_g_w[l]], axis=1), LANES)
        br = zpad(jnp.concatenate([router_e_b[l], router_g_b[l]]).reshape(1, ne + N_GROUPS), LANES)
        x1, h2ext, meta, cnt = _merge(oa, ob, oc, gt, xcur, gate1, shift2, scale2, norm_g[l, 1].reshape(1, D),
                                      w_branch[l].astype(bf16), w_out[l].astype(bf16), wr, br, seq)

        pos, tt = _tables(meta, cnt)
        tok = _plan(pos.reshape(n))
        ntmax = _num_tiles(n)
        y_prev = _moe(h2ext, tok, tt[:ntmax, 0], tt[:ntmax, 1], tt[0:1, 2], w_e_gate[l].astype(bf16),
                      w_e_up[l].astype(bf16), w_e_down[l].astype(bf16))
        g2_prev = gate2
        xcur = x1

    out = _final_combine(xcur, y_prev, g2_prev, seq)
    return out.reshape(nb, seq, D)
```

```python
import functools
import math

import jax
import jax.numpy as jnp
import numpy as np
from jax import lax
from jax.experimental import pallas as pl
from jax.experimental.pallas import tpu as pltpu

f32 = jnp.float32
bf16 = jnp.bfloat16

D = 1024
DEPTH = 4
CHUNK = 64
MIX = 512
HA = 4
HC = 8
Q_LORA, KV_LORA, ROPE = 256, 128, 32
NOPE, VC = 64, 64
N_GROUPS, N_EXP, F_EXP = 4, 8, 256
N_REL_BUCKETS = 32
ROPE_THETA = 10000.0
LAMBDA_INIT_BASE, LAMBDA_INIT_SCALE, LAMBDA_INIT_DECAY = 0.8, 0.6, 0.3
EPS = 1e-6
NEG = -1e30

IN_REAL = 6048
IN_PAD = 6144
XC_OFF = 2560
GATE_OFF = 3072

TM = 512
TQ = 256
TK = 256
LANES = 128
VMEM_LIMIT = 56 * 1024 * 1024
SLABS = D // LANES
PACK_SLABS = SLABS // 2
TOKEN_PITCH = 8
Y_PITCH = 8
X_PITCH = PACK_SLABS + 1
O_PITCH = SLABS + 1


def _cparams(sem, vmem=VMEM_LIMIT):
    return pltpu.CompilerParams(dimension_semantics=sem, vmem_limit_bytes=vmem)


def _const_spec(shape):
    nd = len(shape)
    return pl.BlockSpec(shape, lambda *_: (0,) * nd)


def _rms(x, g_row):
    return x * lax.rsqrt(jnp.mean(x * x, axis=-1, keepdims=True) + EPS) * g_row


def _rows_from_token_major(y_ref, rows):
    return jnp.concatenate([y_ref[pl.ds(k, rows, stride=Y_PITCH), :] for k in range(SLABS)], axis=-1)


def _seg_rinv(x, sd_ref, ex_ref):
    ss = jnp.dot((x * x).astype(bf16), sd_ref[...], preferred_element_type=f32)
    hi = ss.astype(bf16)
    lo = (ss - hi.astype(f32)).astype(bf16)
    full = jnp.dot(jnp.concatenate([hi, lo], axis=-1), ex_ref[...], preferred_element_type=f32)
    return lax.rsqrt(full + EPS)


def _ada_kernel(c_ref, w_ref, b_ref, o_ref):
    c = c_ref[...]
    ca = (c * jax.nn.sigmoid(c)).astype(bf16)
    o_ref[0] = jnp.dot(ca, w_ref[0].astype(bf16), preferred_element_type=f32) + b_ref[0]


def _ada(c, w_ada, b_ada):
    nb = c.shape[0]
    tn = 1024
    return pl.pallas_call(
        _ada_kernel,
        grid=(DEPTH, 6 * D // tn),
        in_specs=[pl.BlockSpec((nb, D), lambda l, j: (0, 0)),
                  pl.BlockSpec((1, D, tn), lambda l, j: (l, 0, j)),
                  pl.BlockSpec((1, 1, tn), lambda l, j: (l, 0, j))],
        out_specs=pl.BlockSpec((1, nb, tn), lambda l, j: (l, 0, j)),
        out_shape=jax.ShapeDtypeStruct((DEPTH, nb, 6 * D), f32),
        compiler_params=_cparams(("arbitrary", "arbitrary")),
        name="ada_mod",
    )(c, w_ada, b_ada.reshape(DEPTH, 1, 6 * D))


def _rope_kernel(pos_ref, invf_ref, cos_ref, sin_ref):
    ang = pos_ref[...].astype(f32) * invf_ref[...]
    lane = lax.broadcasted_iota(jnp.int32, ang.shape, 1)
    rot = (lane >= NOPE) & (lane < NOPE + ROPE)
    first = lane < NOPE + ROPE // 2
    cos_ref[...] = jnp.where(rot, jnp.cos(ang), 1.0)
    s = jnp.sin(ang)
    sin_ref[...] = jnp.where(rot, jnp.where(first, -s, s), 0.0)


def _rope_tables(positions):
    n = positions.size
    inv_freq = ROPE_THETA ** (-jnp.arange(0, ROPE, 2, dtype=f32) / ROPE)
    invf = jnp.zeros((1, LANES), f32).at[0, NOPE:NOPE + ROPE].set(jnp.tile(inv_freq, 2))
    t = math.gcd(n, 2048)
    return pl.pallas_call(
        _rope_kernel,
        grid=(n // t,),
        in_specs=[pl.BlockSpec((t, 1), lambda i: (i, 0)), _const_spec((1, LANES))],
        out_specs=[pl.BlockSpec((t, LANES), lambda i: (i, 0))] * 2,
        out_shape=[jax.ShapeDtypeStruct((n, LANES), f32)] * 2,
        compiler_params=_cparams(("arbitrary",)),
        name="rope_tables",
    )(positions.reshape(n, 1), invf)


def _t5_bucket(rel):
    n = jnp.abs(rel)
    large = 8 + sum((n >= t).astype(jnp.int32) for t in (12, 16, 23, 32, 46, 64, 91))
    return jnp.where(rel > 0, 16, 0) + jnp.where(n < 8, n, large)


def _bias_kernel(tbl_ref, t0_ref, t1_ref, mk_ref):
    r = lax.broadcasted_iota(jnp.int32, (TQ, TK), 0)
    c = lax.broadcasted_iota(jnp.int32, (TQ, TK), 1)
    allowed = (c // CHUNK) <= (r // CHUNK)
    mk = jnp.where(allowed, 0.0, NEG).astype(f32)
    mk_ref[...] = mk
    b0 = _t5_bucket(c - r)
    b1 = _t5_bucket(c - r - TK)
    for h in range(HA):
        far = tbl_ref[N_REL_BUCKETS // 2 - 1, h]
        t0 = jnp.zeros((TQ, TK), f32)
        t1 = jnp.zeros((TQ, TK), f32)
        for b in range(N_REL_BUCKETS):
            v = (tbl_ref[b, h] - far) * LOG2E
            t0 = jnp.where(b0 == b, v, t0)
            t1 = jnp.where(b1 == b, v, t1)
        t0 = t0 + mk
        t0_ref[h, 0:TQ, :] = t0
        t0_ref[h, TQ:2 * TQ, :] = t0
        t1_ref[h, 0:TQ, :] = t1
        t1_ref[h, TQ:2 * TQ, :] = t1


def _bias_tiles(rel_bias):
    return pl.pallas_call(
        _bias_kernel,
        in_specs=[pl.BlockSpec(memory_space=pltpu.SMEM)],
        out_specs=[pl.BlockSpec(memory_space=pltpu.VMEM)] * 3,
        out_shape=[jax.ShapeDtypeStruct((HA, 2 * TQ, TK), f32),
                   jax.ShapeDtypeStruct((HA, 2 * TQ, TK), f32),
                   jax.ShapeDtypeStruct((TQ, TK), f32)],
        name="bias_tiles",
    )(rel_bias)


def _inproj_kernel(*refs, combine):
    if combine:
        (xa_ref, y_ref, g2_ref, sh_ref, sc_ref, ng_ref, w_ref, gq_ref, gk_ref, sd_ref, ex_ref,
         vg_ref, ws_ref, bs_ref,
         xo_ref, qa_ref, ka_ref, va_ref, ob_ref, xc_ref, gt_ref) = refs
        x = xa_ref[...] + g2_ref[0] * _rows_from_token_major(y_ref, TM)
        xo_ref[...] = x
    else:
        (xa_ref, sh_ref, sc_ref, ng_ref, w_ref, gq_ref, gk_ref, sd_ref, ex_ref,
         vg_ref, ws_ref, bs_ref,
         qa_ref, ka_ref, va_ref, ob_ref, xc_ref, gt_ref) = refs
        x = xa_ref[...]
    h = _rms(x, ng_ref[...]) * (1.0 + sc_ref[0]) + sh_ref[0]
    hb = h.astype(bf16)

    q = jnp.dot(hb, w_ref[:, 0:MIX], preferred_element_type=f32)
    qa_ref[...] = (q * _seg_rinv(q, sd_ref, ex_ref) * gq_ref[...]).astype(bf16)
    k = jnp.dot(hb, w_ref[:, MIX:2 * MIX], preferred_element_type=f32)
    ka_ref[...] = (k * _seg_rinv(k, sd_ref, ex_ref) * gk_ref[...]).astype(bf16)
    va_ref[...] = jnp.dot(hb, w_ref[:, 2 * MIX:3 * MIX], preferred_element_type=f32).astype(bf16)

    z = jnp.dot(hb, w_ref[:, 3 * MIX:5 * MIX], preferred_element_type=f32)
    z = 0.5 * z * (1.0 + jnp.tanh(math.sqrt(2.0 / math.pi) * (z + 0.044715 * (z * z * z))))
    u = z[:, :MIX]
    v = _rms(z[:, MIX:], vg_ref[...]).astype(bf16)
    ri = lax.broadcasted_iota(jnp.int32, (LANES, LANES), 0)
    ci = lax.broadcasted_iota(jnp.int32, (LANES, LANES), 1)
    allowed = (ci // CHUNK) <= (ri // CHUNK)
    for g in range(4):
        wm = jnp.where(allowed, ws_ref[g], 0.0).astype(bf16)
        bcol = bs_ref[:, g:g + 1]
        for wd in range(TM // LANES):
            rows = slice(wd * LANES, (wd + 1) * LANES)
            cols = slice(g * LANES, (g + 1) * LANES)
            vs = jnp.dot(wm, v[rows, cols], preferred_element_type=f32) + bcol
            ob_ref[rows, cols] = (u[rows, cols] * vs).astype(bf16)

    xc_ref[...] = jnp.dot(hb, w_ref[:, XC_OFF:GATE_OFF], preferred_element_type=f32)

    for j in range(3):
        gsl = slice(GATE_OFF + j * D, GATE_OFF + (j + 1) * D)
        gl = jnp.dot(hb, w_ref[:, gsl], preferred_element_type=f32)
        gt_ref[:, j * D:(j + 1) * D] = jax.nn.sigmoid(gl).astype(bf16)


def _inproj(xa, y, g2, shift, scale, ng, w, gq, gk, sd, ex, vg, ws, bs, seq):
    n = xa.shape[0]
    spb = seq // TM
    combine = y is not None
    row = lambda w_: pl.BlockSpec((TM, w_), lambda i: (i, 0))
    modspec = pl.BlockSpec((1, 1, D), lambda i: (i // spb, 0, 0))
    in_specs = [row(D)]
    args = [xa]
    if combine:
        in_specs += [pl.BlockSpec((TM * Y_PITCH, LANES), lambda i: (i, 0)), modspec]
        args += [y, g2]
    in_specs += [modspec, modspec, _const_spec((1, D)),
                 pl.BlockSpec((D, IN_PAD), lambda i: (0, 0), pipeline_mode=pl.Buffered(1)),
                 _const_spec((1, MIX)), _const_spec((1, MIX)),
                 _const_spec((MIX, LANES)), _const_spec((2 * LANES, MIX)),
                 _const_spec((1, MIX)), _const_spec((4, LANES, LANES)), _const_spec((LANES, 4))]
    args += [shift, scale, ng, w, gq, gk, sd, ex, vg, ws, bs]
    out_specs = [row(MIX)] * 4 + [row(MIX), row(3 * D)]
    out_shape = [jax.ShapeDtypeStruct((n, MIX), bf16)] * 4 + [
        jax.ShapeDtypeStruct((n, MIX), f32), jax.ShapeDtypeStruct((n, 3 * D), bf16)]
    if combine:
        out_specs = [row(D)] + out_specs
        out_shape = [jax.ShapeDtypeStruct((n, D), f32)] + out_shape
    return pl.pallas_call(
        functools.partial(_inproj_kernel, combine=combine),
        grid=(n // TM,),
        in_specs=in_specs, out_specs=out_specs, out_shape=out_shape,
        compiler_params=_cparams(("arbitrary",)),
        name="inproj",
    )(*args)


LOG2E = math.log2(math.e)


def _two_pass_attention(nunits, q_of, kcols_of, vcols_of, k_ref, v_ref, qi, diag_bias_of, sub_bias_of,
                        sbuf, mbuf, lbuf, abuf):
    def score_block(j, bias_of):
        rows_k = pl.ds(pl.multiple_of(j * TK, TK), TK)
        for u in range(nunits):
            s = lax.dot_general(q_of(u), k_ref[rows_k, kcols_of(u)], (((1,), (1,)), ((), ())),
                                preferred_element_type=f32)
            if bias_of is not None:
                s = s + bias_of(u)
            sbuf[u, j] = s
            mbuf[u] = jnp.maximum(mbuf[u], jnp.maximum(s[:, :LANES], s[:, LANES:]))

    for u in range(nunits):
        mbuf[u] = jnp.full(mbuf.shape[1:], NEG, f32)
    if sub_bias_of is None:
        nfar = qi
    else:
        nfar = jnp.maximum(qi - 1, 0)

        @pl.when(qi >= 1)
        def _():
            score_block(qi - 1, sub_bias_of)

    def far(j, _):
        score_block(j, None)
        return 0

    lax.fori_loop(0, nfar, far, 0)
    score_block(qi, diag_bias_of)

    for u in range(nunits):
        mbuf[u] = jnp.broadcast_to(jnp.max(mbuf[u], axis=-1, keepdims=True), mbuf.shape[1:])
        lbuf[u] = jnp.zeros(lbuf.shape[1:], f32)
        abuf[u] = jnp.zeros(abuf.shape[1:], f32)

    def accumulate(j, _):
        rows_k = pl.ds(pl.multiple_of(j * TK, TK), TK)
        for u in range(nunits):
            s = sbuf[u, j]
            mb = mbuf[u]
            p0 = jnp.exp2(s[:, :LANES] - mb)
            p1 = jnp.exp2(s[:, LANES:] - mb)
            lbuf[u] = lbuf[u] + (p0 + p1)
            p = jnp.concatenate([p0, p1], axis=-1).astype(bf16)
            abuf[u] = abuf[u] + jnp.dot(p, v_ref[rows_k, vcols_of(u)], preferred_element_type=f32)
        return 0

    lax.fori_loop(0, qi + 1, accumulate, 0)


def _normalised(u, lbuf, abuf):
    return abuf[u] / jnp.sum(lbuf[u], axis=-1, keepdims=True)


GA = 2
GC = 4


def _attn_a_kernel(q_ref, k_ref, v_ref, t0_ref, t1_ref, dl_ref, og_ref, o_ref,
                   qstk, sbuf, mbuf, lbuf, abuf, *, lambda_init):
    qi = pl.program_id(2)
    dl = dl_ref[...]
    lam = (jnp.exp(jnp.sum(dl[0:1] * dl[1:2], axis=-1, keepdims=True))
           - jnp.exp(jnp.sum(dl[2:3] * dl[3:4], axis=-1, keepdims=True)) + lambda_init)
    lane = lax.broadcasted_iota(jnp.int32, (TQ, LANES), 1)
    cols = lambda u: slice(u * LANES, (u + 1) * LANES)
    for u in range(GA):
        qh = q_ref[:, cols(u)]
        zero = jnp.zeros_like(qh)
        qstk[u, 0:TQ, :] = jnp.where(lane < 64, qh, zero)
        qstk[u, TQ:2 * TQ, :] = jnp.where(lane >= 64, qh, zero)
    _two_pass_attention(GA, lambda u: qstk[u], cols, cols, k_ref, v_ref, qi,
                        lambda u: t0_ref[u], lambda u: t1_ref[u], sbuf, mbuf, lbuf, abuf)
    for u in range(GA):
        o = _normalised(u, lbuf, abuf)
        oh = o[:TQ] - lam * o[TQ:]
        oh = _rms(oh, og_ref[...]) * (1.0 - lambda_init)
        o_ref[:, cols(u)] = oh.astype(bf16)


def _attn_a(qa, ka, va, t0, t1, dl, og, nb, seq, lambda_init):
    n = qa.shape[0]
    nq = seq // TQ
    w = GA * LANES
    return pl.pallas_call(
        functools.partial(_attn_a_kernel, lambda_init=lambda_init),
        grid=(nb, HA // GA, nq),
        in_specs=[pl.BlockSpec((TQ, w), lambda b, g, i: (b * nq + i, g)),
                  pl.BlockSpec((seq, w), lambda b, g, i: (b, g)),
                  pl.BlockSpec((seq, w), lambda b, g, i: (b, g)),
                  pl.BlockSpec((GA, 2 * TQ, TK), lambda b, g, i: (g, 0, 0)),
                  pl.BlockSpec((GA, 2 * TQ, TK), lambda b, g, i: (g, 0, 0)),
                  _const_spec((4, 64)), _const_spec((1, LANES))],
        out_specs=pl.BlockSpec((TQ, w), lambda b, g, i: (b * nq + i, g)),
        out_shape=jax.ShapeDtypeStruct((n, MIX), bf16),
        scratch_shapes=[pltpu.VMEM((GA, 2 * TQ, LANES), bf16),
                        pltpu.VMEM((GA, seq // TK, 2 * TQ, TK), f32),
                        pltpu.VMEM((GA, 2 * TQ, LANES), f32), pltpu.VMEM((GA, 2 * TQ, LANES), f32),
                        pltpu.VMEM((GA, 2 * TQ, LANES), f32)],
        compiler_params=_cparams(("arbitrary", "arbitrary", "arbitrary")),
        name="attn_diff",
    )(qa, ka, va, t0, t1, dl, og)


def _attn_c_kernel(q_ref, k_ref, v_ref, mk_ref, o_ref, sbuf, mbuf, lbuf, abuf):
    qi = pl.program_id(2)
    lane = lax.broadcasted_iota(jnp.int32, (TQ, LANES), 1)
    cols = lambda u: slice(u * LANES, (u + 1) * LANES)
    _two_pass_attention(GC, lambda u: q_ref[:, cols(u)], cols, lambda u: cols(u // 2), k_ref, v_ref, qi,
                        lambda u: mk_ref[...], None, sbuf, mbuf, lbuf, abuf)
    for hp in range(GC // 2):
        o_ref[:, cols(hp)] = jnp.where(lane < VC, _normalised(2 * hp, lbuf, abuf),
                                       _normalised(2 * hp + 1, lbuf, abuf)).astype(bf16)


def _attn_c(qc, kc, vc, mk, nb, seq):
    n = qc.shape[0]
    nq = seq // TQ
    w = GC * LANES
    wv = GC * VC
    return pl.pallas_call(
        _attn_c_kernel,
        grid=(nb, HC // GC, nq),
        in_specs=[pl.BlockSpec((TQ, w), lambda b, g, i: (b * nq + i, g)),
                  pl.BlockSpec((seq, w), lambda b, g, i: (b, g)),
                  pl.BlockSpec((seq, wv), lambda b, g, i: (b, g)),
                  _const_spec((TQ, TK))],
        out_specs=pl.BlockSpec((TQ, wv), lambda b, g, i: (b * nq + i, g)),
        out_shape=jax.ShapeDtypeStruct((n, MIX), bf16),
        scratch_shapes=[pltpu.VMEM((GC, seq // TK, TQ, TK), f32),
                        pltpu.VMEM((GC, TQ, LANES), f32), pltpu.VMEM((GC, TQ, LANES), f32),
                        pltpu.VMEM((GC, TQ, LANES), f32)],
        compiler_params=_cparams(("arbitrary", "arbitrary", "arbitrary")),
        name="attn_latent",
    )(qc, kc, vc, mk)


def _rope_apply(x, cosf, sinf, lane):
    w = x.shape[-1]
    partner = jnp.where((lane % LANES) < NOPE + ROPE // 2,
                        pltpu.roll(x, w - ROPE // 2, 1), pltpu.roll(x, ROPE // 2, 1))
    return x * cosf + partner * sinf


def _mla_prep_kernel(xc_ref, cos_ref, sin_ref, glq_ref, glkv_ref, wq_ref, wkv_ref, gq_ref, gk_ref, gr_ref,
                     sd_ref, ex_ref, q_ref, k_ref, v_ref):
    cq = _rms(xc_ref[:, 0:Q_LORA], glq_ref[...]).astype(bf16)
    ckv = _rms(xc_ref[:, Q_LORA:Q_LORA + KV_LORA], glkv_ref[...]).astype(bf16)
    cos8 = jnp.concatenate([cos_ref[...]] * HC, axis=-1)
    sin8 = jnp.concatenate([sin_ref[...]] * HC, axis=-1)
    lane8 = lax.broadcasted_iota(jnp.int32, (TM, HC * LANES), 1)

    q = jnp.dot(cq, wq_ref[...], preferred_element_type=f32)
    q = q * _seg_rinv(q, sd_ref, ex_ref) * gq_ref[...]
    q_ref[...] = _rope_apply(q, cos8, sin8, lane8).astype(bf16)

    kv = jnp.dot(ckv, wkv_ref[...], preferred_element_type=f32)
    v_ref[...] = kv[:, HC * LANES:].astype(bf16)
    xr = xc_ref[:, Q_LORA + KV_LORA:Q_LORA + KV_LORA + LANES]
    kr = xr * lax.rsqrt(jnp.sum(xr * xr, axis=-1, keepdims=True) * (1.0 / ROPE) + EPS) * gr_ref[...]
    kr = pltpu.roll(kr, NOPE, 1)
    lane1 = lax.broadcasted_iota(jnp.int32, (TM, LANES), 1)
    kr = _rope_apply(kr, cos_ref[...], sin_ref[...], lane1)
    for h in range(HC):
        cols = slice(h * LANES, (h + 1) * LANES)
        kn = kv[:, cols]
        kn = kn * lax.rsqrt(jnp.sum(kn * kn, axis=-1, keepdims=True) * (1.0 / NOPE) + EPS) * gk_ref[...]
        k_ref[:, cols] = (kn + kr).astype(bf16)


def _mla_prep(xc, cosf, sinf, glq, glkv, wq, wkv, gq, gk, gr, sd, ex):
    n = xc.shape[0]
    row = lambda w_: pl.BlockSpec((TM, w_), lambda i: (i, 0))
    return pl.pallas_call(
        _mla_prep_kernel,
        grid=(n // TM,),
        in_specs=[row(MIX), row(LANES), row(LANES), _const_spec((1, Q_LORA)), _const_spec((1, KV_LORA)),
                  _const_spec((Q_LORA, HC * LANES)), _const_spec((KV_LORA, HC * LANES + MIX)),
                  _const_spec((1, HC * LANES)), _const_spec((1, LANES)), _const_spec((1, LANES)),
                  _const_spec((HC * LANES, LANES)), _const_spec((2 * LANES, HC * LANES))],
        out_specs=[row(HC * LANES), row(HC * LANES), row(MIX)],
        out_shape=[jax.ShapeDtypeStruct((n, HC * LANES), bf16), jax.ShapeDtypeStruct((n, HC * LANES), bf16),
                   jax.ShapeDtypeStruct((n, MIX), bf16)],
        compiler_params=_cparams(("arbitrary",)),
        name="mla_prep",
    )(xc, cosf, sinf, glq, glkv, wq, wkv, gq, gk, gr, sd, ex)


def _merge_kernel(oa_ref, ob_ref, oc_ref, gt_ref, x_ref, g1_ref, sh_ref, sc_ref, ng_ref, wb_ref, wo_ref,
                  wr_ref, br_ref, x1_ref, h2_ref, meta_ref, cnt_ref, run_ref):
    i = pl.program_id(0)

    @pl.when(i == 0)
    def _():
        run_ref[...] = jnp.zeros_like(run_ref)

    merged = jnp.zeros((TM, D), f32)
    for j, o_ref in enumerate((oa_ref, ob_ref, oc_ref)):
        pj = jnp.dot(o_ref[...], wb_ref[j], preferred_element_type=f32)
        merged = merged + gt_ref[:, j * D:(j + 1) * D].astype(f32) * pj
    y = jnp.dot(merged.astype(bf16), wo_ref[...], preferred_element_type=f32)
    x1 = x_ref[...] + g1_ref[0] * y
    x1_ref[...] = x1
    h2 = _rms(x1, ng_ref[...]) * (1.0 + sc_ref[0]) + sh_ref[0]
    hr = h2.astype(bf16).astype(f32)
    lo = lax.shift_right_logical(lax.bitcast_convert_type(hr[:, 0:D // 2], jnp.uint32), jnp.uint32(16))
    hi = lax.bitcast_convert_type(hr[:, D // 2:D], jnp.uint32) & jnp.uint32(0xFFFF0000)
    words = lax.bitcast_convert_type(lo | hi, f32)
    h2_ref[...] = jnp.zeros_like(h2_ref)
    for s in range(PACK_SLABS):
        h2_ref[pl.ds(s, TM, stride=TOKEN_PITCH), :] = words[:, s * LANES:(s + 1) * LANES]

    lg = jnp.dot(h2, wr_ref[...], preferred_element_type=f32, precision=lax.Precision.HIGHEST) + br_ref[...]
    lane = lax.broadcasted_iota(jnp.int32, (TM, LANES), 1)
    n_e = N_GROUPS * N_EXP
    gmask = (lane >= n_e) & (lane < n_e + N_GROUPS)
    gl = jnp.where(gmask, lg, NEG)
    gmax = jnp.max(gl, axis=-1, keepdims=True)
    g_w = 1.0 / jnp.sum(jnp.where(gmask, jnp.exp(gl - gmax), 0.0), axis=-1, keepdims=True)
    gidx = jnp.min(jnp.where(gl == gmax, lane - n_e, LANES), axis=-1, keepdims=True)
    emask = (lane < n_e) & ((lane // N_EXP) == gidx)
    el = jnp.where(emask, lg, NEG)
    m1 = jnp.max(el, axis=-1, keepdims=True)
    i1 = jnp.min(jnp.where(el == m1, lane, LANES), axis=-1, keepdims=True)
    el2 = jnp.where(lane == i1, NEG, el)
    m2 = jnp.max(el2, axis=-1, keepdims=True)
    i2 = jnp.min(jnp.where(el2 == m2, lane, LANES), axis=-1, keepdims=True)
    t = jnp.exp(m2 - m1)
    w1 = g_w / (1.0 + t)
    w2 = g_w * t / (1.0 + t)
    lo = jnp.minimum(i1, i2)
    hi = jnp.maximum(i1, i2)
    w_lo = jnp.where(i1 < i2, w1, w2)
    w_hi = jnp.where(i1 < i2, w2, w1)
    bucket = gidx * (N_EXP * N_EXP) + (lo % N_EXP) * N_EXP + (hi % N_EXP)

    lane2 = lax.broadcasted_iota(jnp.int32, (TM, 2 * LANES), 1)
    onehot = (lane2 == bucket).astype(f32)
    rr = lax.broadcasted_iota(jnp.int32, (TM, TM), 0)
    cc = lax.broadcasted_iota(jnp.int32, (TM, TM), 1)
    tri = (cc < rr).astype(bf16)
    before = jnp.dot(tri, onehot.astype(bf16), preferred_element_type=f32) + run_ref[...]
    rank = jnp.sum(onehot * before, axis=-1, keepdims=True)
    run_ref[...] = run_ref[...] + jnp.sum(onehot, axis=0, keepdims=True)
    cnt_ref[...] = run_ref[...]

    meta = (jnp.where(lane == i1, w1, 0.0) + jnp.where(lane == i2, w2, 0.0)
            + jnp.where(lane == 32, bucket.astype(f32), 0.0) + jnp.where(lane == 33, rank, 0.0)
            + jnp.where(lane == 34, w_lo, 0.0) + jnp.where(lane == 35, w_hi, 0.0))
    meta_ref[...] = meta
    h2_ref[pl.ds(PACK_SLABS, TM, stride=TOKEN_PITCH), :] = meta


def _merge(oa, ob, oc, gt, x, g1, shift, scale, ng, wb, wo, wr, br, seq):
    n = x.shape[0]
    spb = seq // TM
    row = lambda w_: pl.BlockSpec((TM, w_), lambda i: (i, 0))
    modspec = pl.BlockSpec((1, 1, D), lambda i: (i // spb, 0, 0))
    return pl.pallas_call(
        _merge_kernel,
        grid=(n // TM,),
        in_specs=[row(MIX), row(MIX), row(MIX), row(3 * D), row(D), modspec, modspec, modspec,
                  _const_spec((1, D)), _const_spec((3, MIX, D)), _const_spec((D, D)),
                  _const_spec((D, LANES)), _const_spec((1, LANES))],
        out_specs=[row(D), pl.BlockSpec((TM * TOKEN_PITCH, LANES), lambda i: (i, 0)), row(LANES),
                   _const_spec((1, 2 * LANES))],
        out_shape=[jax.ShapeDtypeStruct((n, D), f32), jax.ShapeDtypeStruct((n * TOKEN_PITCH, LANES), f32),
                   jax.ShapeDtypeStruct((n, LANES), f32), jax.ShapeDtypeStruct((1, 2 * LANES), f32)],
        scratch_shapes=[pltpu.VMEM((1, 2 * LANES), f32)],
        compiler_params=_cparams(("arbitrary",)),
        name="merge_route",
    )(oa, ob, oc, gt, x, g1, shift, scale, ng, wb, wo, wr, br)


TE = 128
N_BUCKETS = N_GROUPS * N_EXP * N_EXP
N_PAIRS = N_GROUPS * (N_EXP * (N_EXP - 1) // 2)


def _num_tiles(n):
    return n // TE + N_PAIRS


TP = 2048


def _tables_kernel(meta_ref, cnt_ref, pos_ref, tt_ref):
    cnt = cnt_ref[...]
    ntile = jnp.floor((cnt + (TE - 1)) * (1.0 / TE))
    bi = lax.broadcasted_iota(jnp.int32, (N_BUCKETS, N_BUCKETS), 0)
    bj = lax.broadcasted_iota(jnp.int32, (N_BUCKETS, N_BUCKETS), 1)
    upper = (bi <= bj).astype(bf16)
    incl = jnp.dot(jnp.broadcast_to(ntile, (8, N_BUCKETS)).astype(bf16), upper, preferred_element_type=f32)[0:1]
    excl = incl - ntile

    meta = meta_ref[...]
    tp = meta.shape[0]
    lane = lax.broadcasted_iota(jnp.int32, (tp, LANES), 1)
    bucket = jnp.sum(jnp.where(lane == 32, meta, 0.0), axis=-1, keepdims=True).astype(jnp.int32)
    rank = jnp.sum(jnp.where(lane == 33, meta, 0.0), axis=-1, keepdims=True)
    lane2 = lax.broadcasted_iota(jnp.int32, (tp, N_BUCKETS), 1)
    first_tile = jnp.sum(jnp.where(lane2 == bucket, excl, 0.0), axis=-1, keepdims=True)
    pos_ref[...] = (first_tile * TE + rank).astype(jnp.int32)

    @pl.when(pl.program_id(0) == 0)
    def _():
        tau = lax.broadcasted_iota(jnp.int32, (N_BUCKETS, 1), 0).astype(f32)
        lane_b = lax.broadcasted_iota(jnp.int32, (1, N_BUCKETS), 1).astype(f32)
        tb = jnp.sum((incl <= tau).astype(f32), axis=-1, keepdims=True)
        last_b = jnp.max(jnp.where(cnt > 0, lane_b, 0.0), axis=-1, keepdims=True)
        tbi = jnp.minimum(tb, last_b).astype(jnp.int32)
        grp = lax.shift_right_logical(tbi, 6) * N_EXP
        ea = grp + (lax.shift_right_logical(tbi, 3) & 7)
        eb = grp + (tbi & 7)
        total = jnp.sum(ntile, axis=-1, keepdims=True).astype(jnp.int32)
        lane_t = lax.broadcasted_iota(jnp.int32, (N_BUCKETS, LANES), 1)
        tt_ref[...] = jnp.where(lane_t == 0, ea, jnp.where(lane_t == 1, eb, jnp.where(lane_t == 2, total, 0)))


def _tables(meta, cnt):
    n = meta.shape[0]
    tp = math.gcd(n, TP)
    assert _num_tiles(n) <= N_BUCKETS
    return pl.pallas_call(
        _tables_kernel,
        grid=(n // tp,),
        in_specs=[pl.BlockSpec((tp, LANES), lambda i: (i, 0)), _const_spec((1, N_BUCKETS))],
        out_specs=[pl.BlockSpec((tp, 1), lambda i: (i, 0)), _const_spec((N_BUCKETS, LANES))],
        out_shape=[jax.ShapeDtypeStruct((n, 1), jnp.int32), jax.ShapeDtypeStruct((N_BUCKETS, LANES), jnp.int32)],
        compiler_params=_cparams(("arbitrary",)),
        name="moe_tables",
    )(meta, cnt)


def _plan_kernel(pos_ref, tok_ref, *, n, nslot):
    def fill(s, _):
        tok_ref[s] = -1
        return 0

    lax.fori_loop(0, nslot, fill, 0, unroll=8)

    def place(t, _):
        tok_ref[pos_ref[t]] = t
        return 0

    lax.fori_loop(0, n, place, 0, unroll=8)


def _plan(pos):
    n = pos.shape[0]
    nslot = (_num_tiles(n) + 1) * TE
    smem = pl.BlockSpec(memory_space=pltpu.SMEM)
    return pl.pallas_call(
        functools.partial(_plan_kernel, n=n, nslot=nslot),
        in_specs=[smem], out_specs=smem,
        out_shape=jax.ShapeDtypeStruct((nslot,), jnp.int32),
        name="moe_plan",
    )(pos)


def _moe_kernel(tea_ref, teb_ref, nt_ref, tok_ref, h_hbm, wg_ref, wu_ref, wd_ref, y_hbm,
                xb0, xb1, ob0, ob1, gsem, ssem, *, n):
    i = pl.program_id(0)
    nt = nt_ref[0]
    xbufs = (xb0, xb1)
    obufs = (ob0, ob1)

    def gather_copy(src_tok, sl, r):
        src = h_hbm.at[pl.ds(pl.multiple_of(src_tok * TOKEN_PITCH, TOKEN_PITCH), X_PITCH), :]
        return pltpu.make_async_copy(src, xbufs[sl].at[pl.ds(r * X_PITCH, X_PITCH), :], gsem.at[sl])

    def scatter_copy(dst_tok, sl, r):
        dst = y_hbm.at[pl.ds(pl.multiple_of(dst_tok * Y_PITCH, Y_PITCH), SLABS), :]
        return pltpu.make_async_copy(obufs[sl].at[pl.ds(r * O_PITCH, SLABS), :], dst, ssem.at[sl])

    def start_gathers(tile, sl):
        for r in range(TE):
            gather_copy(jnp.maximum(tok_ref[tile * TE + r], 0), sl, r).start(priority=r % 2)

    def wait_gathers(sl):
        for r in range(TE):
            gather_copy(0, sl, r).wait()

    def start_scatters(tile, sl, to_spare):
        for r in range(TE):
            t = tok_ref[tile * TE + r]
            scatter_copy(jnp.where((t < 0) | to_spare, n + sl * TE + r, t), sl, r).start(priority=r % 2)

    def wait_scatters(sl):
        for r in range(TE):
            scatter_copy(n, sl, r).wait()

    @pl.when(i == 0)
    def _():
        start_gathers(0, 0)
        for sl in range(2):
            obufs[sl][...] = jnp.zeros_like(obufs[sl])
            spare = pltpu.make_async_copy(obufs[sl].at[pl.ds(0, TE * Y_PITCH), :],
                                          y_hbm.at[pl.ds((n + sl * TE) * Y_PITCH, TE * Y_PITCH), :], ssem.at[sl])
            spare.start()
            spare.wait()

    def step(s):
        o = 1 - s
        wait_gathers(s)

        @pl.when(i >= 1)
        def _():
            wait_scatters(s)

        start_gathers(i + 1, o)
        start_scatters(jnp.maximum(i - 1, 0), o, i == 0)

        xb = xbufs[s]
        slab = lambda k: xb[pl.ds(k, TE, stride=X_PITCH), :]
        words = lax.bitcast_convert_type(jnp.concatenate([slab(k) for k in range(PACK_SLABS)], axis=-1),
                                         jnp.uint32)
        lo = lax.bitcast_convert_type(lax.shift_left(words, jnp.uint32(16)), f32)
        hi = lax.bitcast_convert_type(words & jnp.uint32(0xFFFF0000), f32)
        hb = jnp.concatenate([lo, hi], axis=-1).astype(bf16)
        meta = slab(PACK_SLABS)
        lane = lax.broadcasted_iota(jnp.int32, (TE, LANES), 1)
        w_lo = jnp.sum(jnp.where(lane == 34, meta, 0.0), axis=-1, keepdims=True)
        w_hi = jnp.sum(jnp.where(lane == 35, meta, 0.0), axis=-1, keepdims=True)

        def ffn(e, wcol):
            a = jnp.dot(hb, wg_ref[0, e], preferred_element_type=f32)
            u = jnp.dot(hb, wu_ref[0, e], preferred_element_type=f32)
            act = (a * jax.nn.sigmoid(a)) * u * wcol
            return jnp.dot(act.astype(bf16), wd_ref[0, e], preferred_element_type=f32)

        out = ffn(lax.rem(tea_ref[i], N_EXP), w_lo) + ffn(lax.rem(teb_ref[i], N_EXP), w_hi)
        for k in range(SLABS):
            obufs[s][pl.ds(k, TE, stride=O_PITCH), :] = out[:, k * LANES:(k + 1) * LANES]

        @pl.when(i == nt - 1)
        def _():
            start_scatters(i, s, False)
            wait_scatters(o)
            wait_scatters(s)
            wait_gathers(o)

    for s in range(2):
        @pl.when((i < nt) & (lax.rem(i, 2) == s))
        def _(s=s):
            step(s)


def _moe(h2ext, tok, tea, teb, nt, wg, wu, wd):
    n = h2ext.shape[0] // TOKEN_PITCH
    ntmax = _num_tiles(n)
    wspec_in = pl.BlockSpec((1, N_EXP, D, F_EXP), lambda i, tea, teb, nt, tok: (tea[i] // N_EXP, 0, 0, 0))
    wspec_out = pl.BlockSpec((1, N_EXP, F_EXP, D), lambda i, tea, teb, nt, tok: (tea[i] // N_EXP, 0, 0, 0))
    return pl.pallas_call(
        functools.partial(_moe_kernel, n=n),
        grid_spec=pltpu.PrefetchScalarGridSpec(
            num_scalar_prefetch=4,
            grid=(ntmax,),
            in_specs=[pl.BlockSpec(memory_space=pl.ANY), wspec_in, wspec_in, wspec_out],
            out_specs=pl.BlockSpec(memory_space=pl.ANY),
            scratch_shapes=[pltpu.VMEM((TE * X_PITCH, LANES), f32)] * 2 + [pltpu.VMEM((TE * O_PITCH, LANES), f32)] * 2 + [
                pltpu.SemaphoreType.DMA((2,)), pltpu.SemaphoreType.DMA((2,))]),
        out_shape=jax.ShapeDtypeStruct(((n + 2 * TE) * Y_PITCH, LANES), f32),
        compiler_params=_cparams(("arbitrary",)),
        name="moe_sparse",
    )(tea, teb, nt, tok, h2ext, wg, wu, wd)


def _final_kernel(x_ref, y_ref, g_ref, o_ref):
    o_ref[...] = x_ref[...] + g_ref[0] * _rows_from_token_major(y_ref, x_ref.shape[0])


def _final_combine(x1, y, g2, seq):
    n = x1.shape[0]
    t = 512
    spb = seq // t
    row = pl.BlockSpec((t, D), lambda i: (i, 0))
    return pl.pallas_call(
        _final_kernel,
        grid=(n // t,),
        in_specs=[row, pl.BlockSpec((t * Y_PITCH, LANES), lambda i: (i, 0)),
                  pl.BlockSpec((1, 1, D), lambda i: (i // spb, 0, 0))],
        out_specs=row,
        out_shape=jax.ShapeDtypeStruct((n, D), f32),
        compiler_params=_cparams(("arbitrary",)),
        name="final_combine",
    )(x1, y, g2)


def _seg_matrices(width, segs):
    sd = np.zeros((width, LANES), np.float32)
    ex = np.zeros((LANES, width), np.float32)
    for j, (s, ln) in enumerate(segs):
        sd[s:s + ln, j] = 1.0 / ln
        ex[j, s:s + ln] = 1.0
    return jnp.asarray(sd, bf16), jnp.asarray(np.concatenate([ex, ex], axis=0), bf16)


def _head_pad(w, heads, per_head, keep):
    k = w.shape[0]
    w3 = w.reshape(k, heads, per_head)[:, :, :keep]
    return jnp.pad(w3, ((0, 0), (0, 0), (0, LANES - keep))).reshape(k, heads * LANES)


def kernel(x, c, positions, w_ada, b_ada, norm_g, w_in, diff_qk_g, diff_lambda, diff_out_g, rel_bias, sgu_v_g, sgu_w, sgu_b, mla_lat_g, mla_w_uq, mla_w_ukv, mla_qk_g, w_branch, w_out, router_g_w, router_g_b, router_e_w, router_e_b, w_e_gate, w_e_up, w_e_down):
    nb, seq, _ = x.shape
    n = nb * seq
    assert seq % TQ == 0 and seq % TM == 0 and x.shape[2] == D

    mod = _ada(c, w_ada, b_ada)
    cosf, sinf = _rope_tables(positions)
    t0, t1, mk = _bias_tiles(rel_bias)

    sd_a, ex_a = _seg_matrices(MIX, [(s * 64, 64) for s in range(8)])
    segs_q = []
    for h in range(HC):
        segs_q += [(h * LANES, NOPE), (h * LANES + NOPE, ROPE)]
    sd_q, ex_q = _seg_matrices(HC * LANES, segs_q)
    ne = N_GROUPS * N_EXP
    zpad = lambda a, w_: jnp.pad(a, ((0, 0), (0, w_ - a.shape[1])))

    xcur = x.reshape(n, D)
    y_prev, g2_prev = None, None
    for l in range(DEPTH):
        m3 = mod[l].reshape(nb, 1, 6 * D)
        shift1, scale1, gate1, shift2, scale2, gate2 = [m3[:, :, k * D:(k + 1) * D] for k in range(6)]
        lambda_init = LAMBDA_INIT_BASE - LAMBDA_INIT_SCALE * math.exp(-LAMBDA_INIT_DECAY * l)

        w = w_in[l]
        w_pad = jnp.concatenate([w[:, :XC_OFF + 416], jnp.zeros((D, IN_PAD - IN_REAL), f32), w[:, XC_OFF + 416:]],
                                axis=1).astype(bf16)
        gq = (jnp.tile(diff_qk_g[l, 0], 8) * (64 ** -0.5 * LOG2E)).reshape(1, MIX)
        gk = jnp.tile(diff_qk_g[l, 1], 8).reshape(1, MIX)
        outs = _inproj(xcur, y_prev, g2_prev, shift1, scale1, norm_g[l, 0].reshape(1, D), w_pad, gq, gk,
                       sd_a, ex_a, sgu_v_g[l].reshape(1, MIX), sgu_w[l], jnp.transpose(sgu_b[l]), seq)
        if y_prev is not None:
            xcur = outs[0]
            outs = outs[1:]
        qa, ka, va, ob, xc, gt = outs

        oa = _attn_a(qa, ka, va, t0, t1, diff_lambda[l], diff_out_g[l].reshape(1, LANES), nb, seq, lambda_init)

        qkg = mla_qk_g[l]
        wq = _head_pad(mla_w_uq[l], HC, NOPE + ROPE, NOPE + ROPE).astype(bf16)
        wkv3 = mla_w_ukv[l].reshape(KV_LORA, HC, NOPE + VC)
        wk = jnp.pad(wkv3[:, :, :NOPE], ((0, 0), (0, 0), (0, LANES - NOPE))).reshape(KV_LORA, HC * LANES)
        wv = wkv3[:, :, NOPE:].reshape(KV_LORA, HC * VC)
        wkv = jnp.concatenate([wk, wv], axis=1).astype(bf16)
        gq_c = jnp.tile(jnp.pad(qkg[0], (0, LANES - NOPE - ROPE)), HC).reshape(1, HC * LANES) * ((NOPE + ROPE) ** -0.5 * LOG2E)
        gk_c = jnp.pad(qkg[1, :NOPE], (0, LANES - NOPE)).reshape(1, LANES)
        gr_c = jnp.pad(qkg[1, NOPE:], (0, LANES - ROPE)).reshape(1, LANES)
        qc, kc, vc = _mla_prep(xc, cosf, sinf, mla_lat_g[l, :Q_LORA].reshape(1, Q_LORA),
                               mla_lat_g[l, Q_LORA:].reshape(1, KV_LORA), wq, wkv, gq_c, gk_c, gr_c, sd_q, ex_q)
        oc = _attn_c(qc, kc, vc, mk, nb, seq)

        wr = zpad(jnp.concatenate([router_e_w[l], router_g_w[l]], axis=1), LANES)
        br = zpad(jnp.concatenate([router_e_b[l], router_g_b[l]]).reshape(1, ne + N_GROUPS), LANES)
        x1, h2ext, meta, cnt = _merge(oa, ob, oc, gt, xcur, gate1, shift2, scale2, norm_g[l, 1].reshape(1, D),
                                      w_branch[l].astype(bf16), w_out[l].astype(bf16), wr, br, seq)

        pos, tt = _tables(meta, cnt)
        tok = _plan(pos.reshape(n))
        ntmax = _num_tiles(n)
        y_prev = _moe(h2ext, tok, tt[:ntmax, 0], tt[:ntmax, 1], tt[0:1, 2], w_e_gate[l].astype(bf16),
                      w_e_up[l].astype(bf16), w_e_down[l].astype(bf16))
        g2_prev = gate2
        xcur = x1

    out = _final_combine(xcur, y_prev, g2_prev, seq)
    return out.reshape(nb, seq, D)
```

```python
import functools
import math

import jax
import jax.numpy as jnp
import numpy as np
from jax import lax
from jax.experimental import pallas as pl
from jax.experimental.pallas import tpu as pltpu

f32 = jnp.float32
bf16 = jnp.bfloat16

D = 1024
DEPTH = 4
CHUNK = 64
MIX = 512
HA = 4
HC = 8
Q_LORA, KV_LORA, ROPE = 256, 128, 32
NOPE, VC = 64, 64
N_GROUPS, N_EXP, F_EXP = 4, 8, 256
N_REL_BUCKETS = 32
ROPE_THETA = 10000.0
LAMBDA_INIT_BASE, LAMBDA_INIT_SCALE, LAMBDA_INIT_DECAY = 0.8, 0.6, 0.3
EPS = 1e-6
NEG = -1e30

IN_REAL = 6048
IN_PAD = 6144
XC_OFF = 2560
GATE_OFF = 3072

TM = 512
TQ = 256
TK = 256
LANES = 128
VMEM_LIMIT = 56 * 1024 * 1024
SLABS = D // LANES
PACK_SLABS = SLABS // 2
Y_PITCH = 2 * SLABS
O_PITCH = Y_PITCH + 1


def _cparams(sem, vmem=VMEM_LIMIT):
    return pltpu.CompilerParams(dimension_semantics=sem, vmem_limit_bytes=vmem)


def _const_spec(shape):
    nd = len(shape)
    return pl.BlockSpec(shape, lambda *_: (0,) * nd)


def _rms(x, g_row):
    return x * lax.rsqrt(jnp.mean(x * x, axis=-1, keepdims=True) + EPS) * g_row


def _expert_mix(y_ref, meta, rows):
    lane = lax.broadcasted_iota(jnp.int32, (rows, LANES), 1)
    halves = []
    for half in range(2):
        w = jnp.sum(jnp.where(lane == 34 + half, meta, 0.0), axis=-1, keepdims=True)
        yh = jnp.concatenate([y_ref[pl.ds(half * SLABS + k, rows, stride=Y_PITCH), :] for k in range(SLABS)],
                             axis=-1)
        halves.append(w * yh)
    return halves[0] + halves[1]


def _seg_rinv(x, sd_ref, ex_ref):
    ss = jnp.dot((x * x).astype(bf16), sd_ref[...], preferred_element_type=f32)
    hi = ss.astype(bf16)
    lo = (ss - hi.astype(f32)).astype(bf16)
    full = jnp.dot(jnp.concatenate([hi, lo], axis=-1), ex_ref[...], preferred_element_type=f32)
    return lax.rsqrt(full + EPS)


def _ada_kernel(c_ref, w_ref, b_ref, o_ref):
    c = c_ref[...]
    ca = (c * jax.nn.sigmoid(c)).astype(bf16)
    o_ref[0] = jnp.dot(ca, w_ref[0].astype(bf16), preferred_element_type=f32) + b_ref[0]


def _ada(c, w_ada, b_ada):
    nb = c.shape[0]
    tn = 1024
    return pl.pallas_call(
        _ada_kernel,
        grid=(DEPTH, 6 * D // tn),
        in_specs=[pl.BlockSpec((nb, D), lambda l, j: (0, 0)),
                  pl.BlockSpec((1, D, tn), lambda l, j: (l, 0, j)),
                  pl.BlockSpec((1, 1, tn), lambda l, j: (l, 0, j))],
        out_specs=pl.BlockSpec((1, nb, tn), lambda l, j: (l, 0, j)),
        out_shape=jax.ShapeDtypeStruct((DEPTH, nb, 6 * D), f32),
        compiler_params=_cparams(("arbitrary", "arbitrary")),
        name="ada_mod",
    )(c, w_ada, b_ada.reshape(DEPTH, 1, 6 * D))


def _rope_kernel(pos_ref, invf_ref, cos_ref, sin_ref):
    ang = pos_ref[...].astype(f32) * invf_ref[...]
    lane = lax.broadcasted_iota(jnp.int32, ang.shape, 1)
    rot = (lane >= NOPE) & (lane < NOPE + ROPE)
    first = lane < NOPE + ROPE // 2
    cos_ref[...] = jnp.where(rot, jnp.cos(ang), 1.0)
    s = jnp.sin(ang)
    sin_ref[...] = jnp.where(rot, jnp.where(first, -s, s), 0.0)


def _rope_tables(positions):
    n = positions.size
    inv_freq = ROPE_THETA ** (-jnp.arange(0, ROPE, 2, dtype=f32) / ROPE)
    invf = jnp.zeros((1, LANES), f32).at[0, NOPE:NOPE + ROPE].set(jnp.tile(inv_freq, 2))
    t = math.gcd(n, 2048)
    return pl.pallas_call(
        _rope_kernel,
        grid=(n // t,),
        in_specs=[pl.BlockSpec((t, 1), lambda i: (i, 0)), _const_spec((1, LANES))],
        out_specs=[pl.BlockSpec((t, LANES), lambda i: (i, 0))] * 2,
        out_shape=[jax.ShapeDtypeStruct((n, LANES), f32)] * 2,
        compiler_params=_cparams(("arbitrary",)),
        name="rope_tables",
    )(positions.reshape(n, 1), invf)


def _t5_bucket(rel):
    n = jnp.abs(rel)
    large = 8 + sum((n >= t).astype(jnp.int32) for t in (12, 16, 23, 32, 46, 64, 91))
    return jnp.where(rel > 0, 16, 0) + jnp.where(n < 8, n, large)


def _bias_kernel(tbl_ref, t0_ref, t1_ref, mk_ref):
    r = lax.broadcasted_iota(jnp.int32, (TQ, TK), 0)
    c = lax.broadcasted_iota(jnp.int32, (TQ, TK), 1)
    allowed = (c // CHUNK) <= (r // CHUNK)
    mk = jnp.where(allowed, 0.0, NEG).astype(f32)
    mk_ref[...] = mk
    b0 = _t5_bucket(c - r)
    b1 = _t5_bucket(c - r - TK)
    for h in range(HA):
        far = tbl_ref[N_REL_BUCKETS // 2 - 1, h]
        t0 = jnp.zeros((TQ, TK), f32)
        t1 = jnp.zeros((TQ, TK), f32)
        for b in range(N_REL_BUCKETS):
            v = (tbl_ref[b, h] - far) * LOG2E
            t0 = jnp.where(b0 == b, v, t0)
            t1 = jnp.where(b1 == b, v, t1)
        t0 = t0 + mk
        t0_ref[h, 0:TQ, :] = t0
        t0_ref[h, TQ:2 * TQ, :] = t0
        t1_ref[h, 0:TQ, :] = t1
        t1_ref[h, TQ:2 * TQ, :] = t1


def _bias_tiles(rel_bias):
    return pl.pallas_call(
        _bias_kernel,
        in_specs=[pl.BlockSpec(memory_space=pltpu.SMEM)],
        out_specs=[pl.BlockSpec(memory_space=pltpu.VMEM)] * 3,
        out_shape=[jax.ShapeDtypeStruct((HA, 2 * TQ, TK), f32),
                   jax.ShapeDtypeStruct((HA, 2 * TQ, TK), f32),
                   jax.ShapeDtypeStruct((TQ, TK), f32)],
        name="bias_tiles",
    )(rel_bias)


def _inproj_kernel(*refs, combine):
    if combine:
        (xa_ref, y_ref, ym_ref, g2_ref, sh_ref, sc_ref, ng_ref, w_ref, gq_ref, gk_ref, sd_ref, ex_ref,
         vg_ref, ws_ref, bs_ref,
         xo_ref, qa_ref, ka_ref, va_ref, ob_ref, xc_ref, gt_ref) = refs
        x = xa_ref[...] + g2_ref[0] * _expert_mix(y_ref, ym_ref[...], TM)
        xo_ref[...] = x
    else:
        (xa_ref, sh_ref, sc_ref, ng_ref, w_ref, gq_ref, gk_ref, sd_ref, ex_ref,
         vg_ref, ws_ref, bs_ref,
         qa_ref, ka_ref, va_ref, ob_ref, xc_ref, gt_ref) = refs
        x = xa_ref[...]
    h = _rms(x, ng_ref[...]) * (1.0 + sc_ref[0]) + sh_ref[0]
    hb = h.astype(bf16)

    q = jnp.dot(hb, w_ref[:, 0:MIX], preferred_element_type=f32)
    qa_ref[...] = (q * _seg_rinv(q, sd_ref, ex_ref) * gq_ref[...]).astype(bf16)
    k = jnp.dot(hb, w_ref[:, MIX:2 * MIX], preferred_element_type=f32)
    ka_ref[...] = (k * _seg_rinv(k, sd_ref, ex_ref) * gk_ref[...]).astype(bf16)
    va_ref[...] = jnp.dot(hb, w_ref[:, 2 * MIX:3 * MIX], preferred_element_type=f32).astype(bf16)

    z = jnp.dot(hb, w_ref[:, 3 * MIX:5 * MIX], preferred_element_type=f32)
    z = 0.5 * z * (1.0 + jnp.tanh(math.sqrt(2.0 / math.pi) * (z + 0.044715 * (z * z * z))))
    u = z[:, :MIX]
    v = _rms(z[:, MIX:], vg_ref[...]).astype(bf16)
    ri = lax.broadcasted_iota(jnp.int32, (LANES, LANES), 0)
    ci = lax.broadcasted_iota(jnp.int32, (LANES, LANES), 1)
    allowed = (ci // CHUNK) <= (ri // CHUNK)
    for g in range(4):
        wm = jnp.where(allowed, ws_ref[g], 0.0).astype(bf16)
        bcol = bs_ref[:, g:g + 1]
        for wd in range(TM // LANES):
            rows = slice(wd * LANES, (wd + 1) * LANES)
            cols = slice(g * LANES, (g + 1) * LANES)
            vs = jnp.dot(wm, v[rows, cols], preferred_element_type=f32) + bcol
            ob_ref[rows, cols] = (u[rows, cols] * vs).astype(bf16)

    xc_ref[...] = jnp.dot(hb, w_ref[:, XC_OFF:GATE_OFF], preferred_element_type=f32)

    for j in range(3):
        gsl = slice(GATE_OFF + j * D, GATE_OFF + (j + 1) * D)
        gl = jnp.dot(hb, w_ref[:, gsl], preferred_element_type=f32)
        gt_ref[:, j * D:(j + 1) * D] = jax.nn.sigmoid(gl).astype(bf16)


def _inproj(xa, y, g2, shift, scale, ng, w, gq, gk, sd, ex, vg, ws, bs, seq):
    n = xa.shape[0]
    spb = seq // TM
    combine = y is not None
    row = lambda w_: pl.BlockSpec((TM, w_), lambda i: (i, 0))
    modspec = pl.BlockSpec((1, 1, D), lambda i: (i // spb, 0, 0))
    in_specs = [row(D)]
    args = [xa]
    if combine:
        in_specs += [pl.BlockSpec((TM * Y_PITCH, LANES), lambda i: (i, 0)), row(LANES), modspec]
        args += [y[0], y[1], g2]
    in_specs += [modspec, modspec, _const_spec((1, D)),
                 pl.BlockSpec((D, IN_PAD), lambda i: (0, 0), pipeline_mode=pl.Buffered(1)),
                 _const_spec((1, MIX)), _const_spec((1, MIX)),
                 _const_spec((MIX, LANES)), _const_spec((2 * LANES, MIX)),
                 _const_spec((1, MIX)), _const_spec((4, LANES, LANES)), _const_spec((LANES, 4))]
    args += [shift, scale, ng, w, gq, gk, sd, ex, vg, ws, bs]
    out_specs = [row(MIX)] * 4 + [row(MIX), row(3 * D)]
    out_shape = [jax.ShapeDtypeStruct((n, MIX), bf16)] * 4 + [
        jax.ShapeDtypeStruct((n, MIX), f32), jax.ShapeDtypeStruct((n, 3 * D), bf16)]
    if combine:
        out_specs = [row(D)] + out_specs
        out_shape = [jax.ShapeDtypeStruct((n, D), f32)] + out_shape
    return pl.pallas_call(
        functools.partial(_inproj_kernel, combine=combine),
        grid=(n // TM,),
        in_specs=in_specs, out_specs=out_specs, out_shape=out_shape,
        compiler_params=_cparams(("arbitrary",)),
        name="inproj",
    )(*args)


LOG2E = math.log2(math.e)


def _two_pass_attention(nunits, q_of, kcols_of, vcols_of, k_ref, v_ref, qi, diag_bias_of, sub_bias_of,
                        sbuf, mbuf, lbuf, abuf):
    def score_block(j, bias_of):
        rows_k = pl.ds(pl.multiple_of(j * TK, TK), TK)
        for u in range(nunits):
            s = lax.dot_general(q_of(u), k_ref[rows_k, kcols_of(u)], (((1,), (1,)), ((), ())),
                                preferred_element_type=f32)
            if bias_of is not None:
                s = s + bias_of(u)
            sbuf[u, j] = s
            mbuf[u] = jnp.maximum(mbuf[u], jnp.maximum(s[:, :LANES], s[:, LANES:]))

    for u in range(nunits):
        mbuf[u] = jnp.full(mbuf.shape[1:], NEG, f32)
    if sub_bias_of is None:
        nfar = qi
    else:
        nfar = jnp.maximum(qi - 1, 0)

        @pl.when(qi >= 1)
        def _():
            score_block(qi - 1, sub_bias_of)

    def far(j, _):
        score_block(j, None)
        return 0

    lax.fori_loop(0, nfar, far, 0)
    score_block(qi, diag_bias_of)

    for u in range(nunits):
        mbuf[u] = jnp.broadcast_to(jnp.max(mbuf[u], axis=-1, keepdims=True), mbuf.shape[1:])
        lbuf[u] = jnp.zeros(lbuf.shape[1:], f32)
        abuf[u] = jnp.zeros(abuf.shape[1:], f32)

    def accumulate(j, _):
        rows_k = pl.ds(pl.multiple_of(j * TK, TK), TK)
        for u in range(nunits):
            s = sbuf[u, j]
            mb = mbuf[u]
            p0 = jnp.exp2(s[:, :LANES] - mb)
            p1 = jnp.exp2(s[:, LANES:] - mb)
            lbuf[u] = lbuf[u] + (p0 + p1)
            p = jnp.concatenate([p0, p1], axis=-1).astype(bf16)
            abuf[u] = abuf[u] + jnp.dot(p, v_ref[rows_k, vcols_of(u)], preferred_element_type=f32)
        return 0

    lax.fori_loop(0, qi + 1, accumulate, 0)


def _normalised(u, lbuf, abuf):
    return abuf[u] / jnp.sum(lbuf[u], axis=-1, keepdims=True)


GA = 2
GC = 4


def _attn_a_kernel(q_ref, k_ref, v_ref, t0_ref, t1_ref, dl_ref, og_ref, o_ref,
                   qstk, sbuf, mbuf, lbuf, abuf, *, lambda_init):
    qi = pl.program_id(2)
    dl = dl_ref[...]
    lam = (jnp.exp(jnp.sum(dl[0:1] * dl[1:2], axis=-1, keepdims=True))
           - jnp.exp(jnp.sum(dl[2:3] * dl[3:4], axis=-1, keepdims=True)) + lambda_init)
    lane = lax.broadcasted_iota(jnp.int32, (TQ, LANES), 1)
    cols = lambda u: slice(u * LANES, (u + 1) * LANES)
    for u in range(GA):
        qh = q_ref[:, cols(u)]
        zero = jnp.zeros_like(qh)
        qstk[u, 0:TQ, :] = jnp.where(lane < 64, qh, zero)
        qstk[u, TQ:2 * TQ, :] = jnp.where(lane >= 64, qh, zero)
    _two_pass_attention(GA, lambda u: qstk[u], cols, cols, k_ref, v_ref, qi,
                        lambda u: t0_ref[u], lambda u: t1_ref[u], sbuf, mbuf, lbuf, abuf)
    for u in range(GA):
        o = _normalised(u, lbuf, abuf)
        oh = o[:TQ] - lam * o[TQ:]
        oh = _rms(oh, og_ref[...]) * (1.0 - lambda_init)
        o_ref[:, cols(u)] = oh.astype(bf16)


def _attn_a(qa, ka, va, t0, t1, dl, og, nb, seq, lambda_init):
    n = qa.shape[0]
    nq = seq // TQ
    w = GA * LANES
    return pl.pallas_call(
        functools.partial(_attn_a_kernel, lambda_init=lambda_init),
        grid=(nb, HA // GA, nq),
        in_specs=[pl.BlockSpec((TQ, w), lambda b, g, i: (b * nq + i, g)),
                  pl.BlockSpec((seq, w), lambda b, g, i: (b, g)),
                  pl.BlockSpec((seq, w), lambda b, g, i: (b, g)),
                  pl.BlockSpec((GA, 2 * TQ, TK), lambda b, g, i: (g, 0, 0)),
                  pl.BlockSpec((GA, 2 * TQ, TK), lambda b, g, i: (g, 0, 0)),
                  _const_spec((4, 64)), _const_spec((1, LANES))],
        out_specs=pl.BlockSpec((TQ, w), lambda b, g, i: (b * nq + i, g)),
        out_shape=jax.ShapeDtypeStruct((n, MIX), bf16),
        scratch_shapes=[pltpu.VMEM((GA, 2 * TQ, LANES), bf16),
                        pltpu.VMEM((GA, seq // TK, 2 * TQ, TK), f32),
                        pltpu.VMEM((GA, 2 * TQ, LANES), f32), pltpu.VMEM((GA, 2 * TQ, LANES), f32),
                        pltpu.VMEM((GA, 2 * TQ, LANES), f32)],
        compiler_params=_cparams(("arbitrary", "arbitrary", "arbitrary")),
        name="attn_diff",
    )(qa, ka, va, t0, t1, dl, og)


def _attn_c_kernel(q_ref, k_ref, v_ref, mk_ref, o_ref, sbuf, mbuf, lbuf, abuf):
    qi = pl.program_id(2)
    lane = lax.broadcasted_iota(jnp.int32, (TQ, LANES), 1)
    cols = lambda u: slice(u * LANES, (u + 1) * LANES)
    _two_pass_attention(GC, lambda u: q_ref[:, cols(u)], cols, lambda u: cols(u // 2), k_ref, v_ref, qi,
                        lambda u: mk_ref[...], None, sbuf, mbuf, lbuf, abuf)
    for hp in range(GC // 2):
        o_ref[:, cols(hp)] = jnp.where(lane < VC, _normalised(2 * hp, lbuf, abuf),
                                       _normalised(2 * hp + 1, lbuf, abuf)).astype(bf16)


def _attn_c(qc, kc, vc, mk, nb, seq):
    n = qc.shape[0]
    nq = seq // TQ
    w = GC * LANES
    wv = GC * VC
    return pl.pallas_call(
        _attn_c_kernel,
        grid=(nb, HC // GC, nq),
        in_specs=[pl.BlockSpec((TQ, w), lambda b, g, i: (b * nq + i, g)),
                  pl.BlockSpec((seq, w), lambda b, g, i: (b, g)),
                  pl.BlockSpec((seq, wv), lambda b, g, i: (b, g)),
                  _const_spec((TQ, TK))],
        out_specs=pl.BlockSpec((TQ, wv), lambda b, g, i: (b * nq + i, g)),
        out_shape=jax.ShapeDtypeStruct((n, MIX), bf16),
        scratch_shapes=[pltpu.VMEM((GC, seq // TK, TQ, TK), f32),
                        pltpu.VMEM((GC, TQ, LANES), f32), pltpu.VMEM((GC, TQ, LANES), f32),
                        pltpu.VMEM((GC, TQ, LANES), f32)],
        compiler_params=_cparams(("arbitrary", "arbitrary", "arbitrary")),
        name="attn_latent",
    )(qc, kc, vc, mk)


def _rope_apply(x, cosf, sinf, lane):
    w = x.shape[-1]
    partner = jnp.where((lane % LANES) < NOPE + ROPE // 2,
                        pltpu.roll(x, w - ROPE // 2, 1), pltpu.roll(x, ROPE // 2, 1))
    return x * cosf + partner * sinf


def _mla_prep_kernel(xc_ref, cos_ref, sin_ref, glq_ref, glkv_ref, wq_ref, wkv_ref, gq_ref, gk_ref, gr_ref,
                     sd_ref, ex_ref, q_ref, k_ref, v_ref):
    cq = _rms(xc_ref[:, 0:Q_LORA], glq_ref[...]).astype(bf16)
    ckv = _rms(xc_ref[:, Q_LORA:Q_LORA + KV_LORA], glkv_ref[...]).astype(bf16)
    cos8 = jnp.concatenate([cos_ref[...]] * HC, axis=-1)
    sin8 = jnp.concatenate([sin_ref[...]] * HC, axis=-1)
    lane8 = lax.broadcasted_iota(jnp.int32, (TM, HC * LANES), 1)

    q = jnp.dot(cq, wq_ref[...], preferred_element_type=f32)
    q = q * _seg_rinv(q, sd_ref, ex_ref) * gq_ref[...]
    q_ref[...] = _rope_apply(q, cos8, sin8, lane8).astype(bf16)

    kv = jnp.dot(ckv, wkv_ref[...], preferred_element_type=f32)
    v_ref[...] = kv[:, HC * LANES:].astype(bf16)
    xr = xc_ref[:, Q_LORA + KV_LORA:Q_LORA + KV_LORA + LANES]
    kr = xr * lax.rsqrt(jnp.sum(xr * xr, axis=-1, keepdims=True) * (1.0 / ROPE) + EPS) * gr_ref[...]
    kr = pltpu.roll(kr, NOPE, 1)
    lane1 = lax.broadcasted_iota(jnp.int32, (TM, LANES), 1)
    kr = _rope_apply(kr, cos_ref[...], sin_ref[...], lane1)
    for h in range(HC):
        cols = slice(h * LANES, (h + 1) * LANES)
        kn = kv[:, cols]
        kn = kn * lax.rsqrt(jnp.sum(kn * kn, axis=-1, keepdims=True) * (1.0 / NOPE) + EPS) * gk_ref[...]
        k_ref[:, cols] = (kn + kr).astype(bf16)


def _mla_prep(xc, cosf, sinf, glq, glkv, wq, wkv, gq, gk, gr, sd, ex):
    n = xc.shape[0]
    row = lambda w_: pl.BlockSpec((TM, w_), lambda i: (i, 0))
    return pl.pallas_call(
        _mla_prep_kernel,
        grid=(n // TM,),
        in_specs=[row(MIX), row(LANES), row(LANES), _const_spec((1, Q_LORA)), _const_spec((1, KV_LORA)),
                  _const_spec((Q_LORA, HC * LANES)), _const_spec((KV_LORA, HC * LANES + MIX)),
                  _const_spec((1, HC * LANES)), _const_spec((1, LANES)), _const_spec((1, LANES)),
                  _const_spec((HC * LANES, LANES)), _const_spec((2 * LANES, HC * LANES))],
        out_specs=[row(HC * LANES), row(HC * LANES), row(MIX)],
        out_shape=[jax.ShapeDtypeStruct((n, HC * LANES), bf16), jax.ShapeDtypeStruct((n, HC * LANES), bf16),
                   jax.ShapeDtypeStruct((n, MIX), bf16)],
        compiler_params=_cparams(("arbitrary",)),
        name="mla_prep",
    )(xc, cosf, sinf, glq, glkv, wq, wkv, gq, gk, gr, sd, ex)


def _merge_kernel(oa_ref, ob_ref, oc_ref, gt_ref, x_ref, g1_ref, sh_ref, sc_ref, ng_ref, wb_ref, wo_ref,
                  wr_ref, br_ref, x1_ref, h2_ref, meta_ref, cnt_ref, run_ref):
    i = pl.program_id(0)

    @pl.when(i == 0)
    def _():
        run_ref[...] = jnp.zeros_like(run_ref)

    merged = jnp.zeros((TM, D), f32)
    for j, o_ref in enumerate((oa_ref, ob_ref, oc_ref)):
        pj = jnp.dot(o_ref[...], wb_ref[j], preferred_element_type=f32)
        merged = merged + gt_ref[:, j * D:(j + 1) * D].astype(f32) * pj
    y = jnp.dot(merged.astype(bf16), wo_ref[...], preferred_element_type=f32)
    x1 = x_ref[...] + g1_ref[0] * y
    x1_ref[...] = x1
    h2 = _rms(x1, ng_ref[...]) * (1.0 + sc_ref[0]) + sh_ref[0]
    hr = h2.astype(bf16).astype(f32)
    lo = lax.shift_right_logical(lax.bitcast_convert_type(hr[:, 0:D // 2], jnp.uint32), jnp.uint32(16))
    hi = lax.bitcast_convert_type(hr[:, D // 2:D], jnp.uint32) & jnp.uint32(0xFFFF0000)
    words = lax.bitcast_convert_type(lo | hi, f32)
    for s in range(PACK_SLABS):
        h2_ref[pl.ds(s, TM, stride=PACK_SLABS), :] = words[:, s * LANES:(s + 1) * LANES]

    lg = jnp.dot(h2, wr_ref[...], preferred_element_type=f32, precision=lax.Precision.HIGHEST) + br_ref[...]
    lane = lax.broadcasted_iota(jnp.int32, (TM, LANES), 1)
    n_e = N_GROUPS * N_EXP
    gmask = (lane >= n_e) & (lane < n_e + N_GROUPS)
    gl = jnp.where(gmask, lg, NEG)
    gmax = jnp.max(gl, axis=-1, keepdims=True)
    g_w = 1.0 / jnp.sum(jnp.where(gmask, jnp.exp(gl - gmax), 0.0), axis=-1, keepdims=True)
    gidx = jnp.min(jnp.where(gl == gmax, lane - n_e, LANES), axis=-1, keepdims=True)
    emask = (lane < n_e) & ((lane // N_EXP) == gidx)
    el = jnp.where(emask, lg, NEG)
    m1 = jnp.max(el, axis=-1, keepdims=True)
    i1 = jnp.min(jnp.where(el == m1, lane, LANES), axis=-1, keepdims=True)
    el2 = jnp.where(lane == i1, NEG, el)
    m2 = jnp.max(el2, axis=-1, keepdims=True)
    i2 = jnp.min(jnp.where(el2 == m2, lane, LANES), axis=-1, keepdims=True)
    t = jnp.exp(m2 - m1)
    w1 = g_w / (1.0 + t)
    w2 = g_w * t / (1.0 + t)
    lo = jnp.minimum(i1, i2)
    hi = jnp.maximum(i1, i2)
    w_lo = jnp.where(i1 < i2, w1, w2)
    w_hi = jnp.where(i1 < i2, w2, w1)
    bucket = gidx * (N_EXP * N_EXP) + (lo % N_EXP) * N_EXP + (hi % N_EXP)

    lane2 = lax.broadcasted_iota(jnp.int32, (TM, 2 * LANES), 1)
    onehot = (lane2 == bucket).astype(f32)
    rr = lax.broadcasted_iota(jnp.int32, (TM, TM), 0)
    cc = lax.broadcasted_iota(jnp.int32, (TM, TM), 1)
    tri = (cc < rr).astype(bf16)
    before = jnp.dot(tri, onehot.astype(bf16), preferred_element_type=f32) + run_ref[...]
    rank = jnp.sum(onehot * before, axis=-1, keepdims=True)
    run_ref[...] = run_ref[...] + jnp.sum(onehot, axis=0, keepdims=True)
    cnt_ref[...] = run_ref[...]

    meta = (jnp.where(lane == i1, w1, 0.0) + jnp.where(lane == i2, w2, 0.0)
            + jnp.where(lane == 32, bucket.astype(f32), 0.0) + jnp.where(lane == 33, rank, 0.0)
            + jnp.where(lane == 34, w_lo, 0.0) + jnp.where(lane == 35, w_hi, 0.0))
    meta_ref[...] = meta


def _merge(oa, ob, oc, gt, x, g1, shift, scale, ng, wb, wo, wr, br, seq):
    n = x.shape[0]
    spb = seq // TM
    row = lambda w_: pl.BlockSpec((TM, w_), lambda i: (i, 0))
    modspec = pl.BlockSpec((1, 1, D), lambda i: (i // spb, 0, 0))
    return pl.pallas_call(
        _merge_kernel,
        grid=(n // TM,),
        in_specs=[row(MIX), row(MIX), row(MIX), row(3 * D), row(D), modspec, modspec, modspec,
                  _const_spec((1, D)), _const_spec((3, MIX, D)), _const_spec((D, D)),
                  _const_spec((D, LANES)), _const_spec((1, LANES))],
        out_specs=[row(D), pl.BlockSpec((TM * PACK_SLABS, LANES), lambda i: (i, 0)), row(LANES),
                   _const_spec((1, 2 * LANES))],
        out_shape=[jax.ShapeDtypeStruct((n, D), f32), jax.ShapeDtypeStruct((n * PACK_SLABS, LANES), f32),
                   jax.ShapeDtypeStruct((n, LANES), f32), jax.ShapeDtypeStruct((1, 2 * LANES), f32)],
        scratch_shapes=[pltpu.VMEM((1, 2 * LANES), f32)],
        compiler_params=_cparams(("arbitrary",)),
        name="merge_route",
    )(oa, ob, oc, gt, x, g1, shift, scale, ng, wb, wo, wr, br)


TE = 128
N_BUCKETS = N_GROUPS * N_EXP * N_EXP
N_PAIRS = N_GROUPS * (N_EXP * (N_EXP - 1) // 2)


def _num_tiles(n):
    return n // TE + N_PAIRS


TP = 2048


def _tables_kernel(meta_ref, cnt_ref, pos_ref, tt_ref):
    cnt = cnt_ref[...]
    ntile = jnp.floor((cnt + (TE - 1)) * (1.0 / TE))
    bi = lax.broadcasted_iota(jnp.int32, (N_BUCKETS, N_BUCKETS), 0)
    bj = lax.broadcasted_iota(jnp.int32, (N_BUCKETS, N_BUCKETS), 1)
    upper = (bi <= bj).astype(bf16)
    incl = jnp.dot(jnp.broadcast_to(ntile, (8, N_BUCKETS)).astype(bf16), upper, preferred_element_type=f32)[0:1]
    excl = incl - ntile

    meta = meta_ref[...]
    tp = meta.shape[0]
    lane = lax.broadcasted_iota(jnp.int32, (tp, LANES), 1)
    bucket = jnp.sum(jnp.where(lane == 32, meta, 0.0), axis=-1, keepdims=True).astype(jnp.int32)
    rank = jnp.sum(jnp.where(lane == 33, meta, 0.0), axis=-1, keepdims=True)
    lane2 = lax.broadcasted_iota(jnp.int32, (tp, N_BUCKETS), 1)
    first_tile = jnp.sum(jnp.where(lane2 == bucket, excl, 0.0), axis=-1, keepdims=True)
    pos_ref[...] = (first_tile * TE + rank).astype(jnp.int32)

    @pl.when(pl.program_id(0) == 0)
    def _():
        tau = lax.broadcasted_iota(jnp.int32, (N_BUCKETS, 1), 0).astype(f32)
        lane_b = lax.broadcasted_iota(jnp.int32, (1, N_BUCKETS), 1).astype(f32)
        tb = jnp.sum((incl <= tau).astype(f32), axis=-1, keepdims=True)
        last_b = jnp.max(jnp.where(cnt > 0, lane_b, 0.0), axis=-1, keepdims=True)
        tbi = jnp.minimum(tb, last_b).astype(jnp.int32)
        grp = lax.shift_right_logical(tbi, 6) * N_EXP
        ea = grp + (lax.shift_right_logical(tbi, 3) & 7)
        eb = grp + (tbi & 7)
        total = jnp.sum(ntile, axis=-1, keepdims=True).astype(jnp.int32)
        lane_t = lax.broadcasted_iota(jnp.int32, (N_BUCKETS, LANES), 1)
        tt_ref[...] = jnp.where(lane_t == 0, ea, jnp.where(lane_t == 1, eb, jnp.where(lane_t == 2, total, 0)))


def _tables(meta, cnt):
    n = meta.shape[0]
    tp = math.gcd(n, TP)
    assert _num_tiles(n) <= N_BUCKETS
    return pl.pallas_call(
        _tables_kernel,
        grid=(n // tp,),
        in_specs=[pl.BlockSpec((tp, LANES), lambda i: (i, 0)), _const_spec((1, N_BUCKETS))],
        out_specs=[pl.BlockSpec((tp, 1), lambda i: (i, 0)), _const_spec((N_BUCKETS, LANES))],
        out_shape=[jax.ShapeDtypeStruct((n, 1), jnp.int32), jax.ShapeDtypeStruct((N_BUCKETS, LANES), jnp.int32)],
        compiler_params=_cparams(("arbitrary",)),
        name="moe_tables",
    )(meta, cnt)


def _plan_kernel(pos_ref, tok_ref, *, n, nslot):
    def fill(s, _):
        tok_ref[s] = -1
        return 0

    lax.fori_loop(0, nslot, fill, 0, unroll=8)

    def place(t, _):
        tok_ref[pos_ref[t]] = t
        return 0

    lax.fori_loop(0, n, place, 0, unroll=8)


def _plan(pos):
    n = pos.shape[0]
    nslot = (_num_tiles(n) + 1) * TE
    smem = pl.BlockSpec(memory_space=pltpu.SMEM)
    return pl.pallas_call(
        functools.partial(_plan_kernel, n=n, nslot=nslot),
        in_specs=[smem], out_specs=smem,
        out_shape=jax.ShapeDtypeStruct((nslot,), jnp.int32),
        name="moe_plan",
    )(pos)


def _moe_kernel(tea_ref, teb_ref, nt_ref, tok_ref, h_ref, wg_ref, wu_ref, wd_ref, y_hbm,
                xs, ob0, ob1, ssem, *, n):
    i = pl.program_id(0)
    nt = nt_ref[0]
    obufs = (ob0, ob1)

    def scatter_copy(dst_tok, sl, r):
        dst = y_hbm.at[pl.ds(pl.multiple_of(dst_tok * Y_PITCH, Y_PITCH), Y_PITCH), :]
        return pltpu.make_async_copy(obufs[sl].at[pl.ds(r * O_PITCH, Y_PITCH), :], dst, ssem.at[sl])

    def start_scatters(tile, sl, to_spare):
        for r in range(TE):
            t = tok_ref[tile * TE + r]
            scatter_copy(jnp.where((t < 0) | to_spare, n + sl * TE + r, t), sl, r).start(priority=r % 2)

    def wait_scatters(sl):
        for r in range(TE):
            scatter_copy(n, sl, r).wait()

    @pl.when(i == 0)
    def _():
        for sl in range(2):
            obufs[sl][...] = jnp.zeros_like(obufs[sl])
            spare = pltpu.make_async_copy(obufs[sl].at[pl.ds(0, TE * Y_PITCH), :],
                                          y_hbm.at[pl.ds((n + sl * TE) * Y_PITCH, TE * Y_PITCH), :], ssem.at[sl])
            spare.start()
            spare.wait()

    def step(s):
        o = 1 - s

        @pl.when(i >= 1)
        def _():
            wait_scatters(s)

        start_scatters(jnp.maximum(i - 1, 0), o, i == 0)

        for r in range(TE):
            t = jnp.maximum(tok_ref[i * TE + r], 0)
            xs[pl.ds(r * PACK_SLABS, PACK_SLABS), :] = h_ref[pl.ds(pl.multiple_of(t * PACK_SLABS, PACK_SLABS),
                                                                   PACK_SLABS), :]
        slab = lambda k: xs[pl.ds(k, TE, stride=PACK_SLABS), :]
        words = lax.bitcast_convert_type(jnp.concatenate([slab(k) for k in range(PACK_SLABS)], axis=-1),
                                         jnp.uint32)
        lo = lax.bitcast_convert_type(lax.shift_left(words, jnp.uint32(16)), f32)
        hi = lax.bitcast_convert_type(words & jnp.uint32(0xFFFF0000), f32)
        hb = jnp.concatenate([lo, hi], axis=-1).astype(bf16)

        def ffn(e):
            a = jnp.dot(hb, wg_ref[0, e], preferred_element_type=f32)
            u = jnp.dot(hb, wu_ref[0, e], preferred_element_type=f32)
            act = (a * jax.nn.sigmoid(a)) * u
            return jnp.dot(act.astype(bf16), wd_ref[0, e], preferred_element_type=f32)

        for half, e_ref in enumerate((tea_ref, teb_ref)):
            out = ffn(lax.rem(e_ref[i], N_EXP))
            for k in range(SLABS):
                obufs[s][pl.ds(half * SLABS + k, TE, stride=O_PITCH), :] = out[:, k * LANES:(k + 1) * LANES]

        @pl.when(i == nt - 1)
        def _():
            start_scatters(i, s, False)
            wait_scatters(o)
            wait_scatters(s)

    for s in range(2):
        @pl.when((i < nt) & (lax.rem(i, 2) == s))
        def _(s=s):
            step(s)


def _moe(h2ext, tok, tea, teb, nt, wg, wu, wd):
    n = h2ext.shape[0] // PACK_SLABS
    ntmax = _num_tiles(n)
    once = pl.Buffered(1)
    wspec_in = pl.BlockSpec((1, N_EXP, D, F_EXP), lambda i, tea, teb, nt, tok: (tea[i] // N_EXP, 0, 0, 0),
                            pipeline_mode=once)
    wspec_out = pl.BlockSpec((1, N_EXP, F_EXP, D), lambda i, tea, teb, nt, tok: (tea[i] // N_EXP, 0, 0, 0),
                             pipeline_mode=once)
    hspec = pl.BlockSpec((n * PACK_SLABS, LANES), lambda i, tea, teb, nt, tok: (0, 0), pipeline_mode=once)
    return pl.pallas_call(
        functools.partial(_moe_kernel, n=n),
        grid_spec=pltpu.PrefetchScalarGridSpec(
            num_scalar_prefetch=4,
            grid=(ntmax,),
            in_specs=[hspec, wspec_in, wspec_in, wspec_out],
            out_specs=pl.BlockSpec(memory_space=pl.ANY),
            scratch_shapes=[pltpu.VMEM((TE * PACK_SLABS, LANES), f32)] + [pltpu.VMEM((TE * O_PITCH, LANES), f32)] * 2 + [
                pltpu.SemaphoreType.DMA((2,))]),
        out_shape=jax.ShapeDtypeStruct(((n + 2 * TE) * Y_PITCH, LANES), f32),
        compiler_params=_cparams(("arbitrary",), vmem=60 * 1024 * 1024),
        name="moe_sparse",
    )(tea, teb, nt, tok, h2ext, wg, wu, wd)


def _final_kernel(x_ref, y_ref, ym_ref, g_ref, o_ref):
    o_ref[...] = x_ref[...] + g_ref[0] * _expert_mix(y_ref, ym_ref[...], x_ref.shape[0])


def _final_combine(x1, y, g2, seq):
    n = x1.shape[0]
    t = 512
    spb = seq // t
    row = pl.BlockSpec((t, D), lambda i: (i, 0))
    return pl.pallas_call(
        _final_kernel,
        grid=(n // t,),
        in_specs=[row, pl.BlockSpec((t * Y_PITCH, LANES), lambda i: (i, 0)),
                  pl.BlockSpec((t, LANES), lambda i: (i, 0)),
                  pl.BlockSpec((1, 1, D), lambda i: (i // spb, 0, 0))],
        out_specs=row,
        out_shape=jax.ShapeDtypeStruct((n, D), f32),
        compiler_params=_cparams(("arbitrary",)),
        name="final_combine",
    )(x1, y[0], y[1], g2)


def _seg_matrices(width, segs):
    sd = np.zeros((width, LANES), np.float32)
    ex = np.zeros((LANES, width), np.float32)
    for j, (s, ln) in enumerate(segs):
        sd[s:s + ln, j] = 1.0 / ln
        ex[j, s:s + ln] = 1.0
    return jnp.asarray(sd, bf16), jnp.asarray(np.concatenate([ex, ex], axis=0), bf16)


def _head_pad(w, heads, per_head, keep):
    k = w.shape[0]
    w3 = w.reshape(k, heads, per_head)[:, :, :keep]
    return jnp.pad(w3, ((0, 0), (0, 0), (0, LANES - keep))).reshape(k, heads * LANES)


def kernel(x, c, positions, w_ada, b_ada, norm_g, w_in, diff_qk_g, diff_lambda, diff_out_g, rel_bias, sgu_v_g, sgu_w, sgu_b, mla_lat_g, mla_w_uq, mla_w_ukv, mla_qk_g, w_branch, w_out, router_g_w, router_g_b, router_e_w, router_e_b, w_e_gate, w_e_up, w_e_down):
    nb, seq, _ = x.shape
    n = nb * seq
    assert seq % TQ == 0 and seq % TM == 0 and x.shape[2] == D

    mod = _ada(c, w_ada, b_ada)
    cosf, sinf = _rope_tables(positions)
    t0, t1, mk = _bias_tiles(rel_bias)

    sd_a, ex_a = _seg_matrices(MIX, [(s * 64, 64) for s in range(8)])
    segs_q = []
    for h in range(HC):
        segs_q += [(h * LANES, NOPE), (h * LANES + NOPE, ROPE)]
    sd_q, ex_q = _seg_matrices(HC * LANES, segs_q)
    ne = N_GROUPS * N_EXP
    zpad = lambda a, w_: jnp.pad(a, ((0, 0), (0, w_ - a.shape[1])))

    xcur = x.reshape(n, D)
    y_prev, g2_prev = None, None
    for l in range(DEPTH):
        m3 = mod[l].reshape(nb, 1, 6 * D)
        shift1, scale1, gate1, shift2, scale2, gate2 = [m3[:, :, k * D:(k + 1) * D] for k in range(6)]
        lambda_init = LAMBDA_INIT_BASE - LAMBDA_INIT_SCALE * math.exp(-LAMBDA_INIT_DECAY * l)

        w = w_in[l]
        w_pad = jnp.concatenate([w[:, :XC_OFF + 416], jnp.zeros((D, IN_PAD - IN_REAL), f32), w[:, XC_OFF + 416:]],
                                axis=1).astype(bf16)
        gq = (jnp.tile(diff_qk_g[l, 0], 8) * (64 ** -0.5 * LOG2E)).reshape(1, MIX)
        gk = jnp.tile(diff_qk_g[l, 1], 8).reshape(1, MIX)
        outs = _inproj(xcur, y_prev, g2_prev, shift1, scale1, norm_g[l, 0].reshape(1, D), w_pad, gq, gk,
                       sd_a, ex_a, sgu_v_g[l].reshape(1, MIX), sgu_w[l], jnp.transpose(sgu_b[l]), seq)
        if y_prev is not None:
            xcur = outs[0]
            outs = outs[1:]
        qa, ka, va, ob, xc, gt = outs

        oa = _attn_a(qa, ka, va, t0, t1, diff_lambda[l], diff_out_g[l].reshape(1, LANES), nb, seq, lambda_init)

        qkg = mla_qk_g[l]
        wq = _head_pad(mla_w_uq[l], HC, NOPE + ROPE, NOPE + ROPE).astype(bf16)
        wkv3 = mla_w_ukv[l].reshape(KV_LORA, HC, NOPE + VC)
        wk = jnp.pad(wkv3[:, :, :NOPE], ((0, 0), (0, 0), (0, LANES - NOPE))).reshape(KV_LORA, HC * LANES)
        wv = wkv3[:, :, NOPE:].reshape(KV_LORA, HC * VC)
        wkv = jnp.concatenate([wk, wv], axis=1).astype(bf16)
        gq_c = jnp.tile(jnp.pad(qkg[0], (0, LANES - NOPE - ROPE)), HC).reshape(1, HC * LANES) * ((NOPE + ROPE) ** -0.5 * LOG2E)
        gk_c = jnp.pad(qkg[1, :NOPE], (0, LANES - NOPE)).reshape(1, LANES)
        gr_c = jnp.pad(qkg[1, NOPE:], (0, LANES - ROPE)).reshape(1, LANES)
        qc, kc, vc = _mla_prep(xc, cosf, sinf, mla_lat_g[l, :Q_LORA].reshape(1, Q_LORA),
                               mla_lat_g[l, Q_LORA:].reshape(1, KV_LORA), wq, wkv, gq_c, gk_c, gr_c, sd_q, ex_q)
        oc = _attn_c(qc, kc, vc, mk, nb, seq)

        wr = zpad(jnp.concatenate([router_e_w[l], router_g_w[l]], axis=1), LANES)
        br = zpad(jnp.concatenate([router_e_b[l], router_g_b[l]]).reshape(1, ne + N_GROUPS), LANES)
        x1, h2ext, meta, cnt = _merge(oa, ob, oc, gt, xcur, gate1, shift2, scale2, norm_g[l, 1].reshape(1, D),
                                      w_branch[l].astype(bf16), w_out[l].astype(bf16), wr, br, seq)

        pos, tt = _tables(meta, cnt)
        tok = _plan(pos.reshape(n))
        ntmax = _num_tiles(n)
        y_prev = (_moe(h2ext, tok, tt[:ntmax, 0], tt[:ntmax, 1], tt[0:1, 2], w_e_gate[l].astype(bf16),
                       w_e_up[l].astype(bf16), w_e_down[l].astype(bf16)), meta)
        g2_prev = gate2
        xcur = x1

    out = _final_combine(xcur, y_prev, g2_prev, seq)
    return out.reshape(nb, seq, D)
```

```python
import functools
import math

import jax
import jax.numpy as jnp
import numpy as np
from jax import lax
from jax.experimental import pallas as pl
from jax.experimental.pallas import tpu as pltpu

f32 = jnp.float32
bf16 = jnp.bfloat16

D = 1024
DEPTH = 4
CHUNK = 64
MIX = 512
HA = 4
HC = 8
Q_LORA, KV_LORA, ROPE = 256, 128, 32
NOPE, VC = 64, 64
N_GROUPS, N_EXP, F_EXP = 4, 8, 256
N_REL_BUCKETS = 32
ROPE_THETA = 10000.0
LAMBDA_INIT_BASE, LAMBDA_INIT_SCALE, LAMBDA_INIT_DECAY = 0.8, 0.6, 0.3
EPS = 1e-6
NEG = -1e30

IN_REAL = 6048
IN_PAD = 6144
XC_OFF = 2560
GATE_OFF = 3072

TM = 512
TQ = 256
TK = 256
LANES = 128
VMEM_LIMIT = 56 * 1024 * 1024
SLABS = D // LANES
PACK_SLABS = SLABS // 2
Y_PITCH = SLABS
O_PITCH = Y_PITCH + 1


def _cparams(sem, vmem=VMEM_LIMIT):
    return pltpu.CompilerParams(dimension_semantics=sem, vmem_limit_bytes=vmem)


def _const_spec(shape):
    nd = len(shape)
    return pl.BlockSpec(shape, lambda *_: (0,) * nd)


def _rms(x, g_row):
    return x * lax.rsqrt(jnp.mean(x * x, axis=-1, keepdims=True) + EPS) * g_row


def _rows_from_token_major(y_ref, rows):
    return jnp.concatenate([y_ref[pl.ds(k, rows, stride=Y_PITCH), :] for k in range(SLABS)], axis=-1)


def _seg_rinv(x, sd_ref, ex_ref):
    ss = jnp.dot((x * x).astype(bf16), sd_ref[...], preferred_element_type=f32)
    hi = ss.astype(bf16)
    lo = (ss - hi.astype(f32)).astype(bf16)
    full = jnp.dot(jnp.concatenate([hi, lo], axis=-1), ex_ref[...], preferred_element_type=f32)
    return lax.rsqrt(full + EPS)


def _ada_kernel(c_ref, w_ref, b_ref, o_ref):
    c = c_ref[...]
    ca = (c * jax.nn.sigmoid(c)).astype(bf16)
    o_ref[0] = jnp.dot(ca, w_ref[0].astype(bf16), preferred_element_type=f32) + b_ref[0]


def _ada(c, w_ada, b_ada):
    nb = c.shape[0]
    tn = 1024
    return pl.pallas_call(
        _ada_kernel,
        grid=(DEPTH, 6 * D // tn),
        in_specs=[pl.BlockSpec((nb, D), lambda l, j: (0, 0)),
                  pl.BlockSpec((1, D, tn), lambda l, j: (l, 0, j)),
                  pl.BlockSpec((1, 1, tn), lambda l, j: (l, 0, j))],
        out_specs=pl.BlockSpec((1, nb, tn), lambda l, j: (l, 0, j)),
        out_shape=jax.ShapeDtypeStruct((DEPTH, nb, 6 * D), f32),
        compiler_params=_cparams(("arbitrary", "arbitrary")),
        name="ada_mod",
    )(c, w_ada, b_ada.reshape(DEPTH, 1, 6 * D))


def _rope_kernel(pos_ref, invf_ref, cos_ref, sin_ref):
    ang = pos_ref[...].astype(f32) * invf_ref[...]
    lane = lax.broadcasted_iota(jnp.int32, ang.shape, 1)
    rot = (lane >= NOPE) & (lane < NOPE + ROPE)
    first = lane < NOPE + ROPE // 2
    cos_ref[...] = jnp.where(rot, jnp.cos(ang), 1.0)
    s = jnp.sin(ang)
    sin_ref[...] = jnp.where(rot, jnp.where(first, -s, s), 0.0)


def _rope_tables(positions):
    n = positions.size
    inv_freq = ROPE_THETA ** (-jnp.arange(0, ROPE, 2, dtype=f32) / ROPE)
    invf = jnp.zeros((1, LANES), f32).at[0, NOPE:NOPE + ROPE].set(jnp.tile(inv_freq, 2))
    t = math.gcd(n, 2048)
    return pl.pallas_call(
        _rope_kernel,
        grid=(n // t,),
        in_specs=[pl.BlockSpec((t, 1), lambda i: (i, 0)), _const_spec((1, LANES))],
        out_specs=[pl.BlockSpec((t, LANES), lambda i: (i, 0))] * 2,
        out_shape=[jax.ShapeDtypeStruct((n, LANES), f32)] * 2,
        compiler_params=_cparams(("arbitrary",)),
        name="rope_tables",
    )(positions.reshape(n, 1), invf)


def _t5_bucket(rel):
    n = jnp.abs(rel)
    large = 8 + sum((n >= t).astype(jnp.int32) for t in (12, 16, 23, 32, 46, 64, 91))
    return jnp.where(rel > 0, 16, 0) + jnp.where(n < 8, n, large)


def _bias_kernel(tbl_ref, t0_ref, t1_ref, mk_ref):
    r = lax.broadcasted_iota(jnp.int32, (TQ, TK), 0)
    c = lax.broadcasted_iota(jnp.int32, (TQ, TK), 1)
    allowed = (c // CHUNK) <= (r // CHUNK)
    mk = jnp.where(allowed, 0.0, NEG).astype(f32)
    mk_ref[...] = mk
    b0 = _t5_bucket(c - r)
    b1 = _t5_bucket(c - r - TK)
    for h in range(HA):
        far = tbl_ref[N_REL_BUCKETS // 2 - 1, h]
        t0 = jnp.zeros((TQ, TK), f32)
        t1 = jnp.zeros((TQ, TK), f32)
        for b in range(N_REL_BUCKETS):
            v = (tbl_ref[b, h] - far) * LOG2E
            t0 = jnp.where(b0 == b, v, t0)
            t1 = jnp.where(b1 == b, v, t1)
        t0 = t0 + mk
        t0_ref[h, 0:TQ, :] = t0
        t0_ref[h, TQ:2 * TQ, :] = t0
        t1_ref[h, 0:TQ, :] = t1
        t1_ref[h, TQ:2 * TQ, :] = t1


def _bias_tiles(rel_bias):
    return pl.pallas_call(
        _bias_kernel,
        in_specs=[pl.BlockSpec(memory_space=pltpu.SMEM)],
        out_specs=[pl.BlockSpec(memory_space=pltpu.VMEM)] * 3,
        out_shape=[jax.ShapeDtypeStruct((HA, 2 * TQ, TK), f32),
                   jax.ShapeDtypeStruct((HA, 2 * TQ, TK), f32),
                   jax.ShapeDtypeStruct((TQ, TK), f32)],
        name="bias_tiles",
    )(rel_bias)


def _inproj_kernel(*refs, combine):
    if combine:
        (xa_ref, y_ref, g2_ref, sh_ref, sc_ref, ng_ref, w_ref, gq_ref, gk_ref, sd_ref, ex_ref,
         vg_ref, ws_ref, bs_ref,
         xo_ref, qa_ref, ka_ref, va_ref, ob_ref, xc_ref, gt_ref) = refs
        x = xa_ref[...] + g2_ref[0] * _rows_from_token_major(y_ref, TM)
        xo_ref[...] = x
    else:
        (xa_ref, sh_ref, sc_ref, ng_ref, w_ref, gq_ref, gk_ref, sd_ref, ex_ref,
         vg_ref, ws_ref, bs_ref,
         qa_ref, ka_ref, va_ref, ob_ref, xc_ref, gt_ref) = refs
        x = xa_ref[...]
    h = _rms(x, ng_ref[...]) * (1.0 + sc_ref[0]) + sh_ref[0]
    hb = h.astype(bf16)

    q = jnp.dot(hb, w_ref[:, 0:MIX], preferred_element_type=f32)
    qa_ref[...] = (q * _seg_rinv(q, sd_ref, ex_ref) * gq_ref[...]).astype(bf16)
    k = jnp.dot(hb, w_ref[:, MIX:2 * MIX], preferred_element_type=f32)
    ka_ref[...] = (k * _seg_rinv(k, sd_ref, ex_ref) * gk_ref[...]).astype(bf16)
    va_ref[...] = jnp.dot(hb, w_ref[:, 2 * MIX:3 * MIX], preferred_element_type=f32).astype(bf16)

    z = jnp.dot(hb, w_ref[:, 3 * MIX:5 * MIX], preferred_element_type=f32)
    z = 0.5 * z * (1.0 + jnp.tanh(math.sqrt(2.0 / math.pi) * (z + 0.044715 * (z * z * z))))
    u = z[:, :MIX]
    v = _rms(z[:, MIX:], vg_ref[...]).astype(bf16)
    ri = lax.broadcasted_iota(jnp.int32, (LANES, LANES), 0)
    ci = lax.broadcasted_iota(jnp.int32, (LANES, LANES), 1)
    allowed = (ci // CHUNK) <= (ri // CHUNK)
    for g in range(4):
        wm = jnp.where(allowed, ws_ref[g], 0.0).astype(bf16)
        bcol = bs_ref[:, g:g + 1]
        for wd in range(TM // LANES):
            rows = slice(wd * LANES, (wd + 1) * LANES)
            cols = slice(g * LANES, (g + 1) * LANES)
            vs = jnp.dot(wm, v[rows, cols], preferred_element_type=f32) + bcol
            ob_ref[rows, cols] = (u[rows, cols] * vs).astype(bf16)

    xc_ref[...] = jnp.dot(hb, w_ref[:, XC_OFF:GATE_OFF], preferred_element_type=f32)

    for j in range(3):
        gsl = slice(GATE_OFF + j * D, GATE_OFF + (j + 1) * D)
        gl = jnp.dot(hb, w_ref[:, gsl], preferred_element_type=f32)
        gt_ref[:, j * D:(j + 1) * D] = jax.nn.sigmoid(gl).astype(bf16)


def _inproj(xa, y, g2, shift, scale, ng, w, gq, gk, sd, ex, vg, ws, bs, seq):
    n = xa.shape[0]
    spb = seq // TM
    combine = y is not None
    row = lambda w_: pl.BlockSpec((TM, w_), lambda i: (i, 0))
    modspec = pl.BlockSpec((1, 1, D), lambda i: (i // spb, 0, 0))
    in_specs = [row(D)]
    args = [xa]
    if combine:
        in_specs += [pl.BlockSpec((TM * Y_PITCH, LANES), lambda i: (i, 0)), modspec]
        args += [y, g2]
    in_specs += [modspec, modspec, _const_spec((1, D)),
                 pl.BlockSpec((D, IN_PAD), lambda i: (0, 0), pipeline_mode=pl.Buffered(1)),
                 _const_spec((1, MIX)), _const_spec((1, MIX)),
                 _const_spec((MIX, LANES)), _const_spec((2 * LANES, MIX)),
                 _const_spec((1, MIX)), _const_spec((4, LANES, LANES)), _const_spec((LANES, 4))]
    args += [shift, scale, ng, w, gq, gk, sd, ex, vg, ws, bs]
    out_specs = [row(MIX)] * 4 + [row(MIX), row(3 * D)]
    out_shape = [jax.ShapeDtypeStruct((n, MIX), bf16)] * 4 + [
        jax.ShapeDtypeStruct((n, MIX), f32), jax.ShapeDtypeStruct((n, 3 * D), bf16)]
    if combine:
        out_specs = [row(D)] + out_specs
        out_shape = [jax.ShapeDtypeStruct((n, D), f32)] + out_shape
    return pl.pallas_call(
        functools.partial(_inproj_kernel, combine=combine),
        grid=(n // TM,),
        in_specs=in_specs, out_specs=out_specs, out_shape=out_shape,
        compiler_params=_cparams(("arbitrary",)),
        name="inproj",
    )(*args)


LOG2E = math.log2(math.e)


def _two_pass_attention(nunits, q_of, kcols_of, vcols_of, k_ref, v_ref, qi, diag_bias_of, sub_bias_of,
                        sbuf, mbuf, lbuf, abuf):
    def score_block(j, bias_of):
        rows_k = pl.ds(pl.multiple_of(j * TK, TK), TK)
        for u in range(nunits):
            s = lax.dot_general(q_of(u), k_ref[rows_k, kcols_of(u)], (((1,), (1,)), ((), ())),
                                preferred_element_type=f32)
            if bias_of is not None:
                s = s + bias_of(u)
            sbuf[u, j] = s
            mbuf[u] = jnp.maximum(mbuf[u], jnp.maximum(s[:, :LANES], s[:, LANES:]))

    for u in range(nunits):
        mbuf[u] = jnp.full(mbuf.shape[1:], NEG, f32)
    if sub_bias_of is None:
        nfar = qi
    else:
        nfar = jnp.maximum(qi - 1, 0)

        @pl.when(qi >= 1)
        def _():
            score_block(qi - 1, sub_bias_of)

    def far(j, _):
        score_block(j, None)
        return 0

    lax.fori_loop(0, nfar, far, 0)
    score_block(qi, diag_bias_of)

    for u in range(nunits):
        mbuf[u] = jnp.broadcast_to(jnp.max(mbuf[u], axis=-1, keepdims=True), mbuf.shape[1:])
        lbuf[u] = jnp.zeros(lbuf.shape[1:], f32)
        abuf[u] = jnp.zeros(abuf.shape[1:], f32)

    def accumulate(j, _):
        rows_k = pl.ds(pl.multiple_of(j * TK, TK), TK)
        for u in range(nunits):
            s = sbuf[u, j]
            mb = mbuf[u]
            p0 = jnp.exp2(s[:, :LANES] - mb)
            p1 = jnp.exp2(s[:, LANES:] - mb)
            lbuf[u] = lbuf[u] + (p0 + p1)
            p = jnp.concatenate([p0, p1], axis=-1).astype(bf16)
            abuf[u] = abuf[u] + jnp.dot(p, v_ref[rows_k, vcols_of(u)], preferred_element_type=f32)
        return 0

    lax.fori_loop(0, qi + 1, accumulate, 0)


def _normalised(u, lbuf, abuf):
    return abuf[u] / jnp.sum(lbuf[u], axis=-1, keepdims=True)


GA = 2
GC = 4


def _attn_a_kernel(q_ref, k_ref, v_ref, t0_ref, t1_ref, dl_ref, og_ref, o_ref,
                   qstk, sbuf, mbuf, lbuf, abuf, *, lambda_init):
    qi = pl.program_id(2)
    dl = dl_ref[...]
    lam = (jnp.exp(jnp.sum(dl[0:1] * dl[1:2], axis=-1, keepdims=True))
           - jnp.exp(jnp.sum(dl[2:3] * dl[3:4], axis=-1, keepdims=True)) + lambda_init)
    lane = lax.broadcasted_iota(jnp.int32, (TQ, LANES), 1)
    cols = lambda u: slice(u * LANES, (u + 1) * LANES)
    for u in range(GA):
        qh = q_ref[:, cols(u)]
        zero = jnp.zeros_like(qh)
        qstk[u, 0:TQ, :] = jnp.where(lane < 64, qh, zero)
        qstk[u, TQ:2 * TQ, :] = jnp.where(lane >= 64, qh, zero)
    _two_pass_attention(GA, lambda u: qstk[u], cols, cols, k_ref, v_ref, qi,
                        lambda u: t0_ref[u], lambda u: t1_ref[u], sbuf, mbuf, lbuf, abuf)
    for u in range(GA):
        o = _normalised(u, lbuf, abuf)
        oh = o[:TQ] - lam * o[TQ:]
        oh = _rms(oh, og_ref[...]) * (1.0 - lambda_init)
        o_ref[:, cols(u)] = oh.astype(bf16)


def _attn_a(qa, ka, va, t0, t1, dl, og, nb, seq, lambda_init):
    n = qa.shape[0]
    nq = seq // TQ
    w = GA * LANES
    return pl.pallas_call(
        functools.partial(_attn_a_kernel, lambda_init=lambda_init),
        grid=(nb, HA // GA, nq),
        in_specs=[pl.BlockSpec((TQ, w), lambda b, g, i: (b * nq + i, g)),
                  pl.BlockSpec((seq, w), lambda b, g, i: (b, g)),
                  pl.BlockSpec((seq, w), lambda b, g, i: (b, g)),
                  pl.BlockSpec((GA, 2 * TQ, TK), lambda b, g, i: (g, 0, 0)),
                  pl.BlockSpec((GA, 2 * TQ, TK), lambda b, g, i: (g, 0, 0)),
                  _const_spec((4, 64)), _const_spec((1, LANES))],
        out_specs=pl.BlockSpec((TQ, w), lambda b, g, i: (b * nq + i, g)),
        out_shape=jax.ShapeDtypeStruct((n, MIX), bf16),
        scratch_shapes=[pltpu.VMEM((GA, 2 * TQ, LANES), bf16),
                        pltpu.VMEM((GA, seq // TK, 2 * TQ, TK), f32),
                        pltpu.VMEM((GA, 2 * TQ, LANES), f32), pltpu.VMEM((GA, 2 * TQ, LANES), f32),
                        pltpu.VMEM((GA, 2 * TQ, LANES), f32)],
        compiler_params=_cparams(("arbitrary", "arbitrary", "arbitrary")),
        name="attn_diff",
    )(qa, ka, va, t0, t1, dl, og)


def _attn_c_kernel(q_ref, k_ref, v_ref, mk_ref, o_ref, sbuf, mbuf, lbuf, abuf):
    qi = pl.program_id(2)
    lane = lax.broadcasted_iota(jnp.int32, (TQ, LANES), 1)
    cols = lambda u: slice(u * LANES, (u + 1) * LANES)
    _two_pass_attention(GC, lambda u: q_ref[:, cols(u)], cols, lambda u: cols(u // 2), k_ref, v_ref, qi,
                        lambda u: mk_ref[...], None, sbuf, mbuf, lbuf, abuf)
    for hp in range(GC // 2):
        o_ref[:, cols(hp)] = jnp.where(lane < VC, _normalised(2 * hp, lbuf, abuf),
                                       _normalised(2 * hp + 1, lbuf, abuf)).astype(bf16)


def _attn_c(qc, kc, vc, mk, nb, seq):
    n = qc.shape[0]
    nq = seq // TQ
    w = GC * LANES
    wv = GC * VC
    return pl.pallas_call(
        _attn_c_kernel,
        grid=(nb, HC // GC, nq),
        in_specs=[pl.BlockSpec((TQ, w), lambda b, g, i: (b * nq + i, g)),
                  pl.BlockSpec((seq, w), lambda b, g, i: (b, g)),
                  pl.BlockSpec((seq, wv), lambda b, g, i: (b, g)),
                  _const_spec((TQ, TK))],
        out_specs=pl.BlockSpec((TQ, wv), lambda b, g, i: (b * nq + i, g)),
        out_shape=jax.ShapeDtypeStruct((n, MIX), bf16),
        scratch_shapes=[pltpu.VMEM((GC, seq // TK, TQ, TK), f32),
                        pltpu.VMEM((GC, TQ, LANES), f32), pltpu.VMEM((GC, TQ, LANES), f32),
                        pltpu.VMEM((GC, TQ, LANES), f32)],
        compiler_params=_cparams(("arbitrary", "arbitrary", "arbitrary")),
        name="attn_latent",
    )(qc, kc, vc, mk)


def _rope_apply(x, cosf, sinf, lane):
    w = x.shape[-1]
    partner = jnp.where((lane % LANES) < NOPE + ROPE // 2,
                        pltpu.roll(x, w - ROPE // 2, 1), pltpu.roll(x, ROPE // 2, 1))
    return x * cosf + partner * sinf


def _mla_prep_kernel(xc_ref, cos_ref, sin_ref, glq_ref, glkv_ref, wq_ref, wkv_ref, gq_ref, gk_ref, gr_ref,
                     sd_ref, ex_ref, q_ref, k_ref, v_ref):
    cq = _rms(xc_ref[:, 0:Q_LORA], glq_ref[...]).astype(bf16)
    ckv = _rms(xc_ref[:, Q_LORA:Q_LORA + KV_LORA], glkv_ref[...]).astype(bf16)
    cos8 = jnp.concatenate([cos_ref[...]] * HC, axis=-1)
    sin8 = jnp.concatenate([sin_ref[...]] * HC, axis=-1)
    lane8 = lax.broadcasted_iota(jnp.int32, (TM, HC * LANES), 1)

    q = jnp.dot(cq, wq_ref[...], preferred_element_type=f32)
    q = q * _seg_rinv(q, sd_ref, ex_ref) * gq_ref[...]
    q_ref[...] = _rope_apply(q, cos8, sin8, lane8).astype(bf16)

    kv = jnp.dot(ckv, wkv_ref[...], preferred_element_type=f32)
    v_ref[...] = kv[:, HC * LANES:].astype(bf16)
    xr = xc_ref[:, Q_LORA + KV_LORA:Q_LORA + KV_LORA + LANES]
    kr = xr * lax.rsqrt(jnp.sum(xr * xr, axis=-1, keepdims=True) * (1.0 / ROPE) + EPS) * gr_ref[...]
    kr = pltpu.roll(kr, NOPE, 1)
    lane1 = lax.broadcasted_iota(jnp.int32, (TM, LANES), 1)
    kr = _rope_apply(kr, cos_ref[...], sin_ref[...], lane1)
    for h in range(HC):
        cols = slice(h * LANES, (h + 1) * LANES)
        kn = kv[:, cols]
        kn = kn * lax.rsqrt(jnp.sum(kn * kn, axis=-1, keepdims=True) * (1.0 / NOPE) + EPS) * gk_ref[...]
        k_ref[:, cols] = (kn + kr).astype(bf16)


def _mla_prep(xc, cosf, sinf, glq, glkv, wq, wkv, gq, gk, gr, sd, ex):
    n = xc.shape[0]
    row = lambda w_: pl.BlockSpec((TM, w_), lambda i: (i, 0))
    return pl.pallas_call(
        _mla_prep_kernel,
        grid=(n // TM,),
        in_specs=[row(MIX), row(LANES), row(LANES), _const_spec((1, Q_LORA)), _const_spec((1, KV_LORA)),
                  _const_spec((Q_LORA, HC * LANES)), _const_spec((KV_LORA, HC * LANES + MIX)),
                  _const_spec((1, HC * LANES)), _const_spec((1, LANES)), _const_spec((1, LANES)),
                  _const_spec((HC * LANES, LANES)), _const_spec((2 * LANES, HC * LANES))],
        out_specs=[row(HC * LANES), row(HC * LANES), row(MIX)],
        out_shape=[jax.ShapeDtypeStruct((n, HC * LANES), bf16), jax.ShapeDtypeStruct((n, HC * LANES), bf16),
                   jax.ShapeDtypeStruct((n, MIX), bf16)],
        compiler_params=_cparams(("arbitrary",)),
        name="mla_prep",
    )(xc, cosf, sinf, glq, glkv, wq, wkv, gq, gk, gr, sd, ex)


def _merge_kernel(oa_ref, ob_ref, oc_ref, gt_ref, x_ref, g1_ref, sh_ref, sc_ref, ng_ref, wb_ref, wo_ref,
                  wr_ref, br_ref, x1_ref, h2_ref, meta_ref, cnt_ref, run_ref):
    i = pl.program_id(0)

    @pl.when(i == 0)
    def _():
        run_ref[...] = jnp.zeros_like(run_ref)

    merged = jnp.zeros((TM, D), f32)
    for j, o_ref in enumerate((oa_ref, ob_ref, oc_ref)):
        pj = jnp.dot(o_ref[...], wb_ref[j], preferred_element_type=f32)
        merged = merged + gt_ref[:, j * D:(j + 1) * D].astype(f32) * pj
    y = jnp.dot(merged.astype(bf16), wo_ref[...], preferred_element_type=f32)
    x1 = x_ref[...] + g1_ref[0] * y
    x1_ref[...] = x1
    h2 = _rms(x1, ng_ref[...]) * (1.0 + sc_ref[0]) + sh_ref[0]
    hr = h2.astype(bf16).astype(f32)
    lo = lax.shift_right_logical(lax.bitcast_convert_type(hr[:, 0:D // 2], jnp.uint32), jnp.uint32(16))
    hi = lax.bitcast_convert_type(hr[:, D // 2:D], jnp.uint32) & jnp.uint32(0xFFFF0000)
    words = lax.bitcast_convert_type(lo | hi, f32)
    for s in range(PACK_SLABS):
        h2_ref[pl.ds(s, TM, stride=PACK_SLABS), :] = words[:, s * LANES:(s + 1) * LANES]

    lg = jnp.dot(h2, wr_ref[...], preferred_element_type=f32, precision=lax.Precision.HIGHEST) + br_ref[...]
    lane = lax.broadcasted_iota(jnp.int32, (TM, LANES), 1)
    n_e = N_GROUPS * N_EXP
    gmask = (lane >= n_e) & (lane < n_e + N_GROUPS)
    gl = jnp.where(gmask, lg, NEG)
    gmax = jnp.max(gl, axis=-1, keepdims=True)
    g_w = 1.0 / jnp.sum(jnp.where(gmask, jnp.exp(gl - gmax), 0.0), axis=-1, keepdims=True)
    gidx = jnp.min(jnp.where(gl == gmax, lane - n_e, LANES), axis=-1, keepdims=True)
    emask = (lane < n_e) & ((lane // N_EXP) == gidx)
    el = jnp.where(emask, lg, NEG)
    m1 = jnp.max(el, axis=-1, keepdims=True)
    i1 = jnp.min(jnp.where(el == m1, lane, LANES), axis=-1, keepdims=True)
    el2 = jnp.where(lane == i1, NEG, el)
    m2 = jnp.max(el2, axis=-1, keepdims=True)
    i2 = jnp.min(jnp.where(el2 == m2, lane, LANES), axis=-1, keepdims=True)
    t = jnp.exp(m2 - m1)
    w1 = g_w / (1.0 + t)
    w2 = g_w * t / (1.0 + t)
    lo = jnp.minimum(i1, i2)
    hi = jnp.maximum(i1, i2)
    w_lo = jnp.where(i1 < i2, w1, w2)
    w_hi = jnp.where(i1 < i2, w2, w1)
    bucket = gidx * (N_EXP * N_EXP) + (lo % N_EXP) * N_EXP + (hi % N_EXP)

    lane2 = lax.broadcasted_iota(jnp.int32, (TM, 2 * LANES), 1)
    onehot = (lane2 == bucket).astype(f32)
    rr = lax.broadcasted_iota(jnp.int32, (TM, TM), 0)
    cc = lax.broadcasted_iota(jnp.int32, (TM, TM), 1)
    tri = (cc < rr).astype(bf16)
    before = jnp.dot(tri, onehot.astype(bf16), preferred_element_type=f32) + run_ref[...]
    rank = jnp.sum(onehot * before, axis=-1, keepdims=True)
    run_ref[...] = run_ref[...] + jnp.sum(onehot, axis=0, keepdims=True)
    cnt_ref[...] = run_ref[...]

    meta = (jnp.where(lane == i1, w1, 0.0) + jnp.where(lane == i2, w2, 0.0)
            + jnp.where(lane == 32, bucket.astype(f32), 0.0) + jnp.where(lane == 33, rank, 0.0)
            + jnp.where(lane == 34, w_lo, 0.0) + jnp.where(lane == 35, w_hi, 0.0))
    meta_ref[...] = meta


def _merge(oa, ob, oc, gt, x, g1, shift, scale, ng, wb, wo, wr, br, seq):
    n = x.shape[0]
    spb = seq // TM
    row = lambda w_: pl.BlockSpec((TM, w_), lambda i: (i, 0))
    modspec = pl.BlockSpec((1, 1, D), lambda i: (i // spb, 0, 0))
    return pl.pallas_call(
        _merge_kernel,
        grid=(n // TM,),
        in_specs=[row(MIX), row(MIX), row(MIX), row(3 * D), row(D), modspec, modspec, modspec,
                  _const_spec((1, D)), _const_spec((3, MIX, D)), _const_spec((D, D)),
                  _const_spec((D, LANES)), _const_spec((1, LANES))],
        out_specs=[row(D), pl.BlockSpec((TM * PACK_SLABS, LANES), lambda i: (i, 0)), row(LANES),
                   _const_spec((1, 2 * LANES))],
        out_shape=[jax.ShapeDtypeStruct((n, D), f32), jax.ShapeDtypeStruct((n * PACK_SLABS, LANES), f32),
                   jax.ShapeDtypeStruct((n, LANES), f32), jax.ShapeDtypeStruct((1, 2 * LANES), f32)],
        scratch_shapes=[pltpu.VMEM((1, 2 * LANES), f32)],
        compiler_params=_cparams(("arbitrary",)),
        name="merge_route",
    )(oa, ob, oc, gt, x, g1, shift, scale, ng, wb, wo, wr, br)


TE = 128
N_BUCKETS = N_GROUPS * N_EXP * N_EXP
N_PAIRS = N_GROUPS * (N_EXP * (N_EXP - 1) // 2)


def _num_tiles(n):
    return n // TE + N_PAIRS


TP = 2048


def _tables_kernel(meta_ref, cnt_ref, pos_ref, tt_ref):
    cnt = cnt_ref[...]
    ntile = jnp.floor((cnt + (TE - 1)) * (1.0 / TE))
    bi = lax.broadcasted_iota(jnp.int32, (N_BUCKETS, N_BUCKETS), 0)
    bj = lax.broadcasted_iota(jnp.int32, (N_BUCKETS, N_BUCKETS), 1)
    upper = (bi <= bj).astype(bf16)
    incl = jnp.dot(jnp.broadcast_to(ntile, (8, N_BUCKETS)).astype(bf16), upper, preferred_element_type=f32)[0:1]
    excl = incl - ntile

    meta = meta_ref[...]
    tp = meta.shape[0]
    lane = lax.broadcasted_iota(jnp.int32, (tp, LANES), 1)
    bucket = jnp.sum(jnp.where(lane == 32, meta, 0.0), axis=-1, keepdims=True).astype(jnp.int32)
    rank = jnp.sum(jnp.where(lane == 33, meta, 0.0), axis=-1, keepdims=True)
    lane2 = lax.broadcasted_iota(jnp.int32, (tp, N_BUCKETS), 1)
    first_tile = jnp.sum(jnp.where(lane2 == bucket, excl, 0.0), axis=-1, keepdims=True)
    pos_ref[...] = (first_tile * TE + rank).astype(jnp.int32)

    @pl.when(pl.program_id(0) == 0)
    def _():
        tau = lax.broadcasted_iota(jnp.int32, (N_BUCKETS, 1), 0).astype(f32)
        lane_b = lax.broadcasted_iota(jnp.int32, (1, N_BUCKETS), 1).astype(f32)
        tb = jnp.sum((incl <= tau).astype(f32), axis=-1, keepdims=True)
        last_b = jnp.max(jnp.where(cnt > 0, lane_b, 0.0), axis=-1, keepdims=True)
        tbi = jnp.minimum(tb, last_b).astype(jnp.int32)
        grp = lax.shift_right_logical(tbi, 6) * N_EXP
        ea = grp + (lax.shift_right_logical(tbi, 3) & 7)
        eb = grp + (tbi & 7)
        total = jnp.sum(ntile, axis=-1, keepdims=True).astype(jnp.int32)
        lane_t = lax.broadcasted_iota(jnp.int32, (N_BUCKETS, LANES), 1)
        tt_ref[...] = jnp.where(lane_t == 0, ea, jnp.where(lane_t == 1, eb, jnp.where(lane_t == 2, total, 0)))


def _tables(meta, cnt):
    n = meta.shape[0]
    tp = math.gcd(n, TP)
    assert _num_tiles(n) <= N_BUCKETS
    return pl.pallas_call(
        _tables_kernel,
        grid=(n // tp,),
        in_specs=[pl.BlockSpec((tp, LANES), lambda i: (i, 0)), _const_spec((1, N_BUCKETS))],
        out_specs=[pl.BlockSpec((tp, 1), lambda i: (i, 0)), _const_spec((N_BUCKETS, LANES))],
        out_shape=[jax.ShapeDtypeStruct((n, 1), jnp.int32), jax.ShapeDtypeStruct((N_BUCKETS, LANES), jnp.int32)],
        compiler_params=_cparams(("arbitrary",)),
        name="moe_tables",
    )(meta, cnt)


def _plan_kernel(pos_ref, tok_ref, *, n, nslot):
    def fill(s, _):
        tok_ref[s] = -1
        return 0

    lax.fori_loop(0, nslot, fill, 0, unroll=16)

    def place(t, _):
        tok_ref[pos_ref[t]] = t
        return 0

    lax.fori_loop(0, n, place, 0, unroll=16)


def _plan(pos):
    n = pos.shape[0]
    nslot = _num_tiles(n) * TE
    smem = pl.BlockSpec(memory_space=pltpu.SMEM)
    return pl.pallas_call(
        functools.partial(_plan_kernel, n=n, nslot=nslot),
        in_specs=[smem], out_specs=smem,
        out_shape=jax.ShapeDtypeStruct((nslot,), jnp.int32),
        name="moe_plan",
    )(pos)


def _moe_kernel(tea_ref, teb_ref, nt_ref, tok_ref, h_ref, m_ref, wg_ref, wu_ref, wd_ref, y_hbm,
                xs, ms, ob0, ob1, ssem, *, n):
    i = pl.program_id(0)
    nt = nt_ref[0]
    obufs = (ob0, ob1)

    def scatter_copy(dst_tok, sl, r):
        dst = y_hbm.at[pl.ds(pl.multiple_of(dst_tok * Y_PITCH, Y_PITCH), Y_PITCH), :]
        return pltpu.make_async_copy(obufs[sl].at[pl.ds(r * O_PITCH, Y_PITCH), :], dst, ssem.at[sl])

    def start_scatters(tile, sl, to_spare):
        for r in range(TE):
            t = tok_ref[tile * TE + r]
            scatter_copy(jnp.where((t < 0) | to_spare, n + sl * TE + r, t), sl, r).start(priority=r % 2)

    def wait_scatters(sl):
        for r in range(TE):
            scatter_copy(n, sl, r).wait()

    @pl.when(i == 0)
    def _():
        for sl in range(2):
            obufs[sl][...] = jnp.zeros_like(obufs[sl])
            spare = pltpu.make_async_copy(obufs[sl].at[pl.ds(0, TE * Y_PITCH), :],
                                          y_hbm.at[pl.ds((n + sl * TE) * Y_PITCH, TE * Y_PITCH), :], ssem.at[sl])
            spare.start()
            spare.wait()

    def step(s):
        o = 1 - s

        @pl.when(i >= 1)
        def _():
            wait_scatters(s)

        start_scatters(jnp.maximum(i - 1, 0), o, i == 0)

        for r in range(TE):
            t = jnp.maximum(tok_ref[i * TE + r], 0)
            xs[pl.ds(r * PACK_SLABS, PACK_SLABS), :] = h_ref[pl.ds(pl.multiple_of(t * PACK_SLABS, PACK_SLABS),
                                                                   PACK_SLABS), :]
            ms[pl.ds(r, 1), :] = m_ref[pl.ds(t, 1), :]
        slab = lambda k: xs[pl.ds(k, TE, stride=PACK_SLABS), :]
        words = lax.bitcast_convert_type(jnp.concatenate([slab(k) for k in range(PACK_SLABS)], axis=-1),
                                         jnp.uint32)
        lo = lax.bitcast_convert_type(lax.shift_left(words, jnp.uint32(16)), f32)
        hi = lax.bitcast_convert_type(words & jnp.uint32(0xFFFF0000), f32)
        hb = jnp.concatenate([lo, hi], axis=-1).astype(bf16)
        meta = ms[...]
        lane = lax.broadcasted_iota(jnp.int32, (TE, LANES), 1)
        w_lo = jnp.sum(jnp.where(lane == 34, meta, 0.0), axis=-1, keepdims=True)
        w_hi = jnp.sum(jnp.where(lane == 35, meta, 0.0), axis=-1, keepdims=True)

        def ffn(e, wcol):
            a = jnp.dot(hb, wg_ref[0, e], preferred_element_type=f32)
            u = jnp.dot(hb, wu_ref[0, e], preferred_element_type=f32)
            act = (a * jax.nn.sigmoid(a)) * u * wcol
            return jnp.dot(act.astype(bf16), wd_ref[0, e], preferred_element_type=f32)

        out = ffn(lax.rem(tea_ref[i], N_EXP), w_lo) + ffn(lax.rem(teb_ref[i], N_EXP), w_hi)
        for k in range(SLABS):
            obufs[s][pl.ds(k, TE, stride=O_PITCH), :] = out[:, k * LANES:(k + 1) * LANES]

        @pl.when(i == nt - 1)
        def _():
            start_scatters(i, s, False)
            wait_scatters(o)
            wait_scatters(s)

    for s in range(2):
        @pl.when((i < nt) & (lax.rem(i, 2) == s))
        def _(s=s):
            step(s)


def _moe(h2ext, meta, tok, tea, teb, nt, wg, wu, wd):
    n = h2ext.shape[0] // PACK_SLABS
    ntmax = _num_tiles(n)
    once = pl.Buffered(1)
    wspec_in = pl.BlockSpec((1, N_EXP, D, F_EXP), lambda i, tea, teb, nt, tok: (tea[i] // N_EXP, 0, 0, 0),
                            pipeline_mode=once)
    wspec_out = pl.BlockSpec((1, N_EXP, F_EXP, D), lambda i, tea, teb, nt, tok: (tea[i] // N_EXP, 0, 0, 0),
                             pipeline_mode=once)
    hspec = pl.BlockSpec((n * PACK_SLABS, LANES), lambda i, tea, teb, nt, tok: (0, 0), pipeline_mode=once)
    mspec = pl.BlockSpec((n, LANES), lambda i, tea, teb, nt, tok: (0, 0), pipeline_mode=once)
    return pl.pallas_call(
        functools.partial(_moe_kernel, n=n),
        grid_spec=pltpu.PrefetchScalarGridSpec(
            num_scalar_prefetch=4,
            grid=(ntmax,),
            in_specs=[hspec, mspec, wspec_in, wspec_in, wspec_out],
            out_specs=pl.BlockSpec(memory_space=pl.ANY),
            scratch_shapes=[pltpu.VMEM((TE * PACK_SLABS, LANES), f32), pltpu.VMEM((TE, LANES), f32)] + [
                pltpu.VMEM((TE * O_PITCH, LANES), f32)] * 2 + [pltpu.SemaphoreType.DMA((2,))]),
        out_shape=jax.ShapeDtypeStruct(((n + 2 * TE) * Y_PITCH, LANES), f32),
        compiler_params=_cparams(("arbitrary",), vmem=60 * 1024 * 1024),
        name="moe_sparse",
    )(tea, teb, nt, tok, h2ext, meta, wg, wu, wd)


def _final_kernel(x_ref, y_ref, g_ref, o_ref):
    o_ref[...] = x_ref[...] + g_ref[0] * _rows_from_token_major(y_ref, x_ref.shape[0])


def _final_combine(x1, y, g2, seq):
    n = x1.shape[0]
    t = 512
    spb = seq // t
    row = pl.BlockSpec((t, D), lambda i: (i, 0))
    return pl.pallas_call(
        _final_kernel,
        grid=(n // t,),
        in_specs=[row, pl.BlockSpec((t * Y_PITCH, LANES), lambda i: (i, 0)),
                  pl.BlockSpec((1, 1, D), lambda i: (i // spb, 0, 0))],
        out_specs=row,
        out_shape=jax.ShapeDtypeStruct((n, D), f32),
        compiler_params=_cparams(("arbitrary",)),
        name="final_combine",
    )(x1, y, g2)


def _seg_matrices(width, segs):
    sd = np.zeros((width, LANES), np.float32)
    ex = np.zeros((LANES, width), np.float32)
    for j, (s, ln) in enumerate(segs):
        sd[s:s + ln, j] = 1.0 / ln
        ex[j, s:s + ln] = 1.0
    return jnp.asarray(sd, bf16), jnp.asarray(np.concatenate([ex, ex], axis=0), bf16)


def _head_pad(w, heads, per_head, keep):
    k = w.shape[0]
    w3 = w.reshape(k, heads, per_head)[:, :, :keep]
    return jnp.pad(w3, ((0, 0), (0, 0), (0, LANES - keep))).reshape(k, heads * LANES)


def kernel(x, c, positions, w_ada, b_ada, norm_g, w_in, diff_qk_g, diff_lambda, diff_out_g, rel_bias, sgu_v_g, sgu_w, sgu_b, mla_lat_g, mla_w_uq, mla_w_ukv, mla_qk_g, w_branch, w_out, router_g_w, router_g_b, router_e_w, router_e_b, w_e_gate, w_e_up, w_e_down):
    nb, seq, _ = x.shape
    n = nb * seq
    assert seq % TQ == 0 and seq % TM == 0 and x.shape[2] == D

    mod = _ada(c, w_ada, b_ada)
    cosf, sinf = _rope_tables(positions)
    t0, t1, mk = _bias_tiles(rel_bias)

    sd_a, ex_a = _seg_matrices(MIX, [(s * 64, 64) for s in range(8)])
    segs_q = []
    for h in range(HC):
        segs_q += [(h * LANES, NOPE), (h * LANES + NOPE, ROPE)]
    sd_q, ex_q = _seg_matrices(HC * LANES, segs_q)
    ne = N_GROUPS * N_EXP
    zpad = lambda a, w_: jnp.pad(a, ((0, 0), (0, w_ - a.shape[1])))

    xcur = x.reshape(n, D)
    y_prev, g2_prev = None, None
    for l in range(DEPTH):
        m3 = mod[l].reshape(nb, 1, 6 * D)
        shift1, scale1, gate1, shift2, scale2, gate2 = [m3[:, :, k * D:(k + 1) * D] for k in range(6)]
        lambda_init = LAMBDA_INIT_BASE - LAMBDA_INIT_SCALE * math.exp(-LAMBDA_INIT_DECAY * l)

        w = w_in[l]
        w_pad = jnp.concatenate([w[:, :XC_OFF + 416], jnp.zeros((D, IN_PAD - IN_REAL), f32), w[:, XC_OFF + 416:]],
                                axis=1).astype(bf16)
        gq = (jnp.tile(diff_qk_g[l, 0], 8) * (64 ** -0.5 * LOG2E)).reshape(1, MIX)
        gk = jnp.tile(diff_qk_g[l, 1], 8).reshape(1, MIX)
        outs = _inproj(xcur, y_prev, g2_prev, shift1, scale1, norm_g[l, 0].reshape(1, D), w_pad, gq, gk,
                       sd_a, ex_a, sgu_v_g[l].reshape(1, MIX), sgu_w[l], jnp.transpose(sgu_b[l]), seq)
        if y_prev is not None:
            xcur = outs[0]
            outs = outs[1:]
        qa, ka, va, ob, xc, gt = outs

        oa = _attn_a(qa, ka, va, t0, t1, diff_lambda[l], diff_out_g[l].reshape(1, LANES), nb, seq, lambda_init)

        qkg = mla_qk_g[l]
        wq = _head_pad(mla_w_uq[l], HC, NOPE + ROPE, NOPE + ROPE).astype(bf16)
        wkv3 = mla_w_ukv[l].reshape(KV_LORA, HC, NOPE + VC)
        wk = jnp.pad(wkv3[:, :, :NOPE], ((0, 0), (0, 0), (0, LANES - NOPE))).reshape(KV_LORA, HC * LANES)
        wv = wkv3[:, :, NOPE:].reshape(KV_LORA, HC * VC)
        wkv = jnp.concatenate([wk, wv], axis=1).astype(bf16)
        gq_c = jnp.tile(jnp.pad(qkg[0], (0, LANES - NOPE - ROPE)), HC).reshape(1, HC * LANES) * ((NOPE + ROPE) ** -0.5 * LOG2E)
        gk_c = jnp.pad(qkg[1, :NOPE], (0, LANES - NOPE)).reshape(1, LANES)
        gr_c = jnp.pad(qkg[1, NOPE:], (0, LANES - ROPE)).reshape(1, LANES)
        qc, kc, vc = _mla_prep(xc, cosf, sinf, mla_lat_g[l, :Q_LORA].reshape(1, Q_LORA),
                               mla_lat_g[l, Q_LORA:].reshape(1, KV_LORA), wq, wkv, gq_c, gk_c, gr_c, sd_q, ex_q)
        oc = _attn_c(qc, kc, vc, mk, nb, seq)

        wr = zpad(jnp.concatenate([router_e_w[l], router_g_w[l]], axis=1), LANES)
        br = zpad(jnp.concatenate([router_e_b[l], router_g_b[l]]).reshape(1, ne + N_GROUPS), LANES)
        x1, h2ext, meta, cnt = _merge(oa, ob, oc, gt, xcur, gate1, shift2, scale2, norm_g[l, 1].reshape(1, D),
                                      w_branch[l].astype(bf16), w_out[l].astype(bf16), wr, br, seq)

        pos, tt = _tables(meta, cnt)
        tok = _plan(pos.reshape(n))
        ntmax = _num_tiles(n)
        y_prev = _moe(h2ext, meta, tok, tt[:ntmax, 0], tt[:ntmax, 1], tt[0:1, 2], w_e_gate[l].astype(bf16),
                      w_e_up[l].astype(bf16), w_e_down[l].astype(bf16))
        g2_prev = gate2
        xcur = x1

    out = _final_combine(xcur, y_prev, g2_prev, seq)
    return out.reshape(nb, seq, D)
```

```python
import functools
import math

import jax
import jax.numpy as jnp
import numpy as np
from jax import lax
from jax.experimental import pallas as pl
from jax.experimental.pallas import tpu as pltpu

f32 = jnp.float32
bf16 = jnp.bfloat16

D = 1024
DEPTH = 4
CHUNK = 64
MIX = 512
HA = 4
HC = 8
Q_LORA, KV_LORA, ROPE = 256, 128, 32
NOPE, VC = 64, 64
N_GROUPS, N_EXP, F_EXP = 4, 8, 256
N_REL_BUCKETS = 32
ROPE_THETA = 10000.0
LAMBDA_INIT_BASE, LAMBDA_INIT_SCALE, LAMBDA_INIT_DECAY = 0.8, 0.6, 0.3
EPS = 1e-6
NEG = -1e30

IN_REAL = 6048
IN_PAD = 6144
XC_OFF = 2560
GATE_OFF = 3072

TM = 512
TQ = 256
TK = 256
LANES = 128
VMEM_LIMIT = 56 * 1024 * 1024
SLABS = D // LANES
PACK_SLABS = SLABS // 2
Y_PITCH = SLABS
O_PITCH = Y_PITCH + 1


def _cparams(sem, vmem=VMEM_LIMIT):
    return pltpu.CompilerParams(dimension_semantics=sem, vmem_limit_bytes=vmem)


def _const_spec(shape):
    nd = len(shape)
    return pl.BlockSpec(shape, lambda *_: (0,) * nd)


def _rms(x, g_row):
    return x * lax.rsqrt(jnp.mean(x * x, axis=-1, keepdims=True) + EPS) * g_row


def _rows_from_token_major(y_ref, rows):
    return jnp.concatenate([y_ref[pl.ds(k, rows, stride=Y_PITCH), :] for k in range(SLABS)], axis=-1)


def _seg_rinv(x, sd_ref, ex_ref):
    ss = jnp.dot((x * x).astype(bf16), sd_ref[...], preferred_element_type=f32)
    hi = ss.astype(bf16)
    lo = (ss - hi.astype(f32)).astype(bf16)
    full = jnp.dot(jnp.concatenate([hi, lo], axis=-1), ex_ref[...], preferred_element_type=f32)
    return lax.rsqrt(full + EPS)


def _ada_kernel(c_ref, w_ref, b_ref, o_ref):
    c = c_ref[...]
    ca = (c * jax.nn.sigmoid(c)).astype(bf16)
    o_ref[0] = jnp.dot(ca, w_ref[0].astype(bf16), preferred_element_type=f32) + b_ref[0]


def _ada(c, w_ada, b_ada):
    nb = c.shape[0]
    tn = 1024
    return pl.pallas_call(
        _ada_kernel,
        grid=(DEPTH, 6 * D // tn),
        in_specs=[pl.BlockSpec((nb, D), lambda l, j: (0, 0)),
                  pl.BlockSpec((1, D, tn), lambda l, j: (l, 0, j)),
                  pl.BlockSpec((1, 1, tn), lambda l, j: (l, 0, j))],
        out_specs=pl.BlockSpec((1, nb, tn), lambda l, j: (l, 0, j)),
        out_shape=jax.ShapeDtypeStruct((DEPTH, nb, 6 * D), f32),
        compiler_params=_cparams(("arbitrary", "arbitrary")),
        name="ada_mod",
    )(c, w_ada, b_ada.reshape(DEPTH, 1, 6 * D))


def _rope_kernel(pos_ref, invf_ref, cos_ref, sin_ref):
    ang = pos_ref[...].astype(f32) * invf_ref[...]
    lane = lax.broadcasted_iota(jnp.int32, ang.shape, 1)
    rot = (lane >= NOPE) & (lane < NOPE + ROPE)
    first = lane < NOPE + ROPE // 2
    cos_ref[...] = jnp.where(rot, jnp.cos(ang), 1.0)
    s = jnp.sin(ang)
    sin_ref[...] = jnp.where(rot, jnp.where(first, -s, s), 0.0)


def _rope_tables(positions):
    n = positions.size
    inv_freq = ROPE_THETA ** (-jnp.arange(0, ROPE, 2, dtype=f32) / ROPE)
    invf = jnp.zeros((1, LANES), f32).at[0, NOPE:NOPE + ROPE].set(jnp.tile(inv_freq, 2))
    t = math.gcd(n, 2048)
    return pl.pallas_call(
        _rope_kernel,
        grid=(n // t,),
        in_specs=[pl.BlockSpec((t, 1), lambda i: (i, 0)), _const_spec((1, LANES))],
        out_specs=[pl.BlockSpec((t, LANES), lambda i: (i, 0))] * 2,
        out_shape=[jax.ShapeDtypeStruct((n, LANES), f32)] * 2,
        compiler_params=_cparams(("arbitrary",)),
        name="rope_tables",
    )(positions.reshape(n, 1), invf)


def _t5_bucket(rel):
    n = jnp.abs(rel)
    large = 8 + sum((n >= t).astype(jnp.int32) for t in (12, 16, 23, 32, 46, 64, 91))
    return jnp.where(rel > 0, 16, 0) + jnp.where(n < 8, n, large)


def _bias_kernel(tbl_ref, t0_ref, t1_ref, mk_ref):
    r = lax.broadcasted_iota(jnp.int32, (TQ, TK), 0)
    c = lax.broadcasted_iota(jnp.int32, (TQ, TK), 1)
    allowed = (c // CHUNK) <= (r // CHUNK)
    mk = jnp.where(allowed, 0.0, NEG).astype(f32)
    mk_ref[...] = mk
    b0 = _t5_bucket(c - r)
    b1 = _t5_bucket(c - r - TK)
    for h in range(HA):
        far = tbl_ref[N_REL_BUCKETS // 2 - 1, h]
        t0 = jnp.zeros((TQ, TK), f32)
        t1 = jnp.zeros((TQ, TK), f32)
        for b in range(N_REL_BUCKETS):
            v = (tbl_ref[b, h] - far) * LOG2E
            t0 = jnp.where(b0 == b, v, t0)
            t1 = jnp.where(b1 == b, v, t1)
        t0 = t0 + mk
        t0_ref[h, 0:TQ, :] = t0
        t0_ref[h, TQ:2 * TQ, :] = t0
        t1_ref[h, 0:TQ, :] = t1
        t1_ref[h, TQ:2 * TQ, :] = t1


def _bias_tiles(rel_bias):
    return pl.pallas_call(
        _bias_kernel,
        in_specs=[pl.BlockSpec(memory_space=pltpu.SMEM)],
        out_specs=[pl.BlockSpec(memory_space=pltpu.VMEM)] * 3,
        out_shape=[jax.ShapeDtypeStruct((HA, 2 * TQ, TK), f32),
                   jax.ShapeDtypeStruct((HA, 2 * TQ, TK), f32),
                   jax.ShapeDtypeStruct((TQ, TK), f32)],
        name="bias_tiles",
    )(rel_bias)


def _inproj_kernel(*refs, combine):
    if combine:
        (xa_ref, y_ref, g2_ref, sh_ref, sc_ref, ng_ref, w_ref, gq_ref, gk_ref, sd_ref, ex_ref,
         vg_ref, ws_ref, bs_ref,
         xo_ref, qa_ref, ka_ref, va_ref, ob_ref, xc_ref, gt_ref) = refs
        x = xa_ref[...] + g2_ref[0] * _rows_from_token_major(y_ref, TM)
        xo_ref[...] = x
    else:
        (xa_ref, sh_ref, sc_ref, ng_ref, w_ref, gq_ref, gk_ref, sd_ref, ex_ref,
         vg_ref, ws_ref, bs_ref,
         qa_ref, ka_ref, va_ref, ob_ref, xc_ref, gt_ref) = refs
        x = xa_ref[...]
    h = _rms(x, ng_ref[...]) * (1.0 + sc_ref[0]) + sh_ref[0]
    hb = h.astype(bf16)

    q = jnp.dot(hb, w_ref[:, 0:MIX], preferred_element_type=f32)
    qa_ref[...] = (q * _seg_rinv(q, sd_ref, ex_ref) * gq_ref[...]).astype(bf16)
    k = jnp.dot(hb, w_ref[:, MIX:2 * MIX], preferred_element_type=f32)
    ka_ref[...] = (k * _seg_rinv(k, sd_ref, ex_ref) * gk_ref[...]).astype(bf16)
    va_ref[...] = jnp.dot(hb, w_ref[:, 2 * MIX:3 * MIX], preferred_element_type=f32).astype(bf16)

    z = jnp.dot(hb, w_ref[:, 3 * MIX:5 * MIX], preferred_element_type=f32)
    z = 0.5 * z * (1.0 + jnp.tanh(math.sqrt(2.0 / math.pi) * (z + 0.044715 * (z * z * z))))
    u = z[:, :MIX]
    v = _rms(z[:, MIX:], vg_ref[...]).astype(bf16)
    ri = lax.broadcasted_iota(jnp.int32, (LANES, LANES), 0)
    ci = lax.broadcasted_iota(jnp.int32, (LANES, LANES), 1)
    allowed = (ci // CHUNK) <= (ri // CHUNK)
    for g in range(4):
        wm = jnp.where(allowed, ws_ref[g], 0.0).astype(bf16)
        bcol = bs_ref[:, g:g + 1]
        for wd in range(TM // LANES):
            rows = slice(wd * LANES, (wd + 1) * LANES)
            cols = slice(g * LANES, (g + 1) * LANES)
            vs = jnp.dot(wm, v[rows, cols], preferred_element_type=f32) + bcol
            ob_ref[rows, cols] = (u[rows, cols] * vs).astype(bf16)

    xc_ref[...] = jnp.dot(hb, w_ref[:, XC_OFF:GATE_OFF], preferred_element_type=f32)

    for j in range(3):
        gsl = slice(GATE_OFF + j * D, GATE_OFF + (j + 1) * D)
        gl = jnp.dot(hb, w_ref[:, gsl], preferred_element_type=f32)
        gt_ref[:, j * D:(j + 1) * D] = jax.nn.sigmoid(gl).astype(bf16)


def _inproj(xa, y, g2, shift, scale, ng, w, gq, gk, sd, ex, vg, ws, bs, seq):
    n = xa.shape[0]
    spb = seq // TM
    combine = y is not None
    row = lambda w_: pl.BlockSpec((TM, w_), lambda i: (i, 0))
    modspec = pl.BlockSpec((1, 1, D), lambda i: (i // spb, 0, 0))
    in_specs = [row(D)]
    args = [xa]
    if combine:
        in_specs += [pl.BlockSpec((TM * Y_PITCH, LANES), lambda i: (i, 0)), modspec]
        args += [y, g2]
    in_specs += [modspec, modspec, _const_spec((1, D)),
                 pl.BlockSpec((D, IN_PAD), lambda i: (0, 0), pipeline_mode=pl.Buffered(1)),
                 _const_spec((1, MIX)), _const_spec((1, MIX)),
                 _const_spec((MIX, LANES)), _const_spec((2 * LANES, MIX)),
                 _const_spec((1, MIX)), _const_spec((4, LANES, LANES)), _const_spec((LANES, 4))]
    args += [shift, scale, ng, w, gq, gk, sd, ex, vg, ws, bs]
    out_specs = [row(MIX)] * 4 + [row(MIX), row(3 * D)]
    out_shape = [jax.ShapeDtypeStruct((n, MIX), bf16)] * 4 + [
        jax.ShapeDtypeStruct((n, MIX), f32), jax.ShapeDtypeStruct((n, 3 * D), bf16)]
    if combine:
        out_specs = [row(D)] + out_specs
        out_shape = [jax.ShapeDtypeStruct((n, D), f32)] + out_shape
    return pl.pallas_call(
        functools.partial(_inproj_kernel, combine=combine),
        grid=(n // TM,),
        in_specs=in_specs, out_specs=out_specs, out_shape=out_shape,
        compiler_params=_cparams(("arbitrary",)),
        name="inproj",
    )(*args)


LOG2E = math.log2(math.e)


def _two_pass_attention(nunits, q_of, kcols_of, vcols_of, k_ref, v_ref, qi, diag_bias_of, sub_bias_of,
                        sbuf, mbuf, lbuf, abuf):
    def keys(j, nb):
        return pl.ds(pl.multiple_of(j * TK, TK), nb * TK)

    def score_blocks(j, nb, bias_of):
        for u in range(nunits):
            s = lax.dot_general(q_of(u), k_ref[keys(j, nb), kcols_of(u)], (((1,), (1,)), ((), ())),
                                preferred_element_type=f32)
            if bias_of is not None:
                s = s + bias_of(u)
            m = mbuf[u]
            for b in range(nb):
                sbuf[u, j + b] = s[:, b * TK:(b + 1) * TK]
            for c in range(nb * TK // LANES):
                m = jnp.maximum(m, s[:, c * LANES:(c + 1) * LANES])
            mbuf[u] = m

    def blocks_in_pairs(first, count, fn):
        def pair(jj, _):
            fn(first + 2 * jj, 2)
            return 0

        lax.fori_loop(0, count // 2, pair, 0)

        @pl.when(count % 2 == 1)
        def _():
            fn(first + count - 1, 1)

    for u in range(nunits):
        mbuf[u] = jnp.full(mbuf.shape[1:], NEG, f32)
    if sub_bias_of is None:
        nfar = qi
    else:
        nfar = jnp.maximum(qi - 1, 0)

        @pl.when(qi >= 1)
        def _():
            score_blocks(qi - 1, 1, sub_bias_of)

    blocks_in_pairs(0, nfar, lambda j, nb: score_blocks(j, nb, None))
    score_blocks(qi, 1, diag_bias_of)

    for u in range(nunits):
        mbuf[u] = jnp.broadcast_to(jnp.max(mbuf[u], axis=-1, keepdims=True), mbuf.shape[1:])
        lbuf[u] = jnp.zeros(lbuf.shape[1:], f32)
        abuf[u] = jnp.zeros(abuf.shape[1:], f32)

    def accumulate_blocks(j, nb):
        for u in range(nunits):
            mb = mbuf[u]
            lsum = lbuf[u]
            ps = []
            for b in range(nb):
                s = sbuf[u, j + b]
                for c in range(TK // LANES):
                    p = jnp.exp2(s[:, c * LANES:(c + 1) * LANES] - mb)
                    lsum = lsum + p
                    ps.append(p)
            lbuf[u] = lsum
            p = jnp.concatenate(ps, axis=-1).astype(bf16)
            abuf[u] = abuf[u] + jnp.dot(p, v_ref[keys(j, nb), vcols_of(u)], preferred_element_type=f32)

    blocks_in_pairs(0, qi + 1, accumulate_blocks)


def _normalised(u, lbuf, abuf):
    return abuf[u] / jnp.sum(lbuf[u], axis=-1, keepdims=True)


GA = 2
GC = 4


def _attn_a_kernel(q_ref, k_ref, v_ref, t0_ref, t1_ref, dl_ref, og_ref, o_ref,
                   qstk, sbuf, mbuf, lbuf, abuf, *, lambda_init):
    qi = pl.program_id(2)
    dl = dl_ref[...]
    lam = (jnp.exp(jnp.sum(dl[0:1] * dl[1:2], axis=-1, keepdims=True))
           - jnp.exp(jnp.sum(dl[2:3] * dl[3:4], axis=-1, keepdims=True)) + lambda_init)
    lane = lax.broadcasted_iota(jnp.int32, (TQ, LANES), 1)
    cols = lambda u: slice(u * LANES, (u + 1) * LANES)
    for u in range(GA):
        qh = q_ref[:, cols(u)]
        zero = jnp.zeros_like(qh)
        qstk[u, 0:TQ, :] = jnp.where(lane < 64, qh, zero)
        qstk[u, TQ:2 * TQ, :] = jnp.where(lane >= 64, qh, zero)
    _two_pass_attention(GA, lambda u: qstk[u], cols, cols, k_ref, v_ref, qi,
                        lambda u: t0_ref[u], lambda u: t1_ref[u], sbuf, mbuf, lbuf, abuf)
    for u in range(GA):
        o = _normalised(u, lbuf, abuf)
        oh = o[:TQ] - lam * o[TQ:]
        oh = _rms(oh, og_ref[...]) * (1.0 - lambda_init)
        o_ref[:, cols(u)] = oh.astype(bf16)


def _attn_a(qa, ka, va, t0, t1, dl, og, nb, seq, lambda_init):
    n = qa.shape[0]
    nq = seq // TQ
    w = GA * LANES
    return pl.pallas_call(
        functools.partial(_attn_a_kernel, lambda_init=lambda_init),
        grid=(nb, HA // GA, nq),
        in_specs=[pl.BlockSpec((TQ, w), lambda b, g, i: (b * nq + i, g)),
                  pl.BlockSpec((seq, w), lambda b, g, i: (b, g)),
                  pl.BlockSpec((seq, w), lambda b, g, i: (b, g)),
                  pl.BlockSpec((GA, 2 * TQ, TK), lambda b, g, i: (g, 0, 0)),
                  pl.BlockSpec((GA, 2 * TQ, TK), lambda b, g, i: (g, 0, 0)),
                  _const_spec((4, 64)), _const_spec((1, LANES))],
        out_specs=pl.BlockSpec((TQ, w), lambda b, g, i: (b * nq + i, g)),
        out_shape=jax.ShapeDtypeStruct((n, MIX), bf16),
        scratch_shapes=[pltpu.VMEM((GA, 2 * TQ, LANES), bf16),
                        pltpu.VMEM((GA, seq // TK, 2 * TQ, TK), f32),
                        pltpu.VMEM((GA, 2 * TQ, LANES), f32), pltpu.VMEM((GA, 2 * TQ, LANES), f32),
                        pltpu.VMEM((GA, 2 * TQ, LANES), f32)],
        compiler_params=_cparams(("arbitrary", "arbitrary", "arbitrary")),
        name="attn_diff",
    )(qa, ka, va, t0, t1, dl, og)


def _attn_c_kernel(q_ref, k_ref, v_ref, mk_ref, o_ref, sbuf, mbuf, lbuf, abuf):
    qi = pl.program_id(2)
    lane = lax.broadcasted_iota(jnp.int32, (TQ, LANES), 1)
    cols = lambda u: slice(u * LANES, (u + 1) * LANES)
    _two_pass_attention(GC, lambda u: q_ref[:, cols(u)], cols, lambda u: cols(u // 2), k_ref, v_ref, qi,
                        lambda u: mk_ref[...], None, sbuf, mbuf, lbuf, abuf)
    for hp in range(GC // 2):
        o_ref[:, cols(hp)] = jnp.where(lane < VC, _normalised(2 * hp, lbuf, abuf),
                                       _normalised(2 * hp + 1, lbuf, abuf)).astype(bf16)


def _attn_c(qc, kc, vc, mk, nb, seq):
    n = qc.shape[0]
    nq = seq // TQ
    w = GC * LANES
    wv = GC * VC
    return pl.pallas_call(
        _attn_c_kernel,
        grid=(nb, HC // GC, nq),
        in_specs=[pl.BlockSpec((TQ, w), lambda b, g, i: (b * nq + i, g)),
                  pl.BlockSpec((seq, w), lambda b, g, i: (b, g)),
                  pl.BlockSpec((seq, wv), lambda b, g, i: (b, g)),
                  _const_spec((TQ, TK))],
        out_specs=pl.BlockSpec((TQ, wv), lambda b, g, i: (b * nq + i, g)),
        out_shape=jax.ShapeDtypeStruct((n, MIX), bf16),
        scratch_shapes=[pltpu.VMEM((GC, seq // TK, TQ, TK), f32),
                        pltpu.VMEM((GC, TQ, LANES), f32), pltpu.VMEM((GC, TQ, LANES), f32),
                        pltpu.VMEM((GC, TQ, LANES), f32)],
        compiler_params=_cparams(("arbitrary", "arbitrary", "arbitrary")),
        name="attn_latent",
    )(qc, kc, vc, mk)


def _rope_apply(x, cosf, sinf, lane):
    w = x.shape[-1]
    partner = jnp.where((lane % LANES) < NOPE + ROPE // 2,
                        pltpu.roll(x, w - ROPE // 2, 1), pltpu.roll(x, ROPE // 2, 1))
    return x * cosf + partner * sinf


def _mla_prep_kernel(xc_ref, cos_ref, sin_ref, glq_ref, glkv_ref, wq_ref, wkv_ref, gq_ref, gk_ref, gr_ref,
                     sd_ref, ex_ref, q_ref, k_ref, v_ref):
    cq = _rms(xc_ref[:, 0:Q_LORA], glq_ref[...]).astype(bf16)
    ckv = _rms(xc_ref[:, Q_LORA:Q_LORA + KV_LORA], glkv_ref[...]).astype(bf16)
    cos8 = jnp.concatenate([cos_ref[...]] * HC, axis=-1)
    sin8 = jnp.concatenate([sin_ref[...]] * HC, axis=-1)
    lane8 = lax.broadcasted_iota(jnp.int32, (TM, HC * LANES), 1)

    q = jnp.dot(cq, wq_ref[...], preferred_element_type=f32)
    q = q * _seg_rinv(q, sd_ref, ex_ref) * gq_ref[...]
    q_ref[...] = _rope_apply(q, cos8, sin8, lane8).astype(bf16)

    kv = jnp.dot(ckv, wkv_ref[...], preferred_element_type=f32)
    v_ref[...] = kv[:, HC * LANES:].astype(bf16)
    xr = xc_ref[:, Q_LORA + KV_LORA:Q_LORA + KV_LORA + LANES]
    kr = xr * lax.rsqrt(jnp.sum(xr * xr, axis=-1, keepdims=True) * (1.0 / ROPE) + EPS) * gr_ref[...]
    kr = pltpu.roll(kr, NOPE, 1)
    lane1 = lax.broadcasted_iota(jnp.int32, (TM, LANES), 1)
    kr = _rope_apply(kr, cos_ref[...], sin_ref[...], lane1)
    for h in range(HC):
        cols = slice(h * LANES, (h + 1) * LANES)
        kn = kv[:, cols]
        kn = kn * lax.rsqrt(jnp.sum(kn * kn, axis=-1, keepdims=True) * (1.0 / NOPE) + EPS) * gk_ref[...]
        k_ref[:, cols] = (kn + kr).astype(bf16)


def _mla_prep(xc, cosf, sinf, glq, glkv, wq, wkv, gq, gk, gr, sd, ex):
    n = xc.shape[0]
    row = lambda w_: pl.BlockSpec((TM, w_), lambda i: (i, 0))
    return pl.pallas_call(
        _mla_prep_kernel,
        grid=(n // TM,),
        in_specs=[row(MIX), row(LANES), row(LANES), _const_spec((1, Q_LORA)), _const_spec((1, KV_LORA)),
                  _const_spec((Q_LORA, HC * LANES)), _const_spec((KV_LORA, HC * LANES + MIX)),
                  _const_spec((1, HC * LANES)), _const_spec((1, LANES)), _const_spec((1, LANES)),
                  _const_spec((HC * LANES, LANES)), _const_spec((2 * LANES, HC * LANES))],
        out_specs=[row(HC * LANES), row(HC * LANES), row(MIX)],
        out_shape=[jax.ShapeDtypeStruct((n, HC * LANES), bf16), jax.ShapeDtypeStruct((n, HC * LANES), bf16),
                   jax.ShapeDtypeStruct((n, MIX), bf16)],
        compiler_params=_cparams(("arbitrary",)),
        name="mla_prep",
    )(xc, cosf, sinf, glq, glkv, wq, wkv, gq, gk, gr, sd, ex)


def _merge_kernel(oa_ref, ob_ref, oc_ref, gt_ref, x_ref, g1_ref, sh_ref, sc_ref, ng_ref, wb_ref, wo_ref,
                  wr_ref, br_ref, x1_ref, h2_ref, meta_ref, cnt_ref, run_ref):
    i = pl.program_id(0)

    @pl.when(i == 0)
    def _():
        run_ref[...] = jnp.zeros_like(run_ref)

    merged = jnp.zeros((TM, D), f32)
    for j, o_ref in enumerate((oa_ref, ob_ref, oc_ref)):
        pj = jnp.dot(o_ref[...], wb_ref[j], preferred_element_type=f32)
        merged = merged + gt_ref[:, j * D:(j + 1) * D].astype(f32) * pj
    y = jnp.dot(merged.astype(bf16), wo_ref[...], preferred_element_type=f32)
    x1 = x_ref[...] + g1_ref[0] * y
    x1_ref[...] = x1
    h2 = _rms(x1, ng_ref[...]) * (1.0 + sc_ref[0]) + sh_ref[0]
    hr = h2.astype(bf16).astype(f32)
    lo = lax.shift_right_logical(lax.bitcast_convert_type(hr[:, 0:D // 2], jnp.uint32), jnp.uint32(16))
    hi = lax.bitcast_convert_type(hr[:, D // 2:D], jnp.uint32) & jnp.uint32(0xFFFF0000)
    words = lax.bitcast_convert_type(lo | hi, f32)
    for s in range(PACK_SLABS):
        h2_ref[pl.ds(s, TM, stride=PACK_SLABS), :] = words[:, s * LANES:(s + 1) * LANES]

    lg = jnp.dot(h2, wr_ref[...], preferred_element_type=f32, precision=lax.Precision.HIGHEST) + br_ref[...]
    lane = lax.broadcasted_iota(jnp.int32, (TM, LANES), 1)
    n_e = N_GROUPS * N_EXP
    gmask = (lane >= n_e) & (lane < n_e + N_GROUPS)
    gl = jnp.where(gmask, lg, NEG)
    gmax = jnp.max(gl, axis=-1, keepdims=True)
    g_w = 1.0 / jnp.sum(jnp.where(gmask, jnp.exp(gl - gmax), 0.0), axis=-1, keepdims=True)
    gidx = jnp.min(jnp.where(gl == gmax, lane - n_e, LANES), axis=-1, keepdims=True)
    emask = (lane < n_e) & ((lane // N_EXP) == gidx)
    el = jnp.where(emask, lg, NEG)
    m1 = jnp.max(el, axis=-1, keepdims=True)
    i1 = jnp.min(jnp.where(el == m1, lane, LANES), axis=-1, keepdims=True)
    el2 = jnp.where(lane == i1, NEG, el)
    m2 = jnp.max(el2, axis=-1, keepdims=True)
    i2 = jnp.min(jnp.where(el2 == m2, lane, LANES), axis=-1, keepdims=True)
    t = jnp.exp(m2 - m1)
    w1 = g_w / (1.0 + t)
    w2 = g_w * t / (1.0 + t)
    lo = jnp.minimum(i1, i2)
    hi = jnp.maximum(i1, i2)
    w_lo = jnp.where(i1 < i2, w1, w2)
    w_hi = jnp.where(i1 < i2, w2, w1)
    bucket = gidx * (N_EXP * N_EXP) + (lo % N_EXP) * N_EXP + (hi % N_EXP)

    lane2 = lax.broadcasted_iota(jnp.int32, (TM, 2 * LANES), 1)
    onehot = (lane2 == bucket).astype(f32)
    rr = lax.broadcasted_iota(jnp.int32, (TM, TM), 0)
    cc = lax.broadcasted_iota(jnp.int32, (TM, TM), 1)
    tri = (cc < rr).astype(bf16)
    before = jnp.dot(tri, onehot.astype(bf16), preferred_element_type=f32) + run_ref[...]
    rank = jnp.sum(onehot * before, axis=-1, keepdims=True)
    run_ref[...] = run_ref[...] + jnp.sum(onehot, axis=0, keepdims=True)
    cnt_ref[...] = run_ref[...]

    meta = (jnp.where(lane == i1, w1, 0.0) + jnp.where(lane == i2, w2, 0.0)
            + jnp.where(lane == 32, bucket.astype(f32), 0.0) + jnp.where(lane == 33, rank, 0.0)
            + jnp.where(lane == 34, w_lo, 0.0) + jnp.where(lane == 35, w_hi, 0.0))
    meta_ref[...] = meta


def _merge(oa, ob, oc, gt, x, g1, shift, scale, ng, wb, wo, wr, br, seq):
    n = x.shape[0]
    spb = seq // TM
    row = lambda w_: pl.BlockSpec((TM, w_), lambda i: (i, 0))
    modspec = pl.BlockSpec((1, 1, D), lambda i: (i // spb, 0, 0))
    return pl.pallas_call(
        _merge_kernel,
        grid=(n // TM,),
        in_specs=[row(MIX), row(MIX), row(MIX), row(3 * D), row(D), modspec, modspec, modspec,
                  _const_spec((1, D)), _const_spec((3, MIX, D)), _const_spec((D, D)),
                  _const_spec((D, LANES)), _const_spec((1, LANES))],
        out_specs=[row(D), pl.BlockSpec((TM * PACK_SLABS, LANES), lambda i: (i, 0)), row(LANES),
                   _const_spec((1, 2 * LANES))],
        out_shape=[jax.ShapeDtypeStruct((n, D), f32), jax.ShapeDtypeStruct((n * PACK_SLABS, LANES), f32),
                   jax.ShapeDtypeStruct((n, LANES), f32), jax.ShapeDtypeStruct((1, 2 * LANES), f32)],
        scratch_shapes=[pltpu.VMEM((1, 2 * LANES), f32)],
        compiler_params=_cparams(("arbitrary",)),
        name="merge_route",
    )(oa, ob, oc, gt, x, g1, shift, scale, ng, wb, wo, wr, br)


TE = 128
N_BUCKETS = N_GROUPS * N_EXP * N_EXP
N_PAIRS = N_GROUPS * (N_EXP * (N_EXP - 1) // 2)


def _num_tiles(n):
    return n // TE + N_PAIRS


TP = 2048


def _tables_kernel(meta_ref, cnt_ref, pos_ref, tt_ref):
    cnt = cnt_ref[...]
    ntile = jnp.floor((cnt + (TE - 1)) * (1.0 / TE))
    bi = lax.broadcasted_iota(jnp.int32, (N_BUCKETS, N_BUCKETS), 0)
    bj = lax.broadcasted_iota(jnp.int32, (N_BUCKETS, N_BUCKETS), 1)
    upper = (bi <= bj).astype(bf16)
    incl = jnp.dot(jnp.broadcast_to(ntile, (8, N_BUCKETS)).astype(bf16), upper, preferred_element_type=f32)[0:1]
    excl = incl - ntile

    meta = meta_ref[...]
    tp = meta.shape[0]
    lane = lax.broadcasted_iota(jnp.int32, (tp, LANES), 1)
    bucket = jnp.sum(jnp.where(lane == 32, meta, 0.0), axis=-1, keepdims=True).astype(jnp.int32)
    rank = jnp.sum(jnp.where(lane == 33, meta, 0.0), axis=-1, keepdims=True)
    lane2 = lax.broadcasted_iota(jnp.int32, (tp, N_BUCKETS), 1)
    first_tile = jnp.sum(jnp.where(lane2 == bucket, excl, 0.0), axis=-1, keepdims=True)
    pos_ref[...] = (first_tile * TE + rank).astype(jnp.int32)

    @pl.when(pl.program_id(0) == 0)
    def _():
        tau = lax.broadcasted_iota(jnp.int32, (N_BUCKETS, 1), 0).astype(f32)
        lane_b = lax.broadcasted_iota(jnp.int32, (1, N_BUCKETS), 1).astype(f32)
        tb = jnp.sum((incl <= tau).astype(f32), axis=-1, keepdims=True)
        last_b = jnp.max(jnp.where(cnt > 0, lane_b, 0.0), axis=-1, keepdims=True)
        tbi = jnp.minimum(tb, last_b).astype(jnp.int32)
        grp = lax.shift_right_logical(tbi, 6) * N_EXP
        ea = grp + (lax.shift_right_logical(tbi, 3) & 7)
        eb = grp + (tbi & 7)
        total = jnp.sum(ntile, axis=-1, keepdims=True).astype(jnp.int32)
        lane_t = lax.broadcasted_iota(jnp.int32, (N_BUCKETS, LANES), 1)
        tt_ref[...] = jnp.where(lane_t == 0, ea, jnp.where(lane_t == 1, eb, jnp.where(lane_t == 2, total, 0)))


def _tables(meta, cnt):
    n = meta.shape[0]
    tp = math.gcd(n, TP)
    assert _num_tiles(n) <= N_BUCKETS
    return pl.pallas_call(
        _tables_kernel,
        grid=(n // tp,),
        in_specs=[pl.BlockSpec((tp, LANES), lambda i: (i, 0)), _const_spec((1, N_BUCKETS))],
        out_specs=[pl.BlockSpec((tp, 1), lambda i: (i, 0)), _const_spec((N_BUCKETS, LANES))],
        out_shape=[jax.ShapeDtypeStruct((n, 1), jnp.int32), jax.ShapeDtypeStruct((N_BUCKETS, LANES), jnp.int32)],
        compiler_params=_cparams(("arbitrary",)),
        name="moe_tables",
    )(meta, cnt)


def _plan_kernel(pos_ref, tok_ref, *, n, nslot):
    def fill(s, _):
        tok_ref[s] = -1
        return 0

    lax.fori_loop(0, nslot, fill, 0, unroll=16)

    def place(t, _):
        tok_ref[pos_ref[t]] = t
        return 0

    lax.fori_loop(0, n, place, 0, unroll=16)


def _plan(pos):
    n = pos.shape[0]
    nslot = _num_tiles(n) * TE
    smem = pl.BlockSpec(memory_space=pltpu.SMEM)
    return pl.pallas_call(
        functools.partial(_plan_kernel, n=n, nslot=nslot),
        in_specs=[smem], out_specs=smem,
        out_shape=jax.ShapeDtypeStruct((nslot,), jnp.int32),
        name="moe_plan",
    )(pos)


def _moe_kernel(tea_ref, teb_ref, nt_ref, tok_ref, h_ref, m_ref, wg_ref, wu_ref, wd_ref, y_hbm,
                xs, ms, ob0, ob1, ssem, *, n):
    i = pl.program_id(0)
    nt = nt_ref[0]
    obufs = (ob0, ob1)

    def scatter_copy(dst_tok, sl, r):
        dst = y_hbm.at[pl.ds(pl.multiple_of(dst_tok * Y_PITCH, Y_PITCH), Y_PITCH), :]
        return pltpu.make_async_copy(obufs[sl].at[pl.ds(r * O_PITCH, Y_PITCH), :], dst, ssem.at[sl])

    def start_scatters(tile, sl, to_spare, rows=range(TE)):
        for r in rows:
            t = tok_ref[tile * TE + r]
            scatter_copy(jnp.where((t < 0) | to_spare, n + sl * TE + r, t), sl, r).start(priority=r % 2)

    def wait_scatters(sl):
        for r in range(TE):
            scatter_copy(n, sl, r).wait()

    @pl.when(i == 0)
    def _():
        for sl in range(2):
            obufs[sl][...] = jnp.zeros_like(obufs[sl])
            spare = pltpu.make_async_copy(obufs[sl].at[pl.ds(0, TE * Y_PITCH), :],
                                          y_hbm.at[pl.ds((n + sl * TE) * Y_PITCH, TE * Y_PITCH), :], ssem.at[sl])
            spare.start()
            spare.wait()

    def step(s):
        o = 1 - s

        @pl.when(i >= 1)
        def _():
            wait_scatters(s)

        prev = jnp.maximum(i - 1, 0)
        bursts = (range(0, TE // 3), range(TE // 3, 2 * TE // 3), range(2 * TE // 3, TE))
        start_scatters(prev, o, i == 0, bursts[0])

        for r in range(TE):
            t = jnp.maximum(tok_ref[i * TE + r], 0)
            xs[pl.ds(r * PACK_SLABS, PACK_SLABS), :] = h_ref[pl.ds(pl.multiple_of(t * PACK_SLABS, PACK_SLABS),
                                                                   PACK_SLABS), :]
            ms[pl.ds(r, 1), :] = m_ref[pl.ds(t, 1), :]
        slab = lambda k: xs[pl.ds(k, TE, stride=PACK_SLABS), :]
        words = lax.bitcast_convert_type(jnp.concatenate([slab(k) for k in range(PACK_SLABS)], axis=-1),
                                         jnp.uint32)
        lo = lax.bitcast_convert_type(lax.shift_left(words, jnp.uint32(16)), f32)
        hi = lax.bitcast_convert_type(words & jnp.uint32(0xFFFF0000), f32)
        hb = jnp.concatenate([lo, hi], axis=-1).astype(bf16)
        meta = ms[...]
        lane = lax.broadcasted_iota(jnp.int32, (TE, LANES), 1)
        w_lo = jnp.sum(jnp.where(lane == 34, meta, 0.0), axis=-1, keepdims=True)
        w_hi = jnp.sum(jnp.where(lane == 35, meta, 0.0), axis=-1, keepdims=True)

        def ffn(e, wcol):
            a = jnp.dot(hb, wg_ref[0, e], preferred_element_type=f32)
            u = jnp.dot(hb, wu_ref[0, e], preferred_element_type=f32)
            act = (a * jax.nn.sigmoid(a)) * u * wcol
            return jnp.dot(act.astype(bf16), wd_ref[0, e], preferred_element_type=f32)

        ob = obufs[s]

        @pl.when(i < nt)
        def _():
            start_scatters(prev, o, i == 0, bursts[1])
            out = ffn(lax.rem(tea_ref[i], N_EXP), w_lo)
            for k in range(SLABS):
                ob[pl.ds(k, TE, stride=O_PITCH), :] = out[:, k * LANES:(k + 1) * LANES]

        @pl.when(i < nt)
        def _():
            start_scatters(prev, o, i == 0, bursts[2])
            out = ffn(lax.rem(teb_ref[i], N_EXP), w_hi)
            for k in range(SLABS):
                rows_k = pl.ds(k, TE, stride=O_PITCH)
                ob[rows_k, :] = ob[rows_k, :] + out[:, k * LANES:(k + 1) * LANES]

        @pl.when(i == nt - 1)
        def _():
            start_scatters(i, s, False)
            wait_scatters(o)
            wait_scatters(s)

    for s in range(2):
        @pl.when((i < nt) & (lax.rem(i, 2) == s))
        def _(s=s):
            step(s)


def _moe(h2ext, meta, tok, tea, teb, nt, wg, wu, wd):
    n = h2ext.shape[0] // PACK_SLABS
    ntmax = _num_tiles(n)
    once = pl.Buffered(1)
    wspec_in = pl.BlockSpec((1, N_EXP, D, F_EXP), lambda i, tea, teb, nt, tok: (tea[i] // N_EXP, 0, 0, 0),
                            pipeline_mode=once)
    wspec_out = pl.BlockSpec((1, N_EXP, F_EXP, D), lambda i, tea, teb, nt, tok: (tea[i] // N_EXP, 0, 0, 0),
                             pipeline_mode=once)
    hspec = pl.BlockSpec((n * PACK_SLABS, LANES), lambda i, tea, teb, nt, tok: (0, 0), pipeline_mode=once)
    mspec = pl.BlockSpec((n, LANES), lambda i, tea, teb, nt, tok: (0, 0), pipeline_mode=once)
    return pl.pallas_call(
        functools.partial(_moe_kernel, n=n),
        grid_spec=pltpu.PrefetchScalarGridSpec(
            num_scalar_prefetch=4,
            grid=(ntmax,),
            in_specs=[hspec, mspec, wspec_in, wspec_in, wspec_out],
            out_specs=pl.BlockSpec(memory_space=pl.ANY),
            scratch_shapes=[pltpu.VMEM((TE * PACK_SLABS, LANES), f32), pltpu.VMEM((TE, LANES), f32)] + [
                pltpu.VMEM((TE * O_PITCH, LANES), f32)] * 2 + [pltpu.SemaphoreType.DMA((2,))]),
        out_shape=jax.ShapeDtypeStruct(((n + 2 * TE) * Y_PITCH, LANES), f32),
        compiler_params=_cparams(("arbitrary",), vmem=60 * 1024 * 1024),
        name="moe_sparse",
    )(tea, teb, nt, tok, h2ext, meta, wg, wu, wd)


def _final_kernel(x_ref, y_ref, g_ref, o_ref):
    o_ref[...] = x_ref[...] + g_ref[0] * _rows_from_token_major(y_ref, x_ref.shape[0])


def _final_combine(x1, y, g2, seq):
    n = x1.shape[0]
    t = 512
    spb = seq // t
    row = pl.BlockSpec((t, D), lambda i: (i, 0))
    return pl.pallas_call(
        _final_kernel,
        grid=(n // t,),
        in_specs=[row, pl.BlockSpec((t * Y_PITCH, LANES), lambda i: (i, 0)),
                  pl.BlockSpec((1, 1, D), lambda i: (i // spb, 0, 0))],
        out_specs=row,
        out_shape=jax.ShapeDtypeStruct((n, D), f32),
        compiler_params=_cparams(("arbitrary",)),
        name="final_combine",
    )(x1, y, g2)


def _seg_matrices(width, segs):
    sd = np.zeros((width, LANES), np.float32)
    ex = np.zeros((LANES, width), np.float32)
    for j, (s, ln) in enumerate(segs):
        sd[s:s + ln, j] = 1.0 / ln
        ex[j, s:s + ln] = 1.0
    return jnp.asarray(sd, bf16), jnp.asarray(np.concatenate([ex, ex], axis=0), bf16)


def _head_pad(w, heads, per_head, keep):
    k = w.shape[0]
    w3 = w.reshape(k, heads, per_head)[:, :, :keep]
    return jnp.pad(w3, ((0, 0), (0, 0), (0, LANES - keep))).reshape(k, heads * LANES)


def kernel(x, c, positions, w_ada, b_ada, norm_g, w_in, diff_qk_g, diff_lambda, diff_out_g, rel_bias, sgu_v_g, sgu_w, sgu_b, mla_lat_g, mla_w_uq, mla_w_ukv, mla_qk_g, w_branch, w_out, router_g_w, router_g_b, router_e_w, router_e_b, w_e_gate, w_e_up, w_e_down):
    nb, seq, _ = x.shape
    n = nb * seq
    assert seq % TQ == 0 and seq % TM == 0 and x.shape[2] == D

    mod = _ada(c, w_ada, b_ada)
    cosf, sinf = _rope_tables(positions)
    t0, t1, mk = _bias_tiles(rel_bias)

    sd_a, ex_a = _seg_matrices(MIX, [(s * 64, 64) for s in range(8)])
    segs_q = []
    for h in range(HC):
        segs_q += [(h * LANES, NOPE), (h * LANES + NOPE, ROPE)]
    sd_q, ex_q = _seg_matrices(HC * LANES, segs_q)
    ne = N_GROUPS * N_EXP
    zpad = lambda a, w_: jnp.pad(a, ((0, 0), (0, w_ - a.shape[1])))

    xcur = x.reshape(n, D)
    y_prev, g2_prev = None, None
    for l in range(DEPTH):
        m3 = mod[l].reshape(nb, 1, 6 * D)
        shift1, scale1, gate1, shift2, scale2, gate2 = [m3[:, :, k * D:(k + 1) * D] for k in range(6)]
        lambda_init = LAMBDA_INIT_BASE - LAMBDA_INIT_SCALE * math.exp(-LAMBDA_INIT_DECAY * l)

        w = w_in[l]
        w_pad = jnp.concatenate([w[:, :XC_OFF + 416], jnp.zeros((D, IN_PAD - IN_REAL), f32), w[:, XC_OFF + 416:]],
                                axis=1).astype(bf16)
        gq = (jnp.tile(diff_qk_g[l, 0], 8) * (64 ** -0.5 * LOG2E)).reshape(1, MIX)
        gk = jnp.tile(diff_qk_g[l, 1], 8).reshape(1, MIX)
        outs = _inproj(xcur, y_prev, g2_prev, shift1, scale1, norm_g[l, 0].reshape(1, D), w_pad, gq, gk,
                       sd_a, ex_a, sgu_v_g[l].reshape(1, MIX), sgu_w[l], jnp.transpose(sgu_b[l]), seq)
        if y_prev is not None:
            xcur = outs[0]
            outs = outs[1:]
        qa, ka, va, ob, xc, gt = outs

        oa = _attn_a(qa, ka, va, t0, t1, diff_lambda[l], diff_out_g[l].reshape(1, LANES), nb, seq, lambda_init)

        qkg = mla_qk_g[l]
        wq = _head_pad(mla_w_uq[l], HC, NOPE + ROPE, NOPE + ROPE).astype(bf16)
        wkv3 = mla_w_ukv[l].reshape(KV_LORA, HC, NOPE + VC)
        wk = jnp.pad(wkv3[:, :, :NOPE], ((0, 0), (0, 0), (0, LANES - NOPE))).reshape(KV_LORA, HC * LANES)
        wv = wkv3[:, :, NOPE:].reshape(KV_LORA, HC * VC)
        wkv = jnp.concatenate([wk, wv], axis=1).astype(bf16)
        gq_c = jnp.tile(jnp.pad(qkg[0], (0, LANES - NOPE - ROPE)), HC).reshape(1, HC * LANES) * ((NOPE + ROPE) ** -0.5 * LOG2E)
        gk_c = jnp.pad(qkg[1, :NOPE], (0, LANES - NOPE)).reshape(1, LANES)
        gr_c = jnp.pad(qkg[1, NOPE:], (0, LANES - ROPE)).reshape(1, LANES)
        qc, kc, vc = _mla_prep(xc, cosf, sinf, mla_lat_g[l, :Q_LORA].reshape(1, Q_LORA),
                               mla_lat_g[l, Q_LORA:].reshape(1, KV_LORA), wq, wkv, gq_c, gk_c, gr_c, sd_q, ex_q)
        oc = _attn_c(qc, kc, vc, mk, nb, seq)

        wr = zpad(jnp.concatenate([router_e_w[l], router_g_w[l]], axis=1), LANES)
        br = zpad(jnp.concatenate([router_e_b[l], router_g_b[l]]).reshape(1, ne + N_GROUPS), LANES)
        x1, h2ext, meta, cnt = _merge(oa, ob, oc, gt, xcur, gate1, shift2, scale2, norm_g[l, 1].reshape(1, D),
                                      w_branch[l].astype(bf16), w_out[l].astype(bf16), wr, br, seq)

        pos, tt = _tables(meta, cnt)
        tok = _plan(pos.reshape(n))
        ntmax = _num_tiles(n)
        y_prev = _moe(h2ext, meta, tok, tt[:ntmax, 0], tt[:ntmax, 1], tt[0:1, 2], w_e_gate[l].astype(bf16),
                      w_e_up[l].astype(bf16), w_e_down[l].astype(bf16))
        g2_prev = gate2
        xcur = x1

    out = _final_combine(xcur, y_prev, g2_prev, seq)
    return out.reshape(nb, seq, D)
```

```python
import functools
import math

import jax
import jax.numpy as jnp
import numpy as np
from jax import lax
from jax.experimental import pallas as pl
from jax.experimental.pallas import tpu as pltpu

f32 = jnp.float32
bf16 = jnp.bfloat16

D = 1024
DEPTH = 4
CHUNK = 64
MIX = 512
HA = 4
HC = 8
Q_LORA, KV_LORA, ROPE = 256, 128, 32
NOPE, VC = 64, 64
N_GROUPS, N_EXP, F_EXP = 4, 8, 256
N_REL_BUCKETS = 32
ROPE_THETA = 10000.0
LAMBDA_INIT_BASE, LAMBDA_INIT_SCALE, LAMBDA_INIT_DECAY = 0.8, 0.6, 0.3
EPS = 1e-6
NEG = -1e30

IN_REAL = 6048
IN_PAD = 6144
XC_OFF = 2560
GATE_OFF = 3072

TM = 512
TQ = 256
TK = 256
LANES = 128
VMEM_LIMIT = 56 * 1024 * 1024
SLABS = D // LANES
PACK_SLABS = SLABS // 2
Y_PITCH = SLABS
O_PITCH = Y_PITCH + 1


def _cparams(sem, vmem=VMEM_LIMIT):
    return pltpu.CompilerParams(dimension_semantics=sem, vmem_limit_bytes=vmem)


def _const_spec(shape):
    nd = len(shape)
    return pl.BlockSpec(shape, lambda *_: (0,) * nd)


def _rms(x, g_row):
    return x * lax.rsqrt(jnp.mean(x * x, axis=-1, keepdims=True) + EPS) * g_row


def _rows_from_token_major(y_ref, rows):
    return jnp.concatenate([y_ref[pl.ds(k, rows, stride=Y_PITCH), :] for k in range(SLABS)], axis=-1)


def _seg_rinv(x, sd_ref, ex_ref):
    ss = jnp.dot((x * x).astype(bf16), sd_ref[...], preferred_element_type=f32)
    hi = ss.astype(bf16)
    lo = (ss - hi.astype(f32)).astype(bf16)
    full = jnp.dot(jnp.concatenate([hi, lo], axis=-1), ex_ref[...], preferred_element_type=f32)
    return lax.rsqrt(full + EPS)


def _ada_kernel(c_ref, w_ref, b_ref, o_ref):
    c = c_ref[...]
    ca = (c * jax.nn.sigmoid(c)).astype(bf16)
    o_ref[0] = jnp.dot(ca, w_ref[0].astype(bf16), preferred_element_type=f32) + b_ref[0]


def _ada(c, w_ada, b_ada):
    nb = c.shape[0]
    tn = 1024
    return pl.pallas_call(
        _ada_kernel,
        grid=(DEPTH, 6 * D // tn),
        in_specs=[pl.BlockSpec((nb, D), lambda l, j: (0, 0)),
                  pl.BlockSpec((1, D, tn), lambda l, j: (l, 0, j)),
                  pl.BlockSpec((1, 1, tn), lambda l, j: (l, 0, j))],
        out_specs=pl.BlockSpec((1, nb, tn), lambda l, j: (l, 0, j)),
        out_shape=jax.ShapeDtypeStruct((DEPTH, nb, 6 * D), f32),
        compiler_params=_cparams(("arbitrary", "arbitrary")),
        name="ada_mod",
    )(c, w_ada, b_ada.reshape(DEPTH, 1, 6 * D))


def _rope_kernel(pos_ref, invf_ref, cos_ref, sin_ref):
    ang = pos_ref[...].astype(f32) * invf_ref[...]
    lane = lax.broadcasted_iota(jnp.int32, ang.shape, 1)
    rot = (lane >= NOPE) & (lane < NOPE + ROPE)
    first = lane < NOPE + ROPE // 2
    cos_ref[...] = jnp.where(rot, jnp.cos(ang), 1.0)
    s = jnp.sin(ang)
    sin_ref[...] = jnp.where(rot, jnp.where(first, -s, s), 0.0)


def _rope_tables(positions):
    n = positions.size
    inv_freq = ROPE_THETA ** (-jnp.arange(0, ROPE, 2, dtype=f32) / ROPE)
    invf = jnp.zeros((1, LANES), f32).at[0, NOPE:NOPE + ROPE].set(jnp.tile(inv_freq, 2))
    t = math.gcd(n, 2048)
    return pl.pallas_call(
        _rope_kernel,
        grid=(n // t,),
        in_specs=[pl.BlockSpec((t, 1), lambda i: (i, 0)), _const_spec((1, LANES))],
        out_specs=[pl.BlockSpec((t, LANES), lambda i: (i, 0))] * 2,
        out_shape=[jax.ShapeDtypeStruct((n, LANES), f32)] * 2,
        compiler_params=_cparams(("arbitrary",)),
        name="rope_tables",
    )(positions.reshape(n, 1), invf)


def _t5_bucket(rel):
    n = jnp.abs(rel)
    large = 8 + sum((n >= t).astype(jnp.int32) for t in (12, 16, 23, 32, 46, 64, 91))
    return jnp.where(rel > 0, 16, 0) + jnp.where(n < 8, n, large)


def _bias_kernel(tbl_ref, t0_ref, t1_ref, mk_ref):
    r = lax.broadcasted_iota(jnp.int32, (TQ, TK), 0)
    c = lax.broadcasted_iota(jnp.int32, (TQ, TK), 1)
    allowed = (c // CHUNK) <= (r // CHUNK)
    mk = jnp.where(allowed, 0.0, NEG).astype(f32)
    mk_ref[...] = mk
    b0 = _t5_bucket(c - r)
    b1 = _t5_bucket(c - r - TK)
    for h in range(HA):
        far = tbl_ref[N_REL_BUCKETS // 2 - 1, h]
        t0 = jnp.zeros((TQ, TK), f32)
        t1 = jnp.zeros((TQ, TK), f32)
        for b in range(N_REL_BUCKETS):
            v = (tbl_ref[b, h] - far) * LOG2E
            t0 = jnp.where(b0 == b, v, t0)
            t1 = jnp.where(b1 == b, v, t1)
        t0 = t0 + mk
        t0_ref[h, 0:TQ, :] = t0
        t0_ref[h, TQ:2 * TQ, :] = t0
        t1_ref[h, 0:TQ, :] = t1
        t1_ref[h, TQ:2 * TQ, :] = t1


def _bias_tiles(rel_bias):
    return pl.pallas_call(
        _bias_kernel,
        in_specs=[pl.BlockSpec(memory_space=pltpu.SMEM)],
        out_specs=[pl.BlockSpec(memory_space=pltpu.VMEM)] * 3,
        out_shape=[jax.ShapeDtypeStruct((HA, 2 * TQ, TK), f32),
                   jax.ShapeDtypeStruct((HA, 2 * TQ, TK), f32),
                   jax.ShapeDtypeStruct((TQ, TK), f32)],
        name="bias_tiles",
    )(rel_bias)


def _inproj_kernel(*refs, combine):
    if combine:
        (xa_ref, y_ref, g2_ref, sh_ref, sc_ref, ng_ref, w_ref, gq_ref, gk_ref, sd_ref, ex_ref,
         vg_ref, ws_ref, bs_ref,
         xo_ref, qa_ref, ka_ref, va_ref, ob_ref, xc_ref, gt_ref) = refs
        x = xa_ref[...] + g2_ref[0] * _rows_from_token_major(y_ref, TM)
        xo_ref[...] = x
    else:
        (xa_ref, sh_ref, sc_ref, ng_ref, w_ref, gq_ref, gk_ref, sd_ref, ex_ref,
         vg_ref, ws_ref, bs_ref,
         qa_ref, ka_ref, va_ref, ob_ref, xc_ref, gt_ref) = refs
        x = xa_ref[...]
    h = _rms(x, ng_ref[...]) * (1.0 + sc_ref[0]) + sh_ref[0]
    hb = h.astype(bf16)

    q = jnp.dot(hb, w_ref[:, 0:MIX], preferred_element_type=f32)
    qa_ref[...] = (q * _seg_rinv(q, sd_ref, ex_ref) * gq_ref[...]).astype(bf16)
    k = jnp.dot(hb, w_ref[:, MIX:2 * MIX], preferred_element_type=f32)
    ka_ref[...] = (k * _seg_rinv(k, sd_ref, ex_ref) * gk_ref[...]).astype(bf16)
    va_ref[...] = jnp.dot(hb, w_ref[:, 2 * MIX:3 * MIX], preferred_element_type=f32).astype(bf16)

    z = jnp.dot(hb, w_ref[:, 3 * MIX:5 * MIX], preferred_element_type=f32)
    z = 0.5 * z * (1.0 + jnp.tanh(math.sqrt(2.0 / math.pi) * (z + 0.044715 * (z * z * z))))
    u = z[:, :MIX]
    v = _rms(z[:, MIX:], vg_ref[...]).astype(bf16)
    ri = lax.broadcasted_iota(jnp.int32, (LANES, LANES), 0)
    ci = lax.broadcasted_iota(jnp.int32, (LANES, LANES), 1)
    allowed = (ci // CHUNK) <= (ri // CHUNK)
    for g in range(4):
        wm = jnp.where(allowed, ws_ref[g], 0.0).astype(bf16)
        bcol = bs_ref[:, g:g + 1]
        for wd in range(TM // LANES):
            rows = slice(wd * LANES, (wd + 1) * LANES)
            cols = slice(g * LANES, (g + 1) * LANES)
            vs = jnp.dot(wm, v[rows, cols], preferred_element_type=f32) + bcol
            ob_ref[rows, cols] = (u[rows, cols] * vs).astype(bf16)

    xc_ref[...] = jnp.dot(hb, w_ref[:, XC_OFF:GATE_OFF], preferred_element_type=f32)

    for j in range(3):
        gsl = slice(GATE_OFF + j * D, GATE_OFF + (j + 1) * D)
        gl = jnp.dot(hb, w_ref[:, gsl], preferred_element_type=f32)
        gt_ref[:, j * D:(j + 1) * D] = jax.nn.sigmoid(gl).astype(bf16)


def _inproj(xa, y, g2, shift, scale, ng, w, gq, gk, sd, ex, vg, ws, bs, seq):
    n = xa.shape[0]
    spb = seq // TM
    combine = y is not None
    row = lambda w_: pl.BlockSpec((TM, w_), lambda i: (i, 0))
    modspec = pl.BlockSpec((1, 1, D), lambda i: (i // spb, 0, 0))
    in_specs = [row(D)]
    args = [xa]
    if combine:
        in_specs += [pl.BlockSpec((TM * Y_PITCH, LANES), lambda i: (i, 0)), modspec]
        args += [y, g2]
    in_specs += [modspec, modspec, _const_spec((1, D)),
                 pl.BlockSpec((D, IN_PAD), lambda i: (0, 0), pipeline_mode=pl.Buffered(1)),
                 _const_spec((1, MIX)), _const_spec((1, MIX)),
                 _const_spec((MIX, LANES)), _const_spec((2 * LANES, MIX)),
                 _const_spec((1, MIX)), _const_spec((4, LANES, LANES)), _const_spec((LANES, 4))]
    args += [shift, scale, ng, w, gq, gk, sd, ex, vg, ws, bs]
    out_specs = [row(MIX)] * 4 + [row(MIX), row(3 * D)]
    out_shape = [jax.ShapeDtypeStruct((n, MIX), bf16)] * 4 + [
        jax.ShapeDtypeStruct((n, MIX), f32), jax.ShapeDtypeStruct((n, 3 * D), bf16)]
    if combine:
        out_specs = [row(D)] + out_specs
        out_shape = [jax.ShapeDtypeStruct((n, D), f32)] + out_shape
    return pl.pallas_call(
        functools.partial(_inproj_kernel, combine=combine),
        grid=(n // TM,),
        in_specs=in_specs, out_specs=out_specs, out_shape=out_shape,
        compiler_params=_cparams(("arbitrary",)),
        name="inproj",
    )(*args)


LOG2E = math.log2(math.e)


def _two_pass_attention(nunits, q_of, kcols_of, vcols_of, k_ref, v_ref, qi, diag_bias_of, sub_bias_of,
                        sbuf, mbuf, lbuf, abuf):
    def keys(j, nb):
        return pl.ds(pl.multiple_of(j * TK, TK), nb * TK)

    def score_blocks(j, nb, bias_of):
        for u in range(nunits):
            s = lax.dot_general(q_of(u), k_ref[keys(j, nb), kcols_of(u)], (((1,), (1,)), ((), ())),
                                preferred_element_type=f32)
            if bias_of is not None:
                s = s + bias_of(u)
            m = mbuf[u]
            for b in range(nb):
                sbuf[u, j + b] = s[:, b * TK:(b + 1) * TK]
            for c in range(nb * TK // LANES):
                m = jnp.maximum(m, s[:, c * LANES:(c + 1) * LANES])
            mbuf[u] = m

    def blocks_in_pairs(first, count, fn):
        def pair(jj, _):
            fn(first + 2 * jj, 2)
            return 0

        lax.fori_loop(0, count // 2, pair, 0)

        @pl.when(count % 2 == 1)
        def _():
            fn(first + count - 1, 1)

    for u in range(nunits):
        mbuf[u] = jnp.full(mbuf.shape[1:], NEG, f32)
    if sub_bias_of is None:
        nfar = qi
    else:
        nfar = jnp.maximum(qi - 1, 0)

        @pl.when(qi >= 1)
        def _():
            score_blocks(qi - 1, 1, sub_bias_of)

    blocks_in_pairs(0, nfar, lambda j, nb: score_blocks(j, nb, None))
    score_blocks(qi, 1, diag_bias_of)

    for u in range(nunits):
        mbuf[u] = jnp.broadcast_to(jnp.max(mbuf[u], axis=-1, keepdims=True), mbuf.shape[1:])
        lbuf[u] = jnp.zeros(lbuf.shape[1:], f32)
        abuf[u] = jnp.zeros(abuf.shape[1:], f32)

    def accumulate_blocks(j, nb):
        for u in range(nunits):
            mb = mbuf[u]
            lsum = lbuf[u]
            ps = []
            for b in range(nb):
                s = sbuf[u, j + b]
                for c in range(TK // LANES):
                    p = jnp.exp2(s[:, c * LANES:(c + 1) * LANES] - mb)
                    lsum = lsum + p
                    ps.append(p)
            lbuf[u] = lsum
            p = jnp.concatenate(ps, axis=-1).astype(bf16)
            abuf[u] = abuf[u] + jnp.dot(p, v_ref[keys(j, nb), vcols_of(u)], preferred_element_type=f32)

    blocks_in_pairs(0, qi + 1, accumulate_blocks)


def _normalised(u, lbuf, abuf):
    return abuf[u] / jnp.sum(lbuf[u], axis=-1, keepdims=True)


GA = 2
GC = 4


def _attn_a_kernel(q_ref, k_ref, v_ref, t0_ref, t1_ref, dl_ref, og_ref, o_ref,
                   qstk, sbuf, mbuf, lbuf, abuf, *, lambda_init):
    qi = pl.program_id(2)
    dl = dl_ref[...]
    lam = (jnp.exp(jnp.sum(dl[0:1] * dl[1:2], axis=-1, keepdims=True))
           - jnp.exp(jnp.sum(dl[2:3] * dl[3:4], axis=-1, keepdims=True)) + lambda_init)
    lane = lax.broadcasted_iota(jnp.int32, (TQ, LANES), 1)
    cols = lambda u: slice(u * LANES, (u + 1) * LANES)
    for u in range(GA):
        qh = q_ref[:, cols(u)]
        zero = jnp.zeros_like(qh)
        qstk[u, 0:TQ, :] = jnp.where(lane < 64, qh, zero)
        qstk[u, TQ:2 * TQ, :] = jnp.where(lane >= 64, qh, zero)
    _two_pass_attention(GA, lambda u: qstk[u], cols, cols, k_ref, v_ref, qi,
                        lambda u: t0_ref[u], lambda u: t1_ref[u], sbuf, mbuf, lbuf, abuf)
    for u in range(GA):
        o = _normalised(u, lbuf, abuf)
        oh = o[:TQ] - lam * o[TQ:]
        oh = _rms(oh, og_ref[...]) * (1.0 - lambda_init)
        o_ref[:, cols(u)] = oh.astype(bf16)


def _attn_a(qa, ka, va, t0, t1, dl, og, nb, seq, lambda_init):
    n = qa.shape[0]
    nq = seq // TQ
    w = GA * LANES
    return pl.pallas_call(
        functools.partial(_attn_a_kernel, lambda_init=lambda_init),
        grid=(nb, HA // GA, nq),
        in_specs=[pl.BlockSpec((TQ, w), lambda b, g, i: (b * nq + i, g)),
                  pl.BlockSpec((seq, w), lambda b, g, i: (b, g)),
                  pl.BlockSpec((seq, w), lambda b, g, i: (b, g)),
                  pl.BlockSpec((GA, 2 * TQ, TK), lambda b, g, i: (g, 0, 0)),
                  pl.BlockSpec((GA, 2 * TQ, TK), lambda b, g, i: (g, 0, 0)),
                  _const_spec((4, 64)), _const_spec((1, LANES))],
        out_specs=pl.BlockSpec((TQ, w), lambda b, g, i: (b * nq + i, g)),
        out_shape=jax.ShapeDtypeStruct((n, MIX), bf16),
        scratch_shapes=[pltpu.VMEM((GA, 2 * TQ, LANES), bf16),
                        pltpu.VMEM((GA, seq // TK, 2 * TQ, TK), f32),
                        pltpu.VMEM((GA, 2 * TQ, LANES), f32), pltpu.VMEM((GA, 2 * TQ, LANES), f32),
                        pltpu.VMEM((GA, 2 * TQ, LANES), f32)],
        compiler_params=_cparams(("arbitrary", "arbitrary", "arbitrary")),
        name="attn_diff",
    )(qa, ka, va, t0, t1, dl, og)


def _attn_c_kernel(q_ref, k_ref, v_ref, mk_ref, o_ref, sbuf, mbuf, lbuf, abuf):
    qi = pl.program_id(2)
    lane = lax.broadcasted_iota(jnp.int32, (TQ, LANES), 1)
    cols = lambda u: slice(u * LANES, (u + 1) * LANES)
    _two_pass_attention(GC, lambda u: q_ref[:, cols(u)], cols, lambda u: cols(u // 2), k_ref, v_ref, qi,
                        lambda u: mk_ref[...], None, sbuf, mbuf, lbuf, abuf)
    for hp in range(GC // 2):
        o_ref[:, cols(hp)] = jnp.where(lane < VC, _normalised(2 * hp, lbuf, abuf),
                                       _normalised(2 * hp + 1, lbuf, abuf)).astype(bf16)


def _attn_c(qc, kc, vc, mk, nb, seq):
    n = qc.shape[0]
    nq = seq // TQ
    w = GC * LANES
    wv = GC * VC
    return pl.pallas_call(
        _attn_c_kernel,
        grid=(nb, HC // GC, nq),
        in_specs=[pl.BlockSpec((TQ, w), lambda b, g, i: (b * nq + i, g)),
                  pl.BlockSpec((seq, w), lambda b, g, i: (b, g)),
                  pl.BlockSpec((seq, wv), lambda b, g, i: (b, g)),
                  _const_spec((TQ, TK))],
        out_specs=pl.BlockSpec((TQ, wv), lambda b, g, i: (b * nq + i, g)),
        out_shape=jax.ShapeDtypeStruct((n, MIX), bf16),
        scratch_shapes=[pltpu.VMEM((GC, seq // TK, TQ, TK), f32),
                        pltpu.VMEM((GC, TQ, LANES), f32), pltpu.VMEM((GC, TQ, LANES), f32),
                        pltpu.VMEM((GC, TQ, LANES), f32)],
        compiler_params=_cparams(("arbitrary", "arbitrary", "arbitrary")),
        name="attn_latent",
    )(qc, kc, vc, mk)


def _rope_apply(x, cosf, sinf, lane):
    w = x.shape[-1]
    partner = jnp.where((lane % LANES) < NOPE + ROPE // 2,
                        pltpu.roll(x, w - ROPE // 2, 1), pltpu.roll(x, ROPE // 2, 1))
    return x * cosf + partner * sinf


def _mla_prep_kernel(xc_ref, cos_ref, sin_ref, glq_ref, glkv_ref, wq_ref, wkv_ref, gq_ref, gk_ref, gr_ref,
                     sd_ref, ex_ref, q_ref, k_ref, v_ref):
    cq = _rms(xc_ref[:, 0:Q_LORA], glq_ref[...]).astype(bf16)
    ckv = _rms(xc_ref[:, Q_LORA:Q_LORA + KV_LORA], glkv_ref[...]).astype(bf16)
    cos8 = jnp.concatenate([cos_ref[...]] * HC, axis=-1)
    sin8 = jnp.concatenate([sin_ref[...]] * HC, axis=-1)
    lane8 = lax.broadcasted_iota(jnp.int32, (TM, HC * LANES), 1)

    q = jnp.dot(cq, wq_ref[...], preferred_element_type=f32)
    q = q * _seg_rinv(q, sd_ref, ex_ref) * gq_ref[...]
    q_ref[...] = _rope_apply(q, cos8, sin8, lane8).astype(bf16)

    kv = jnp.dot(ckv, wkv_ref[...], preferred_element_type=f32)
    v_ref[...] = kv[:, HC * LANES:].astype(bf16)
    xr = xc_ref[:, Q_LORA + KV_LORA:Q_LORA + KV_LORA + LANES]
    kr = xr * lax.rsqrt(jnp.sum(xr * xr, axis=-1, keepdims=True) * (1.0 / ROPE) + EPS) * gr_ref[...]
    kr = pltpu.roll(kr, NOPE, 1)
    lane1 = lax.broadcasted_iota(jnp.int32, (TM, LANES), 1)
    kr = _rope_apply(kr, cos_ref[...], sin_ref[...], lane1)
    for h in range(HC):
        cols = slice(h * LANES, (h + 1) * LANES)
        kn = kv[:, cols]
        kn = kn * lax.rsqrt(jnp.sum(kn * kn, axis=-1, keepdims=True) * (1.0 / NOPE) + EPS) * gk_ref[...]
        k_ref[:, cols] = (kn + kr).astype(bf16)


def _mla_prep(xc, cosf, sinf, glq, glkv, wq, wkv, gq, gk, gr, sd, ex):
    n = xc.shape[0]
    row = lambda w_: pl.BlockSpec((TM, w_), lambda i: (i, 0))
    return pl.pallas_call(
        _mla_prep_kernel,
        grid=(n // TM,),
        in_specs=[row(MIX), row(LANES), row(LANES), _const_spec((1, Q_LORA)), _const_spec((1, KV_LORA)),
                  _const_spec((Q_LORA, HC * LANES)), _const_spec((KV_LORA, HC * LANES + MIX)),
                  _const_spec((1, HC * LANES)), _const_spec((1, LANES)), _const_spec((1, LANES)),
                  _const_spec((HC * LANES, LANES)), _const_spec((2 * LANES, HC * LANES))],
        out_specs=[row(HC * LANES), row(HC * LANES), row(MIX)],
        out_shape=[jax.ShapeDtypeStruct((n, HC * LANES), bf16), jax.ShapeDtypeStruct((n, HC * LANES), bf16),
                   jax.ShapeDtypeStruct((n, MIX), bf16)],
        compiler_params=_cparams(("arbitrary",)),
        name="mla_prep",
    )(xc, cosf, sinf, glq, glkv, wq, wkv, gq, gk, gr, sd, ex)


def _merge_kernel(oa_ref, ob_ref, oc_ref, gt_ref, x_ref, g1_ref, sh_ref, sc_ref, ng_ref, wb_ref, wo_ref,
                  wr_ref, br_ref, x1_ref, h2_ref, meta_ref, cnt_ref, run_ref):
    i = pl.program_id(0)

    @pl.when(i == 0)
    def _():
        run_ref[...] = jnp.zeros_like(run_ref)

    merged = jnp.zeros((TM, D), f32)
    for j, o_ref in enumerate((oa_ref, ob_ref, oc_ref)):
        pj = jnp.dot(o_ref[...], wb_ref[j], preferred_element_type=f32)
        merged = merged + gt_ref[:, j * D:(j + 1) * D].astype(f32) * pj
    y = jnp.dot(merged.astype(bf16), wo_ref[...], preferred_element_type=f32)
    x1 = x_ref[...] + g1_ref[0] * y
    x1_ref[...] = x1
    h2 = _rms(x1, ng_ref[...]) * (1.0 + sc_ref[0]) + sh_ref[0]
    hr = h2.astype(bf16).astype(f32)
    lo = lax.shift_right_logical(lax.bitcast_convert_type(hr[:, 0:D // 2], jnp.uint32), jnp.uint32(16))
    hi = lax.bitcast_convert_type(hr[:, D // 2:D], jnp.uint32) & jnp.uint32(0xFFFF0000)
    words = lax.bitcast_convert_type(lo | hi, f32)
    for s in range(PACK_SLABS):
        h2_ref[pl.ds(s, TM, stride=PACK_SLABS), :] = words[:, s * LANES:(s + 1) * LANES]

    h_hi = h2.astype(bf16)
    h_lo = (h2 - h_hi.astype(f32)).astype(bf16)
    t = jnp.dot(h_hi, wr_ref[...], preferred_element_type=f32)
    lg = (t[:, :LANES] + t[:, LANES:] + jnp.dot(h_lo, wr_ref[:, 0:LANES], preferred_element_type=f32)
          + br_ref[...])
    lane = lax.broadcasted_iota(jnp.int32, (TM, LANES), 1)
    n_e = N_GROUPS * N_EXP
    gmask = (lane >= n_e) & (lane < n_e + N_GROUPS)
    gl = jnp.where(gmask, lg, NEG)
    gmax = jnp.max(gl, axis=-1, keepdims=True)
    g_w = 1.0 / jnp.sum(jnp.where(gmask, jnp.exp(gl - gmax), 0.0), axis=-1, keepdims=True)
    gidx = jnp.min(jnp.where(gl == gmax, lane - n_e, LANES), axis=-1, keepdims=True)
    emask = (lane < n_e) & ((lane // N_EXP) == gidx)
    el = jnp.where(emask, lg, NEG)
    m1 = jnp.max(el, axis=-1, keepdims=True)
    i1 = jnp.min(jnp.where(el == m1, lane, LANES), axis=-1, keepdims=True)
    el2 = jnp.where(lane == i1, NEG, el)
    m2 = jnp.max(el2, axis=-1, keepdims=True)
    i2 = jnp.min(jnp.where(el2 == m2, lane, LANES), axis=-1, keepdims=True)
    t = jnp.exp(m2 - m1)
    w1 = g_w / (1.0 + t)
    w2 = g_w * t / (1.0 + t)
    lo = jnp.minimum(i1, i2)
    hi = jnp.maximum(i1, i2)
    w_lo = jnp.where(i1 < i2, w1, w2)
    w_hi = jnp.where(i1 < i2, w2, w1)
    bucket = gidx * (N_EXP * N_EXP) + (lo % N_EXP) * N_EXP + (hi % N_EXP)

    lane2 = lax.broadcasted_iota(jnp.int32, (TM, 2 * LANES), 1)
    onehot = (lane2 == bucket).astype(f32)
    rr = lax.broadcasted_iota(jnp.int32, (TM, TM), 0)
    cc = lax.broadcasted_iota(jnp.int32, (TM, TM), 1)
    tri = (cc < rr).astype(bf16)
    before = jnp.dot(tri, onehot.astype(bf16), preferred_element_type=f32) + run_ref[...]
    rank = jnp.sum(onehot * before, axis=-1, keepdims=True)
    run_ref[...] = run_ref[...] + jnp.sum(onehot, axis=0, keepdims=True)
    cnt_ref[...] = run_ref[...]

    meta = (jnp.where(lane == i1, w1, 0.0) + jnp.where(lane == i2, w2, 0.0)
            + jnp.where(lane == 32, bucket.astype(f32), 0.0) + jnp.where(lane == 33, rank, 0.0)
            + jnp.where(lane == 34, w_lo, 0.0) + jnp.where(lane == 35, w_hi, 0.0))
    meta_ref[...] = meta


def _merge(oa, ob, oc, gt, x, g1, shift, scale, ng, wb, wo, wr, br, seq):
    n = x.shape[0]
    spb = seq // TM
    row = lambda w_: pl.BlockSpec((TM, w_), lambda i: (i, 0))
    modspec = pl.BlockSpec((1, 1, D), lambda i: (i // spb, 0, 0))
    return pl.pallas_call(
        _merge_kernel,
        grid=(n // TM,),
        in_specs=[row(MIX), row(MIX), row(MIX), row(3 * D), row(D), modspec, modspec, modspec,
                  _const_spec((1, D)), _const_spec((3, MIX, D)), _const_spec((D, D)),
                  _const_spec((D, 2 * LANES)), _const_spec((1, LANES))],
        out_specs=[row(D), pl.BlockSpec((TM * PACK_SLABS, LANES), lambda i: (i, 0)), row(LANES),
                   _const_spec((1, 2 * LANES))],
        out_shape=[jax.ShapeDtypeStruct((n, D), f32), jax.ShapeDtypeStruct((n * PACK_SLABS, LANES), f32),
                   jax.ShapeDtypeStruct((n, LANES), f32), jax.ShapeDtypeStruct((1, 2 * LANES), f32)],
        scratch_shapes=[pltpu.VMEM((1, 2 * LANES), f32)],
        compiler_params=_cparams(("arbitrary",)),
        name="merge_route",
    )(oa, ob, oc, gt, x, g1, shift, scale, ng, wb, wo, wr, br)


TE = 128
N_BUCKETS = N_GROUPS * N_EXP * N_EXP
N_PAIRS = N_GROUPS * (N_EXP * (N_EXP - 1) // 2)


def _num_tiles(n):
    return n // TE + N_PAIRS


TP = 2048


def _tables_kernel(meta_ref, cnt_ref, pos_ref, tt_ref):
    cnt = cnt_ref[...]
    ntile = jnp.floor((cnt + (TE - 1)) * (1.0 / TE))
    bi = lax.broadcasted_iota(jnp.int32, (N_BUCKETS, N_BUCKETS), 0)
    bj = lax.broadcasted_iota(jnp.int32, (N_BUCKETS, N_BUCKETS), 1)
    upper = (bi <= bj).astype(bf16)
    incl = jnp.dot(jnp.broadcast_to(ntile, (8, N_BUCKETS)).astype(bf16), upper, preferred_element_type=f32)[0:1]
    excl = incl - ntile

    meta = meta_ref[...]
    tp = meta.shape[0]
    lane = lax.broadcasted_iota(jnp.int32, (tp, LANES), 1)
    bucket = jnp.sum(jnp.where(lane == 32, meta, 0.0), axis=-1, keepdims=True).astype(jnp.int32)
    rank = jnp.sum(jnp.where(lane == 33, meta, 0.0), axis=-1, keepdims=True)
    lane2 = lax.broadcasted_iota(jnp.int32, (tp, N_BUCKETS), 1)
    first_tile = jnp.sum(jnp.where(lane2 == bucket, excl, 0.0), axis=-1, keepdims=True)
    pos_ref[...] = (first_tile * TE + rank).astype(jnp.int32)

    @pl.when(pl.program_id(0) == 0)
    def _():
        tau = lax.broadcasted_iota(jnp.int32, (N_BUCKETS, 1), 0).astype(f32)
        lane_b = lax.broadcasted_iota(jnp.int32, (1, N_BUCKETS), 1).astype(f32)
        tb = jnp.sum((incl <= tau).astype(f32), axis=-1, keepdims=True)
        last_b = jnp.max(jnp.where(cnt > 0, lane_b, 0.0), axis=-1, keepdims=True)
        tbi = jnp.minimum(tb, last_b).astype(jnp.int32)
        grp = lax.shift_right_logical(tbi, 6) * N_EXP
        ea = grp + (lax.shift_right_logical(tbi, 3) & 7)
        eb = grp + (tbi & 7)
        total = jnp.sum(ntile, axis=-1, keepdims=True).astype(jnp.int32)
        lane_t = lax.broadcasted_iota(jnp.int32, (N_BUCKETS, LANES), 1)
        tt_ref[...] = jnp.where(lane_t == 0, ea, jnp.where(lane_t == 1, eb, jnp.where(lane_t == 2, total, 0)))


def _tables(meta, cnt):
    n = meta.shape[0]
    tp = math.gcd(n, TP)
    assert _num_tiles(n) <= N_BUCKETS
    return pl.pallas_call(
        _tables_kernel,
        grid=(n // tp,),
        in_specs=[pl.BlockSpec((tp, LANES), lambda i: (i, 0)), _const_spec((1, N_BUCKETS))],
        out_specs=[pl.BlockSpec((tp, 1), lambda i: (i, 0)), _const_spec((N_BUCKETS, LANES))],
        out_shape=[jax.ShapeDtypeStruct((n, 1), jnp.int32), jax.ShapeDtypeStruct((N_BUCKETS, LANES), jnp.int32)],
        compiler_params=_cparams(("arbitrary",)),
        name="moe_tables",
    )(meta, cnt)


def _plan_kernel(pos_ref, tok_ref, *, n, nslot):
    def fill(s, _):
        tok_ref[s] = -1
        return 0

    lax.fori_loop(0, nslot, fill, 0, unroll=16)

    def place(t, _):
        tok_ref[pos_ref[t]] = t
        return 0

    lax.fori_loop(0, n, place, 0, unroll=16)


def _plan(pos):
    n = pos.shape[0]
    nslot = _num_tiles(n) * TE
    smem = pl.BlockSpec(memory_space=pltpu.SMEM)
    return pl.pallas_call(
        functools.partial(_plan_kernel, n=n, nslot=nslot),
        in_specs=[smem], out_specs=smem,
        out_shape=jax.ShapeDtypeStruct((nslot,), jnp.int32),
        name="moe_plan",
    )(pos)


def _moe_kernel(tea_ref, teb_ref, nt_ref, tok_ref, h_ref, m_ref, wg_ref, wu_ref, wd_ref, y_hbm,
                xs, ms, ob0, ob1, ssem, *, n):
    i = pl.program_id(0)
    nt = nt_ref[0]
    obufs = (ob0, ob1)

    def scatter_copy(dst_tok, sl, r):
        dst = y_hbm.at[pl.ds(pl.multiple_of(dst_tok * Y_PITCH, Y_PITCH), Y_PITCH), :]
        return pltpu.make_async_copy(obufs[sl].at[pl.ds(r * O_PITCH, Y_PITCH), :], dst, ssem.at[sl])

    def start_scatters(tile, sl, to_spare):
        for r in range(TE):
            t = tok_ref[tile * TE + r]
            scatter_copy(jnp.where((t < 0) | to_spare, n + sl * TE + r, t), sl, r).start(priority=r % 2)

    def wait_scatters(sl):
        for r in range(TE):
            scatter_copy(n, sl, r).wait()

    @pl.when(i == 0)
    def _():
        for sl in range(2):
            obufs[sl][...] = jnp.zeros_like(obufs[sl])
            spare = pltpu.make_async_copy(obufs[sl].at[pl.ds(0, TE * Y_PITCH), :],
                                          y_hbm.at[pl.ds((n + sl * TE) * Y_PITCH, TE * Y_PITCH), :], ssem.at[sl])
            spare.start()
            spare.wait()

    def step(s):
        o = 1 - s

        @pl.when(i >= 1)
        def _():
            wait_scatters(s)

        start_scatters(jnp.maximum(i - 1, 0), o, i == 0)

        for r in range(TE):
            t = jnp.maximum(tok_ref[i * TE + r], 0)
            xs[pl.ds(r * PACK_SLABS, PACK_SLABS), :] = h_ref[pl.ds(pl.multiple_of(t * PACK_SLABS, PACK_SLABS),
                                                                   PACK_SLABS), :]
            ms[pl.ds(r, 1), :] = m_ref[pl.ds(t, 1), :]
        slab = lambda k: xs[pl.ds(k, TE, stride=PACK_SLABS), :]
        words = lax.bitcast_convert_type(jnp.concatenate([slab(k) for k in range(PACK_SLABS)], axis=-1),
                                         jnp.uint32)
        lo = lax.bitcast_convert_type(lax.shift_left(words, jnp.uint32(16)), f32)
        hi = lax.bitcast_convert_type(words & jnp.uint32(0xFFFF0000), f32)
        hb = jnp.concatenate([lo, hi], axis=-1).astype(bf16)
        meta = ms[...]
        lane = lax.broadcasted_iota(jnp.int32, (TE, LANES), 1)
        w_lo = jnp.sum(jnp.where(lane == 34, meta, 0.0), axis=-1, keepdims=True)
        w_hi = jnp.sum(jnp.where(lane == 35, meta, 0.0), axis=-1, keepdims=True)

        def ffn(e, wcol):
            a = jnp.dot(hb, wg_ref[0, e], preferred_element_type=f32)
            u = jnp.dot(hb, wu_ref[0, e], preferred_element_type=f32)
            act = (a * jax.nn.sigmoid(a)) * u * wcol
            return jnp.dot(act.astype(bf16), wd_ref[0, e], preferred_element_type=f32)

        out = ffn(lax.rem(tea_ref[i], N_EXP), w_lo) + ffn(lax.rem(teb_ref[i], N_EXP), w_hi)
        for k in range(SLABS):
            obufs[s][pl.ds(k, TE, stride=O_PITCH), :] = out[:, k * LANES:(k + 1) * LANES]

        @pl.when(i == nt - 1)
        def _():
            start_scatters(i, s, False)
            wait_scatters(o)
            wait_scatters(s)

    for s in range(2):
        @pl.when((i < nt) & (lax.rem(i, 2) == s))
        def _(s=s):
            step(s)


def _moe(h2ext, meta, tok, tea, teb, nt, wg, wu, wd):
    n = h2ext.shape[0] // PACK_SLABS
    ntmax = _num_tiles(n)
    once = pl.Buffered(1)
    wspec_in = pl.BlockSpec((1, N_EXP, D, F_EXP), lambda i, tea, teb, nt, tok: (tea[i] // N_EXP, 0, 0, 0),
                            pipeline_mode=once)
    wspec_out = pl.BlockSpec((1, N_EXP, F_EXP, D), lambda i, tea, teb, nt, tok: (tea[i] // N_EXP, 0, 0, 0),
                             pipeline_mode=once)
    hspec = pl.BlockSpec((n * PACK_SLABS, LANES), lambda i, tea, teb, nt, tok: (0, 0), pipeline_mode=once)
    mspec = pl.BlockSpec((n, LANES), lambda i, tea, teb, nt, tok: (0, 0), pipeline_mode=once)
    return pl.pallas_call(
        functools.partial(_moe_kernel, n=n),
        grid_spec=pltpu.PrefetchScalarGridSpec(
            num_scalar_prefetch=4,
            grid=(ntmax,),
            in_specs=[hspec, mspec, wspec_in, wspec_in, wspec_out],
            out_specs=pl.BlockSpec(memory_space=pl.ANY),
            scratch_shapes=[pltpu.VMEM((TE * PACK_SLABS, LANES), f32), pltpu.VMEM((TE, LANES), f32)] + [
                pltpu.VMEM((TE * O_PITCH, LANES), f32)] * 2 + [pltpu.SemaphoreType.DMA((2,))]),
        out_shape=jax.ShapeDtypeStruct(((n + 2 * TE) * Y_PITCH, LANES), f32),
        compiler_params=_cparams(("arbitrary",), vmem=60 * 1024 * 1024),
        name="moe_sparse",
    )(tea, teb, nt, tok, h2ext, meta, wg, wu, wd)


def _final_kernel(x_ref, y_ref, g_ref, o_ref):
    o_ref[...] = x_ref[...] + g_ref[0] * _rows_from_token_major(y_ref, x_ref.shape[0])


def _final_combine(x1, y, g2, seq):
    n = x1.shape[0]
    t = 512
    spb = seq // t
    row = pl.BlockSpec((t, D), lambda i: (i, 0))
    return pl.pallas_call(
        _final_kernel,
        grid=(n // t,),
        in_specs=[row, pl.BlockSpec((t * Y_PITCH, LANES), lambda i: (i, 0)),
                  pl.BlockSpec((1, 1, D), lambda i: (i // spb, 0, 0))],
        out_specs=row,
        out_shape=jax.ShapeDtypeStruct((n, D), f32),
        compiler_params=_cparams(("arbitrary",)),
        name="final_combine",
    )(x1, y, g2)


def _seg_matrices(width, segs):
    sd = np.zeros((width, LANES), np.float32)
    ex = np.zeros((LANES, width), np.float32)
    for j, (s, ln) in enumerate(segs):
        sd[s:s + ln, j] = 1.0 / ln
        ex[j, s:s + ln] = 1.0
    return jnp.asarray(sd, bf16), jnp.asarray(np.concatenate([ex, ex], axis=0), bf16)


def _head_pad(w, heads, per_head, keep):
    k = w.shape[0]
    w3 = w.reshape(k, heads, per_head)[:, :, :keep]
    return jnp.pad(w3, ((0, 0), (0, 0), (0, LANES - keep))).reshape(k, heads * LANES)


def kernel(x, c, positions, w_ada, b_ada, norm_g, w_in, diff_qk_g, diff_lambda, diff_out_g, rel_bias, sgu_v_g, sgu_w, sgu_b, mla_lat_g, mla_w_uq, mla_w_ukv, mla_qk_g, w_branch, w_out, router_g_w, router_g_b, router_e_w, router_e_b, w_e_gate, w_e_up, w_e_down):
    nb, seq, _ = x.shape
    n = nb * seq
    assert seq % TQ == 0 and seq % TM == 0 and x.shape[2] == D

    mod = _ada(c, w_ada, b_ada)
    cosf, sinf = _rope_tables(positions)
    t0, t1, mk = _bias_tiles(rel_bias)

    sd_a, ex_a = _seg_matrices(MIX, [(s * 64, 64) for s in range(8)])
    segs_q = []
    for h in range(HC):
        segs_q += [(h * LANES, NOPE), (h * LANES + NOPE, ROPE)]
    sd_q, ex_q = _seg_matrices(HC * LANES, segs_q)
    ne = N_GROUPS * N_EXP
    zpad = lambda a, w_: jnp.pad(a, ((0, 0), (0, w_ - a.shape[1])))

    xcur = x.reshape(n, D)
    y_prev, g2_prev = None, None
    for l in range(DEPTH):
        m3 = mod[l].reshape(nb, 1, 6 * D)
        shift1, scale1, gate1, shift2, scale2, gate2 = [m3[:, :, k * D:(k + 1) * D] for k in range(6)]
        lambda_init = LAMBDA_INIT_BASE - LAMBDA_INIT_SCALE * math.exp(-LAMBDA_INIT_DECAY * l)

        w = w_in[l]
        w_pad = jnp.concatenate([w[:, :XC_OFF + 416].astype(bf16), jnp.zeros((D, IN_PAD - IN_REAL), bf16),
                                 w[:, XC_OFF + 416:].astype(bf16)], axis=1)
        gq = (jnp.tile(diff_qk_g[l, 0], 8) * (64 ** -0.5 * LOG2E)).reshape(1, MIX)
        gk = jnp.tile(diff_qk_g[l, 1], 8).reshape(1, MIX)
        outs = _inproj(xcur, y_prev, g2_prev, shift1, scale1, norm_g[l, 0].reshape(1, D), w_pad, gq, gk,
                       sd_a, ex_a, sgu_v_g[l].reshape(1, MIX), sgu_w[l], jnp.transpose(sgu_b[l]), seq)
        if y_prev is not None:
            xcur = outs[0]
            outs = outs[1:]
        qa, ka, va, ob, xc, gt = outs

        oa = _attn_a(qa, ka, va, t0, t1, diff_lambda[l], diff_out_g[l].reshape(1, LANES), nb, seq, lambda_init)

        qkg = mla_qk_g[l]
        wq = _head_pad(mla_w_uq[l], HC, NOPE + ROPE, NOPE + ROPE).astype(bf16)
        wkv3 = mla_w_ukv[l].reshape(KV_LORA, HC, NOPE + VC)
        wk = jnp.pad(wkv3[:, :, :NOPE], ((0, 0), (0, 0), (0, LANES - NOPE))).reshape(KV_LORA, HC * LANES)
        wv = wkv3[:, :, NOPE:].reshape(KV_LORA, HC * VC)
        wkv = jnp.concatenate([wk, wv], axis=1).astype(bf16)
        gq_c = jnp.tile(jnp.pad(qkg[0], (0, LANES - NOPE - ROPE)), HC).reshape(1, HC * LANES) * ((NOPE + ROPE) ** -0.5 * LOG2E)
        gk_c = jnp.pad(qkg[1, :NOPE], (0, LANES - NOPE)).reshape(1, LANES)
        gr_c = jnp.pad(qkg[1, NOPE:], (0, LANES - ROPE)).reshape(1, LANES)
        qc, kc, vc = _mla_prep(xc, cosf, sinf, mla_lat_g[l, :Q_LORA].reshape(1, Q_LORA),
                               mla_lat_g[l, Q_LORA:].reshape(1, KV_LORA), wq, wkv, gq_c, gk_c, gr_c, sd_q, ex_q)
        oc = _attn_c(qc, kc, vc, mk, nb, seq)

        wr = zpad(jnp.concatenate([router_e_w[l], router_g_w[l]], axis=1), LANES)
        wr_hi = wr.astype(bf16)
        wr = jnp.concatenate([wr_hi, (wr - wr_hi.astype(f32)).astype(bf16)], axis=1)
        br = zpad(jnp.concatenate([router_e_b[l], router_g_b[l]]).reshape(1, ne + N_GROUPS), LANES)
        x1, h2ext, meta, cnt = _merge(oa, ob, oc, gt, xcur, gate1, shift2, scale2, norm_g[l, 1].reshape(1, D),
                                      w_branch[l].astype(bf16), w_out[l].astype(bf16), wr, br, seq)

        pos, tt = _tables(meta, cnt)
        tok = _plan(pos.reshape(n))
        ntmax = _num_tiles(n)
        y_prev = _moe(h2ext, meta, tok, tt[:ntmax, 0], tt[:ntmax, 1], tt[0:1, 2], w_e_gate[l].astype(bf16),
                      w_e_up[l].astype(bf16), w_e_down[l].astype(bf16))
        g2_prev = gate2
        xcur = x1

    out = _final_combine(xcur, y_prev, g2_prev, seq)
    return out.reshape(nb, seq, D)
```

```python
import functools
import math

import jax
import jax.numpy as jnp
import numpy as np
from jax import lax
from jax.experimental import pallas as pl
from jax.experimental.pallas import tpu as pltpu

f32 = jnp.float32
bf16 = jnp.bfloat16

D = 1024
DEPTH = 4
CHUNK = 64
MIX = 512
HA = 4
HC = 8
Q_LORA, KV_LORA, ROPE = 256, 128, 32
NOPE, VC = 64, 64
N_GROUPS, N_EXP, F_EXP = 4, 8, 256
N_REL_BUCKETS = 32
ROPE_THETA = 10000.0
LAMBDA_INIT_BASE, LAMBDA_INIT_SCALE, LAMBDA_INIT_DECAY = 0.8, 0.6, 0.3
EPS = 1e-6
NEG = -1e30

XC_OFF = 2560
MAIN_W = XC_OFF + Q_LORA + KV_LORA + ROPE

TM = 512
TQ = 256
TK = 256
LANES = 128
VMEM_LIMIT = 56 * 1024 * 1024
SLABS = D // LANES
PACK_SLABS = SLABS // 2
Y_PITCH = SLABS
O_PITCH = Y_PITCH + 1


def _cparams(sem, vmem=VMEM_LIMIT):
    return pltpu.CompilerParams(dimension_semantics=sem, vmem_limit_bytes=vmem)


def _const_spec(shape):
    nd = len(shape)
    return pl.BlockSpec(shape, lambda *_: (0,) * nd)


def _rms(x, g_row):
    return x * lax.rsqrt(jnp.mean(x * x, axis=-1, keepdims=True) + EPS) * g_row


def _rows_from_token_major(y_ref, rows):
    return jnp.concatenate([y_ref[pl.ds(k, rows, stride=Y_PITCH), :] for k in range(SLABS)], axis=-1)


def _seg_rinv(x, sd_ref, ex_ref):
    ss = jnp.dot((x * x).astype(bf16), sd_ref[...], preferred_element_type=f32)
    hi = ss.astype(bf16)
    lo = (ss - hi.astype(f32)).astype(bf16)
    full = jnp.dot(jnp.concatenate([hi, lo], axis=-1), ex_ref[...], preferred_element_type=f32)
    return lax.rsqrt(full + EPS)


def _ada_kernel(c_ref, w_ref, b_ref, o_ref):
    c = c_ref[...]
    ca = (c * jax.nn.sigmoid(c)).astype(bf16)
    o_ref[0] = jnp.dot(ca, w_ref[0].astype(bf16), preferred_element_type=f32) + b_ref[0]


def _ada(c, w_ada, b_ada):
    nb = c.shape[0]
    tn = 1024
    return pl.pallas_call(
        _ada_kernel,
        grid=(DEPTH, 6 * D // tn),
        in_specs=[pl.BlockSpec((nb, D), lambda l, j: (0, 0)),
                  pl.BlockSpec((1, D, tn), lambda l, j: (l, 0, j)),
                  pl.BlockSpec((1, 1, tn), lambda l, j: (l, 0, j))],
        out_specs=pl.BlockSpec((1, nb, tn), lambda l, j: (l, 0, j)),
        out_shape=jax.ShapeDtypeStruct((DEPTH, nb, 6 * D), f32),
        compiler_params=_cparams(("arbitrary", "arbitrary")),
        name="ada_mod",
    )(c, w_ada, b_ada.reshape(DEPTH, 1, 6 * D))


def _rope_kernel(pos_ref, invf_ref, cos_ref, sin_ref):
    ang = pos_ref[...].astype(f32) * invf_ref[...]
    lane = lax.broadcasted_iota(jnp.int32, ang.shape, 1)
    rot = (lane >= NOPE) & (lane < NOPE + ROPE)
    first = lane < NOPE + ROPE // 2
    cos_ref[...] = jnp.where(rot, jnp.cos(ang), 1.0)
    s = jnp.sin(ang)
    sin_ref[...] = jnp.where(rot, jnp.where(first, -s, s), 0.0)


def _rope_tables(positions):
    n = positions.size
    inv_freq = ROPE_THETA ** (-jnp.arange(0, ROPE, 2, dtype=f32) / ROPE)
    invf = jnp.zeros((1, LANES), f32).at[0, NOPE:NOPE + ROPE].set(jnp.tile(inv_freq, 2))
    t = math.gcd(n, 2048)
    return pl.pallas_call(
        _rope_kernel,
        grid=(n // t,),
        in_specs=[pl.BlockSpec((t, 1), lambda i: (i, 0)), _const_spec((1, LANES))],
        out_specs=[pl.BlockSpec((t, LANES), lambda i: (i, 0))] * 2,
        out_shape=[jax.ShapeDtypeStruct((n, LANES), f32)] * 2,
        compiler_params=_cparams(("arbitrary",)),
        name="rope_tables",
    )(positions.reshape(n, 1), invf)


def _t5_bucket(rel):
    n = jnp.abs(rel)
    large = 8 + sum((n >= t).astype(jnp.int32) for t in (12, 16, 23, 32, 46, 64, 91))
    return jnp.where(rel > 0, 16, 0) + jnp.where(n < 8, n, large)


def _bias_kernel(tbl_ref, t0_ref, t1_ref, mk_ref):
    r = lax.broadcasted_iota(jnp.int32, (TQ, TK), 0)
    c = lax.broadcasted_iota(jnp.int32, (TQ, TK), 1)
    allowed = (c // CHUNK) <= (r // CHUNK)
    mk = jnp.where(allowed, 0.0, NEG).astype(f32)
    mk_ref[...] = mk
    b0 = _t5_bucket(c - r)
    b1 = _t5_bucket(c - r - TK)
    for h in range(HA):
        far = tbl_ref[N_REL_BUCKETS // 2 - 1, h]
        t0 = jnp.zeros((TQ, TK), f32)
        t1 = jnp.zeros((TQ, TK), f32)
        for b in range(N_REL_BUCKETS):
            v = (tbl_ref[b, h] - far) * LOG2E
            t0 = jnp.where(b0 == b, v, t0)
            t1 = jnp.where(b1 == b, v, t1)
        t0 = t0 + mk
        t0_ref[h, 0:TQ, :] = t0
        t0_ref[h, TQ:2 * TQ, :] = t0
        t1_ref[h, 0:TQ, :] = t1
        t1_ref[h, TQ:2 * TQ, :] = t1


def _bias_tiles(rel_bias):
    return pl.pallas_call(
        _bias_kernel,
        in_specs=[pl.BlockSpec(memory_space=pltpu.SMEM)],
        out_specs=[pl.BlockSpec(memory_space=pltpu.VMEM)] * 3,
        out_shape=[jax.ShapeDtypeStruct((HA, 2 * TQ, TK), f32),
                   jax.ShapeDtypeStruct((HA, 2 * TQ, TK), f32),
                   jax.ShapeDtypeStruct((TQ, TK), f32)],
        name="bias_tiles",
    )(rel_bias)


def _inproj_kernel(*refs, combine):
    if combine:
        (xa_ref, y_ref, g2_ref, sh_ref, sc_ref, ng_ref, w_ref, wgt_ref, gq_ref, gk_ref, sd_ref, ex_ref,
         vg_ref, ws_ref, bs_ref,
         xo_ref, qa_ref, ka_ref, va_ref, ob_ref, xc_ref, gt_ref) = refs
        x = xa_ref[...] + g2_ref[0] * _rows_from_token_major(y_ref, TM)
        xo_ref[...] = x
    else:
        (xa_ref, sh_ref, sc_ref, ng_ref, w_ref, wgt_ref, gq_ref, gk_ref, sd_ref, ex_ref,
         vg_ref, ws_ref, bs_ref,
         qa_ref, ka_ref, va_ref, ob_ref, xc_ref, gt_ref) = refs
        x = xa_ref[...]
    h = _rms(x, ng_ref[...]) * (1.0 + sc_ref[0]) + sh_ref[0]
    hb = h.astype(bf16)

    q = jnp.dot(hb, w_ref[0, :, 0:MIX], preferred_element_type=f32)
    qa_ref[...] = (q * _seg_rinv(q, sd_ref, ex_ref) * gq_ref[...]).astype(bf16)
    k = jnp.dot(hb, w_ref[0, :, MIX:2 * MIX], preferred_element_type=f32)
    ka_ref[...] = (k * _seg_rinv(k, sd_ref, ex_ref) * gk_ref[...]).astype(bf16)
    va_ref[...] = jnp.dot(hb, w_ref[0, :, 2 * MIX:3 * MIX], preferred_element_type=f32).astype(bf16)

    z = jnp.dot(hb, w_ref[0, :, 3 * MIX:5 * MIX], preferred_element_type=f32)
    z = 0.5 * z * (1.0 + jnp.tanh(math.sqrt(2.0 / math.pi) * (z + 0.044715 * (z * z * z))))
    u = z[:, :MIX]
    v = _rms(z[:, MIX:], vg_ref[...]).astype(bf16)
    ri = lax.broadcasted_iota(jnp.int32, (LANES, LANES), 0)
    ci = lax.broadcasted_iota(jnp.int32, (LANES, LANES), 1)
    allowed = (ci // CHUNK) <= (ri // CHUNK)
    for g in range(4):
        wm = jnp.where(allowed, ws_ref[g], 0.0).astype(bf16)
        bcol = bs_ref[:, g:g + 1]
        for wd in range(TM // LANES):
            rows = slice(wd * LANES, (wd + 1) * LANES)
            cols = slice(g * LANES, (g + 1) * LANES)
            vs = jnp.dot(wm, v[rows, cols], preferred_element_type=f32) + bcol
            ob_ref[rows, cols] = (u[rows, cols] * vs).astype(bf16)

    lat = Q_LORA + KV_LORA
    xc_ref[:, 0:lat] = jnp.dot(hb, w_ref[0, :, XC_OFF:XC_OFF + lat], preferred_element_type=f32)
    xr = jnp.dot(hb, w_ref[0, :, XC_OFF + lat:XC_OFF + lat + ROPE], preferred_element_type=f32)
    xc_ref[:, lat:lat + LANES] = jnp.concatenate([xr, jnp.zeros((TM, LANES - ROPE), f32)], axis=-1)

    for j in range(3):
        gl = jnp.dot(hb, wgt_ref[0, :, j * D:(j + 1) * D], preferred_element_type=f32)
        gt_ref[:, j * D:(j + 1) * D] = jax.nn.sigmoid(gl).astype(bf16)


def _inproj(xa, y, g2, shift, scale, ng, w, wgt, layer, gq, gk, sd, ex, vg, ws, bs, seq):
    n = xa.shape[0]
    spb = seq // TM
    combine = y is not None
    row = lambda w_: pl.BlockSpec((TM, w_), lambda i: (i, 0))
    modspec = pl.BlockSpec((1, 1, D), lambda i: (i // spb, 0, 0))
    in_specs = [row(D)]
    args = [xa]
    if combine:
        in_specs += [pl.BlockSpec((TM * Y_PITCH, LANES), lambda i: (i, 0)), modspec]
        args += [y, g2]
    in_specs += [modspec, modspec, _const_spec((1, D)),
                 pl.BlockSpec((1, D, MAIN_W), lambda i: (layer, 0, 0), pipeline_mode=pl.Buffered(1)),
                 pl.BlockSpec((1, D, 3 * D), lambda i: (layer, 0, 0), pipeline_mode=pl.Buffered(1)),
                 _const_spec((1, MIX)), _const_spec((1, MIX)),
                 _const_spec((MIX, LANES)), _const_spec((2 * LANES, MIX)),
                 _const_spec((1, MIX)), _const_spec((4, LANES, LANES)), _const_spec((LANES, 4))]
    args += [shift, scale, ng, w, wgt, gq, gk, sd, ex, vg, ws, bs]
    out_specs = [row(MIX)] * 4 + [row(MIX), row(3 * D)]
    out_shape = [jax.ShapeDtypeStruct((n, MIX), bf16)] * 4 + [
        jax.ShapeDtypeStruct((n, MIX), f32), jax.ShapeDtypeStruct((n, 3 * D), bf16)]
    if combine:
        out_specs = [row(D)] + out_specs
        out_shape = [jax.ShapeDtypeStruct((n, D), f32)] + out_shape
    return pl.pallas_call(
        functools.partial(_inproj_kernel, combine=combine),
        grid=(n // TM,),
        in_specs=in_specs, out_specs=out_specs, out_shape=out_shape,
        compiler_params=_cparams(("arbitrary",)),
        name="inproj",
    )(*args)


LOG2E = math.log2(math.e)


def _two_pass_attention(nunits, q_of, kcols_of, vcols_of, k_ref, v_ref, qi, diag_bias_of, sub_bias_of,
                        sbuf, mbuf, lbuf, abuf):
    def keys(j, nb):
        return pl.ds(pl.multiple_of(j * TK, TK), nb * TK)

    def score_blocks(j, nb, bias_of):
        for u in range(nunits):
            s = lax.dot_general(q_of(u), k_ref[keys(j, nb), kcols_of(u)], (((1,), (1,)), ((), ())),
                                preferred_element_type=f32)
            if bias_of is not None:
                s = s + bias_of(u)
            m = mbuf[u]
            for b in range(nb):
                sbuf[u, j + b] = s[:, b * TK:(b + 1) * TK]
            for c in range(nb * TK // LANES):
                m = jnp.maximum(m, s[:, c * LANES:(c + 1) * LANES])
            mbuf[u] = m

    def blocks_in_pairs(first, count, fn):
        def pair(jj, _):
            fn(first + 2 * jj, 2)
            return 0

        lax.fori_loop(0, count // 2, pair, 0)

        @pl.when(count % 2 == 1)
        def _():
            fn(first + count - 1, 1)

    for u in range(nunits):
        mbuf[u] = jnp.full(mbuf.shape[1:], NEG, f32)
    if sub_bias_of is None:
        nfar = qi
    else:
        nfar = jnp.maximum(qi - 1, 0)

        @pl.when(qi >= 1)
        def _():
            score_blocks(qi - 1, 1, sub_bias_of)

    blocks_in_pairs(0, nfar, lambda j, nb: score_blocks(j, nb, None))
    score_blocks(qi, 1, diag_bias_of)

    for u in range(nunits):
        mbuf[u] = jnp.broadcast_to(jnp.max(mbuf[u], axis=-1, keepdims=True), mbuf.shape[1:])
        lbuf[u] = jnp.zeros(lbuf.shape[1:], f32)
        abuf[u] = jnp.zeros(abuf.shape[1:], f32)

    def accumulate_blocks(j, nb):
        for u in range(nunits):
            mb = mbuf[u]
            lsum = lbuf[u]
            ps = []
            for b in range(nb):
                s = sbuf[u, j + b]
                for c in range(TK // LANES):
                    p = jnp.exp2(s[:, c * LANES:(c + 1) * LANES] - mb)
                    lsum = lsum + p
                    ps.append(p)
            lbuf[u] = lsum
            p = jnp.concatenate(ps, axis=-1).astype(bf16)
            abuf[u] = abuf[u] + jnp.dot(p, v_ref[keys(j, nb), vcols_of(u)], preferred_element_type=f32)

    blocks_in_pairs(0, qi + 1, accumulate_blocks)


def _normalised(u, lbuf, abuf):
    return abuf[u] / jnp.sum(lbuf[u], axis=-1, keepdims=True)


GA = 2
GC = 4


def _attn_a_kernel(q_ref, k_ref, v_ref, t0_ref, t1_ref, dl_ref, og_ref, o_ref,
                   qstk, sbuf, mbuf, lbuf, abuf, *, lambda_init):
    qi = pl.program_id(2)
    dl = dl_ref[...]
    lam = (jnp.exp(jnp.sum(dl[0:1] * dl[1:2], axis=-1, keepdims=True))
           - jnp.exp(jnp.sum(dl[2:3] * dl[3:4], axis=-1, keepdims=True)) + lambda_init)
    lane = lax.broadcasted_iota(jnp.int32, (TQ, LANES), 1)
    cols = lambda u: slice(u * LANES, (u + 1) * LANES)
    for u in range(GA):
        qh = q_ref[:, cols(u)]
        zero = jnp.zeros_like(qh)
        qstk[u, 0:TQ, :] = jnp.where(lane < 64, qh, zero)
        qstk[u, TQ:2 * TQ, :] = jnp.where(lane >= 64, qh, zero)
    _two_pass_attention(GA, lambda u: qstk[u], cols, cols, k_ref, v_ref, qi,
                        lambda u: t0_ref[u], lambda u: t1_ref[u], sbuf, mbuf, lbuf, abuf)
    for u in range(GA):
        o = _normalised(u, lbuf, abuf)
        oh = o[:TQ] - lam * o[TQ:]
        oh = _rms(oh, og_ref[...]) * (1.0 - lambda_init)
        o_ref[:, cols(u)] = oh.astype(bf16)


def _attn_a(qa, ka, va, t0, t1, dl, og, nb, seq, lambda_init):
    n = qa.shape[0]
    nq = seq // TQ
    w = GA * LANES
    return pl.pallas_call(
        functools.partial(_attn_a_kernel, lambda_init=lambda_init),
        grid=(nb, HA // GA, nq),
        in_specs=[pl.BlockSpec((TQ, w), lambda b, g, i: (b * nq + i, g)),
                  pl.BlockSpec((seq, w), lambda b, g, i: (b, g)),
                  pl.BlockSpec((seq, w), lambda b, g, i: (b, g)),
                  pl.BlockSpec((GA, 2 * TQ, TK), lambda b, g, i: (g, 0, 0)),
                  pl.BlockSpec((GA, 2 * TQ, TK), lambda b, g, i: (g, 0, 0)),
                  _const_spec((4, 64)), _const_spec((1, LANES))],
        out_specs=pl.BlockSpec((TQ, w), lambda b, g, i: (b * nq + i, g)),
        out_shape=jax.ShapeDtypeStruct((n, MIX), bf16),
        scratch_shapes=[pltpu.VMEM((GA, 2 * TQ, LANES), bf16),
                        pltpu.VMEM((GA, seq // TK, 2 * TQ, TK), f32),
                        pltpu.VMEM((GA, 2 * TQ, LANES), f32), pltpu.VMEM((GA, 2 * TQ, LANES), f32),
                        pltpu.VMEM((GA, 2 * TQ, LANES), f32)],
        compiler_params=_cparams(("arbitrary", "arbitrary", "arbitrary")),
        name="attn_diff",
    )(qa, ka, va, t0, t1, dl, og)


def _attn_c_kernel(q_ref, k_ref, v_ref, mk_ref, o_ref, sbuf, mbuf, lbuf, abuf):
    qi = pl.program_id(2)
    lane = lax.broadcasted_iota(jnp.int32, (TQ, LANES), 1)
    cols = lambda u: slice(u * LANES, (u + 1) * LANES)
    _two_pass_attention(GC, lambda u: q_ref[:, cols(u)], cols, lambda u: cols(u // 2), k_ref, v_ref, qi,
                        lambda u: mk_ref[...], None, sbuf, mbuf, lbuf, abuf)
    for hp in range(GC // 2):
        o_ref[:, cols(hp)] = jnp.where(lane < VC, _normalised(2 * hp, lbuf, abuf),
                                       _normalised(2 * hp + 1, lbuf, abuf)).astype(bf16)


def _attn_c(qc, kc, vc, mk, nb, seq):
    n = qc.shape[0]
    nq = seq // TQ
    w = GC * LANES
    wv = GC * VC
    return pl.pallas_call(
        _attn_c_kernel,
        grid=(nb, HC // GC, nq),
        in_specs=[pl.BlockSpec((TQ, w), lambda b, g, i: (b * nq + i, g)),
                  pl.BlockSpec((seq, w), lambda b, g, i: (b, g)),
                  pl.BlockSpec((seq, wv), lambda b, g, i: (b, g)),
                  _const_spec((TQ, TK))],
        out_specs=pl.BlockSpec((TQ, wv), lambda b, g, i: (b * nq + i, g)),
        out_shape=jax.ShapeDtypeStruct((n, MIX), bf16),
        scratch_shapes=[pltpu.VMEM((GC, seq // TK, TQ, TK), f32),
                        pltpu.VMEM((GC, TQ, LANES), f32), pltpu.VMEM((GC, TQ, LANES), f32),
                        pltpu.VMEM((GC, TQ, LANES), f32)],
        compiler_params=_cparams(("arbitrary", "arbitrary", "arbitrary")),
        name="attn_latent",
    )(qc, kc, vc, mk)


def _rope_apply(x, cosf, sinf, lane):
    w = x.shape[-1]
    partner = jnp.where((lane % LANES) < NOPE + ROPE // 2,
                        pltpu.roll(x, w - ROPE // 2, 1), pltpu.roll(x, ROPE // 2, 1))
    return x * cosf + partner * sinf


def _mla_prep_kernel(xc_ref, cos_ref, sin_ref, glq_ref, glkv_ref, wq_ref, wkv_ref, gq_ref, gk_ref, gr_ref,
                     sd_ref, ex_ref, q_ref, k_ref, v_ref):
    cq = _rms(xc_ref[:, 0:Q_LORA], glq_ref[...]).astype(bf16)
    ckv = _rms(xc_ref[:, Q_LORA:Q_LORA + KV_LORA], glkv_ref[...]).astype(bf16)
    cos8 = jnp.concatenate([cos_ref[...]] * HC, axis=-1)
    sin8 = jnp.concatenate([sin_ref[...]] * HC, axis=-1)
    lane8 = lax.broadcasted_iota(jnp.int32, (TM, HC * LANES), 1)

    q = jnp.dot(cq, wq_ref[...], preferred_element_type=f32)
    q = q * _seg_rinv(q, sd_ref, ex_ref) * gq_ref[...]
    q_ref[...] = _rope_apply(q, cos8, sin8, lane8).astype(bf16)

    kv = jnp.dot(ckv, wkv_ref[...], preferred_element_type=f32)
    v_ref[...] = kv[:, HC * LANES:].astype(bf16)
    xr = xc_ref[:, Q_LORA + KV_LORA:Q_LORA + KV_LORA + LANES]
    kr = xr * lax.rsqrt(jnp.sum(xr * xr, axis=-1, keepdims=True) * (1.0 / ROPE) + EPS) * gr_ref[...]
    kr = pltpu.roll(kr, NOPE, 1)
    lane1 = lax.broadcasted_iota(jnp.int32, (TM, LANES), 1)
    kr = _rope_apply(kr, cos_ref[...], sin_ref[...], lane1)
    for h in range(HC):
        cols = slice(h * LANES, (h + 1) * LANES)
        kn = kv[:, cols]
        kn = kn * lax.rsqrt(jnp.sum(kn * kn, axis=-1, keepdims=True) * (1.0 / NOPE) + EPS) * gk_ref[...]
        k_ref[:, cols] = (kn + kr).astype(bf16)


def _mla_prep(xc, cosf, sinf, glq, glkv, wq, wkv, gq, gk, gr, sd, ex):
    n = xc.shape[0]
    row = lambda w_: pl.BlockSpec((TM, w_), lambda i: (i, 0))
    return pl.pallas_call(
        _mla_prep_kernel,
        grid=(n // TM,),
        in_specs=[row(MIX), row(LANES), row(LANES), _const_spec((1, Q_LORA)), _const_spec((1, KV_LORA)),
                  _const_spec((Q_LORA, HC * LANES)), _const_spec((KV_LORA, HC * LANES + MIX)),
                  _const_spec((1, HC * LANES)), _const_spec((1, LANES)), _const_spec((1, LANES)),
                  _const_spec((HC * LANES, LANES)), _const_spec((2 * LANES, HC * LANES))],
        out_specs=[row(HC * LANES), row(HC * LANES), row(MIX)],
        out_shape=[jax.ShapeDtypeStruct((n, HC * LANES), bf16), jax.ShapeDtypeStruct((n, HC * LANES), bf16),
                   jax.ShapeDtypeStruct((n, MIX), bf16)],
        compiler_params=_cparams(("arbitrary",)),
        name="mla_prep",
    )(xc, cosf, sinf, glq, glkv, wq, wkv, gq, gk, gr, sd, ex)


def _merge_kernel(oa_ref, ob_ref, oc_ref, gt_ref, x_ref, g1_ref, sh_ref, sc_ref, ng_ref, wb_ref, wo_ref,
                  wr_ref, br_ref, x1_ref, h2_ref, meta_ref, cnt_ref, run_ref):
    i = pl.program_id(0)

    @pl.when(i == 0)
    def _():
        run_ref[...] = jnp.zeros_like(run_ref)

    merged = jnp.zeros((TM, D), f32)
    for j, o_ref in enumerate((oa_ref, ob_ref, oc_ref)):
        pj = jnp.dot(o_ref[...], wb_ref[j], preferred_element_type=f32)
        merged = merged + gt_ref[:, j * D:(j + 1) * D].astype(f32) * pj
    y = jnp.dot(merged.astype(bf16), wo_ref[...], preferred_element_type=f32)
    x1 = x_ref[...] + g1_ref[0] * y
    x1_ref[...] = x1
    h2 = _rms(x1, ng_ref[...]) * (1.0 + sc_ref[0]) + sh_ref[0]
    hr = h2.astype(bf16).astype(f32)
    lo = lax.shift_right_logical(lax.bitcast_convert_type(hr[:, 0:D // 2], jnp.uint32), jnp.uint32(16))
    hi = lax.bitcast_convert_type(hr[:, D // 2:D], jnp.uint32) & jnp.uint32(0xFFFF0000)
    words = lax.bitcast_convert_type(lo | hi, f32)
    for s in range(PACK_SLABS):
        h2_ref[pl.ds(s, TM, stride=PACK_SLABS), :] = words[:, s * LANES:(s + 1) * LANES]

    h_hi = h2.astype(bf16)
    h_lo = (h2 - h_hi.astype(f32)).astype(bf16)
    t = jnp.dot(h_hi, wr_ref[...], preferred_element_type=f32)
    lg = (t[:, :LANES] + t[:, LANES:] + jnp.dot(h_lo, wr_ref[:, 0:LANES], preferred_element_type=f32)
          + br_ref[...])
    lane = lax.broadcasted_iota(jnp.int32, (TM, LANES), 1)
    n_e = N_GROUPS * N_EXP
    gmask = (lane >= n_e) & (lane < n_e + N_GROUPS)
    gl = jnp.where(gmask, lg, NEG)
    gmax = jnp.max(gl, axis=-1, keepdims=True)
    g_w = 1.0 / jnp.sum(jnp.where(gmask, jnp.exp(gl - gmax), 0.0), axis=-1, keepdims=True)
    gidx = jnp.min(jnp.where(gl == gmax, lane - n_e, LANES), axis=-1, keepdims=True)
    emask = (lane < n_e) & ((lane // N_EXP) == gidx)
    el = jnp.where(emask, lg, NEG)
    m1 = jnp.max(el, axis=-1, keepdims=True)
    i1 = jnp.min(jnp.where(el == m1, lane, LANES), axis=-1, keepdims=True)
    el2 = jnp.where(lane == i1, NEG, el)
    m2 = jnp.max(el2, axis=-1, keepdims=True)
    i2 = jnp.min(jnp.where(el2 == m2, lane, LANES), axis=-1, keepdims=True)
    t = jnp.exp(m2 - m1)
    w1 = g_w / (1.0 + t)
    w2 = g_w * t / (1.0 + t)
    lo = jnp.minimum(i1, i2)
    hi = jnp.maximum(i1, i2)
    w_lo = jnp.where(i1 < i2, w1, w2)
    w_hi = jnp.where(i1 < i2, w2, w1)
    bucket = gidx * (N_EXP * N_EXP) + (lo % N_EXP) * N_EXP + (hi % N_EXP)

    lane2 = lax.broadcasted_iota(jnp.int32, (TM, 2 * LANES), 1)
    onehot = (lane2 == bucket).astype(f32)
    rr = lax.broadcasted_iota(jnp.int32, (TM, TM), 0)
    cc = lax.broadcasted_iota(jnp.int32, (TM, TM), 1)
    tri = (cc < rr).astype(bf16)
    before = jnp.dot(tri, onehot.astype(bf16), preferred_element_type=f32) + run_ref[...]
    rank = jnp.sum(onehot * before, axis=-1, keepdims=True)
    run_ref[...] = run_ref[...] + jnp.sum(onehot, axis=0, keepdims=True)
    cnt_ref[...] = run_ref[...]

    meta = (jnp.where(lane == i1, w1, 0.0) + jnp.where(lane == i2, w2, 0.0)
            + jnp.where(lane == 32, bucket.astype(f32), 0.0) + jnp.where(lane == 33, rank, 0.0)
            + jnp.where(lane == 34, w_lo, 0.0) + jnp.where(lane == 35, w_hi, 0.0))
    meta_ref[...] = meta


def _merge(oa, ob, oc, gt, x, g1, shift, scale, ng, wb, wo, wr, br, seq):
    n = x.shape[0]
    spb = seq // TM
    row = lambda w_: pl.BlockSpec((TM, w_), lambda i: (i, 0))
    modspec = pl.BlockSpec((1, 1, D), lambda i: (i // spb, 0, 0))
    return pl.pallas_call(
        _merge_kernel,
        grid=(n // TM,),
        in_specs=[row(MIX), row(MIX), row(MIX), row(3 * D), row(D), modspec, modspec, modspec,
                  _const_spec((1, D)), _const_spec((3, MIX, D)), _const_spec((D, D)),
                  _const_spec((D, 2 * LANES)), _const_spec((1, LANES))],
        out_specs=[row(D), pl.BlockSpec((TM * PACK_SLABS, LANES), lambda i: (i, 0)), row(LANES),
                   _const_spec((1, 2 * LANES))],
        out_shape=[jax.ShapeDtypeStruct((n, D), f32), jax.ShapeDtypeStruct((n * PACK_SLABS, LANES), f32),
                   jax.ShapeDtypeStruct((n, LANES), f32), jax.ShapeDtypeStruct((1, 2 * LANES), f32)],
        scratch_shapes=[pltpu.VMEM((1, 2 * LANES), f32)],
        compiler_params=_cparams(("arbitrary",)),
        name="merge_route",
    )(oa, ob, oc, gt, x, g1, shift, scale, ng, wb, wo, wr, br)


TE = 128
N_BUCKETS = N_GROUPS * N_EXP * N_EXP
N_PAIRS = N_GROUPS * (N_EXP * (N_EXP - 1) // 2)


def _num_tiles(n):
    return n // TE + N_PAIRS


TP = 2048


def _tables_kernel(meta_ref, cnt_ref, pos_ref, tt_ref):
    cnt = cnt_ref[...]
    ntile = jnp.floor((cnt + (TE - 1)) * (1.0 / TE))
    bi = lax.broadcasted_iota(jnp.int32, (N_BUCKETS, N_BUCKETS), 0)
    bj = lax.broadcasted_iota(jnp.int32, (N_BUCKETS, N_BUCKETS), 1)
    upper = (bi <= bj).astype(bf16)
    incl = jnp.dot(jnp.broadcast_to(ntile, (8, N_BUCKETS)).astype(bf16), upper, preferred_element_type=f32)[0:1]
    excl = incl - ntile

    meta = meta_ref[...]
    tp = meta.shape[0]
    lane = lax.broadcasted_iota(jnp.int32, (tp, LANES), 1)
    bucket = jnp.sum(jnp.where(lane == 32, meta, 0.0), axis=-1, keepdims=True).astype(jnp.int32)
    rank = jnp.sum(jnp.where(lane == 33, meta, 0.0), axis=-1, keepdims=True)
    lane2 = lax.broadcasted_iota(jnp.int32, (tp, N_BUCKETS), 1)
    first_tile = jnp.sum(jnp.where(lane2 == bucket, excl, 0.0), axis=-1, keepdims=True)
    pos_ref[...] = (first_tile * TE + rank).astype(jnp.int32)

    @pl.when(pl.program_id(0) == 0)
    def _():
        tau = lax.broadcasted_iota(jnp.int32, (N_BUCKETS, 1), 0).astype(f32)
        lane_b = lax.broadcasted_iota(jnp.int32, (1, N_BUCKETS), 1).astype(f32)
        tb = jnp.sum((incl <= tau).astype(f32), axis=-1, keepdims=True)
        last_b = jnp.max(jnp.where(cnt > 0, lane_b, 0.0), axis=-1, keepdims=True)
        tbi = jnp.minimum(tb, last_b).astype(jnp.int32)
        grp = lax.shift_right_logical(tbi, 6) * N_EXP
        ea = grp + (lax.shift_right_logical(tbi, 3) & 7)
        eb = grp + (tbi & 7)
        total = jnp.sum(ntile, axis=-1, keepdims=True).astype(jnp.int32)
        lane_t = lax.broadcasted_iota(jnp.int32, (N_BUCKETS, LANES), 1)
        tt_ref[...] = jnp.where(lane_t == 0, ea, jnp.where(lane_t == 1, eb, jnp.where(lane_t == 2, total, 0)))


def _tables(meta, cnt):
    n = meta.shape[0]
    tp = math.gcd(n, TP)
    assert _num_tiles(n) <= N_BUCKETS
    return pl.pallas_call(
        _tables_kernel,
        grid=(n // tp,),
        in_specs=[pl.BlockSpec((tp, LANES), lambda i: (i, 0)), _const_spec((1, N_BUCKETS))],
        out_specs=[pl.BlockSpec((tp, 1), lambda i: (i, 0)), _const_spec((N_BUCKETS, LANES))],
        out_shape=[jax.ShapeDtypeStruct((n, 1), jnp.int32), jax.ShapeDtypeStruct((N_BUCKETS, LANES), jnp.int32)],
        compiler_params=_cparams(("arbitrary",)),
        name="moe_tables",
    )(meta, cnt)


def _plan_kernel(pos_ref, tok_ref, *, n, nslot):
    def fill(s, _):
        tok_ref[s] = -1
        return 0

    lax.fori_loop(0, nslot, fill, 0, unroll=16)

    def place(t, _):
        tok_ref[pos_ref[t]] = t
        return 0

    lax.fori_loop(0, n, place, 0, unroll=16)


def _plan(pos):
    n = pos.shape[0]
    nslot = _num_tiles(n) * TE
    smem = pl.BlockSpec(memory_space=pltpu.SMEM)
    return pl.pallas_call(
        functools.partial(_plan_kernel, n=n, nslot=nslot),
        in_specs=[smem], out_specs=smem,
        out_shape=jax.ShapeDtypeStruct((nslot,), jnp.int32),
        name="moe_plan",
    )(pos)


def _moe_kernel(tea_ref, teb_ref, nt_ref, tok_ref, h_ref, m_ref, wg_ref, wu_ref, wd_ref, y_hbm,
                xs, ms, ob0, ob1, ssem, *, n):
    i = pl.program_id(0)
    nt = nt_ref[0]
    obufs = (ob0, ob1)

    def scatter_copy(dst_tok, sl, r):
        dst = y_hbm.at[pl.ds(pl.multiple_of(dst_tok * Y_PITCH, Y_PITCH), Y_PITCH), :]
        return pltpu.make_async_copy(obufs[sl].at[pl.ds(r * O_PITCH, Y_PITCH), :], dst, ssem.at[sl])

    def start_scatters(tile, sl, to_spare):
        for r in range(TE):
            t = tok_ref[tile * TE + r]
            scatter_copy(jnp.where((t < 0) | to_spare, n + sl * TE + r, t), sl, r).start(priority=r % 2)

    def wait_scatters(sl):
        for r in range(TE):
            scatter_copy(n, sl, r).wait()

    @pl.when(i == 0)
    def _():
        for sl in range(2):
            obufs[sl][...] = jnp.zeros_like(obufs[sl])
            spare = pltpu.make_async_copy(obufs[sl].at[pl.ds(0, TE * Y_PITCH), :],
                                          y_hbm.at[pl.ds((n + sl * TE) * Y_PITCH, TE * Y_PITCH), :], ssem.at[sl])
            spare.start()
            spare.wait()

    def step(s):
        o = 1 - s

        @pl.when(i >= 1)
        def _():
            wait_scatters(s)

        start_scatters(jnp.maximum(i - 1, 0), o, i == 0)

        for r in range(TE):
            t = jnp.maximum(tok_ref[i * TE + r], 0)
            xs[pl.ds(r * PACK_SLABS, PACK_SLABS), :] = h_ref[pl.ds(pl.multiple_of(t * PACK_SLABS, PACK_SLABS),
                                                                   PACK_SLABS), :]
            ms[pl.ds(r, 1), :] = m_ref[pl.ds(t, 1), :]
        slab = lambda k: xs[pl.ds(k, TE, stride=PACK_SLABS), :]
        words = lax.bitcast_convert_type(jnp.concatenate([slab(k) for k in range(PACK_SLABS)], axis=-1),
                                         jnp.uint32)
        lo = lax.bitcast_convert_type(lax.shift_left(words, jnp.uint32(16)), f32)
        hi = lax.bitcast_convert_type(words & jnp.uint32(0xFFFF0000), f32)
        hb = jnp.concatenate([lo, hi], axis=-1).astype(bf16)
        meta = ms[...]
        lane = lax.broadcasted_iota(jnp.int32, (TE, LANES), 1)
        w_lo = jnp.sum(jnp.where(lane == 34, meta, 0.0), axis=-1, keepdims=True)
        w_hi = jnp.sum(jnp.where(lane == 35, meta, 0.0), axis=-1, keepdims=True)

        def ffn(e, wcol):
            a = jnp.dot(hb, wg_ref[0, 0, e], preferred_element_type=f32)
            u = jnp.dot(hb, wu_ref[0, 0, e], preferred_element_type=f32)
            act = (a * jax.nn.sigmoid(a)) * u * wcol
            return jnp.dot(act.astype(bf16), wd_ref[0, 0, e], preferred_element_type=f32)

        out = ffn(lax.rem(tea_ref[i], N_EXP), w_lo) + ffn(lax.rem(teb_ref[i], N_EXP), w_hi)
        for k in range(SLABS):
            obufs[s][pl.ds(k, TE, stride=O_PITCH), :] = out[:, k * LANES:(k + 1) * LANES]

        @pl.when(i == nt - 1)
        def _():
            start_scatters(i, s, False)
            wait_scatters(o)
            wait_scatters(s)

    for s in range(2):
        @pl.when((i < nt) & (lax.rem(i, 2) == s))
        def _(s=s):
            step(s)


def _moe(h2ext, meta, tok, tea, teb, nt, wg, wu, wd, layer):
    n = h2ext.shape[0] // PACK_SLABS
    ntmax = _num_tiles(n)
    once = pl.Buffered(1)
    wspec_in = pl.BlockSpec((1, 1, N_EXP, D, F_EXP),
                            lambda i, tea, teb, nt, tok: (layer, tea[i] // N_EXP, 0, 0, 0), pipeline_mode=once)
    wspec_out = pl.BlockSpec((1, 1, N_EXP, F_EXP, D),
                             lambda i, tea, teb, nt, tok: (layer, tea[i] // N_EXP, 0, 0, 0), pipeline_mode=once)
    hspec = pl.BlockSpec((n * PACK_SLABS, LANES), lambda i, tea, teb, nt, tok: (0, 0), pipeline_mode=once)
    mspec = pl.BlockSpec((n, LANES), lambda i, tea, teb, nt, tok: (0, 0), pipeline_mode=once)
    return pl.pallas_call(
        functools.partial(_moe_kernel, n=n),
        grid_spec=pltpu.PrefetchScalarGridSpec(
            num_scalar_prefetch=4,
            grid=(ntmax,),
            in_specs=[hspec, mspec, wspec_in, wspec_in, wspec_out],
            out_specs=pl.BlockSpec(memory_space=pl.ANY),
            scratch_shapes=[pltpu.VMEM((TE * PACK_SLABS, LANES), f32), pltpu.VMEM((TE, LANES), f32)] + [
                pltpu.VMEM((TE * O_PITCH, LANES), f32)] * 2 + [pltpu.SemaphoreType.DMA((2,))]),
        out_shape=jax.ShapeDtypeStruct(((n + 2 * TE) * Y_PITCH, LANES), f32),
        compiler_params=_cparams(("arbitrary",), vmem=60 * 1024 * 1024),
        name="moe_sparse",
    )(tea, teb, nt, tok, h2ext, meta, wg, wu, wd)


def _final_kernel(x_ref, y_ref, g_ref, o_ref):
    o_ref[...] = x_ref[...] + g_ref[0] * _rows_from_token_major(y_ref, x_ref.shape[0])


def _final_combine(x1, y, g2, seq):
    n = x1.shape[0]
    t = 512
    spb = seq // t
    row = pl.BlockSpec((t, D), lambda i: (i, 0))
    return pl.pallas_call(
        _final_kernel,
        grid=(n // t,),
        in_specs=[row, pl.BlockSpec((t * Y_PITCH, LANES), lambda i: (i, 0)),
                  pl.BlockSpec((1, 1, D), lambda i: (i // spb, 0, 0))],
        out_specs=row,
        out_shape=jax.ShapeDtypeStruct((n, D), f32),
        compiler_params=_cparams(("arbitrary",)),
        name="final_combine",
    )(x1, y, g2)


def _seg_matrices(width, segs):
    sd = np.zeros((width, LANES), np.float32)
    ex = np.zeros((LANES, width), np.float32)
    for j, (s, ln) in enumerate(segs):
        sd[s:s + ln, j] = 1.0 / ln
        ex[j, s:s + ln] = 1.0
    return jnp.asarray(sd, bf16), jnp.asarray(np.concatenate([ex, ex], axis=0), bf16)


def _head_pad(w, heads, per_head, keep):
    k = w.shape[0]
    w3 = w.reshape(k, heads, per_head)[:, :, :keep]
    return jnp.pad(w3, ((0, 0), (0, 0), (0, LANES - keep))).reshape(k, heads * LANES)


def kernel(x, c, positions, w_ada, b_ada, norm_g, w_in, diff_qk_g, diff_lambda, diff_out_g, rel_bias, sgu_v_g, sgu_w, sgu_b, mla_lat_g, mla_w_uq, mla_w_ukv, mla_qk_g, w_branch, w_out, router_g_w, router_g_b, router_e_w, router_e_b, w_e_gate, w_e_up, w_e_down):
    nb, seq, _ = x.shape
    n = nb * seq
    assert seq % TQ == 0 and seq % TM == 0 and x.shape[2] == D

    mod = _ada(c, w_ada, b_ada)
    cosf, sinf = _rope_tables(positions)
    t0, t1, mk = _bias_tiles(rel_bias)

    sd_a, ex_a = _seg_matrices(MIX, [(s * 64, 64) for s in range(8)])
    segs_q = []
    for h in range(HC):
        segs_q += [(h * LANES, NOPE), (h * LANES + NOPE, ROPE)]
    sd_q, ex_q = _seg_matrices(HC * LANES, segs_q)
    ne = N_GROUPS * N_EXP
    zpad = lambda a, w_: jnp.pad(a, ((0, 0), (0, w_ - a.shape[1])))

    w_main = w_in[:, :, :MAIN_W].astype(bf16)
    w_gates = w_in[:, :, MAIN_W:].astype(bf16)
    wg_all, wu_all, wd_all = w_e_gate.astype(bf16), w_e_up.astype(bf16), w_e_down.astype(bf16)

    xcur = x.reshape(n, D)
    y_prev, g2_prev = None, None
    for l in range(DEPTH):
        m3 = mod[l].reshape(nb, 1, 6 * D)
        shift1, scale1, gate1, shift2, scale2, gate2 = [m3[:, :, k * D:(k + 1) * D] for k in range(6)]
        lambda_init = LAMBDA_INIT_BASE - LAMBDA_INIT_SCALE * math.exp(-LAMBDA_INIT_DECAY * l)

        gq = (jnp.tile(diff_qk_g[l, 0], 8) * (64 ** -0.5 * LOG2E)).reshape(1, MIX)
        gk = jnp.tile(diff_qk_g[l, 1], 8).reshape(1, MIX)
        outs = _inproj(xcur, y_prev, g2_prev, shift1, scale1, norm_g[l, 0].reshape(1, D), w_main, w_gates, l,
                       gq, gk, sd_a, ex_a, sgu_v_g[l].reshape(1, MIX), sgu_w[l], jnp.transpose(sgu_b[l]), seq)
        if y_prev is not None:
            xcur = outs[0]
            outs = outs[1:]
        qa, ka, va, ob, xc, gt = outs

        oa = _attn_a(qa, ka, va, t0, t1, diff_lambda[l], diff_out_g[l].reshape(1, LANES), nb, seq, lambda_init)

        qkg = mla_qk_g[l]
        wq = _head_pad(mla_w_uq[l], HC, NOPE + ROPE, NOPE + ROPE).astype(bf16)
        wkv3 = mla_w_ukv[l].reshape(KV_LORA, HC, NOPE + VC)
        wk = jnp.pad(wkv3[:, :, :NOPE], ((0, 0), (0, 0), (0, LANES - NOPE))).reshape(KV_LORA, HC * LANES)
        wv = wkv3[:, :, NOPE:].reshape(KV_LORA, HC * VC)
        wkv = jnp.concatenate([wk, wv], axis=1).astype(bf16)
        gq_c = jnp.tile(jnp.pad(qkg[0], (0, LANES - NOPE - ROPE)), HC).reshape(1, HC * LANES) * ((NOPE + ROPE) ** -0.5 * LOG2E)
        gk_c = jnp.pad(qkg[1, :NOPE], (0, LANES - NOPE)).reshape(1, LANES)
        gr_c = jnp.pad(qkg[1, NOPE:], (0, LANES - ROPE)).reshape(1, LANES)
        qc, kc, vc = _mla_prep(xc, cosf, sinf, mla_lat_g[l, :Q_LORA].reshape(1, Q_LORA),
                               mla_lat_g[l, Q_LORA:].reshape(1, KV_LORA), wq, wkv, gq_c, gk_c, gr_c, sd_q, ex_q)
        oc = _attn_c(qc, kc, vc, mk, nb, seq)

        wr = zpad(jnp.concatenate([router_e_w[l], router_g_w[l]], axis=1), LANES)
        wr_hi = wr.astype(bf16)
        wr = jnp.concatenate([wr_hi, (wr - wr_hi.astype(f32)).astype(bf16)], axis=1)
        br = zpad(jnp.concatenate([router_e_b[l], router_g_b[l]]).reshape(1, ne + N_GROUPS), LANES)
        x1, h2ext, meta, cnt = _merge(oa, ob, oc, gt, xcur, gate1, shift2, scale2, norm_g[l, 1].reshape(1, D),
                                      w_branch[l].astype(bf16), w_out[l].astype(bf16), wr, br, seq)

        pos, tt = _tables(meta, cnt)
        tok = _plan(pos.reshape(n))
        ntmax = _num_tiles(n)
        y_prev = _moe(h2ext, meta, tok, tt[:ntmax, 0], tt[:ntmax, 1], tt[0:1, 2], wg_all, wu_all, wd_all, l)
        g2_prev = gate2
        xcur = x1

    out = _final_combine(xcur, y_prev, g2_prev, seq)
    return out.reshape(nb, seq, D)
```

```python
import functools
import math

import jax
import jax.numpy as jnp
import numpy as np
from jax import lax
from jax.experimental import pallas as pl
from jax.experimental.pallas import tpu as pltpu

f32 = jnp.float32
bf16 = jnp.bfloat16

D = 1024
DEPTH = 4
CHUNK = 64
MIX = 512
HA = 4
HC = 8
Q_LORA, KV_LORA, ROPE = 256, 128, 32
NOPE, VC = 64, 64
N_GROUPS, N_EXP, F_EXP = 4, 8, 256
N_REL_BUCKETS = 32
ROPE_THETA = 10000.0
LAMBDA_INIT_BASE, LAMBDA_INIT_SCALE, LAMBDA_INIT_DECAY = 0.8, 0.6, 0.3
EPS = 1e-6
NEG = -1e30

XC_OFF = 2560
MAIN_W = XC_OFF + Q_LORA + KV_LORA + ROPE

TM = 512
TQ = 256
TK = 256
LANES = 128
VMEM_LIMIT = 56 * 1024 * 1024
SLABS = D // LANES
PACK_SLABS = SLABS // 2
Y_PITCH = SLABS
O_PITCH = Y_PITCH + 1


def _cparams(sem, vmem=VMEM_LIMIT):
    return pltpu.CompilerParams(dimension_semantics=sem, vmem_limit_bytes=vmem)


def _const_spec(shape):
    nd = len(shape)
    return pl.BlockSpec(shape, lambda *_: (0,) * nd)


def _rms(x, g_row):
    return x * lax.rsqrt(jnp.mean(x * x, axis=-1, keepdims=True) + EPS) * g_row


def _rows_from_token_major(y_ref, rows):
    return jnp.concatenate([y_ref[pl.ds(k, rows, stride=Y_PITCH), :] for k in range(SLABS)], axis=-1)


def _seg_rinv(x, sd_ref, ex_ref):
    ss = jnp.dot((x * x).astype(bf16), sd_ref[...], preferred_element_type=f32)
    hi = ss.astype(bf16)
    lo = (ss - hi.astype(f32)).astype(bf16)
    full = jnp.dot(jnp.concatenate([hi, lo], axis=-1), ex_ref[...], preferred_element_type=f32)
    return lax.rsqrt(full + EPS)


def _ada_kernel(c_ref, w_ref, b_ref, o_ref):
    c = c_ref[...]
    ca = (c * jax.nn.sigmoid(c)).astype(bf16)
    o_ref[0] = jnp.dot(ca, w_ref[0].astype(bf16), preferred_element_type=f32) + b_ref[0]


def _ada(c, w_ada, b_ada):
    nb = c.shape[0]
    tn = 1024
    return pl.pallas_call(
        _ada_kernel,
        grid=(DEPTH, 6 * D // tn),
        in_specs=[pl.BlockSpec((nb, D), lambda l, j: (0, 0)),
                  pl.BlockSpec((1, D, tn), lambda l, j: (l, 0, j)),
                  pl.BlockSpec((1, 1, tn), lambda l, j: (l, 0, j))],
        out_specs=pl.BlockSpec((1, nb, tn), lambda l, j: (l, 0, j)),
        out_shape=jax.ShapeDtypeStruct((DEPTH, nb, 6 * D), f32),
        compiler_params=_cparams(("arbitrary", "arbitrary")),
        name="ada_mod",
    )(c, w_ada, b_ada.reshape(DEPTH, 1, 6 * D))


def _rope_kernel(pos_ref, invf_ref, cos_ref, sin_ref):
    ang = pos_ref[...].astype(f32) * invf_ref[...]
    lane = lax.broadcasted_iota(jnp.int32, ang.shape, 1)
    rot = (lane >= NOPE) & (lane < NOPE + ROPE)
    first = lane < NOPE + ROPE // 2
    cos_ref[...] = jnp.where(rot, jnp.cos(ang), 1.0)
    s = jnp.sin(ang)
    sin_ref[...] = jnp.where(rot, jnp.where(first, -s, s), 0.0)


def _rope_tables(positions):
    n = positions.size
    inv_freq = ROPE_THETA ** (-jnp.arange(0, ROPE, 2, dtype=f32) / ROPE)
    invf = jnp.zeros((1, LANES), f32).at[0, NOPE:NOPE + ROPE].set(jnp.tile(inv_freq, 2))
    t = math.gcd(n, 2048)
    return pl.pallas_call(
        _rope_kernel,
        grid=(n // t,),
        in_specs=[pl.BlockSpec((t, 1), lambda i: (i, 0)), _const_spec((1, LANES))],
        out_specs=[pl.BlockSpec((t, LANES), lambda i: (i, 0))] * 2,
        out_shape=[jax.ShapeDtypeStruct((n, LANES), f32)] * 2,
        compiler_params=_cparams(("arbitrary",)),
        name="rope_tables",
    )(positions.reshape(n, 1), invf)


def _t5_bucket(rel):
    n = jnp.abs(rel)
    large = 8 + sum((n >= t).astype(jnp.int32) for t in (12, 16, 23, 32, 46, 64, 91))
    return jnp.where(rel > 0, 16, 0) + jnp.where(n < 8, n, large)


def _bias_kernel(tbl_ref, t0_ref, t1_ref, mk_ref):
    r = lax.broadcasted_iota(jnp.int32, (TQ, TK), 0)
    c = lax.broadcasted_iota(jnp.int32, (TQ, TK), 1)
    allowed = (c // CHUNK) <= (r // CHUNK)
    mk = jnp.where(allowed, 0.0, NEG).astype(f32)
    mk_ref[...] = mk
    b0 = _t5_bucket(c - r)
    b1 = _t5_bucket(c - r - TK)
    for h in range(HA):
        far = tbl_ref[N_REL_BUCKETS // 2 - 1, h]
        t0 = jnp.zeros((TQ, TK), f32)
        t1 = jnp.zeros((TQ, TK), f32)
        for b in range(N_REL_BUCKETS):
            v = (tbl_ref[b, h] - far) * LOG2E
            t0 = jnp.where(b0 == b, v, t0)
            t1 = jnp.where(b1 == b, v, t1)
        t0 = t0 + mk
        t0_ref[h, 0:TQ, :] = t0
        t0_ref[h, TQ:2 * TQ, :] = t0
        t1_ref[h, 0:TQ, :] = t1
        t1_ref[h, TQ:2 * TQ, :] = t1


def _bias_tiles(rel_bias):
    return pl.pallas_call(
        _bias_kernel,
        in_specs=[pl.BlockSpec(memory_space=pltpu.SMEM)],
        out_specs=[pl.BlockSpec(memory_space=pltpu.VMEM)] * 3,
        out_shape=[jax.ShapeDtypeStruct((HA, 2 * TQ, TK), f32),
                   jax.ShapeDtypeStruct((HA, 2 * TQ, TK), f32),
                   jax.ShapeDtypeStruct((TQ, TK), f32)],
        name="bias_tiles",
    )(rel_bias)


def _inproj_kernel(*refs, combine):
    if combine:
        (xa_ref, y_ref, g2_ref, sh_ref, sc_ref, ng_ref, w_ref, wgt_ref, gq_ref, gk_ref, sd_ref, ex_ref,
         vg_ref, ws_ref, bs_ref,
         xo_ref, qa_ref, ka_ref, va_ref, ob_ref, xc_ref, gt_ref) = refs
        x = xa_ref[...] + g2_ref[0] * _rows_from_token_major(y_ref, TM)
        xo_ref[...] = x
    else:
        (xa_ref, sh_ref, sc_ref, ng_ref, w_ref, wgt_ref, gq_ref, gk_ref, sd_ref, ex_ref,
         vg_ref, ws_ref, bs_ref,
         qa_ref, ka_ref, va_ref, ob_ref, xc_ref, gt_ref) = refs
        x = xa_ref[...]
    h = _rms(x, ng_ref[...]) * (1.0 + sc_ref[0]) + sh_ref[0]
    hb = h.astype(bf16)

    q = jnp.dot(hb, w_ref[0, :, 0:MIX], preferred_element_type=f32)
    qa_ref[...] = (q * _seg_rinv(q, sd_ref, ex_ref) * gq_ref[...]).astype(bf16)
    k = jnp.dot(hb, w_ref[0, :, MIX:2 * MIX], preferred_element_type=f32)
    ka_ref[...] = (k * _seg_rinv(k, sd_ref, ex_ref) * gk_ref[...]).astype(bf16)
    va_ref[...] = jnp.dot(hb, w_ref[0, :, 2 * MIX:3 * MIX], preferred_element_type=f32).astype(bf16)

    z = jnp.dot(hb, w_ref[0, :, 3 * MIX:5 * MIX], preferred_element_type=f32)
    z = 0.5 * z * (1.0 + jnp.tanh(math.sqrt(2.0 / math.pi) * (z + 0.044715 * (z * z * z))))
    u = z[:, :MIX]
    v = _rms(z[:, MIX:], vg_ref[...]).astype(bf16)
    ri = lax.broadcasted_iota(jnp.int32, (LANES, LANES), 0)
    ci = lax.broadcasted_iota(jnp.int32, (LANES, LANES), 1)
    allowed = (ci // CHUNK) <= (ri // CHUNK)
    for g in range(4):
        wm = jnp.where(allowed, ws_ref[g], 0.0).astype(bf16)
        bcol = bs_ref[:, g:g + 1]
        cols = slice(g * LANES, (g + 1) * LANES)
        nw = TM // LANES
        vcat = jnp.concatenate([v[wd * LANES:(wd + 1) * LANES, cols] for wd in range(nw)], axis=-1)
        vs = jnp.dot(wm, vcat, preferred_element_type=f32) + bcol
        for wd in range(nw):
            rows = slice(wd * LANES, (wd + 1) * LANES)
            ob_ref[rows, cols] = (u[rows, cols] * vs[:, wd * LANES:(wd + 1) * LANES]).astype(bf16)

    lat = Q_LORA + KV_LORA
    xc = jnp.dot(hb, w_ref[0, :, XC_OFF:MAIN_W], preferred_element_type=f32)
    xc_ref[:, 0:lat] = xc[:, 0:lat]
    xc_ref[:, lat:lat + LANES] = jnp.concatenate([xc[:, lat:lat + ROPE], jnp.zeros((TM, LANES - ROPE), f32)],
                                                 axis=-1)

    for j in range(3):
        gl = jnp.dot(hb, wgt_ref[0, :, j * D:(j + 1) * D], preferred_element_type=f32)
        gt_ref[:, j * D:(j + 1) * D] = jax.nn.sigmoid(gl).astype(bf16)


def _inproj(xa, y, g2, shift, scale, ng, w, wgt, layer, gq, gk, sd, ex, vg, ws, bs, seq):
    n = xa.shape[0]
    spb = seq // TM
    combine = y is not None
    row = lambda w_: pl.BlockSpec((TM, w_), lambda i: (i, 0))
    modspec = pl.BlockSpec((1, 1, D), lambda i: (i // spb, 0, 0))
    in_specs = [row(D)]
    args = [xa]
    if combine:
        in_specs += [pl.BlockSpec((TM * Y_PITCH, LANES), lambda i: (i, 0)), modspec]
        args += [y, g2]
    in_specs += [modspec, modspec, _const_spec((1, D)),
                 pl.BlockSpec((1, D, MAIN_W), lambda i: (layer, 0, 0), pipeline_mode=pl.Buffered(1)),
                 pl.BlockSpec((1, D, 3 * D), lambda i: (layer, 0, 0), pipeline_mode=pl.Buffered(1)),
                 _const_spec((1, MIX)), _const_spec((1, MIX)),
                 _const_spec((MIX, LANES)), _const_spec((2 * LANES, MIX)),
                 _const_spec((1, MIX)), _const_spec((4, LANES, LANES)), _const_spec((LANES, 4))]
    args += [shift, scale, ng, w, wgt, gq, gk, sd, ex, vg, ws, bs]
    out_specs = [row(MIX)] * 4 + [row(MIX), row(3 * D)]
    out_shape = [jax.ShapeDtypeStruct((n, MIX), bf16)] * 4 + [
        jax.ShapeDtypeStruct((n, MIX), f32), jax.ShapeDtypeStruct((n, 3 * D), bf16)]
    if combine:
        out_specs = [row(D)] + out_specs
        out_shape = [jax.ShapeDtypeStruct((n, D), f32)] + out_shape
    return pl.pallas_call(
        functools.partial(_inproj_kernel, combine=combine),
        grid=(n // TM,),
        in_specs=in_specs, out_specs=out_specs, out_shape=out_shape,
        compiler_params=_cparams(("arbitrary",)),
        name="inproj",
    )(*args)


LOG2E = math.log2(math.e)


def _two_pass_attention(nunits, q_of, kcols_of, vcols_of, k_ref, v_ref, qi, diag_bias_of, sub_bias_of,
                        sbuf, mbuf, lbuf, abuf):
    def keys(j, nb):
        return pl.ds(pl.multiple_of(j * TK, TK), nb * TK)

    def score_blocks(j, nb, bias_of):
        for u in range(nunits):
            s = lax.dot_general(q_of(u), k_ref[keys(j, nb), kcols_of(u)], (((1,), (1,)), ((), ())),
                                preferred_element_type=f32)
            if bias_of is not None:
                s = s + bias_of(u)
            m = mbuf[u]
            for b in range(nb):
                sbuf[u, j + b] = s[:, b * TK:(b + 1) * TK]
            for c in range(nb * TK // LANES):
                m = jnp.maximum(m, s[:, c * LANES:(c + 1) * LANES])
            mbuf[u] = m

    def blocks_in_pairs(first, count, fn):
        def pair(jj, _):
            fn(first + 2 * jj, 2)
            return 0

        lax.fori_loop(0, count // 2, pair, 0)

        @pl.when(count % 2 == 1)
        def _():
            fn(first + count - 1, 1)

    for u in range(nunits):
        mbuf[u] = jnp.full(mbuf.shape[1:], NEG, f32)
    if sub_bias_of is None:
        nfar = qi
    else:
        nfar = jnp.maximum(qi - 1, 0)

        @pl.when(qi >= 1)
        def _():
            score_blocks(qi - 1, 1, sub_bias_of)

    blocks_in_pairs(0, nfar, lambda j, nb: score_blocks(j, nb, None))
    score_blocks(qi, 1, diag_bias_of)

    for u in range(nunits):
        mbuf[u] = jnp.broadcast_to(jnp.max(mbuf[u], axis=-1, keepdims=True), mbuf.shape[1:])
        lbuf[u] = jnp.zeros(lbuf.shape[1:], f32)
        abuf[u] = jnp.zeros(abuf.shape[1:], f32)

    def accumulate_blocks(j, nb):
        for u in range(nunits):
            mb = mbuf[u]
            lsum = lbuf[u]
            ps = []
            for b in range(nb):
                s = sbuf[u, j + b]
                for c in range(TK // LANES):
                    p = jnp.exp2(s[:, c * LANES:(c + 1) * LANES] - mb)
                    lsum = lsum + p
                    ps.append(p)
            lbuf[u] = lsum
            p = jnp.concatenate(ps, axis=-1).astype(bf16)
            abuf[u] = abuf[u] + jnp.dot(p, v_ref[keys(j, nb), vcols_of(u)], preferred_element_type=f32)

    blocks_in_pairs(0, qi + 1, accumulate_blocks)


def _normalised(u, lbuf, abuf):
    return abuf[u] / jnp.sum(lbuf[u], axis=-1, keepdims=True)


GA = 2
GC = 4


def _attn_a_kernel(q_ref, k_ref, v_ref, t0_ref, t1_ref, dl_ref, og_ref, o_ref,
                   qstk, sbuf, mbuf, lbuf, abuf, *, lambda_init):
    qi = pl.program_id(2)
    dl = dl_ref[...]
    lam = (jnp.exp(jnp.sum(dl[0:1] * dl[1:2], axis=-1, keepdims=True))
           - jnp.exp(jnp.sum(dl[2:3] * dl[3:4], axis=-1, keepdims=True)) + lambda_init)
    lane = lax.broadcasted_iota(jnp.int32, (TQ, LANES), 1)
    cols = lambda u: slice(u * LANES, (u + 1) * LANES)
    for u in range(GA):
        qh = q_ref[:, cols(u)]
        zero = jnp.zeros_like(qh)
        qstk[u, 0:TQ, :] = jnp.where(lane < 64, qh, zero)
        qstk[u, TQ:2 * TQ, :] = jnp.where(lane >= 64, qh, zero)
    _two_pass_attention(GA, lambda u: qstk[u], cols, cols, k_ref, v_ref, qi,
                        lambda u: t0_ref[u], lambda u: t1_ref[u], sbuf, mbuf, lbuf, abuf)
    for u in range(GA):
        o = _normalised(u, lbuf, abuf)
        oh = o[:TQ] - lam * o[TQ:]
        oh = _rms(oh, og_ref[...]) * (1.0 - lambda_init)
        o_ref[:, cols(u)] = oh.astype(bf16)


def _attn_a(qa, ka, va, t0, t1, dl, og, nb, seq, lambda_init):
    n = qa.shape[0]
    nq = seq // TQ
    w = GA * LANES
    return pl.pallas_call(
        functools.partial(_attn_a_kernel, lambda_init=lambda_init),
        grid=(nb, HA // GA, nq),
        in_specs=[pl.BlockSpec((TQ, w), lambda b, g, i: (b * nq + i, g)),
                  pl.BlockSpec((seq, w), lambda b, g, i: (b, g)),
                  pl.BlockSpec((seq, w), lambda b, g, i: (b, g)),
                  pl.BlockSpec((GA, 2 * TQ, TK), lambda b, g, i: (g, 0, 0)),
                  pl.BlockSpec((GA, 2 * TQ, TK), lambda b, g, i: (g, 0, 0)),
                  _const_spec((4, 64)), _const_spec((1, LANES))],
        out_specs=pl.BlockSpec((TQ, w), lambda b, g, i: (b * nq + i, g)),
        out_shape=jax.ShapeDtypeStruct((n, MIX), bf16),
        scratch_shapes=[pltpu.VMEM((GA, 2 * TQ, LANES), bf16),
                        pltpu.VMEM((GA, seq // TK, 2 * TQ, TK), f32),
                        pltpu.VMEM((GA, 2 * TQ, LANES), f32), pltpu.VMEM((GA, 2 * TQ, LANES), f32),
                        pltpu.VMEM((GA, 2 * TQ, LANES), f32)],
        compiler_params=_cparams(("arbitrary", "arbitrary", "arbitrary")),
        name="attn_diff",
    )(qa, ka, va, t0, t1, dl, og)


def _attn_c_kernel(q_ref, k_ref, v_ref, mk_ref, o_ref, sbuf, mbuf, lbuf, abuf):
    qi = pl.program_id(2)
    lane = lax.broadcasted_iota(jnp.int32, (TQ, LANES), 1)
    cols = lambda u: slice(u * LANES, (u + 1) * LANES)
    _two_pass_attention(GC, lambda u: q_ref[:, cols(u)], cols, lambda u: cols(u // 2), k_ref, v_ref, qi,
                        lambda u: mk_ref[...], None, sbuf, mbuf, lbuf, abuf)
    for hp in range(GC // 2):
        o_ref[:, cols(hp)] = jnp.where(lane < VC, _normalised(2 * hp, lbuf, abuf),
                                       _normalised(2 * hp + 1, lbuf, abuf)).astype(bf16)


def _attn_c(qc, kc, vc, mk, nb, seq):
    n = qc.shape[0]
    nq = seq // TQ
    w = GC * LANES
    wv = GC * VC
    return pl.pallas_call(
        _attn_c_kernel,
        grid=(nb, HC // GC, nq),
        in_specs=[pl.BlockSpec((TQ, w), lambda b, g, i: (b * nq + i, g)),
                  pl.BlockSpec((seq, w), lambda b, g, i: (b, g)),
                  pl.BlockSpec((seq, wv), lambda b, g, i: (b, g)),
                  _const_spec((TQ, TK))],
        out_specs=pl.BlockSpec((TQ, wv), lambda b, g, i: (b * nq + i, g)),
        out_shape=jax.ShapeDtypeStruct((n, MIX), bf16),
        scratch_shapes=[pltpu.VMEM((GC, seq // TK, TQ, TK), f32),
                        pltpu.VMEM((GC, TQ, LANES), f32), pltpu.VMEM((GC, TQ, LANES), f32),
                        pltpu.VMEM((GC, TQ, LANES), f32)],
        compiler_params=_cparams(("arbitrary", "arbitrary", "arbitrary")),
        name="attn_latent",
    )(qc, kc, vc, mk)


def _rope_apply(x, cosf, sinf, lane):
    w = x.shape[-1]
    partner = jnp.where((lane % LANES) < NOPE + ROPE // 2,
                        pltpu.roll(x, w - ROPE // 2, 1), pltpu.roll(x, ROPE // 2, 1))
    return x * cosf + partner * sinf


def _mla_prep_kernel(xc_ref, cos_ref, sin_ref, glq_ref, glkv_ref, wq_ref, wkv_ref, gq_ref, gk_ref, gr_ref,
                     sd_ref, ex_ref, q_ref, k_ref, v_ref):
    cq = _rms(xc_ref[:, 0:Q_LORA], glq_ref[...]).astype(bf16)
    ckv = _rms(xc_ref[:, Q_LORA:Q_LORA + KV_LORA], glkv_ref[...]).astype(bf16)
    cos8 = jnp.concatenate([cos_ref[...]] * HC, axis=-1)
    sin8 = jnp.concatenate([sin_ref[...]] * HC, axis=-1)
    lane8 = lax.broadcasted_iota(jnp.int32, (TM, HC * LANES), 1)

    q = jnp.dot(cq, wq_ref[...], preferred_element_type=f32)
    q = q * _seg_rinv(q, sd_ref, ex_ref) * gq_ref[...]
    q_ref[...] = _rope_apply(q, cos8, sin8, lane8).astype(bf16)

    kv = jnp.dot(ckv, wkv_ref[...], preferred_element_type=f32)
    v_ref[...] = kv[:, HC * LANES:].astype(bf16)
    xr = xc_ref[:, Q_LORA + KV_LORA:Q_LORA + KV_LORA + LANES]
    kr = xr * lax.rsqrt(jnp.sum(xr * xr, axis=-1, keepdims=True) * (1.0 / ROPE) + EPS) * gr_ref[...]
    kr = pltpu.roll(kr, NOPE, 1)
    lane1 = lax.broadcasted_iota(jnp.int32, (TM, LANES), 1)
    kr = _rope_apply(kr, cos_ref[...], sin_ref[...], lane1)
    for h in range(HC):
        cols = slice(h * LANES, (h + 1) * LANES)
        kn = kv[:, cols]
        kn = kn * lax.rsqrt(jnp.sum(kn * kn, axis=-1, keepdims=True) * (1.0 / NOPE) + EPS) * gk_ref[...]
        k_ref[:, cols] = (kn + kr).astype(bf16)


def _mla_prep(xc, cosf, sinf, glq, glkv, wq, wkv, gq, gk, gr, sd, ex):
    n = xc.shape[0]
    row = lambda w_: pl.BlockSpec((TM, w_), lambda i: (i, 0))
    return pl.pallas_call(
        _mla_prep_kernel,
        grid=(n // TM,),
        in_specs=[row(MIX), row(LANES), row(LANES), _const_spec((1, Q_LORA)), _const_spec((1, KV_LORA)),
                  _const_spec((Q_LORA, HC * LANES)), _const_spec((KV_LORA, HC * LANES + MIX)),
                  _const_spec((1, HC * LANES)), _const_spec((1, LANES)), _const_spec((1, LANES)),
                  _const_spec((HC * LANES, LANES)), _const_spec((2 * LANES, HC * LANES))],
        out_specs=[row(HC * LANES), row(HC * LANES), row(MIX)],
        out_shape=[jax.ShapeDtypeStruct((n, HC * LANES), bf16), jax.ShapeDtypeStruct((n, HC * LANES), bf16),
                   jax.ShapeDtypeStruct((n, MIX), bf16)],
        compiler_params=_cparams(("arbitrary",)),
        name="mla_prep",
    )(xc, cosf, sinf, glq, glkv, wq, wkv, gq, gk, gr, sd, ex)


def _merge_kernel(oa_ref, ob_ref, oc_ref, gt_ref, x_ref, g1_ref, sh_ref, sc_ref, ng_ref, wb_ref, wo_ref,
                  wr_ref, br_ref, x1_ref, h2_ref, meta_ref, cnt_ref, run_ref):
    i = pl.program_id(0)

    @pl.when(i == 0)
    def _():
        run_ref[...] = jnp.zeros_like(run_ref)

    merged = jnp.zeros((TM, D), f32)
    for j, o_ref in enumerate((oa_ref, ob_ref, oc_ref)):
        pj = jnp.dot(o_ref[...], wb_ref[j], preferred_element_type=f32)
        merged = merged + gt_ref[:, j * D:(j + 1) * D].astype(f32) * pj
    y = jnp.dot(merged.astype(bf16), wo_ref[...], preferred_element_type=f32)
    x1 = x_ref[...] + g1_ref[0] * y
    x1_ref[...] = x1
    h2 = _rms(x1, ng_ref[...]) * (1.0 + sc_ref[0]) + sh_ref[0]
    hr = h2.astype(bf16).astype(f32)
    lo = lax.shift_right_logical(lax.bitcast_convert_type(hr[:, 0:D // 2], jnp.uint32), jnp.uint32(16))
    hi = lax.bitcast_convert_type(hr[:, D // 2:D], jnp.uint32) & jnp.uint32(0xFFFF0000)
    words = lax.bitcast_convert_type(lo | hi, f32)
    for s in range(PACK_SLABS):
        h2_ref[pl.ds(s, TM, stride=PACK_SLABS), :] = words[:, s * LANES:(s + 1) * LANES]

    h_hi = h2.astype(bf16)
    h_lo = (h2 - h_hi.astype(f32)).astype(bf16)
    t = jnp.dot(h_hi, wr_ref[...], preferred_element_type=f32)
    lg = (t[:, :LANES] + t[:, LANES:] + jnp.dot(h_lo, wr_ref[:, 0:LANES], preferred_element_type=f32)
          + br_ref[...])
    lane = lax.broadcasted_iota(jnp.int32, (TM, LANES), 1)
    n_e = N_GROUPS * N_EXP
    gmask = (lane >= n_e) & (lane < n_e + N_GROUPS)
    gl = jnp.where(gmask, lg, NEG)
    gmax = jnp.max(gl, axis=-1, keepdims=True)
    g_w = 1.0 / jnp.sum(jnp.where(gmask, jnp.exp(gl - gmax), 0.0), axis=-1, keepdims=True)
    gidx = jnp.min(jnp.where(gl == gmax, lane - n_e, LANES), axis=-1, keepdims=True)
    emask = (lane < n_e) & ((lane // N_EXP) == gidx)
    el = jnp.where(emask, lg, NEG)
    m1 = jnp.max(el, axis=-1, keepdims=True)
    i1 = jnp.min(jnp.where(el == m1, lane, LANES), axis=-1, keepdims=True)
    el2 = jnp.where(lane == i1, NEG, el)
    m2 = jnp.max(el2, axis=-1, keepdims=True)
    i2 = jnp.min(jnp.where(el2 == m2, lane, LANES), axis=-1, keepdims=True)
    t = jnp.exp(m2 - m1)
    w1 = g_w / (1.0 + t)
    w2 = g_w * t / (1.0 + t)
    lo = jnp.minimum(i1, i2)
    hi = jnp.maximum(i1, i2)
    w_lo = jnp.where(i1 < i2, w1, w2)
    w_hi = jnp.where(i1 < i2, w2, w1)
    bucket = gidx * (N_EXP * N_EXP) + (lo % N_EXP) * N_EXP + (hi % N_EXP)

    lane2 = lax.broadcasted_iota(jnp.int32, (TM, 2 * LANES), 1)
    onehot = (lane2 == bucket).astype(f32)
    rr = lax.broadcasted_iota(jnp.int32, (TM, TM), 0)
    cc = lax.broadcasted_iota(jnp.int32, (TM, TM), 1)
    tri = (cc < rr).astype(bf16)
    before = jnp.dot(tri, onehot.astype(bf16), preferred_element_type=f32) + run_ref[...]
    rank = jnp.sum(onehot * before, axis=-1, keepdims=True)
    run_ref[...] = run_ref[...] + jnp.sum(onehot, axis=0, keepdims=True)
    cnt_ref[...] = run_ref[...]

    meta = (jnp.where(lane == i1, w1, 0.0) + jnp.where(lane == i2, w2, 0.0)
            + jnp.where(lane == 32, bucket.astype(f32), 0.0) + jnp.where(lane == 33, rank, 0.0)
            + jnp.where(lane == 34, w_lo, 0.0) + jnp.where(lane == 35, w_hi, 0.0))
    meta_ref[...] = meta


def _merge(oa, ob, oc, gt, x, g1, shift, scale, ng, wb, wo, wr, br, seq):
    n = x.shape[0]
    spb = seq // TM
    row = lambda w_: pl.BlockSpec((TM, w_), lambda i: (i, 0))
    modspec = pl.BlockSpec((1, 1, D), lambda i: (i // spb, 0, 0))
    return pl.pallas_call(
        _merge_kernel,
        grid=(n // TM,),
        in_specs=[row(MIX), row(MIX), row(MIX), row(3 * D), row(D), modspec, modspec, modspec,
                  _const_spec((1, D)), _const_spec((3, MIX, D)), _const_spec((D, D)),
                  _const_spec((D, 2 * LANES)), _const_spec((1, LANES))],
        out_specs=[row(D), pl.BlockSpec((TM * PACK_SLABS, LANES), lambda i: (i, 0)), row(LANES),
                   _const_spec((1, 2 * LANES))],
        out_shape=[jax.ShapeDtypeStruct((n, D), f32), jax.ShapeDtypeStruct((n * PACK_SLABS, LANES), f32),
                   jax.ShapeDtypeStruct((n, LANES), f32), jax.ShapeDtypeStruct((1, 2 * LANES), f32)],
        scratch_shapes=[pltpu.VMEM((1, 2 * LANES), f32)],
        compiler_params=_cparams(("arbitrary",)),
        name="merge_route",
    )(oa, ob, oc, gt, x, g1, shift, scale, ng, wb, wo, wr, br)


TE = 128
N_BUCKETS = N_GROUPS * N_EXP * N_EXP
N_PAIRS = N_GROUPS * (N_EXP * (N_EXP - 1) // 2)


def _num_tiles(n):
    return n // TE + N_PAIRS


TP = 2048


def _tables_kernel(meta_ref, cnt_ref, pos_ref, tt_ref):
    cnt = cnt_ref[...]
    ntile = jnp.floor((cnt + (TE - 1)) * (1.0 / TE))
    bi = lax.broadcasted_iota(jnp.int32, (N_BUCKETS, N_BUCKETS), 0)
    bj = lax.broadcasted_iota(jnp.int32, (N_BUCKETS, N_BUCKETS), 1)
    upper = (bi <= bj).astype(bf16)
    incl = jnp.dot(jnp.broadcast_to(ntile, (8, N_BUCKETS)).astype(bf16), upper, preferred_element_type=f32)[0:1]
    excl = incl - ntile

    meta = meta_ref[...]
    tp = meta.shape[0]
    lane = lax.broadcasted_iota(jnp.int32, (tp, LANES), 1)
    bucket = jnp.sum(jnp.where(lane == 32, meta, 0.0), axis=-1, keepdims=True).astype(jnp.int32)
    rank = jnp.sum(jnp.where(lane == 33, meta, 0.0), axis=-1, keepdims=True)
    lane2 = lax.broadcasted_iota(jnp.int32, (tp, N_BUCKETS), 1)
    first_tile = jnp.sum(jnp.where(lane2 == bucket, excl, 0.0), axis=-1, keepdims=True)
    pos_ref[...] = (first_tile * TE + rank).astype(jnp.int32)

    @pl.when(pl.program_id(0) == 0)
    def _():
        tau = lax.broadcasted_iota(jnp.int32, (N_BUCKETS, 1), 0).astype(f32)
        lane_b = lax.broadcasted_iota(jnp.int32, (1, N_BUCKETS), 1).astype(f32)
        tb = jnp.sum((incl <= tau).astype(f32), axis=-1, keepdims=True)
        last_b = jnp.max(jnp.where(cnt > 0, lane_b, 0.0), axis=-1, keepdims=True)
        own = lane_b == tb
        in_bucket = jnp.sum(jnp.where(own, cnt, 0.0), axis=-1, keepdims=True)
        first = jnp.sum(jnp.where(own, excl, 0.0), axis=-1, keepdims=True)
        valid = jnp.clip(in_bucket - (tau - first) * TE, 0.0, float(TE)).astype(jnp.int32)
        tbi = jnp.minimum(tb, last_b).astype(jnp.int32)
        grp = lax.shift_right_logical(tbi, 6) * N_EXP
        ea = grp + (lax.shift_right_logical(tbi, 3) & 7)
        eb = grp + (tbi & 7)
        total = jnp.sum(ntile, axis=-1, keepdims=True).astype(jnp.int32)
        lane_t = lax.broadcasted_iota(jnp.int32, (N_BUCKETS, LANES), 1)
        tt_ref[...] = jnp.where(lane_t == 0, ea, jnp.where(lane_t == 1, eb, jnp.where(
            lane_t == 2, total, jnp.where(lane_t == 3, valid, 0))))


def _tables(meta, cnt):
    n = meta.shape[0]
    tp = math.gcd(n, TP)
    assert _num_tiles(n) <= N_BUCKETS
    return pl.pallas_call(
        _tables_kernel,
        grid=(n // tp,),
        in_specs=[pl.BlockSpec((tp, LANES), lambda i: (i, 0)), _const_spec((1, N_BUCKETS))],
        out_specs=[pl.BlockSpec((tp, 1), lambda i: (i, 0)), _const_spec((N_BUCKETS, LANES))],
        out_shape=[jax.ShapeDtypeStruct((n, 1), jnp.int32), jax.ShapeDtypeStruct((N_BUCKETS, LANES), jnp.int32)],
        compiler_params=_cparams(("arbitrary",)),
        name="moe_tables",
    )(meta, cnt)


def _plan_kernel(pos_ref, tok_ref, *, n, nslot):
    def fill(s, _):
        tok_ref[s] = -1
        return 0

    lax.fori_loop(0, nslot, fill, 0, unroll=16)

    def place(t, _):
        tok_ref[pos_ref[t]] = t
        return 0

    lax.fori_loop(0, n, place, 0, unroll=16)


def _plan(pos):
    n = pos.shape[0]
    nslot = _num_tiles(n) * TE
    smem = pl.BlockSpec(memory_space=pltpu.SMEM)
    return pl.pallas_call(
        functools.partial(_plan_kernel, n=n, nslot=nslot),
        in_specs=[smem], out_specs=smem,
        out_shape=jax.ShapeDtypeStruct((nslot,), jnp.int32),
        name="moe_plan",
    )(pos)


SCATTER_GROUP = 32


def _moe_kernel(tea_ref, teb_ref, nt_ref, tok_ref, tc_ref, h_ref, m_ref, wg_ref, wu_ref, wd_ref, y_hbm,
                xs, ms, ob0, ob1, ssem, *, n):
    i = pl.program_id(0)
    nt = nt_ref[0]
    obufs = (ob0, ob1)

    def scatter_copy(dst_tok, sl, r):
        dst = y_hbm.at[pl.ds(pl.multiple_of(dst_tok * Y_PITCH, Y_PITCH), Y_PITCH), :]
        return pltpu.make_async_copy(obufs[sl].at[pl.ds(r * O_PITCH, Y_PITCH), :], dst, ssem.at[sl])

    def real_slots(tile):
        return jnp.where(tile >= 0, tc_ref[jnp.maximum(tile, 0)], 0)

    def scatter_groups(tile, fn):
        cnt = real_slots(tile)
        for g in range(TE // SCATTER_GROUP):
            @pl.when(cnt > g * SCATTER_GROUP)
            def _(g=g):
                for r in range(g * SCATTER_GROUP, (g + 1) * SCATTER_GROUP):
                    fn(r)

    def start_scatters(tile, sl):
        base = jnp.maximum(tile, 0) * TE

        def start(r):
            t = tok_ref[base + r]
            scatter_copy(jnp.where(t < 0, n + sl * TE + r, t), sl, r).start(priority=r % 2)

        scatter_groups(tile, start)

    def wait_scatters(tile, sl):
        scatter_groups(tile, lambda r: scatter_copy(n, sl, r).wait())

    @pl.when(i == 0)
    def _():
        for sl in range(2):
            obufs[sl][...] = jnp.zeros_like(obufs[sl])
            spare = pltpu.make_async_copy(obufs[sl].at[pl.ds(0, TE * Y_PITCH), :],
                                          y_hbm.at[pl.ds((n + sl * TE) * Y_PITCH, TE * Y_PITCH), :], ssem.at[sl])
            spare.start()
            spare.wait()

    def step(s):
        o = 1 - s

        wait_scatters(i - 2, s)
        start_scatters(i - 1, o)

        for r in range(TE):
            t = jnp.maximum(tok_ref[i * TE + r], 0)
            xs[pl.ds(r * PACK_SLABS, PACK_SLABS), :] = h_ref[pl.ds(pl.multiple_of(t * PACK_SLABS, PACK_SLABS),
                                                                   PACK_SLABS), :]
            ms[pl.ds(r, 1), :] = m_ref[pl.ds(t, 1), :]
        slab = lambda k: xs[pl.ds(k, TE, stride=PACK_SLABS), :]
        words = lax.bitcast_convert_type(jnp.concatenate([slab(k) for k in range(PACK_SLABS)], axis=-1),
                                         jnp.uint32)
        lo = lax.bitcast_convert_type(lax.shift_left(words, jnp.uint32(16)), f32)
        hi = lax.bitcast_convert_type(words & jnp.uint32(0xFFFF0000), f32)
        hb = jnp.concatenate([lo, hi], axis=-1).astype(bf16)
        meta = ms[...]
        lane = lax.broadcasted_iota(jnp.int32, (TE, LANES), 1)
        w_lo = jnp.sum(jnp.where(lane == 34, meta, 0.0), axis=-1, keepdims=True)
        w_hi = jnp.sum(jnp.where(lane == 35, meta, 0.0), axis=-1, keepdims=True)

        def ffn(e, wcol):
            a = jnp.dot(hb, wg_ref[0, 0, e], preferred_element_type=f32)
            u = jnp.dot(hb, wu_ref[0, 0, e], preferred_element_type=f32)
            act = (a * jax.nn.sigmoid(a)) * u * wcol
            return jnp.dot(act.astype(bf16), wd_ref[0, 0, e], preferred_element_type=f32)

        out = ffn(lax.rem(tea_ref[i], N_EXP), w_lo) + ffn(lax.rem(teb_ref[i], N_EXP), w_hi)
        for k in range(SLABS):
            obufs[s][pl.ds(k, TE, stride=O_PITCH), :] = out[:, k * LANES:(k + 1) * LANES]

        @pl.when(i == nt - 1)
        def _():
            start_scatters(i, s)
            wait_scatters(i - 1, o)
            wait_scatters(i, s)

    for s in range(2):
        @pl.when((i < nt) & (lax.rem(i, 2) == s))
        def _(s=s):
            step(s)


def _moe(h2ext, meta, tok, tea, teb, nt, tc, wg, wu, wd, layer):
    n = h2ext.shape[0] // PACK_SLABS
    ntmax = _num_tiles(n)
    once = pl.Buffered(1)
    wspec_in = pl.BlockSpec((1, 1, N_EXP, D, F_EXP),
                            lambda i, tea, teb, nt, tok, tc: (layer, tea[i] // N_EXP, 0, 0, 0), pipeline_mode=once)
    wspec_out = pl.BlockSpec((1, 1, N_EXP, F_EXP, D),
                             lambda i, tea, teb, nt, tok, tc: (layer, tea[i] // N_EXP, 0, 0, 0), pipeline_mode=once)
    hspec = pl.BlockSpec((n * PACK_SLABS, LANES), lambda i, tea, teb, nt, tok, tc: (0, 0), pipeline_mode=once)
    mspec = pl.BlockSpec((n, LANES), lambda i, tea, teb, nt, tok, tc: (0, 0), pipeline_mode=once)
    return pl.pallas_call(
        functools.partial(_moe_kernel, n=n),
        grid_spec=pltpu.PrefetchScalarGridSpec(
            num_scalar_prefetch=5,
            grid=(ntmax,),
            in_specs=[hspec, mspec, wspec_in, wspec_in, wspec_out],
            out_specs=pl.BlockSpec(memory_space=pl.ANY),
            scratch_shapes=[pltpu.VMEM((TE * PACK_SLABS, LANES), f32), pltpu.VMEM((TE, LANES), f32)] + [
                pltpu.VMEM((TE * O_PITCH, LANES), f32)] * 2 + [pltpu.SemaphoreType.DMA((2,))]),
        out_shape=jax.ShapeDtypeStruct(((n + 2 * TE) * Y_PITCH, LANES), f32),
        compiler_params=_cparams(("arbitrary",), vmem=60 * 1024 * 1024),
        name="moe_sparse",
    )(tea, teb, nt, tok, tc, h2ext, meta, wg, wu, wd)


def _final_kernel(x_ref, y_ref, g_ref, o_ref):
    o_ref[...] = x_ref[...] + g_ref[0] * _rows_from_token_major(y_ref, x_ref.shape[0])


def _final_combine(x1, y, g2, seq):
    n = x1.shape[0]
    t = 512
    spb = seq // t
    row = pl.BlockSpec((t, D), lambda i: (i, 0))
    return pl.pallas_call(
        _final_kernel,
        grid=(n // t,),
        in_specs=[row, pl.BlockSpec((t * Y_PITCH, LANES), lambda i: (i, 0)),
                  pl.BlockSpec((1, 1, D), lambda i: (i // spb, 0, 0))],
        out_specs=row,
        out_shape=jax.ShapeDtypeStruct((n, D), f32),
        compiler_params=_cparams(("arbitrary",)),
        name="final_combine",
    )(x1, y, g2)


def _seg_matrices(width, segs):
    sd = np.zeros((width, LANES), np.float32)
    ex = np.zeros((LANES, width), np.float32)
    for j, (s, ln) in enumerate(segs):
        sd[s:s + ln, j] = 1.0 / ln
        ex[j, s:s + ln] = 1.0
    return jnp.asarray(sd, bf16), jnp.asarray(np.concatenate([ex, ex], axis=0), bf16)


def _head_pad(w, heads, per_head, keep):
    k = w.shape[0]
    w3 = w.reshape(k, heads, per_head)[:, :, :keep]
    return jnp.pad(w3, ((0, 0), (0, 0), (0, LANES - keep))).reshape(k, heads * LANES)


def kernel(x, c, positions, w_ada, b_ada, norm_g, w_in, diff_qk_g, diff_lambda, diff_out_g, rel_bias, sgu_v_g, sgu_w, sgu_b, mla_lat_g, mla_w_uq, mla_w_ukv, mla_qk_g, w_branch, w_out, router_g_w, router_g_b, router_e_w, router_e_b, w_e_gate, w_e_up, w_e_down):
    nb, seq, _ = x.shape
    n = nb * seq
    assert seq % TQ == 0 and seq % TM == 0 and x.shape[2] == D

    mod = _ada(c, w_ada, b_ada)
    cosf, sinf = _rope_tables(positions)
    t0, t1, mk = _bias_tiles(rel_bias)

    sd_a, ex_a = _seg_matrices(MIX, [(s * 64, 64) for s in range(8)])
    segs_q = []
    for h in range(HC):
        segs_q += [(h * LANES, NOPE), (h * LANES + NOPE, ROPE)]
    sd_q, ex_q = _seg_matrices(HC * LANES, segs_q)
    ne = N_GROUPS * N_EXP
    zpad = lambda a, w_: jnp.pad(a, ((0, 0), (0, w_ - a.shape[1])))

    w_main = w_in[:, :, :MAIN_W].astype(bf16)
    w_gates = w_in[:, :, MAIN_W:].astype(bf16)
    wg_all, wu_all, wd_all = w_e_gate.astype(bf16), w_e_up.astype(bf16), w_e_down.astype(bf16)

    xcur = x.reshape(n, D)
    y_prev, g2_prev = None, None
    for l in range(DEPTH):
        m3 = mod[l].reshape(nb, 1, 6 * D)
        shift1, scale1, gate1, shift2, scale2, gate2 = [m3[:, :, k * D:(k + 1) * D] for k in range(6)]
        lambda_init = LAMBDA_INIT_BASE - LAMBDA_INIT_SCALE * math.exp(-LAMBDA_INIT_DECAY * l)

        gq = (jnp.tile(diff_qk_g[l, 0], 8) * (64 ** -0.5 * LOG2E)).reshape(1, MIX)
        gk = jnp.tile(diff_qk_g[l, 1], 8).reshape(1, MIX)
        outs = _inproj(xcur, y_prev, g2_prev, shift1, scale1, norm_g[l, 0].reshape(1, D), w_main, w_gates, l,
                       gq, gk, sd_a, ex_a, sgu_v_g[l].reshape(1, MIX), sgu_w[l], jnp.transpose(sgu_b[l]), seq)
        if y_prev is not None:
            xcur = outs[0]
            outs = outs[1:]
        qa, ka, va, ob, xc, gt = outs

        oa = _attn_a(qa, ka, va, t0, t1, diff_lambda[l], diff_out_g[l].reshape(1, LANES), nb, seq, lambda_init)

        qkg = mla_qk_g[l]
        wq = _head_pad(mla_w_uq[l], HC, NOPE + ROPE, NOPE + ROPE).astype(bf16)
        wkv3 = mla_w_ukv[l].reshape(KV_LORA, HC, NOPE + VC)
        wk = jnp.pad(wkv3[:, :, :NOPE], ((0, 0), (0, 0), (0, LANES - NOPE))).reshape(KV_LORA, HC * LANES)
        wv = wkv3[:, :, NOPE:].reshape(KV_LORA, HC * VC)
        wkv = jnp.concatenate([wk, wv], axis=1).astype(bf16)
        gq_c = jnp.tile(jnp.pad(qkg[0], (0, LANES - NOPE - ROPE)), HC).reshape(1, HC * LANES) * ((NOPE + ROPE) ** -0.5 * LOG2E)
        gk_c = jnp.pad(qkg[1, :NOPE], (0, LANES - NOPE)).reshape(1, LANES)
        gr_c = jnp.pad(qkg[1, NOPE:], (0, LANES - ROPE)).reshape(1, LANES)
        qc, kc, vc = _mla_prep(xc, cosf, sinf, mla_lat_g[l, :Q_LORA].reshape(1, Q_LORA),
                               mla_lat_g[l, Q_LORA:].reshape(1, KV_LORA), wq, wkv, gq_c, gk_c, gr_c, sd_q, ex_q)
        oc = _attn_c(qc, kc, vc, mk, nb, seq)

        wr = zpad(jnp.concatenate([router_e_w[l], router_g_w[l]], axis=1), LANES)
        wr_hi = wr.astype(bf16)
        wr = jnp.concatenate([wr_hi, (wr - wr_hi.astype(f32)).astype(bf16)], axis=1)
        br = zpad(jnp.concatenate([router_e_b[l], router_g_b[l]]).reshape(1, ne + N_GROUPS), LANES)
        x1, h2ext, meta, cnt = _merge(oa, ob, oc, gt, xcur, gate1, shift2, scale2, norm_g[l, 1].reshape(1, D),
                                      w_branch[l].astype(bf16), w_out[l].astype(bf16), wr, br, seq)

        pos, tt = _tables(meta, cnt)
        tok = _plan(pos.reshape(n))
        ntmax = _num_tiles(n)
        y_prev = _moe(h2ext, meta, tok, tt[:ntmax, 0], tt[:ntmax, 1], tt[0:1, 2], tt[:ntmax, 3],
                      wg_all, wu_all, wd_all, l)
        g2_prev = gate2
        xcur = x1

    out = _final_combine(xcur, y_prev, g2_prev, seq)
    return out.reshape(nb, seq, D)
```

```python
import functools
import math

import jax
import jax.numpy as jnp
import numpy as np
from jax import lax
from jax.experimental import pallas as pl
from jax.experimental.pallas import tpu as pltpu

f32 = jnp.float32
bf16 = jnp.bfloat16

D = 1024
DEPTH = 4
CHUNK = 64
MIX = 512
HA = 4
HC = 8
Q_LORA, KV_LORA, ROPE = 256, 128, 32
NOPE, VC = 64, 64
N_GROUPS, N_EXP, F_EXP = 4, 8, 256
N_REL_BUCKETS = 32
ROPE_THETA = 10000.0
LAMBDA_INIT_BASE, LAMBDA_INIT_SCALE, LAMBDA_INIT_DECAY = 0.8, 0.6, 0.3
EPS = 1e-6
NEG = -1e30

XC_OFF = 2560
MAIN_W = XC_OFF + Q_LORA + KV_LORA + ROPE

TM = 512
TQ = 256
TK = 256
LANES = 128
VMEM_LIMIT = 56 * 1024 * 1024
SLABS = D // LANES
PACK_SLABS = SLABS // 2
Y_PITCH = SLABS
O_PITCH = Y_PITCH + 1


def _cparams(sem, vmem=VMEM_LIMIT):
    return pltpu.CompilerParams(dimension_semantics=sem, vmem_limit_bytes=vmem)


def _const_spec(shape):
    nd = len(shape)
    return pl.BlockSpec(shape, lambda *_: (0,) * nd)


def _rms(x, g_row):
    return x * lax.rsqrt(jnp.mean(x * x, axis=-1, keepdims=True) + EPS) * g_row


def _rows_from_token_major(y_ref, rows):
    return jnp.concatenate([y_ref[pl.ds(k, rows, stride=Y_PITCH), :] for k in range(SLABS)], axis=-1)


def _seg_rinv(x, sd_ref, ex_ref):
    ss = jnp.dot((x * x).astype(bf16), sd_ref[...], preferred_element_type=f32)
    hi = ss.astype(bf16)
    lo = (ss - hi.astype(f32)).astype(bf16)
    full = jnp.dot(jnp.concatenate([hi, lo], axis=-1), ex_ref[...], preferred_element_type=f32)
    return lax.rsqrt(full + EPS)


def _ada_kernel(c_ref, w_ref, b_ref, o_ref):
    c = c_ref[...]
    ca = (c * jax.nn.sigmoid(c)).astype(bf16)
    o_ref[0] = jnp.dot(ca, w_ref[0].astype(bf16), preferred_element_type=f32) + b_ref[0]


def _ada(c, w_ada, b_ada):
    nb = c.shape[0]
    tn = 1024
    return pl.pallas_call(
        _ada_kernel,
        grid=(DEPTH, 6 * D // tn),
        in_specs=[pl.BlockSpec((nb, D), lambda l, j: (0, 0)),
                  pl.BlockSpec((1, D, tn), lambda l, j: (l, 0, j)),
                  pl.BlockSpec((1, 1, tn), lambda l, j: (l, 0, j))],
        out_specs=pl.BlockSpec((1, nb, tn), lambda l, j: (l, 0, j)),
        out_shape=jax.ShapeDtypeStruct((DEPTH, nb, 6 * D), f32),
        compiler_params=_cparams(("arbitrary", "arbitrary")),
        name="ada_mod",
    )(c, w_ada, b_ada.reshape(DEPTH, 1, 6 * D))


def _rope_kernel(pos_ref, invf_ref, cos_ref, sin_ref):
    ang = pos_ref[...].astype(f32) * invf_ref[...]
    lane = lax.broadcasted_iota(jnp.int32, ang.shape, 1)
    rot = (lane >= NOPE) & (lane < NOPE + ROPE)
    first = lane < NOPE + ROPE // 2
    cos_ref[...] = jnp.where(rot, jnp.cos(ang), 1.0)
    s = jnp.sin(ang)
    sin_ref[...] = jnp.where(rot, jnp.where(first, -s, s), 0.0)


def _rope_tables(positions):
    n = positions.size
    inv_freq = ROPE_THETA ** (-jnp.arange(0, ROPE, 2, dtype=f32) / ROPE)
    invf = jnp.zeros((1, LANES), f32).at[0, NOPE:NOPE + ROPE].set(jnp.tile(inv_freq, 2))
    t = math.gcd(n, 2048)
    return pl.pallas_call(
        _rope_kernel,
        grid=(n // t,),
        in_specs=[pl.BlockSpec((t, 1), lambda i: (i, 0)), _const_spec((1, LANES))],
        out_specs=[pl.BlockSpec((t, LANES), lambda i: (i, 0))] * 2,
        out_shape=[jax.ShapeDtypeStruct((n, LANES), f32)] * 2,
        compiler_params=_cparams(("arbitrary",)),
        name="rope_tables",
    )(positions.reshape(n, 1), invf)


def _t5_bucket(rel):
    n = jnp.abs(rel)
    large = 8 + sum((n >= t).astype(jnp.int32) for t in (12, 16, 23, 32, 46, 64, 91))
    return jnp.where(rel > 0, 16, 0) + jnp.where(n < 8, n, large)


def _bias_kernel(tbl_ref, t0_ref, t1_ref, mk_ref):
    r = lax.broadcasted_iota(jnp.int32, (TQ, TK), 0)
    c = lax.broadcasted_iota(jnp.int32, (TQ, TK), 1)
    allowed = (c // CHUNK) <= (r // CHUNK)
    mk = jnp.where(allowed, 0.0, NEG).astype(f32)
    mk_ref[...] = mk
    b0 = _t5_bucket(c - r)
    b1 = _t5_bucket(c - r - TK)
    for h in range(HA):
        far = tbl_ref[N_REL_BUCKETS // 2 - 1, h]
        t0 = jnp.zeros((TQ, TK), f32)
        t1 = jnp.zeros((TQ, TK), f32)
        for b in range(N_REL_BUCKETS):
            v = (tbl_ref[b, h] - far) * LOG2E
            t0 = jnp.where(b0 == b, v, t0)
            t1 = jnp.where(b1 == b, v, t1)
        t0 = t0 + mk
        t0_ref[h, 0:TQ, :] = t0
        t0_ref[h, TQ:2 * TQ, :] = t0
        t1_ref[h, 0:TQ, :] = t1
        t1_ref[h, TQ:2 * TQ, :] = t1


def _bias_tiles(rel_bias):
    return pl.pallas_call(
        _bias_kernel,
        in_specs=[pl.BlockSpec(memory_space=pltpu.SMEM)],
        out_specs=[pl.BlockSpec(memory_space=pltpu.VMEM)] * 3,
        out_shape=[jax.ShapeDtypeStruct((HA, 2 * TQ, TK), f32),
                   jax.ShapeDtypeStruct((HA, 2 * TQ, TK), f32),
                   jax.ShapeDtypeStruct((TQ, TK), f32)],
        name="bias_tiles",
    )(rel_bias)


def _inproj_kernel(*refs, combine):
    if combine:
        (xa_ref, y_ref, g2_ref, sh_ref, sc_ref, ng_ref, w_ref, wgt_ref, gq_ref, gk_ref, sd_ref, ex_ref,
         vg_ref, ws_ref, bs_ref,
         xo_ref, qa_ref, ka_ref, va_ref, ob_ref, xc_ref, gt_ref) = refs
        x = xa_ref[...] + g2_ref[0] * _rows_from_token_major(y_ref, TM)
        xo_ref[...] = x
    else:
        (xa_ref, sh_ref, sc_ref, ng_ref, w_ref, wgt_ref, gq_ref, gk_ref, sd_ref, ex_ref,
         vg_ref, ws_ref, bs_ref,
         qa_ref, ka_ref, va_ref, ob_ref, xc_ref, gt_ref) = refs
        x = xa_ref[...]
    h = _rms(x, ng_ref[...]) * (1.0 + sc_ref[0]) + sh_ref[0]
    hb = h.astype(bf16)

    q = jnp.dot(hb, w_ref[0, :, 0:MIX], preferred_element_type=f32)
    qa_ref[...] = (q * _seg_rinv(q, sd_ref, ex_ref) * gq_ref[...]).astype(bf16)
    k = jnp.dot(hb, w_ref[0, :, MIX:2 * MIX], preferred_element_type=f32)
    ka_ref[...] = (k * _seg_rinv(k, sd_ref, ex_ref) * gk_ref[...]).astype(bf16)
    va_ref[...] = jnp.dot(hb, w_ref[0, :, 2 * MIX:3 * MIX], preferred_element_type=f32).astype(bf16)

    z = jnp.dot(hb, w_ref[0, :, 3 * MIX:5 * MIX], preferred_element_type=f32)
    z = 0.5 * z * (1.0 + jnp.tanh(math.sqrt(2.0 / math.pi) * (z + 0.044715 * (z * z * z))))
    u = z[:, :MIX]
    v = _rms(z[:, MIX:], vg_ref[...]).astype(bf16)
    ri = lax.broadcasted_iota(jnp.int32, (LANES, LANES), 0)
    ci = lax.broadcasted_iota(jnp.int32, (LANES, LANES), 1)
    allowed = (ci // CHUNK) <= (ri // CHUNK)
    for g in range(4):
        wm = jnp.where(allowed, ws_ref[g], 0.0).astype(bf16)
        bcol = bs_ref[:, g:g + 1]
        cols = slice(g * LANES, (g + 1) * LANES)
        nw = TM // LANES
        vcat = jnp.concatenate([v[wd * LANES:(wd + 1) * LANES, cols] for wd in range(nw)], axis=-1)
        vs = jnp.dot(wm, vcat, preferred_element_type=f32) + bcol
        for wd in range(nw):
            rows = slice(wd * LANES, (wd + 1) * LANES)
            ob_ref[rows, cols] = (u[rows, cols] * vs[:, wd * LANES:(wd + 1) * LANES]).astype(bf16)

    lat = Q_LORA + KV_LORA
    xc = jnp.dot(hb, w_ref[0, :, XC_OFF:MAIN_W], preferred_element_type=f32)
    xc_ref[:, 0:lat] = xc[:, 0:lat]
    xc_ref[:, lat:lat + LANES] = jnp.concatenate([xc[:, lat:lat + ROPE], jnp.zeros((TM, LANES - ROPE), f32)],
                                                 axis=-1)

    for j in range(3):
        gl = jnp.dot(hb, wgt_ref[0, :, j * D:(j + 1) * D], preferred_element_type=f32)
        gt_ref[:, j * D:(j + 1) * D] = jax.nn.sigmoid(gl).astype(bf16)


def _inproj(xa, y, g2, shift, scale, ng, w, wgt, layer, gq, gk, sd, ex, vg, ws, bs, seq):
    n = xa.shape[0]
    spb = seq // TM
    combine = y is not None
    row = lambda w_: pl.BlockSpec((TM, w_), lambda i: (i, 0))
    modspec = pl.BlockSpec((1, 1, D), lambda i: (i // spb, 0, 0))
    in_specs = [row(D)]
    args = [xa]
    if combine:
        in_specs += [pl.BlockSpec((TM * Y_PITCH, LANES), lambda i: (i, 0)), modspec]
        args += [y, g2]
    in_specs += [modspec, modspec, _const_spec((1, D)),
                 pl.BlockSpec((1, D, MAIN_W), lambda i: (layer, 0, 0), pipeline_mode=pl.Buffered(1)),
                 pl.BlockSpec((1, D, 3 * D), lambda i: (layer, 0, 0), pipeline_mode=pl.Buffered(1)),
                 _const_spec((1, MIX)), _const_spec((1, MIX)),
                 _const_spec((MIX, LANES)), _const_spec((2 * LANES, MIX)),
                 _const_spec((1, MIX)), _const_spec((4, LANES, LANES)), _const_spec((LANES, 4))]
    args += [shift, scale, ng, w, wgt, gq, gk, sd, ex, vg, ws, bs]
    out_specs = [row(MIX)] * 4 + [row(MIX), row(3 * D)]
    out_shape = [jax.ShapeDtypeStruct((n, MIX), bf16)] * 4 + [
        jax.ShapeDtypeStruct((n, MIX), f32), jax.ShapeDtypeStruct((n, 3 * D), bf16)]
    if combine:
        out_specs = [row(D)] + out_specs
        out_shape = [jax.ShapeDtypeStruct((n, D), f32)] + out_shape
    return pl.pallas_call(
        functools.partial(_inproj_kernel, combine=combine),
        grid=(n // TM,),
        in_specs=in_specs, out_specs=out_specs, out_shape=out_shape,
        compiler_params=_cparams(("arbitrary",)),
        name="inproj",
    )(*args)


LOG2E = math.log2(math.e)


def _two_pass_attention(nunits, q_of, kcols_of, vcols_of, k_ref, v_ref, qi, diag_bias_of, sub_bias_of,
                        sbuf, mbuf, lbuf, abuf):
    def keys(j, nb):
        return pl.ds(pl.multiple_of(j * TK, TK), nb * TK)

    def score_blocks(j, nb, bias_of):
        for u in range(nunits):
            s = lax.dot_general(q_of(u), k_ref[keys(j, nb), kcols_of(u)], (((1,), (1,)), ((), ())),
                                preferred_element_type=f32)
            if bias_of is not None:
                s = s + bias_of(u)
            m = mbuf[u]
            for b in range(nb):
                sbuf[u, j + b] = s[:, b * TK:(b + 1) * TK]
            for c in range(nb * TK // LANES):
                m = jnp.maximum(m, s[:, c * LANES:(c + 1) * LANES])
            mbuf[u] = m

    def blocks_in_pairs(first, count, fn):
        def pair(jj, _):
            fn(first + 2 * jj, 2)
            return 0

        lax.fori_loop(0, count // 2, pair, 0)

        @pl.when(count % 2 == 1)
        def _():
            fn(first + count - 1, 1)

    for u in range(nunits):
        mbuf[u] = jnp.full(mbuf.shape[1:], NEG, f32)
    if sub_bias_of is None:
        nfar = qi
    else:
        nfar = jnp.maximum(qi - 1, 0)

        @pl.when(qi >= 1)
        def _():
            score_blocks(qi - 1, 1, sub_bias_of)

    blocks_in_pairs(0, nfar, lambda j, nb: score_blocks(j, nb, None))
    score_blocks(qi, 1, diag_bias_of)

    for u in range(nunits):
        mbuf[u] = jnp.broadcast_to(jnp.max(mbuf[u], axis=-1, keepdims=True), mbuf.shape[1:])
        lbuf[u] = jnp.zeros(lbuf.shape[1:], f32)
        abuf[u] = jnp.zeros(abuf.shape[1:], f32)

    def accumulate_blocks(j, nb):
        for u in range(nunits):
            mb = mbuf[u]
            lsum = lbuf[u]
            ps = []
            for b in range(nb):
                s = sbuf[u, j + b]
                for c in range(TK // LANES):
                    p = jnp.exp2(s[:, c * LANES:(c + 1) * LANES] - mb)
                    lsum = lsum + p
                    ps.append(p)
            lbuf[u] = lsum
            p = jnp.concatenate(ps, axis=-1).astype(bf16)
            abuf[u] = abuf[u] + jnp.dot(p, v_ref[keys(j, nb), vcols_of(u)], preferred_element_type=f32)

    blocks_in_pairs(0, qi + 1, accumulate_blocks)


def _normalised(u, lbuf, abuf):
    return abuf[u] / jnp.sum(lbuf[u], axis=-1, keepdims=True)


GA = 2
GC = 4


def _attn_a_kernel(q_ref, k_ref, v_ref, t0_ref, t1_ref, dl_ref, og_ref, o_ref,
                   qstk, sbuf, mbuf, lbuf, abuf, *, lambda_init):
    qi = pl.program_id(2)
    dl = dl_ref[...]
    lam = (jnp.exp(jnp.sum(dl[0:1] * dl[1:2], axis=-1, keepdims=True))
           - jnp.exp(jnp.sum(dl[2:3] * dl[3:4], axis=-1, keepdims=True)) + lambda_init)
    lane = lax.broadcasted_iota(jnp.int32, (TQ, LANES), 1)
    cols = lambda u: slice(u * LANES, (u + 1) * LANES)
    for u in range(GA):
        qh = q_ref[:, cols(u)]
        zero = jnp.zeros_like(qh)
        qstk[u, 0:TQ, :] = jnp.where(lane < 64, qh, zero)
        qstk[u, TQ:2 * TQ, :] = jnp.where(lane >= 64, qh, zero)
    _two_pass_attention(GA, lambda u: qstk[u], cols, cols, k_ref, v_ref, qi,
                        lambda u: t0_ref[u], lambda u: t1_ref[u], sbuf, mbuf, lbuf, abuf)
    for u in range(GA):
        o = _normalised(u, lbuf, abuf)
        oh = o[:TQ] - lam * o[TQ:]
        oh = _rms(oh, og_ref[...]) * (1.0 - lambda_init)
        o_ref[:, cols(u)] = oh.astype(bf16)


def _attn_a(qa, ka, va, t0, t1, dl, og, nb, seq, lambda_init):
    n = qa.shape[0]
    nq = seq // TQ
    w = GA * LANES
    return pl.pallas_call(
        functools.partial(_attn_a_kernel, lambda_init=lambda_init),
        grid=(nb, HA // GA, nq),
        in_specs=[pl.BlockSpec((TQ, w), lambda b, g, i: (b * nq + i, g)),
                  pl.BlockSpec((seq, w), lambda b, g, i: (b, g)),
                  pl.BlockSpec((seq, w), lambda b, g, i: (b, g)),
                  pl.BlockSpec((GA, 2 * TQ, TK), lambda b, g, i: (g, 0, 0)),
                  pl.BlockSpec((GA, 2 * TQ, TK), lambda b, g, i: (g, 0, 0)),
                  _const_spec((4, 64)), _const_spec((1, LANES))],
        out_specs=pl.BlockSpec((TQ, w), lambda b, g, i: (b * nq + i, g)),
        out_shape=jax.ShapeDtypeStruct((n, MIX), bf16),
        scratch_shapes=[pltpu.VMEM((GA, 2 * TQ, LANES), bf16),
                        pltpu.VMEM((GA, seq // TK, 2 * TQ, TK), f32),
                        pltpu.VMEM((GA, 2 * TQ, LANES), f32), pltpu.VMEM((GA, 2 * TQ, LANES), f32),
                        pltpu.VMEM((GA, 2 * TQ, LANES), f32)],
        compiler_params=_cparams(("arbitrary", "arbitrary", "arbitrary")),
        name="attn_diff",
    )(qa, ka, va, t0, t1, dl, og)


def _attn_c_kernel(q_ref, k_ref, v_ref, mk_ref, o_ref, sbuf, mbuf, lbuf, abuf):
    qi = pl.program_id(2)
    lane = lax.broadcasted_iota(jnp.int32, (TQ, LANES), 1)
    cols = lambda u: slice(u * LANES, (u + 1) * LANES)
    _two_pass_attention(GC, lambda u: q_ref[:, cols(u)], cols, lambda u: cols(u // 2), k_ref, v_ref, qi,
                        lambda u: mk_ref[...], None, sbuf, mbuf, lbuf, abuf)
    for hp in range(GC // 2):
        o_ref[:, cols(hp)] = jnp.where(lane < VC, _normalised(2 * hp, lbuf, abuf),
                                       _normalised(2 * hp + 1, lbuf, abuf)).astype(bf16)


def _attn_c(qc, kc, vc, mk, nb, seq):
    n = qc.shape[0]
    nq = seq // TQ
    w = GC * LANES
    wv = GC * VC
    return pl.pallas_call(
        _attn_c_kernel,
        grid=(nb, HC // GC, nq),
        in_specs=[pl.BlockSpec((TQ, w), lambda b, g, i: (b * nq + i, g)),
                  pl.BlockSpec((seq, w), lambda b, g, i: (b, g)),
                  pl.BlockSpec((seq, wv), lambda b, g, i: (b, g)),
                  _const_spec((TQ, TK))],
        out_specs=pl.BlockSpec((TQ, wv), lambda b, g, i: (b * nq + i, g)),
        out_shape=jax.ShapeDtypeStruct((n, MIX), bf16),
        scratch_shapes=[pltpu.VMEM((GC, seq // TK, TQ, TK), f32),
                        pltpu.VMEM((GC, TQ, LANES), f32), pltpu.VMEM((GC, TQ, LANES), f32),
                        pltpu.VMEM((GC, TQ, LANES), f32)],
        compiler_params=_cparams(("arbitrary", "arbitrary", "arbitrary")),
        name="attn_latent",
    )(qc, kc, vc, mk)


def _rope_apply(x, cosf, sinf, lane):
    w = x.shape[-1]
    partner = jnp.where((lane % LANES) < NOPE + ROPE // 2,
                        pltpu.roll(x, w - ROPE // 2, 1), pltpu.roll(x, ROPE // 2, 1))
    return x * cosf + partner * sinf


def _mla_prep_kernel(xc_ref, cos_ref, sin_ref, glq_ref, glkv_ref, wq_ref, wkv_ref, gq_ref, gk_ref, gr_ref,
                     sd_ref, ex_ref, q_ref, k_ref, v_ref):
    cq = _rms(xc_ref[:, 0:Q_LORA], glq_ref[...]).astype(bf16)
    ckv = _rms(xc_ref[:, Q_LORA:Q_LORA + KV_LORA], glkv_ref[...]).astype(bf16)
    cos8 = jnp.concatenate([cos_ref[...]] * HC, axis=-1)
    sin8 = jnp.concatenate([sin_ref[...]] * HC, axis=-1)
    lane8 = lax.broadcasted_iota(jnp.int32, (TM, HC * LANES), 1)

    hw = HC * LANES
    q2 = jnp.dot(cq, wq_ref[...], preferred_element_type=f32)
    rinv = _seg_rinv(q2[:, :hw], sd_ref, ex_ref)
    q_ref[...] = ((q2[:, :hw] * rinv * gq_ref[:, :hw]) * cos8
                  + (q2[:, hw:] * rinv * gq_ref[:, hw:]) * sin8).astype(bf16)

    kv = jnp.dot(ckv, wkv_ref[...], preferred_element_type=f32)
    v_ref[...] = kv[:, HC * LANES:].astype(bf16)
    xr = xc_ref[:, Q_LORA + KV_LORA:Q_LORA + KV_LORA + LANES]
    kr = xr * lax.rsqrt(jnp.sum(xr * xr, axis=-1, keepdims=True) * (1.0 / ROPE) + EPS) * gr_ref[...]
    kr = pltpu.roll(kr, NOPE, 1)
    lane1 = lax.broadcasted_iota(jnp.int32, (TM, LANES), 1)
    kr = _rope_apply(kr, cos_ref[...], sin_ref[...], lane1)
    for h in range(HC):
        cols = slice(h * LANES, (h + 1) * LANES)
        kn = kv[:, cols]
        kn = kn * lax.rsqrt(jnp.sum(kn * kn, axis=-1, keepdims=True) * (1.0 / NOPE) + EPS) * gk_ref[...]
        k_ref[:, cols] = (kn + kr).astype(bf16)


def _mla_prep(xc, cosf, sinf, glq, glkv, wq, wkv, gq, gk, gr, sd, ex):
    n = xc.shape[0]
    row = lambda w_: pl.BlockSpec((TM, w_), lambda i: (i, 0))
    return pl.pallas_call(
        _mla_prep_kernel,
        grid=(n // TM,),
        in_specs=[row(MIX), row(LANES), row(LANES), _const_spec((1, Q_LORA)), _const_spec((1, KV_LORA)),
                  _const_spec((Q_LORA, 2 * HC * LANES)), _const_spec((KV_LORA, HC * LANES + MIX)),
                  _const_spec((1, 2 * HC * LANES)), _const_spec((1, LANES)), _const_spec((1, LANES)),
                  _const_spec((HC * LANES, LANES)), _const_spec((2 * LANES, HC * LANES))],
        out_specs=[row(HC * LANES), row(HC * LANES), row(MIX)],
        out_shape=[jax.ShapeDtypeStruct((n, HC * LANES), bf16), jax.ShapeDtypeStruct((n, HC * LANES), bf16),
                   jax.ShapeDtypeStruct((n, MIX), bf16)],
        compiler_params=_cparams(("arbitrary",)),
        name="mla_prep",
    )(xc, cosf, sinf, glq, glkv, wq, wkv, gq, gk, gr, sd, ex)


def _merge_kernel(oa_ref, ob_ref, oc_ref, gt_ref, x_ref, g1_ref, sh_ref, sc_ref, ng_ref, wb_ref, wo_ref,
                  wr_ref, br_ref, x1_ref, h2_ref, meta_ref, cnt_ref, run_ref):
    i = pl.program_id(0)

    @pl.when(i == 0)
    def _():
        run_ref[...] = jnp.zeros_like(run_ref)

    merged = jnp.zeros((TM, D), f32)
    for j, o_ref in enumerate((oa_ref, ob_ref, oc_ref)):
        pj = jnp.dot(o_ref[...], wb_ref[j], preferred_element_type=f32)
        merged = merged + gt_ref[:, j * D:(j + 1) * D].astype(f32) * pj
    y = jnp.dot(merged.astype(bf16), wo_ref[...], preferred_element_type=f32)
    x1 = x_ref[...] + g1_ref[0] * y
    x1_ref[...] = x1
    h2 = _rms(x1, ng_ref[...]) * (1.0 + sc_ref[0]) + sh_ref[0]
    hr = h2.astype(bf16).astype(f32)
    lo = lax.shift_right_logical(lax.bitcast_convert_type(hr[:, 0:D // 2], jnp.uint32), jnp.uint32(16))
    hi = lax.bitcast_convert_type(hr[:, D // 2:D], jnp.uint32) & jnp.uint32(0xFFFF0000)
    words = lax.bitcast_convert_type(lo | hi, f32)
    for s in range(PACK_SLABS):
        h2_ref[pl.ds(s, TM, stride=PACK_SLABS), :] = words[:, s * LANES:(s + 1) * LANES]

    h_hi = h2.astype(bf16)
    h_lo = (h2 - h_hi.astype(f32)).astype(bf16)
    t = jnp.dot(h_hi, wr_ref[...], preferred_element_type=f32)
    lg = (t[:, :LANES] + t[:, LANES:] + jnp.dot(h_lo, wr_ref[:, 0:LANES], preferred_element_type=f32)
          + br_ref[...])
    lane = lax.broadcasted_iota(jnp.int32, (TM, LANES), 1)
    n_e = N_GROUPS * N_EXP
    gmask = (lane >= n_e) & (lane < n_e + N_GROUPS)
    gl = jnp.where(gmask, lg, NEG)
    gmax = jnp.max(gl, axis=-1, keepdims=True)
    g_w = 1.0 / jnp.sum(jnp.where(gmask, jnp.exp(gl - gmax), 0.0), axis=-1, keepdims=True)
    gidx = jnp.min(jnp.where(gl == gmax, lane - n_e, LANES), axis=-1, keepdims=True)
    emask = (lane < n_e) & ((lane // N_EXP) == gidx)
    el = jnp.where(emask, lg, NEG)
    m1 = jnp.max(el, axis=-1, keepdims=True)
    i1 = jnp.min(jnp.where(el == m1, lane, LANES), axis=-1, keepdims=True)
    el2 = jnp.where(lane == i1, NEG, el)
    m2 = jnp.max(el2, axis=-1, keepdims=True)
    i2 = jnp.min(jnp.where(el2 == m2, lane, LANES), axis=-1, keepdims=True)
    t = jnp.exp(m2 - m1)
    w1 = g_w / (1.0 + t)
    w2 = g_w * t / (1.0 + t)
    lo = jnp.minimum(i1, i2)
    hi = jnp.maximum(i1, i2)
    w_lo = jnp.where(i1 < i2, w1, w2)
    w_hi = jnp.where(i1 < i2, w2, w1)
    bucket = gidx * (N_EXP * N_EXP) + (lo % N_EXP) * N_EXP + (hi % N_EXP)

    lane2 = lax.broadcasted_iota(jnp.int32, (TM, 2 * LANES), 1)
    onehot = (lane2 == bucket).astype(f32)
    rr = lax.broadcasted_iota(jnp.int32, (TM, TM), 0)
    cc = lax.broadcasted_iota(jnp.int32, (TM, TM), 1)
    tri = (cc < rr).astype(bf16)
    before = jnp.dot(tri, onehot.astype(bf16), preferred_element_type=f32) + run_ref[...]
    rank = jnp.sum(onehot * before, axis=-1, keepdims=True)
    run_ref[...] = run_ref[...] + jnp.sum(onehot, axis=0, keepdims=True)
    cnt_ref[...] = run_ref[...]

    meta = (jnp.where(lane == i1, w1, 0.0) + jnp.where(lane == i2, w2, 0.0)
            + jnp.where(lane == 32, bucket.astype(f32), 0.0) + jnp.where(lane == 33, rank, 0.0)
            + jnp.where(lane == 34, w_lo, 0.0) + jnp.where(lane == 35, w_hi, 0.0))
    meta_ref[...] = meta


def _merge(oa, ob, oc, gt, x, g1, shift, scale, ng, wb, wo, wr, br, seq):
    n = x.shape[0]
    spb = seq // TM
    row = lambda w_: pl.BlockSpec((TM, w_), lambda i: (i, 0))
    modspec = pl.BlockSpec((1, 1, D), lambda i: (i // spb, 0, 0))
    return pl.pallas_call(
        _merge_kernel,
        grid=(n // TM,),
        in_specs=[row(MIX), row(MIX), row(MIX), row(3 * D), row(D), modspec, modspec, modspec,
                  _const_spec((1, D)), _const_spec((3, MIX, D)), _const_spec((D, D)),
                  _const_spec((D, 2 * LANES)), _const_spec((1, LANES))],
        out_specs=[row(D), pl.BlockSpec((TM * PACK_SLABS, LANES), lambda i: (i, 0)), row(LANES),
                   _const_spec((1, 2 * LANES))],
        out_shape=[jax.ShapeDtypeStruct((n, D), f32), jax.ShapeDtypeStruct((n * PACK_SLABS, LANES), f32),
                   jax.ShapeDtypeStruct((n, LANES), f32), jax.ShapeDtypeStruct((1, 2 * LANES), f32)],
        scratch_shapes=[pltpu.VMEM((1, 2 * LANES), f32)],
        compiler_params=_cparams(("arbitrary",)),
        name="merge_route",
    )(oa, ob, oc, gt, x, g1, shift, scale, ng, wb, wo, wr, br)


TE = 128
N_BUCKETS = N_GROUPS * N_EXP * N_EXP
N_PAIRS = N_GROUPS * (N_EXP * (N_EXP - 1) // 2)


def _num_tiles(n):
    return n // TE + N_PAIRS


TP = 2048


def _tables_kernel(meta_ref, cnt_ref, pos_ref, tt_ref):
    cnt = cnt_ref[...]
    ntile = jnp.floor((cnt + (TE - 1)) * (1.0 / TE))
    bi = lax.broadcasted_iota(jnp.int32, (N_BUCKETS, N_BUCKETS), 0)
    bj = lax.broadcasted_iota(jnp.int32, (N_BUCKETS, N_BUCKETS), 1)
    upper = (bi <= bj).astype(bf16)
    incl = jnp.dot(jnp.broadcast_to(ntile, (8, N_BUCKETS)).astype(bf16), upper, preferred_element_type=f32)[0:1]
    excl = incl - ntile

    meta = meta_ref[...]
    tp = meta.shape[0]
    lane = lax.broadcasted_iota(jnp.int32, (tp, LANES), 1)
    bucket = jnp.sum(jnp.where(lane == 32, meta, 0.0), axis=-1, keepdims=True).astype(jnp.int32)
    rank = jnp.sum(jnp.where(lane == 33, meta, 0.0), axis=-1, keepdims=True)
    lane2 = lax.broadcasted_iota(jnp.int32, (tp, N_BUCKETS), 1)
    first_tile = jnp.sum(jnp.where(lane2 == bucket, excl, 0.0), axis=-1, keepdims=True)
    pos_ref[...] = (first_tile * TE + rank).astype(jnp.int32)

    @pl.when(pl.program_id(0) == 0)
    def _():
        tau = lax.broadcasted_iota(jnp.int32, (N_BUCKETS, 1), 0).astype(f32)
        lane_b = lax.broadcasted_iota(jnp.int32, (1, N_BUCKETS), 1).astype(f32)
        tb = jnp.sum((incl <= tau).astype(f32), axis=-1, keepdims=True)
        last_b = jnp.max(jnp.where(cnt > 0, lane_b, 0.0), axis=-1, keepdims=True)
        own = lane_b == tb
        in_bucket = jnp.sum(jnp.where(own, cnt, 0.0), axis=-1, keepdims=True)
        first = jnp.sum(jnp.where(own, excl, 0.0), axis=-1, keepdims=True)
        valid = jnp.clip(in_bucket - (tau - first) * TE, 0.0, float(TE)).astype(jnp.int32)
        tbi = jnp.minimum(tb, last_b).astype(jnp.int32)
        grp = lax.shift_right_logical(tbi, 6) * N_EXP
        ea = grp + (lax.shift_right_logical(tbi, 3) & 7)
        eb = grp + (tbi & 7)
        total = jnp.sum(ntile, axis=-1, keepdims=True).astype(jnp.int32)
        lane_t = lax.broadcasted_iota(jnp.int32, (N_BUCKETS, LANES), 1)
        tt_ref[...] = jnp.where(lane_t == 0, ea, jnp.where(lane_t == 1, eb, jnp.where(
            lane_t == 2, total, jnp.where(lane_t == 3, valid, 0))))


def _tables(meta, cnt):
    n = meta.shape[0]
    tp = math.gcd(n, TP)
    assert _num_tiles(n) <= N_BUCKETS
    return pl.pallas_call(
        _tables_kernel,
        grid=(n // tp,),
        in_specs=[pl.BlockSpec((tp, LANES), lambda i: (i, 0)), _const_spec((1, N_BUCKETS))],
        out_specs=[pl.BlockSpec((tp, 1), lambda i: (i, 0)), _const_spec((N_BUCKETS, LANES))],
        out_shape=[jax.ShapeDtypeStruct((n, 1), jnp.int32), jax.ShapeDtypeStruct((N_BUCKETS, LANES), jnp.int32)],
        compiler_params=_cparams(("arbitrary",)),
        name="moe_tables",
    )(meta, cnt)


def _plan_kernel(pos_ref, tok_ref, *, n, nslot):
    def fill(s, _):
        tok_ref[s] = -1
        return 0

    lax.fori_loop(0, nslot, fill, 0, unroll=16)

    def place(t, _):
        tok_ref[pos_ref[t]] = t
        return 0

    lax.fori_loop(0, n, place, 0, unroll=16)


def _plan(pos):
    n = pos.shape[0]
    nslot = _num_tiles(n) * TE
    smem = pl.BlockSpec(memory_space=pltpu.SMEM)
    return pl.pallas_call(
        functools.partial(_plan_kernel, n=n, nslot=nslot),
        in_specs=[smem], out_specs=smem,
        out_shape=jax.ShapeDtypeStruct((nslot,), jnp.int32),
        name="moe_plan",
    )(pos)


SCATTER_GROUP = 32


def _moe_kernel(tea_ref, teb_ref, nt_ref, tok_ref, tc_ref, h_ref, m_ref, wg_ref, wu_ref, wd_ref, y_hbm,
                xs, ms, ob0, ob1, ssem, *, n):
    i = pl.program_id(0)
    nt = nt_ref[0]
    obufs = (ob0, ob1)

    def scatter_copy(dst_tok, sl, r):
        dst = y_hbm.at[pl.ds(pl.multiple_of(dst_tok * Y_PITCH, Y_PITCH), Y_PITCH), :]
        return pltpu.make_async_copy(obufs[sl].at[pl.ds(r * O_PITCH, Y_PITCH), :], dst, ssem.at[sl])

    def real_slots(tile):
        return jnp.where(tile >= 0, tc_ref[jnp.maximum(tile, 0)], 0)

    def scatter_groups(tile, fn):
        cnt = real_slots(tile)
        for g in range(TE // SCATTER_GROUP):
            @pl.when(cnt > g * SCATTER_GROUP)
            def _(g=g):
                for r in range(g * SCATTER_GROUP, (g + 1) * SCATTER_GROUP):
                    fn(r)

    def start_scatters(tile, sl):
        base = jnp.maximum(tile, 0) * TE

        def start(r):
            t = tok_ref[base + r]
            scatter_copy(jnp.where(t < 0, n + sl * TE + r, t), sl, r).start(priority=r % 2)

        scatter_groups(tile, start)

    def wait_scatters(tile, sl):
        scatter_groups(tile, lambda r: scatter_copy(n, sl, r).wait())

    @pl.when(i == 0)
    def _():
        for sl in range(2):
            obufs[sl][...] = jnp.zeros_like(obufs[sl])
            spare = pltpu.make_async_copy(obufs[sl].at[pl.ds(0, TE * Y_PITCH), :],
                                          y_hbm.at[pl.ds((n + sl * TE) * Y_PITCH, TE * Y_PITCH), :], ssem.at[sl])
            spare.start()
            spare.wait()

    def step(s):
        o = 1 - s

        wait_scatters(i - 2, s)
        start_scatters(i - 1, o)

        for r in range(TE):
            t = jnp.maximum(tok_ref[i * TE + r], 0)
            xs[pl.ds(r * PACK_SLABS, PACK_SLABS), :] = h_ref[pl.ds(pl.multiple_of(t * PACK_SLABS, PACK_SLABS),
                                                                   PACK_SLABS), :]
            ms[pl.ds(r, 1), :] = m_ref[pl.ds(t, 1), :]
        slab = lambda k: xs[pl.ds(k, TE, stride=PACK_SLABS), :]
        words = lax.bitcast_convert_type(jnp.concatenate([slab(k) for k in range(PACK_SLABS)], axis=-1),
                                         jnp.uint32)
        lo = lax.bitcast_convert_type(lax.shift_left(words, jnp.uint32(16)), f32)
        hi = lax.bitcast_convert_type(words & jnp.uint32(0xFFFF0000), f32)
        hb = jnp.concatenate([lo, hi], axis=-1).astype(bf16)
        meta = ms[...]
        lane = lax.broadcasted_iota(jnp.int32, (TE, LANES), 1)
        w_lo = jnp.sum(jnp.where(lane == 34, meta, 0.0), axis=-1, keepdims=True)
        w_hi = jnp.sum(jnp.where(lane == 35, meta, 0.0), axis=-1, keepdims=True)

        def ffn(e, wcol):
            a = jnp.dot(hb, wg_ref[0, 0, e], preferred_element_type=f32)
            u = jnp.dot(hb, wu_ref[0, 0, e], preferred_element_type=f32)
            act = (a * jax.nn.sigmoid(a)) * u * wcol
            return jnp.dot(act.astype(bf16), wd_ref[0, 0, e], preferred_element_type=f32)

        out = ffn(lax.rem(tea_ref[i], N_EXP), w_lo) + ffn(lax.rem(teb_ref[i], N_EXP), w_hi)
        for k in range(SLABS):
            obufs[s][pl.ds(k, TE, stride=O_PITCH), :] = out[:, k * LANES:(k + 1) * LANES]

        @pl.when(i == nt - 1)
        def _():
            start_scatters(i, s)
            wait_scatters(i - 1, o)
            wait_scatters(i, s)

    for s in range(2):
        @pl.when((i < nt) & (lax.rem(i, 2) == s))
        def _(s=s):
            step(s)


def _moe(h2ext, meta, tok, tea, teb, nt, tc, wg, wu, wd, layer):
    n = h2ext.shape[0] // PACK_SLABS
    ntmax = _num_tiles(n)
    once = pl.Buffered(1)
    wspec_in = pl.BlockSpec((1, 1, N_EXP, D, F_EXP),
                            lambda i, tea, teb, nt, tok, tc: (layer, tea[i] // N_EXP, 0, 0, 0), pipeline_mode=once)
    wspec_out = pl.BlockSpec((1, 1, N_EXP, F_EXP, D),
                             lambda i, tea, teb, nt, tok, tc: (layer, tea[i] // N_EXP, 0, 0, 0), pipeline_mode=once)
    hspec = pl.BlockSpec((n * PACK_SLABS, LANES), lambda i, tea, teb, nt, tok, tc: (0, 0), pipeline_mode=once)
    mspec = pl.BlockSpec((n, LANES), lambda i, tea, teb, nt, tok, tc: (0, 0), pipeline_mode=once)
    return pl.pallas_call(
        functools.partial(_moe_kernel, n=n),
        grid_spec=pltpu.PrefetchScalarGridSpec(
            num_scalar_prefetch=5,
            grid=(ntmax,),
            in_specs=[hspec, mspec, wspec_in, wspec_in, wspec_out],
            out_specs=pl.BlockSpec(memory_space=pl.ANY),
            scratch_shapes=[pltpu.VMEM((TE * PACK_SLABS, LANES), f32), pltpu.VMEM((TE, LANES), f32)] + [
                pltpu.VMEM((TE * O_PITCH, LANES), f32)] * 2 + [pltpu.SemaphoreType.DMA((2,))]),
        out_shape=jax.ShapeDtypeStruct(((n + 2 * TE) * Y_PITCH, LANES), f32),
        compiler_params=_cparams(("arbitrary",), vmem=60 * 1024 * 1024),
        name="moe_sparse",
    )(tea, teb, nt, tok, tc, h2ext, meta, wg, wu, wd)


def _final_kernel(x_ref, y_ref, g_ref, o_ref):
    o_ref[...] = x_ref[...] + g_ref[0] * _rows_from_token_major(y_ref, x_ref.shape[0])


def _final_combine(x1, y, g2, seq):
    n = x1.shape[0]
    t = 512
    spb = seq // t
    row = pl.BlockSpec((t, D), lambda i: (i, 0))
    return pl.pallas_call(
        _final_kernel,
        grid=(n // t,),
        in_specs=[row, pl.BlockSpec((t * Y_PITCH, LANES), lambda i: (i, 0)),
                  pl.BlockSpec((1, 1, D), lambda i: (i // spb, 0, 0))],
        out_specs=row,
        out_shape=jax.ShapeDtypeStruct((n, D), f32),
        compiler_params=_cparams(("arbitrary",)),
        name="final_combine",
    )(x1, y, g2)


def _seg_matrices(width, segs):
    sd = np.zeros((width, LANES), np.float32)
    ex = np.zeros((LANES, width), np.float32)
    for j, (s, ln) in enumerate(segs):
        sd[s:s + ln, j] = 1.0 / ln
        ex[j, s:s + ln] = 1.0
    return jnp.asarray(sd, bf16), jnp.asarray(np.concatenate([ex, ex], axis=0), bf16)


def _head_pad(w, heads, per_head, keep):
    k = w.shape[0]
    w3 = w.reshape(k, heads, per_head)[:, :, :keep]
    return jnp.pad(w3, ((0, 0), (0, 0), (0, LANES - keep))).reshape(k, heads * LANES)


def kernel(x, c, positions, w_ada, b_ada, norm_g, w_in, diff_qk_g, diff_lambda, diff_out_g, rel_bias, sgu_v_g, sgu_w, sgu_b, mla_lat_g, mla_w_uq, mla_w_ukv, mla_qk_g, w_branch, w_out, router_g_w, router_g_b, router_e_w, router_e_b, w_e_gate, w_e_up, w_e_down):
    nb, seq, _ = x.shape
    n = nb * seq
    assert seq % TQ == 0 and seq % TM == 0 and x.shape[2] == D

    mod = _ada(c, w_ada, b_ada)
    cosf, sinf = _rope_tables(positions)
    t0, t1, mk = _bias_tiles(rel_bias)

    sd_a, ex_a = _seg_matrices(MIX, [(s * 64, 64) for s in range(8)])
    segs_q = []
    for h in range(HC):
        segs_q += [(h * LANES, NOPE), (h * LANES + NOPE, ROPE)]
    sd_q, ex_q = _seg_matrices(HC * LANES, segs_q)
    ne = N_GROUPS * N_EXP
    zpad = lambda a, w_: jnp.pad(a, ((0, 0), (0, w_ - a.shape[1])))

    w_main = w_in[:, :, :MAIN_W].astype(bf16)
    w_gates = w_in[:, :, MAIN_W:].astype(bf16)
    wg_all, wu_all, wd_all = w_e_gate.astype(bf16), w_e_up.astype(bf16), w_e_down.astype(bf16)

    xcur = x.reshape(n, D)
    y_prev, g2_prev = None, None
    for l in range(DEPTH):
        m3 = mod[l].reshape(nb, 1, 6 * D)
        shift1, scale1, gate1, shift2, scale2, gate2 = [m3[:, :, k * D:(k + 1) * D] for k in range(6)]
        lambda_init = LAMBDA_INIT_BASE - LAMBDA_INIT_SCALE * math.exp(-LAMBDA_INIT_DECAY * l)

        gq = (jnp.tile(diff_qk_g[l, 0], 8) * (64 ** -0.5 * LOG2E)).reshape(1, MIX)
        gk = jnp.tile(diff_qk_g[l, 1], 8).reshape(1, MIX)
        outs = _inproj(xcur, y_prev, g2_prev, shift1, scale1, norm_g[l, 0].reshape(1, D), w_main, w_gates, l,
                       gq, gk, sd_a, ex_a, sgu_v_g[l].reshape(1, MIX), sgu_w[l], jnp.transpose(sgu_b[l]), seq)
        if y_prev is not None:
            xcur = outs[0]
            outs = outs[1:]
        qa, ka, va, ob, xc, gt = outs

        oa = _attn_a(qa, ka, va, t0, t1, diff_lambda[l], diff_out_g[l].reshape(1, LANES), nb, seq, lambda_init)

        qkg = mla_qk_g[l]
        swap = np.concatenate([np.arange(NOPE), NOPE + ROPE // 2 + np.arange(ROPE // 2), NOPE + np.arange(ROPE // 2)])
        wq3 = mla_w_uq[l].reshape(Q_LORA, HC, NOPE + ROPE)
        wq = jnp.concatenate([_head_pad(mla_w_uq[l], HC, NOPE + ROPE, NOPE + ROPE),
                              _head_pad(wq3[:, :, swap].reshape(Q_LORA, -1), HC, NOPE + ROPE, NOPE + ROPE)],
                             axis=1).astype(bf16)
        wkv3 = mla_w_ukv[l].reshape(KV_LORA, HC, NOPE + VC)
        wk = jnp.pad(wkv3[:, :, :NOPE], ((0, 0), (0, 0), (0, LANES - NOPE))).reshape(KV_LORA, HC * LANES)
        wv = wkv3[:, :, NOPE:].reshape(KV_LORA, HC * VC)
        wkv = jnp.concatenate([wk, wv], axis=1).astype(bf16)
        gq_c = jnp.concatenate([jnp.tile(jnp.pad(g_, (0, LANES - NOPE - ROPE)), HC) for g_ in (qkg[0], qkg[0][swap])]
                               ).reshape(1, 2 * HC * LANES) * ((NOPE + ROPE) ** -0.5 * LOG2E)
        gk_c = jnp.pad(qkg[1, :NOPE], (0, LANES - NOPE)).reshape(1, LANES)
        gr_c = jnp.pad(qkg[1, NOPE:], (0, LANES - ROPE)).reshape(1, LANES)
        qc, kc, vc = _mla_prep(xc, cosf, sinf, mla_lat_g[l, :Q_LORA].reshape(1, Q_LORA),
                               mla_lat_g[l, Q_LORA:].reshape(1, KV_LORA), wq, wkv, gq_c, gk_c, gr_c, sd_q, ex_q)
        oc = _attn_c(qc, kc, vc, mk, nb, seq)

        wr = zpad(jnp.concatenate([router_e_w[l], router_g_w[l]], axis=1), LANES)
        wr_hi = wr.astype(bf16)
        wr = jnp.concatenate([wr_hi, (wr - wr_hi.astype(f32)).astype(bf16)], axis=1)
        br = zpad(jnp.concatenate([router_e_b[l], router_g_b[l]]).reshape(1, ne + N_GROUPS), LANES)
        x1, h2ext, meta, cnt = _merge(oa, ob, oc, gt, xcur, gate1, shift2, scale2, norm_g[l, 1].reshape(1, D),
                                      w_branch[l].astype(bf16), w_out[l].astype(bf16), wr, br, seq)

        pos, tt = _tables(meta, cnt)
        tok = _plan(pos.reshape(n))
        ntmax = _num_tiles(n)
        y_prev = _moe(h2ext, meta, tok, tt[:ntmax, 0], tt[:ntmax, 1], tt[0:1, 2], tt[:ntmax, 3],
                      wg_all, wu_all, wd_all, l)
        g2_prev = gate2
        xcur = x1

    out = _final_combine(xcur, y_prev, g2_prev, seq)
    return out.reshape(nb, seq, D)
```

```python
import functools
import math

import jax
import jax.numpy as jnp
import numpy as np
from jax import lax
from jax.experimental import pallas as pl
from jax.experimental.pallas import tpu as pltpu

f32 = jnp.float32
bf16 = jnp.bfloat16

D = 1024
DEPTH = 4
CHUNK = 64
MIX = 512
HA = 4
HC = 8
Q_LORA, KV_LORA, ROPE = 256, 128, 32
NOPE, VC = 64, 64
N_GROUPS, N_EXP, F_EXP = 4, 8, 256
N_REL_BUCKETS = 32
ROPE_THETA = 10000.0
LAMBDA_INIT_BASE, LAMBDA_INIT_SCALE, LAMBDA_INIT_DECAY = 0.8, 0.6, 0.3
EPS = 1e-6
NEG = -1e30

XC_OFF = 2560
MAIN_W = XC_OFF + Q_LORA + KV_LORA + ROPE

TM = 512
TQ = 256
TK = 256
LANES = 128
VMEM_LIMIT = 56 * 1024 * 1024
SLABS = D // LANES
PACK_SLABS = SLABS // 2
Y_PITCH = SLABS
O_PITCH = Y_PITCH + 1


def _cparams(sem, vmem=VMEM_LIMIT):
    return pltpu.CompilerParams(dimension_semantics=sem, vmem_limit_bytes=vmem)


def _const_spec(shape):
    nd = len(shape)
    return pl.BlockSpec(shape, lambda *_: (0,) * nd)


def _rms(x, g_row):
    return x * lax.rsqrt(jnp.mean(x * x, axis=-1, keepdims=True) + EPS) * g_row


def _rows_from_token_major(y_ref, rows):
    return jnp.concatenate([y_ref[pl.ds(k, rows, stride=Y_PITCH), :] for k in range(SLABS)], axis=-1)


def _seg_rinv(x, sd_ref, ex_ref):
    ss = jnp.dot((x * x).astype(bf16), sd_ref[...], preferred_element_type=f32)
    hi = ss.astype(bf16)
    lo = (ss - hi.astype(f32)).astype(bf16)
    full = jnp.dot(jnp.concatenate([hi, lo], axis=-1), ex_ref[...], preferred_element_type=f32)
    return lax.rsqrt(full + EPS)


def _ada_kernel(c_ref, w_ref, b_ref, o_ref):
    c = c_ref[...]
    ca = (c * jax.nn.sigmoid(c)).astype(bf16)
    o_ref[0] = jnp.dot(ca, w_ref[0].astype(bf16), preferred_element_type=f32) + b_ref[0]


def _ada(c, w_ada, b_ada):
    nb = c.shape[0]
    tn = 1024
    return pl.pallas_call(
        _ada_kernel,
        grid=(DEPTH, 6 * D // tn),
        in_specs=[pl.BlockSpec((nb, D), lambda l, j: (0, 0)),
                  pl.BlockSpec((1, D, tn), lambda l, j: (l, 0, j)),
                  pl.BlockSpec((1, 1, tn), lambda l, j: (l, 0, j))],
        out_specs=pl.BlockSpec((1, nb, tn), lambda l, j: (l, 0, j)),
        out_shape=jax.ShapeDtypeStruct((DEPTH, nb, 6 * D), f32),
        compiler_params=_cparams(("arbitrary", "arbitrary")),
        name="ada_mod",
    )(c, w_ada, b_ada.reshape(DEPTH, 1, 6 * D))


def _rope_kernel(pos_ref, invf_ref, cos_ref, sin_ref):
    ang = pos_ref[...].astype(f32) * invf_ref[...]
    lane = lax.broadcasted_iota(jnp.int32, ang.shape, 1)
    rot = (lane >= NOPE) & (lane < NOPE + ROPE)
    first = lane < NOPE + ROPE // 2
    cos_ref[...] = jnp.where(rot, jnp.cos(ang), 1.0)
    s = jnp.sin(ang)
    sin_ref[...] = jnp.where(rot, jnp.where(first, -s, s), 0.0)


def _rope_tables(positions):
    n = positions.size
    inv_freq = ROPE_THETA ** (-jnp.arange(0, ROPE, 2, dtype=f32) / ROPE)
    invf = jnp.zeros((1, LANES), f32).at[0, NOPE:NOPE + ROPE].set(jnp.tile(inv_freq, 2))
    t = math.gcd(n, 2048)
    return pl.pallas_call(
        _rope_kernel,
        grid=(n // t,),
        in_specs=[pl.BlockSpec((t, 1), lambda i: (i, 0)), _const_spec((1, LANES))],
        out_specs=[pl.BlockSpec((t, LANES), lambda i: (i, 0))] * 2,
        out_shape=[jax.ShapeDtypeStruct((n, LANES), f32)] * 2,
        compiler_params=_cparams(("arbitrary",)),
        name="rope_tables",
    )(positions.reshape(n, 1), invf)


def _t5_bucket(rel):
    n = jnp.abs(rel)
    large = 8 + sum((n >= t).astype(jnp.int32) for t in (12, 16, 23, 32, 46, 64, 91))
    return jnp.where(rel > 0, 16, 0) + jnp.where(n < 8, n, large)


def _bias_kernel(tbl_ref, t0_ref, t1_ref, mk_ref):
    r = lax.broadcasted_iota(jnp.int32, (TQ, TK), 0)
    c = lax.broadcasted_iota(jnp.int32, (TQ, TK), 1)
    allowed = (c // CHUNK) <= (r // CHUNK)
    mk = jnp.where(allowed, 0.0, NEG).astype(f32)
    mk_ref[...] = mk
    b0 = _t5_bucket(c - r)
    b1 = _t5_bucket(c - r - TK)
    for h in range(HA):
        far = tbl_ref[N_REL_BUCKETS // 2 - 1, h]
        t0 = jnp.zeros((TQ, TK), f32)
        t1 = jnp.zeros((TQ, TK), f32)
        for b in range(N_REL_BUCKETS):
            v = (tbl_ref[b, h] - far) * LOG2E
            t0 = jnp.where(b0 == b, v, t0)
            t1 = jnp.where(b1 == b, v, t1)
        t0 = t0 + mk
        t0_ref[h, 0:TQ, :] = t0
        t0_ref[h, TQ:2 * TQ, :] = t0
        t1_ref[h, 0:TQ, :] = t1
        t1_ref[h, TQ:2 * TQ, :] = t1


def _bias_tiles(rel_bias):
    return pl.pallas_call(
        _bias_kernel,
        in_specs=[pl.BlockSpec(memory_space=pltpu.SMEM)],
        out_specs=[pl.BlockSpec(memory_space=pltpu.VMEM)] * 3,
        out_shape=[jax.ShapeDtypeStruct((HA, 2 * TQ, TK), f32),
                   jax.ShapeDtypeStruct((HA, 2 * TQ, TK), f32),
                   jax.ShapeDtypeStruct((TQ, TK), f32)],
        name="bias_tiles",
    )(rel_bias)


def _inproj_kernel(*refs, combine):
    if combine:
        (xa_ref, y_ref, g2_ref, sh_ref, sc_ref, ng_ref, w_ref, wgt_ref, gq_ref, gk_ref, sd_ref, ex_ref,
         vg_ref, ws_ref, bs_ref,
         xo_ref, qa_ref, ka_ref, va_ref, ob_ref, xc_ref, gt_ref) = refs
        x = xa_ref[...] + g2_ref[0] * _rows_from_token_major(y_ref, TM)
        xo_ref[...] = x
    else:
        (xa_ref, sh_ref, sc_ref, ng_ref, w_ref, wgt_ref, gq_ref, gk_ref, sd_ref, ex_ref,
         vg_ref, ws_ref, bs_ref,
         qa_ref, ka_ref, va_ref, ob_ref, xc_ref, gt_ref) = refs
        x = xa_ref[...]
    h = _rms(x, ng_ref[...]) * (1.0 + sc_ref[0]) + sh_ref[0]
    hb = h.astype(bf16)

    q = jnp.dot(hb, w_ref[0, :, 0:MIX], preferred_element_type=f32)
    qa_ref[...] = (q * _seg_rinv(q, sd_ref, ex_ref) * gq_ref[...]).astype(bf16)
    k = jnp.dot(hb, w_ref[0, :, MIX:2 * MIX], preferred_element_type=f32)
    ka_ref[...] = (k * _seg_rinv(k, sd_ref, ex_ref) * gk_ref[...]).astype(bf16)
    va_ref[...] = jnp.dot(hb, w_ref[0, :, 2 * MIX:3 * MIX], preferred_element_type=f32).astype(bf16)

    z = jnp.dot(hb, w_ref[0, :, 3 * MIX:5 * MIX], preferred_element_type=f32)
    z = 0.5 * z * (1.0 + jnp.tanh(math.sqrt(2.0 / math.pi) * (z + 0.044715 * (z * z * z))))
    u = z[:, :MIX]
    v = _rms(z[:, MIX:], vg_ref[...]).astype(bf16)
    ri = lax.broadcasted_iota(jnp.int32, (LANES, LANES), 0)
    ci = lax.broadcasted_iota(jnp.int32, (LANES, LANES), 1)
    allowed = (ci // CHUNK) <= (ri // CHUNK)
    for g in range(4):
        wm = jnp.where(allowed, ws_ref[g], 0.0).astype(bf16)
        bcol = bs_ref[:, g:g + 1]
        cols = slice(g * LANES, (g + 1) * LANES)
        nw = TM // LANES
        vcat = jnp.concatenate([v[wd * LANES:(wd + 1) * LANES, cols] for wd in range(nw)], axis=-1)
        vs = jnp.dot(wm, vcat, preferred_element_type=f32) + bcol
        for wd in range(nw):
            rows = slice(wd * LANES, (wd + 1) * LANES)
            ob_ref[rows, cols] = (u[rows, cols] * vs[:, wd * LANES:(wd + 1) * LANES]).astype(bf16)

    lat = Q_LORA + KV_LORA
    xc = jnp.dot(hb, w_ref[0, :, XC_OFF:MAIN_W], preferred_element_type=f32)
    xc_ref[:, 0:lat] = xc[:, 0:lat]
    xc_ref[:, lat:lat + LANES] = jnp.concatenate([xc[:, lat:lat + ROPE], jnp.zeros((TM, LANES - ROPE), f32)],
                                                 axis=-1)

    for j in range(3):
        gl = jnp.dot(hb, wgt_ref[0, :, j * D:(j + 1) * D], preferred_element_type=f32)
        gt_ref[:, j * D:(j + 1) * D] = jax.nn.sigmoid(gl).astype(bf16)


def _inproj(xa, y, g2, shift, scale, ng, w, wgt, layer, gq, gk, sd, ex, vg, ws, bs, seq):
    n = xa.shape[0]
    spb = seq // TM
    combine = y is not None
    row = lambda w_: pl.BlockSpec((TM, w_), lambda i: (i, 0))
    modspec = pl.BlockSpec((1, 1, D), lambda i: (i // spb, 0, 0))
    in_specs = [row(D)]
    args = [xa]
    if combine:
        in_specs += [pl.BlockSpec((TM * Y_PITCH, LANES), lambda i: (i, 0)), modspec]
        args += [y, g2]
    in_specs += [modspec, modspec, _const_spec((1, D)),
                 pl.BlockSpec((1, D, MAIN_W), lambda i: (layer, 0, 0), pipeline_mode=pl.Buffered(1)),
                 pl.BlockSpec((1, D, 3 * D), lambda i: (layer, 0, 0), pipeline_mode=pl.Buffered(1)),
                 _const_spec((1, MIX)), _const_spec((1, MIX)),
                 _const_spec((MIX, LANES)), _const_spec((2 * LANES, MIX)),
                 _const_spec((1, MIX)), _const_spec((4, LANES, LANES)), _const_spec((LANES, 4))]
    args += [shift, scale, ng, w, wgt, gq, gk, sd, ex, vg, ws, bs]
    out_specs = [row(MIX)] * 4 + [row(MIX), row(3 * D)]
    out_shape = [jax.ShapeDtypeStruct((n, MIX), bf16)] * 4 + [
        jax.ShapeDtypeStruct((n, MIX), f32), jax.ShapeDtypeStruct((n, 3 * D), bf16)]
    if combine:
        out_specs = [row(D)] + out_specs
        out_shape = [jax.ShapeDtypeStruct((n, D), f32)] + out_shape
    return pl.pallas_call(
        functools.partial(_inproj_kernel, combine=combine),
        grid=(n // TM,),
        in_specs=in_specs, out_specs=out_specs, out_shape=out_shape,
        compiler_params=_cparams(("arbitrary",)),
        name="inproj",
    )(*args)


LOG2E = math.log2(math.e)


def _two_pass_attention(nunits, q_of, kcols_of, vcols_of, k_ref, v_ref, qi, diag_bias_of, sub_bias_of,
                        sbuf, mbuf, lbuf, abuf):
    def keys(j, nb):
        return pl.ds(pl.multiple_of(j * TK, TK), nb * TK)

    def score_blocks(j, nb, bias_of):
        for u in range(nunits):
            s = lax.dot_general(q_of(u), k_ref[keys(j, nb), kcols_of(u)], (((1,), (1,)), ((), ())),
                                preferred_element_type=f32)
            if bias_of is not None:
                s = s + bias_of(u)
            m = mbuf[u]
            for b in range(nb):
                sbuf[u, j + b] = s[:, b * TK:(b + 1) * TK]
            for c in range(nb * TK // LANES):
                m = jnp.maximum(m, s[:, c * LANES:(c + 1) * LANES])
            mbuf[u] = m

    def blocks_in_pairs(first, count, fn):
        def pair(jj, _):
            fn(first + 2 * jj, 2)
            return 0

        lax.fori_loop(0, count // 2, pair, 0)

        @pl.when(count % 2 == 1)
        def _():
            fn(first + count - 1, 1)

    for u in range(nunits):
        mbuf[u] = jnp.full(mbuf.shape[1:], NEG, f32)
    if sub_bias_of is None:
        nfar = qi
    else:
        nfar = jnp.maximum(qi - 1, 0)

        @pl.when(qi >= 1)
        def _():
            score_blocks(qi - 1, 1, sub_bias_of)

    blocks_in_pairs(0, nfar, lambda j, nb: score_blocks(j, nb, None))
    score_blocks(qi, 1, diag_bias_of)

    for u in range(nunits):
        mbuf[u] = jnp.broadcast_to(jnp.max(mbuf[u], axis=-1, keepdims=True), mbuf.shape[1:])
        lbuf[u] = jnp.zeros(lbuf.shape[1:], f32)
        abuf[u] = jnp.zeros(abuf.shape[1:], f32)

    def accumulate_blocks(j, nb):
        for u in range(nunits):
            mb = mbuf[u]
            lsum = lbuf[u]
            ps = []
            for b in range(nb):
                s = sbuf[u, j + b]
                for c in range(TK // LANES):
                    p = jnp.exp2(s[:, c * LANES:(c + 1) * LANES] - mb)
                    lsum = lsum + p
                    ps.append(p)
            lbuf[u] = lsum
            p = jnp.concatenate(ps, axis=-1).astype(bf16)
            abuf[u] = abuf[u] + jnp.dot(p, v_ref[keys(j, nb), vcols_of(u)], preferred_element_type=f32)

    blocks_in_pairs(0, qi + 1, accumulate_blocks)


def _normalised(u, lbuf, abuf):
    return abuf[u] / jnp.sum(lbuf[u], axis=-1, keepdims=True)


GA = 4
GC = 8


def _attn_a_kernel(q_ref, k_ref, v_ref, t0_ref, t1_ref, dl_ref, og_ref, o_ref,
                   qstk, sbuf, mbuf, lbuf, abuf, *, lambda_init):
    qi = pl.program_id(2)
    dl = dl_ref[...]
    lam = (jnp.exp(jnp.sum(dl[0:1] * dl[1:2], axis=-1, keepdims=True))
           - jnp.exp(jnp.sum(dl[2:3] * dl[3:4], axis=-1, keepdims=True)) + lambda_init)
    lane = lax.broadcasted_iota(jnp.int32, (TQ, LANES), 1)
    cols = lambda u: slice(u * LANES, (u + 1) * LANES)
    for u in range(GA):
        qh = q_ref[:, cols(u)]
        zero = jnp.zeros_like(qh)
        qstk[u, 0:TQ, :] = jnp.where(lane < 64, qh, zero)
        qstk[u, TQ:2 * TQ, :] = jnp.where(lane >= 64, qh, zero)
    _two_pass_attention(GA, lambda u: qstk[u], cols, cols, k_ref, v_ref, qi,
                        lambda u: t0_ref[u], lambda u: t1_ref[u], sbuf, mbuf, lbuf, abuf)
    for u in range(GA):
        o = _normalised(u, lbuf, abuf)
        oh = o[:TQ] - lam * o[TQ:]
        oh = _rms(oh, og_ref[...]) * (1.0 - lambda_init)
        o_ref[:, cols(u)] = oh.astype(bf16)


def _attn_a(qa, ka, va, t0, t1, dl, og, nb, seq, lambda_init):
    n = qa.shape[0]
    nq = seq // TQ
    w = GA * LANES
    return pl.pallas_call(
        functools.partial(_attn_a_kernel, lambda_init=lambda_init),
        grid=(nb, HA // GA, nq),
        in_specs=[pl.BlockSpec((TQ, w), lambda b, g, i: (b * nq + i, g)),
                  pl.BlockSpec((seq, w), lambda b, g, i: (b, g)),
                  pl.BlockSpec((seq, w), lambda b, g, i: (b, g)),
                  pl.BlockSpec((GA, 2 * TQ, TK), lambda b, g, i: (g, 0, 0)),
                  pl.BlockSpec((GA, 2 * TQ, TK), lambda b, g, i: (g, 0, 0)),
                  _const_spec((4, 64)), _const_spec((1, LANES))],
        out_specs=pl.BlockSpec((TQ, w), lambda b, g, i: (b * nq + i, g)),
        out_shape=jax.ShapeDtypeStruct((n, MIX), bf16),
        scratch_shapes=[pltpu.VMEM((GA, 2 * TQ, LANES), bf16),
                        pltpu.VMEM((GA, seq // TK, 2 * TQ, TK), f32),
                        pltpu.VMEM((GA, 2 * TQ, LANES), f32), pltpu.VMEM((GA, 2 * TQ, LANES), f32),
                        pltpu.VMEM((GA, 2 * TQ, LANES), f32)],
        compiler_params=_cparams(("arbitrary", "arbitrary", "arbitrary")),
        name="attn_diff",
    )(qa, ka, va, t0, t1, dl, og)


def _attn_c_kernel(q_ref, k_ref, v_ref, mk_ref, o_ref, sbuf, mbuf, lbuf, abuf):
    qi = pl.program_id(2)
    lane = lax.broadcasted_iota(jnp.int32, (TQ, LANES), 1)
    cols = lambda u: slice(u * LANES, (u + 1) * LANES)
    _two_pass_attention(GC, lambda u: q_ref[:, cols(u)], cols, lambda u: cols(u // 2), k_ref, v_ref, qi,
                        lambda u: mk_ref[...], None, sbuf, mbuf, lbuf, abuf)
    for hp in range(GC // 2):
        o_ref[:, cols(hp)] = jnp.where(lane < VC, _normalised(2 * hp, lbuf, abuf),
                                       _normalised(2 * hp + 1, lbuf, abuf)).astype(bf16)


def _attn_c(qc, kc, vc, mk, nb, seq):
    n = qc.shape[0]
    nq = seq // TQ
    w = GC * LANES
    wv = GC * VC
    return pl.pallas_call(
        _attn_c_kernel,
        grid=(nb, HC // GC, nq),
        in_specs=[pl.BlockSpec((TQ, w), lambda b, g, i: (b * nq + i, g)),
                  pl.BlockSpec((seq, w), lambda b, g, i: (b, g)),
                  pl.BlockSpec((seq, wv), lambda b, g, i: (b, g)),
                  _const_spec((TQ, TK))],
        out_specs=pl.BlockSpec((TQ, wv), lambda b, g, i: (b * nq + i, g)),
        out_shape=jax.ShapeDtypeStruct((n, MIX), bf16),
        scratch_shapes=[pltpu.VMEM((GC, seq // TK, TQ, TK), f32),
                        pltpu.VMEM((GC, TQ, LANES), f32), pltpu.VMEM((GC, TQ, LANES), f32),
                        pltpu.VMEM((GC, TQ, LANES), f32)],
        compiler_params=_cparams(("arbitrary", "arbitrary", "arbitrary")),
        name="attn_latent",
    )(qc, kc, vc, mk)


def _rope_apply(x, cosf, sinf, lane):
    w = x.shape[-1]
    partner = jnp.where((lane % LANES) < NOPE + ROPE // 2,
                        pltpu.roll(x, w - ROPE // 2, 1), pltpu.roll(x, ROPE // 2, 1))
    return x * cosf + partner * sinf


def _mla_prep_kernel(xc_ref, cos_ref, sin_ref, glq_ref, glkv_ref, wq_ref, wkv_ref, gq_ref, gk_ref, gr_ref,
                     sd_ref, ex_ref, q_ref, k_ref, v_ref):
    cq = _rms(xc_ref[:, 0:Q_LORA], glq_ref[...]).astype(bf16)
    ckv = _rms(xc_ref[:, Q_LORA:Q_LORA + KV_LORA], glkv_ref[...]).astype(bf16)
    cos8 = jnp.concatenate([cos_ref[...]] * HC, axis=-1)
    sin8 = jnp.concatenate([sin_ref[...]] * HC, axis=-1)
    lane8 = lax.broadcasted_iota(jnp.int32, (TM, HC * LANES), 1)

    hw = HC * LANES
    q2 = jnp.dot(cq, wq_ref[...], preferred_element_type=f32)
    rinv = _seg_rinv(q2[:, :hw], sd_ref, ex_ref)
    q_ref[...] = ((q2[:, :hw] * rinv * gq_ref[:, :hw]) * cos8
                  + (q2[:, hw:] * rinv * gq_ref[:, hw:]) * sin8).astype(bf16)

    kv = jnp.dot(ckv, wkv_ref[...], preferred_element_type=f32)
    v_ref[...] = kv[:, HC * LANES:].astype(bf16)
    xr = xc_ref[:, Q_LORA + KV_LORA:Q_LORA + KV_LORA + LANES]
    kr = xr * lax.rsqrt(jnp.sum(xr * xr, axis=-1, keepdims=True) * (1.0 / ROPE) + EPS) * gr_ref[...]
    kr = pltpu.roll(kr, NOPE, 1)
    lane1 = lax.broadcasted_iota(jnp.int32, (TM, LANES), 1)
    kr = _rope_apply(kr, cos_ref[...], sin_ref[...], lane1)
    for h in range(HC):
        cols = slice(h * LANES, (h + 1) * LANES)
        kn = kv[:, cols]
        kn = kn * lax.rsqrt(jnp.sum(kn * kn, axis=-1, keepdims=True) * (1.0 / NOPE) + EPS) * gk_ref[...]
        k_ref[:, cols] = (kn + kr).astype(bf16)


def _mla_prep(xc, cosf, sinf, glq, glkv, wq, wkv, gq, gk, gr, sd, ex):
    n = xc.shape[0]
    row = lambda w_: pl.BlockSpec((TM, w_), lambda i: (i, 0))
    return pl.pallas_call(
        _mla_prep_kernel,
        grid=(n // TM,),
        in_specs=[row(MIX), row(LANES), row(LANES), _const_spec((1, Q_LORA)), _const_spec((1, KV_LORA)),
                  _const_spec((Q_LORA, 2 * HC * LANES)), _const_spec((KV_LORA, HC * LANES + MIX)),
                  _const_spec((1, 2 * HC * LANES)), _const_spec((1, LANES)), _const_spec((1, LANES)),
                  _const_spec((HC * LANES, LANES)), _const_spec((2 * LANES, HC * LANES))],
        out_specs=[row(HC * LANES), row(HC * LANES), row(MIX)],
        out_shape=[jax.ShapeDtypeStruct((n, HC * LANES), bf16), jax.ShapeDtypeStruct((n, HC * LANES), bf16),
                   jax.ShapeDtypeStruct((n, MIX), bf16)],
        compiler_params=_cparams(("arbitrary",)),
        name="mla_prep",
    )(xc, cosf, sinf, glq, glkv, wq, wkv, gq, gk, gr, sd, ex)


def _merge_kernel(oa_ref, ob_ref, oc_ref, gt_ref, x_ref, g1_ref, sh_ref, sc_ref, ng_ref, wb_ref, wo_ref,
                  wr_ref, br_ref, x1_ref, h2_ref, meta_ref, cnt_ref, run_ref):
    i = pl.program_id(0)

    @pl.when(i == 0)
    def _():
        run_ref[...] = jnp.zeros_like(run_ref)

    merged = jnp.zeros((TM, D), f32)
    for j, o_ref in enumerate((oa_ref, ob_ref, oc_ref)):
        pj = jnp.dot(o_ref[...], wb_ref[j], preferred_element_type=f32)
        merged = merged + gt_ref[:, j * D:(j + 1) * D].astype(f32) * pj
    y = jnp.dot(merged.astype(bf16), wo_ref[...], preferred_element_type=f32)
    x1 = x_ref[...] + g1_ref[0] * y
    x1_ref[...] = x1
    h2 = _rms(x1, ng_ref[...]) * (1.0 + sc_ref[0]) + sh_ref[0]
    hr = h2.astype(bf16).astype(f32)
    lo = lax.shift_right_logical(lax.bitcast_convert_type(hr[:, 0:D // 2], jnp.uint32), jnp.uint32(16))
    hi = lax.bitcast_convert_type(hr[:, D // 2:D], jnp.uint32) & jnp.uint32(0xFFFF0000)
    words = lax.bitcast_convert_type(lo | hi, f32)
    for s in range(PACK_SLABS):
        h2_ref[pl.ds(s, TM, stride=PACK_SLABS), :] = words[:, s * LANES:(s + 1) * LANES]

    h_hi = h2.astype(bf16)
    h_lo = (h2 - h_hi.astype(f32)).astype(bf16)
    t = jnp.dot(h_hi, wr_ref[...], preferred_element_type=f32)
    lg = (t[:, :LANES] + t[:, LANES:] + jnp.dot(h_lo, wr_ref[:, 0:LANES], preferred_element_type=f32)
          + br_ref[...])
    lane = lax.broadcasted_iota(jnp.int32, (TM, LANES), 1)
    n_e = N_GROUPS * N_EXP
    gmask = (lane >= n_e) & (lane < n_e + N_GROUPS)
    gl = jnp.where(gmask, lg, NEG)
    gmax = jnp.max(gl, axis=-1, keepdims=True)
    g_w = 1.0 / jnp.sum(jnp.where(gmask, jnp.exp(gl - gmax), 0.0), axis=-1, keepdims=True)
    gidx = jnp.min(jnp.where(gl == gmax, lane - n_e, LANES), axis=-1, keepdims=True)
    emask = (lane < n_e) & ((lane // N_EXP) == gidx)
    el = jnp.where(emask, lg, NEG)
    m1 = jnp.max(el, axis=-1, keepdims=True)
    i1 = jnp.min(jnp.where(el == m1, lane, LANES), axis=-1, keepdims=True)
    el2 = jnp.where(lane == i1, NEG, el)
    m2 = jnp.max(el2, axis=-1, keepdims=True)
    i2 = jnp.min(jnp.where(el2 == m2, lane, LANES), axis=-1, keepdims=True)
    t = jnp.exp(m2 - m1)
    w1 = g_w / (1.0 + t)
    w2 = g_w * t / (1.0 + t)
    lo = jnp.minimum(i1, i2)
    hi = jnp.maximum(i1, i2)
    w_lo = jnp.where(i1 < i2, w1, w2)
    w_hi = jnp.where(i1 < i2, w2, w1)
    bucket = gidx * (N_EXP * N_EXP) + (lo % N_EXP) * N_EXP + (hi % N_EXP)

    lane2 = lax.broadcasted_iota(jnp.int32, (TM, 2 * LANES), 1)
    onehot = (lane2 == bucket).astype(f32)
    rr = lax.broadcasted_iota(jnp.int32, (TM, TM), 0)
    cc = lax.broadcasted_iota(jnp.int32, (TM, TM), 1)
    tri = (cc < rr).astype(bf16)
    before = jnp.dot(tri, onehot.astype(bf16), preferred_element_type=f32) + run_ref[...]
    rank = jnp.sum(onehot * before, axis=-1, keepdims=True)
    run_ref[...] = run_ref[...] + jnp.sum(onehot, axis=0, keepdims=True)
    cnt_ref[...] = run_ref[...]

    meta = (jnp.where(lane == i1, w1, 0.0) + jnp.where(lane == i2, w2, 0.0)
            + jnp.where(lane == 32, bucket.astype(f32), 0.0) + jnp.where(lane == 33, rank, 0.0)
            + jnp.where(lane == 34, w_lo, 0.0) + jnp.where(lane == 35, w_hi, 0.0))
    meta_ref[...] = meta


def _merge(oa, ob, oc, gt, x, g1, shift, scale, ng, wb, wo, wr, br, seq):
    n = x.shape[0]
    spb = seq // TM
    row = lambda w_: pl.BlockSpec((TM, w_), lambda i: (i, 0))
    modspec = pl.BlockSpec((1, 1, D), lambda i: (i // spb, 0, 0))
    return pl.pallas_call(
        _merge_kernel,
        grid=(n // TM,),
        in_specs=[row(MIX), row(MIX), row(MIX), row(3 * D), row(D), modspec, modspec, modspec,
                  _const_spec((1, D)), _const_spec((3, MIX, D)), _const_spec((D, D)),
                  _const_spec((D, 2 * LANES)), _const_spec((1, LANES))],
        out_specs=[row(D), pl.BlockSpec((TM * PACK_SLABS, LANES), lambda i: (i, 0)), row(LANES),
                   _const_spec((1, 2 * LANES))],
        out_shape=[jax.ShapeDtypeStruct((n, D), f32), jax.ShapeDtypeStruct((n * PACK_SLABS, LANES), f32),
                   jax.ShapeDtypeStruct((n, LANES), f32), jax.ShapeDtypeStruct((1, 2 * LANES), f32)],
        scratch_shapes=[pltpu.VMEM((1, 2 * LANES), f32)],
        compiler_params=_cparams(("arbitrary",)),
        name="merge_route",
    )(oa, ob, oc, gt, x, g1, shift, scale, ng, wb, wo, wr, br)


TE = 128
N_BUCKETS = N_GROUPS * N_EXP * N_EXP
N_PAIRS = N_GROUPS * (N_EXP * (N_EXP - 1) // 2)


def _num_tiles(n):
    return n // TE + N_PAIRS


TP = 2048


def _tables_kernel(meta_ref, cnt_ref, pos_ref, tt_ref):
    cnt = cnt_ref[...]
    ntile = jnp.floor((cnt + (TE - 1)) * (1.0 / TE))
    bi = lax.broadcasted_iota(jnp.int32, (N_BUCKETS, N_BUCKETS), 0)
    bj = lax.broadcasted_iota(jnp.int32, (N_BUCKETS, N_BUCKETS), 1)
    upper = (bi <= bj).astype(bf16)
    incl = jnp.dot(jnp.broadcast_to(ntile, (8, N_BUCKETS)).astype(bf16), upper, preferred_element_type=f32)[0:1]
    excl = incl - ntile

    meta = meta_ref[...]
    tp = meta.shape[0]
    lane = lax.broadcasted_iota(jnp.int32, (tp, LANES), 1)
    bucket = jnp.sum(jnp.where(lane == 32, meta, 0.0), axis=-1, keepdims=True).astype(jnp.int32)
    rank = jnp.sum(jnp.where(lane == 33, meta, 0.0), axis=-1, keepdims=True)
    lane2 = lax.broadcasted_iota(jnp.int32, (tp, N_BUCKETS), 1)
    first_tile = jnp.sum(jnp.where(lane2 == bucket, excl, 0.0), axis=-1, keepdims=True)
    pos_ref[...] = (first_tile * TE + rank).astype(jnp.int32)

    @pl.when(pl.program_id(0) == 0)
    def _():
        tau = lax.broadcasted_iota(jnp.int32, (N_BUCKETS, 1), 0).astype(f32)
        lane_b = lax.broadcasted_iota(jnp.int32, (1, N_BUCKETS), 1).astype(f32)
        tb = jnp.sum((incl <= tau).astype(f32), axis=-1, keepdims=True)
        last_b = jnp.max(jnp.where(cnt > 0, lane_b, 0.0), axis=-1, keepdims=True)
        own = lane_b == tb
        in_bucket = jnp.sum(jnp.where(own, cnt, 0.0), axis=-1, keepdims=True)
        first = jnp.sum(jnp.where(own, excl, 0.0), axis=-1, keepdims=True)
        valid = jnp.clip(in_bucket - (tau - first) * TE, 0.0, float(TE)).astype(jnp.int32)
        tbi = jnp.minimum(tb, last_b).astype(jnp.int32)
        grp = lax.shift_right_logical(tbi, 6) * N_EXP
        ea = grp + (lax.shift_right_logical(tbi, 3) & 7)
        eb = grp + (tbi & 7)
        total = jnp.sum(ntile, axis=-1, keepdims=True).astype(jnp.int32)
        lane_t = lax.broadcasted_iota(jnp.int32, (N_BUCKETS, LANES), 1)
        tt_ref[...] = jnp.where(lane_t == 0, ea, jnp.where(lane_t == 1, eb, jnp.where(
            lane_t == 2, total, jnp.where(lane_t == 3, valid, 0))))


def _tables(meta, cnt):
    n = meta.shape[0]
    tp = math.gcd(n, TP)
    assert _num_tiles(n) <= N_BUCKETS
    return pl.pallas_call(
        _tables_kernel,
        grid=(n // tp,),
        in_specs=[pl.BlockSpec((tp, LANES), lambda i: (i, 0)), _const_spec((1, N_BUCKETS))],
        out_specs=[pl.BlockSpec((tp, 1), lambda i: (i, 0)), _const_spec((N_BUCKETS, LANES))],
        out_shape=[jax.ShapeDtypeStruct((n, 1), jnp.int32), jax.ShapeDtypeStruct((N_BUCKETS, LANES), jnp.int32)],
        compiler_params=_cparams(("arbitrary",)),
        name="moe_tables",
    )(meta, cnt)


def _plan_kernel(pos_ref, tok_ref, *, n, nslot):
    def fill(s, _):
        tok_ref[s] = -1
        return 0

    lax.fori_loop(0, nslot, fill, 0, unroll=16)

    def place(t, _):
        tok_ref[pos_ref[t]] = t
        return 0

    lax.fori_loop(0, n, place, 0, unroll=16)


def _plan(pos):
    n = pos.shape[0]
    nslot = _num_tiles(n) * TE
    smem = pl.BlockSpec(memory_space=pltpu.SMEM)
    return pl.pallas_call(
        functools.partial(_plan_kernel, n=n, nslot=nslot),
        in_specs=[smem], out_specs=smem,
        out_shape=jax.ShapeDtypeStruct((nslot,), jnp.int32),
        name="moe_plan",
    )(pos)


SCATTER_GROUP = 32


def _moe_kernel(tea_ref, teb_ref, nt_ref, tok_ref, tc_ref, h_ref, m_ref, wg_ref, wu_ref, wd_ref, y_hbm,
                xs, ms, ob0, ob1, ssem, *, n):
    i = pl.program_id(0)
    nt = nt_ref[0]
    obufs = (ob0, ob1)

    def scatter_copy(dst_tok, sl, r):
        dst = y_hbm.at[pl.ds(pl.multiple_of(dst_tok * Y_PITCH, Y_PITCH), Y_PITCH), :]
        return pltpu.make_async_copy(obufs[sl].at[pl.ds(r * O_PITCH, Y_PITCH), :], dst, ssem.at[sl])

    def real_slots(tile):
        return jnp.where(tile >= 0, tc_ref[jnp.maximum(tile, 0)], 0)

    def scatter_groups(tile, fn):
        cnt = real_slots(tile)
        for g in range(TE // SCATTER_GROUP):
            @pl.when(cnt > g * SCATTER_GROUP)
            def _(g=g):
                for r in range(g * SCATTER_GROUP, (g + 1) * SCATTER_GROUP):
                    fn(r)

    def start_scatters(tile, sl):
        base = jnp.maximum(tile, 0) * TE

        def start(r):
            t = tok_ref[base + r]
            scatter_copy(jnp.where(t < 0, n + sl * TE + r, t), sl, r).start(priority=r % 2)

        scatter_groups(tile, start)

    def wait_scatters(tile, sl):
        scatter_groups(tile, lambda r: scatter_copy(n, sl, r).wait())

    @pl.when(i == 0)
    def _():
        for sl in range(2):
            obufs[sl][...] = jnp.zeros_like(obufs[sl])
            spare = pltpu.make_async_copy(obufs[sl].at[pl.ds(0, TE * Y_PITCH), :],
                                          y_hbm.at[pl.ds((n + sl * TE) * Y_PITCH, TE * Y_PITCH), :], ssem.at[sl])
            spare.start()
            spare.wait()

    def step(s):
        o = 1 - s

        wait_scatters(i - 2, s)
        start_scatters(i - 1, o)

        for r in range(TE):
            t = jnp.maximum(tok_ref[i * TE + r], 0)
            xs[pl.ds(r * PACK_SLABS, PACK_SLABS), :] = h_ref[pl.ds(pl.multiple_of(t * PACK_SLABS, PACK_SLABS),
                                                                   PACK_SLABS), :]
            ms[pl.ds(r, 1), :] = m_ref[pl.ds(t, 1), :]
        slab = lambda k: xs[pl.ds(k, TE, stride=PACK_SLABS), :]
        words = lax.bitcast_convert_type(jnp.concatenate([slab(k) for k in range(PACK_SLABS)], axis=-1),
                                         jnp.uint32)
        lo = lax.bitcast_convert_type(lax.shift_left(words, jnp.uint32(16)), f32)
        hi = lax.bitcast_convert_type(words & jnp.uint32(0xFFFF0000), f32)
        hb = jnp.concatenate([lo, hi], axis=-1).astype(bf16)
        meta = ms[...]
        lane = lax.broadcasted_iota(jnp.int32, (TE, LANES), 1)
        w_lo = jnp.sum(jnp.where(lane == 34, meta, 0.0), axis=-1, keepdims=True)
        w_hi = jnp.sum(jnp.where(lane == 35, meta, 0.0), axis=-1, keepdims=True)

        def ffn(e, wcol):
            a = jnp.dot(hb, wg_ref[0, 0, e], preferred_element_type=f32)
            u = jnp.dot(hb, wu_ref[0, 0, e], preferred_element_type=f32)
            act = (a * jax.nn.sigmoid(a)) * u * wcol
            return jnp.dot(act.astype(bf16), wd_ref[0, 0, e], preferred_element_type=f32)

        out = ffn(lax.rem(tea_ref[i], N_EXP), w_lo) + ffn(lax.rem(teb_ref[i], N_EXP), w_hi)
        for k in range(SLABS):
            obufs[s][pl.ds(k, TE, stride=O_PITCH), :] = out[:, k * LANES:(k + 1) * LANES]

        @pl.when(i == nt - 1)
        def _():
            start_scatters(i, s)
            wait_scatters(i - 1, o)
            wait_scatters(i, s)

    for s in range(2):
        @pl.when((i < nt) & (lax.rem(i, 2) == s))
        def _(s=s):
            step(s)


def _moe(h2ext, meta, tok, tea, teb, nt, tc, wg, wu, wd, layer):
    n = h2ext.shape[0] // PACK_SLABS
    ntmax = _num_tiles(n)
    once = pl.Buffered(1)
    wspec_in = pl.BlockSpec((1, 1, N_EXP, D, F_EXP),
                            lambda i, tea, teb, nt, tok, tc: (layer, tea[i] // N_EXP, 0, 0, 0), pipeline_mode=once)
    wspec_out = pl.BlockSpec((1, 1, N_EXP, F_EXP, D),
                             lambda i, tea, teb, nt, tok, tc: (layer, tea[i] // N_EXP, 0, 0, 0), pipeline_mode=once)
    hspec = pl.BlockSpec((n * PACK_SLABS, LANES), lambda i, tea, teb, nt, tok, tc: (0, 0), pipeline_mode=once)
    mspec = pl.BlockSpec((n, LANES), lambda i, tea, teb, nt, tok, tc: (0, 0), pipeline_mode=once)
    return pl.pallas_call(
        functools.partial(_moe_kernel, n=n),
        grid_spec=pltpu.PrefetchScalarGridSpec(
            num_scalar_prefetch=5,
            grid=(ntmax,),
            in_specs=[hspec, mspec, wspec_in, wspec_in, wspec_out],
            out_specs=pl.BlockSpec(memory_space=pl.ANY),
            scratch_shapes=[pltpu.VMEM((TE * PACK_SLABS, LANES), f32), pltpu.VMEM((TE, LANES), f32)] + [
                pltpu.VMEM((TE * O_PITCH, LANES), f32)] * 2 + [pltpu.SemaphoreType.DMA((2,))]),
        out_shape=jax.ShapeDtypeStruct(((n + 2 * TE) * Y_PITCH, LANES), f32),
        compiler_params=_cparams(("arbitrary",), vmem=60 * 1024 * 1024),
        name="moe_sparse",
    )(tea, teb, nt, tok, tc, h2ext, meta, wg, wu, wd)


def _final_kernel(x_ref, y_ref, g_ref, o_ref):
    o_ref[...] = x_ref[...] + g_ref[0] * _rows_from_token_major(y_ref, x_ref.shape[0])


def _final_combine(x1, y, g2, seq):
    n = x1.shape[0]
    t = 512
    spb = seq // t
    row = pl.BlockSpec((t, D), lambda i: (i, 0))
    return pl.pallas_call(
        _final_kernel,
        grid=(n // t,),
        in_specs=[row, pl.BlockSpec((t * Y_PITCH, LANES), lambda i: (i, 0)),
                  pl.BlockSpec((1, 1, D), lambda i: (i // spb, 0, 0))],
        out_specs=row,
        out_shape=jax.ShapeDtypeStruct((n, D), f32),
        compiler_params=_cparams(("arbitrary",)),
        name="final_combine",
    )(x1, y, g2)


def _seg_matrices(width, segs):
    sd = np.zeros((width, LANES), np.float32)
    ex = np.zeros((LANES, width), np.float32)
    for j, (s, ln) in enumerate(segs):
        sd[s:s + ln, j] = 1.0 / ln
        ex[j, s:s + ln] = 1.0
    return jnp.asarray(sd, bf16), jnp.asarray(np.concatenate([ex, ex], axis=0), bf16)


def _head_pad(w, heads, per_head, keep):
    k = w.shape[0]
    w3 = w.reshape(k, heads, per_head)[:, :, :keep]
    return jnp.pad(w3, ((0, 0), (0, 0), (0, LANES - keep))).reshape(k, heads * LANES)


def kernel(x, c, positions, w_ada, b_ada, norm_g, w_in, diff_qk_g, diff_lambda, diff_out_g, rel_bias, sgu_v_g, sgu_w, sgu_b, mla_lat_g, mla_w_uq, mla_w_ukv, mla_qk_g, w_branch, w_out, router_g_w, router_g_b, router_e_w, router_e_b, w_e_gate, w_e_up, w_e_down):
    nb, seq, _ = x.shape
    n = nb * seq
    assert seq % TQ == 0 and seq % TM == 0 and x.shape[2] == D

    mod = _ada(c, w_ada, b_ada)
    cosf, sinf = _rope_tables(positions)
    t0, t1, mk = _bias_tiles(rel_bias)

    sd_a, ex_a = _seg_matrices(MIX, [(s * 64, 64) for s in range(8)])
    segs_q = []
    for h in range(HC):
        segs_q += [(h * LANES, NOPE), (h * LANES + NOPE, ROPE)]
    sd_q, ex_q = _seg_matrices(HC * LANES, segs_q)
    ne = N_GROUPS * N_EXP
    zpad = lambda a, w_: jnp.pad(a, ((0, 0), (0, w_ - a.shape[1])))

    w_main = w_in[:, :, :MAIN_W].astype(bf16)
    w_gates = w_in[:, :, MAIN_W:].astype(bf16)
    wg_all, wu_all, wd_all = w_e_gate.astype(bf16), w_e_up.astype(bf16), w_e_down.astype(bf16)

    xcur = x.reshape(n, D)
    y_prev, g2_prev = None, None
    for l in range(DEPTH):
        m3 = mod[l].reshape(nb, 1, 6 * D)
        shift1, scale1, gate1, shift2, scale2, gate2 = [m3[:, :, k * D:(k + 1) * D] for k in range(6)]
        lambda_init = LAMBDA_INIT_BASE - LAMBDA_INIT_SCALE * math.exp(-LAMBDA_INIT_DECAY * l)

        gq = (jnp.tile(diff_qk_g[l, 0], 8) * (64 ** -0.5 * LOG2E)).reshape(1, MIX)
        gk = jnp.tile(diff_qk_g[l, 1], 8).reshape(1, MIX)
        outs = _inproj(xcur, y_prev, g2_prev, shift1, scale1, norm_g[l, 0].reshape(1, D), w_main, w_gates, l,
                       gq, gk, sd_a, ex_a, sgu_v_g[l].reshape(1, MIX), sgu_w[l], jnp.transpose(sgu_b[l]), seq)
        if y_prev is not None:
            xcur = outs[0]
            outs = outs[1:]
        qa, ka, va, ob, xc, gt = outs

        oa = _attn_a(qa, ka, va, t0, t1, diff_lambda[l], diff_out_g[l].reshape(1, LANES), nb, seq, lambda_init)

        qkg = mla_qk_g[l]
        swap = np.concatenate([np.arange(NOPE), NOPE + ROPE // 2 + np.arange(ROPE // 2), NOPE + np.arange(ROPE // 2)])
        wq3 = mla_w_uq[l].reshape(Q_LORA, HC, NOPE + ROPE)
        wq = jnp.concatenate([_head_pad(mla_w_uq[l], HC, NOPE + ROPE, NOPE + ROPE),
                              _head_pad(wq3[:, :, swap].reshape(Q_LORA, -1), HC, NOPE + ROPE, NOPE + ROPE)],
                             axis=1).astype(bf16)
        wkv3 = mla_w_ukv[l].reshape(KV_LORA, HC, NOPE + VC)
        wk = jnp.pad(wkv3[:, :, :NOPE], ((0, 0), (0, 0), (0, LANES - NOPE))).reshape(KV_LORA, HC * LANES)
        wv = wkv3[:, :, NOPE:].reshape(KV_LORA, HC * VC)
        wkv = jnp.concatenate([wk, wv], axis=1).astype(bf16)
        gq_c = jnp.concatenate([jnp.tile(jnp.pad(g_, (0, LANES - NOPE - ROPE)), HC) for g_ in (qkg[0], qkg[0][swap])]
                               ).reshape(1, 2 * HC * LANES) * ((NOPE + ROPE) ** -0.5 * LOG2E)
        gk_c = jnp.pad(qkg[1, :NOPE], (0, LANES - NOPE)).reshape(1, LANES)
        gr_c = jnp.pad(qkg[1, NOPE:], (0, LANES - ROPE)).reshape(1, LANES)
        qc, kc, vc = _mla_prep(xc, cosf, sinf, mla_lat_g[l, :Q_LORA].reshape(1, Q_LORA),
                               mla_lat_g[l, Q_LORA:].reshape(1, KV_LORA), wq, wkv, gq_c, gk_c, gr_c, sd_q, ex_q)
        oc = _attn_c(qc, kc, vc, mk, nb, seq)

        wr = zpad(jnp.concatenate([router_e_w[l], router_g_w[l]], axis=1), LANES)
        wr_hi = wr.astype(bf16)
        wr = jnp.concatenate([wr_hi, (wr - wr_hi.astype(f32)).astype(bf16)], axis=1)
        br = zpad(jnp.concatenate([router_e_b[l], router_g_b[l]]).reshape(1, ne + N_GROUPS), LANES)
        x1, h2ext, meta, cnt = _merge(oa, ob, oc, gt, xcur, gate1, shift2, scale2, norm_g[l, 1].reshape(1, D),
                                      w_branch[l].astype(bf16), w_out[l].astype(bf16), wr, br, seq)

        pos, tt = _tables(meta, cnt)
        tok = _plan(pos.reshape(n))
        ntmax = _num_tiles(n)
        y_prev = _moe(h2ext, meta, tok, tt[:ntmax, 0], tt[:ntmax, 1], tt[0:1, 2], tt[:ntmax, 3],
                      wg_all, wu_all, wd_all, l)
        g2_prev = gate2
        xcur = x1

    out = _final_combine(xcur, y_prev, g2_prev, seq)
    return out.reshape(nb, seq, D)
```

```python
import functools
import math

import jax
import jax.numpy as jnp
import numpy as np
from jax import lax
from jax.experimental import pallas as pl
from jax.experimental.pallas import tpu as pltpu

f32 = jnp.float32
bf16 = jnp.bfloat16

D = 1024
DEPTH = 4
CHUNK = 64
MIX = 512
HA = 4
HC = 8
Q_LORA, KV_LORA, ROPE = 256, 128, 32
NOPE, VC = 64, 64
N_GROUPS, N_EXP, F_EXP = 4, 8, 256
N_REL_BUCKETS = 32
ROPE_THETA = 10000.0
LAMBDA_INIT_BASE, LAMBDA_INIT_SCALE, LAMBDA_INIT_DECAY = 0.8, 0.6, 0.3
EPS = 1e-6
NEG = -1e30

XC_OFF = 2560
MAIN_W = XC_OFF + Q_LORA + KV_LORA + ROPE

TM = 512
TQ = 256
TK = 256
LANES = 128
VMEM_LIMIT = 56 * 1024 * 1024
VMEM_LIMIT_RESIDENT = 60 * 1024 * 1024
SLABS = D // LANES
PACK_SLABS = SLABS // 2
Y_PITCH = SLABS
O_PITCH = Y_PITCH + 1


def _cparams(sem, vmem=VMEM_LIMIT):
    return pltpu.CompilerParams(dimension_semantics=sem, vmem_limit_bytes=vmem)


def _const_spec(shape):
    nd = len(shape)
    return pl.BlockSpec(shape, lambda *_: (0,) * nd)


def _rms(x, g_row):
    return x * lax.rsqrt(jnp.mean(x * x, axis=-1, keepdims=True) + EPS) * g_row


def _rows_from_token_major(y_ref, rows):
    return jnp.concatenate([y_ref[pl.ds(k, rows, stride=Y_PITCH), :] for k in range(SLABS)], axis=-1)


def _seg_rinv(x, sd_ref, ex_ref):
    ss = jnp.dot((x * x).astype(bf16), sd_ref[...], preferred_element_type=f32)
    hi = ss.astype(bf16)
    lo = (ss - hi.astype(f32)).astype(bf16)
    full = jnp.dot(jnp.concatenate([hi, lo], axis=-1), ex_ref[...], preferred_element_type=f32)
    return lax.rsqrt(full + EPS)


def _ada_kernel(c_ref, w_ref, b_ref, o_ref):
    c = c_ref[...]
    ca = (c * jax.nn.sigmoid(c)).astype(bf16)
    o_ref[0] = jnp.dot(ca, w_ref[0].astype(bf16), preferred_element_type=f32) + b_ref[0]


def _ada(c, w_ada, b_ada):
    nb = c.shape[0]
    tn = 1024
    return pl.pallas_call(
        _ada_kernel,
        grid=(DEPTH, 6 * D // tn),
        in_specs=[pl.BlockSpec((nb, D), lambda l, j: (0, 0)),
                  pl.BlockSpec((1, D, tn), lambda l, j: (l, 0, j)),
                  pl.BlockSpec((1, 1, tn), lambda l, j: (l, 0, j))],
        out_specs=pl.BlockSpec((1, nb, tn), lambda l, j: (l, 0, j)),
        out_shape=jax.ShapeDtypeStruct((DEPTH, nb, 6 * D), f32),
        compiler_params=_cparams(("arbitrary", "arbitrary")),
        name="ada_mod",
    )(c, w_ada, b_ada.reshape(DEPTH, 1, 6 * D))


def _rope_kernel(pos_ref, invf_ref, cos_ref, sin_ref):
    ang = pos_ref[...].astype(f32) * invf_ref[...]
    lane = lax.broadcasted_iota(jnp.int32, ang.shape, 1)
    rot = (lane >= NOPE) & (lane < NOPE + ROPE)
    first = lane < NOPE + ROPE // 2
    cos_ref[...] = jnp.where(rot, jnp.cos(ang), 1.0)
    s = jnp.sin(ang)
    sin_ref[...] = jnp.where(rot, jnp.where(first, -s, s), 0.0)


def _rope_tables(positions):
    n = positions.size
    inv_freq = ROPE_THETA ** (-jnp.arange(0, ROPE, 2, dtype=f32) / ROPE)
    invf = jnp.zeros((1, LANES), f32).at[0, NOPE:NOPE + ROPE].set(jnp.tile(inv_freq, 2))
    t = math.gcd(n, 2048)
    return pl.pallas_call(
        _rope_kernel,
        grid=(n // t,),
        in_specs=[pl.BlockSpec((t, 1), lambda i: (i, 0)), _const_spec((1, LANES))],
        out_specs=[pl.BlockSpec((t, LANES), lambda i: (i, 0))] * 2,
        out_shape=[jax.ShapeDtypeStruct((n, LANES), f32)] * 2,
        compiler_params=_cparams(("arbitrary",)),
        name="rope_tables",
    )(positions.reshape(n, 1), invf)


def _t5_bucket(rel):
    n = jnp.abs(rel)
    large = 8 + sum((n >= t).astype(jnp.int32) for t in (12, 16, 23, 32, 46, 64, 91))
    return jnp.where(rel > 0, 16, 0) + jnp.where(n < 8, n, large)


def _bias_kernel(tbl_ref, t0_ref, t1_ref, mk_ref):
    r = lax.broadcasted_iota(jnp.int32, (TQ, TK), 0)
    c = lax.broadcasted_iota(jnp.int32, (TQ, TK), 1)
    allowed = (c // CHUNK) <= (r // CHUNK)
    mk = jnp.where(allowed, 0.0, NEG).astype(f32)
    mk_ref[...] = mk
    b0 = _t5_bucket(c - r)
    b1 = _t5_bucket(c - r - TK)
    for h in range(HA):
        far = tbl_ref[N_REL_BUCKETS // 2 - 1, h]
        t0 = jnp.zeros((TQ, TK), f32)
        t1 = jnp.zeros((TQ, TK), f32)
        for b in range(N_REL_BUCKETS):
            v = (tbl_ref[b, h] - far) * LOG2E
            t0 = jnp.where(b0 == b, v, t0)
            t1 = jnp.where(b1 == b, v, t1)
        t0 = t0 + mk
        t0_ref[h, 0:TQ, :] = t0
        t0_ref[h, TQ:2 * TQ, :] = t0
        t1_ref[h, 0:TQ, :] = t1
        t1_ref[h, TQ:2 * TQ, :] = t1


def _bias_tiles(rel_bias):
    return pl.pallas_call(
        _bias_kernel,
        in_specs=[pl.BlockSpec(memory_space=pltpu.SMEM)],
        out_specs=[pl.BlockSpec(memory_space=pltpu.VMEM)] * 3,
        out_shape=[jax.ShapeDtypeStruct((HA, 2 * TQ, TK), f32),
                   jax.ShapeDtypeStruct((HA, 2 * TQ, TK), f32),
                   jax.ShapeDtypeStruct((TQ, TK), f32)],
        name="bias_tiles",
    )(rel_bias)


def _inproj_kernel(*refs, combine):
    if combine:
        (xa_ref, y_ref, g2_ref, sh_ref, sc_ref, ng_ref, w_ref, wgt_ref, gq_ref, gk_ref, sd_ref, ex_ref,
         vg_ref, ws_ref, bs_ref,
         xo_ref, qa_ref, ka_ref, va_ref, ob_ref, xc_ref, gt_ref) = refs
        x = xa_ref[...] + g2_ref[0] * _rows_from_token_major(y_ref, TM)
        xo_ref[...] = x
    else:
        (xa_ref, sh_ref, sc_ref, ng_ref, w_ref, wgt_ref, gq_ref, gk_ref, sd_ref, ex_ref,
         vg_ref, ws_ref, bs_ref,
         qa_ref, ka_ref, va_ref, ob_ref, xc_ref, gt_ref) = refs
        x = xa_ref[...]
    h = _rms(x, ng_ref[...]) * (1.0 + sc_ref[0]) + sh_ref[0]
    hb = h.astype(bf16)

    q = jnp.dot(hb, w_ref[0, :, 0:MIX], preferred_element_type=f32)
    qa_ref[...] = (q * _seg_rinv(q, sd_ref, ex_ref) * gq_ref[...]).astype(bf16)
    k = jnp.dot(hb, w_ref[0, :, MIX:2 * MIX], preferred_element_type=f32)
    ka_ref[...] = (k * _seg_rinv(k, sd_ref, ex_ref) * gk_ref[...]).astype(bf16)
    va_ref[...] = jnp.dot(hb, w_ref[0, :, 2 * MIX:3 * MIX], preferred_element_type=f32).astype(bf16)

    z = jnp.dot(hb, w_ref[0, :, 3 * MIX:5 * MIX], preferred_element_type=f32)
    z = 0.5 * z * (1.0 + jnp.tanh(math.sqrt(2.0 / math.pi) * (z + 0.044715 * (z * z * z))))
    u = z[:, :MIX]
    v = _rms(z[:, MIX:], vg_ref[...]).astype(bf16)
    ri = lax.broadcasted_iota(jnp.int32, (LANES, LANES), 0)
    ci = lax.broadcasted_iota(jnp.int32, (LANES, LANES), 1)
    allowed = (ci // CHUNK) <= (ri // CHUNK)
    for g in range(4):
        wm = jnp.where(allowed, ws_ref[g], 0.0).astype(bf16)
        bcol = bs_ref[:, g:g + 1]
        cols = slice(g * LANES, (g + 1) * LANES)
        nw = TM // LANES
        vcat = jnp.concatenate([v[wd * LANES:(wd + 1) * LANES, cols] for wd in range(nw)], axis=-1)
        vs = jnp.dot(wm, vcat, preferred_element_type=f32) + bcol
        for wd in range(nw):
            rows = slice(wd * LANES, (wd + 1) * LANES)
            ob_ref[rows, cols] = (u[rows, cols] * vs[:, wd * LANES:(wd + 1) * LANES]).astype(bf16)

    lat = Q_LORA + KV_LORA
    xc = jnp.dot(hb, w_ref[0, :, XC_OFF:MAIN_W], preferred_element_type=f32)
    xc_ref[:, 0:lat] = xc[:, 0:lat]
    xc_ref[:, lat:lat + LANES] = jnp.concatenate([xc[:, lat:lat + ROPE], jnp.zeros((TM, LANES - ROPE), f32)],
                                                 axis=-1)

    for j in range(3):
        gl = jnp.dot(hb, wgt_ref[0, :, j * D:(j + 1) * D], preferred_element_type=f32)
        gt_ref[:, j * D:(j + 1) * D] = jax.nn.sigmoid(gl).astype(bf16)


def _inproj(xa, y, g2, shift, scale, ng, w, wgt, layer, gq, gk, sd, ex, vg, ws, bs, seq):
    n = xa.shape[0]
    spb = seq // TM
    combine = y is not None
    row = lambda w_: pl.BlockSpec((TM, w_), lambda i: (i, 0))
    modspec = pl.BlockSpec((1, 1, D), lambda i: (i // spb, 0, 0))
    in_specs = [row(D)]
    args = [xa]
    if combine:
        in_specs += [pl.BlockSpec((TM * Y_PITCH, LANES), lambda i: (i, 0)), modspec]
        args += [y, g2]
    in_specs += [modspec, modspec, _const_spec((1, D)),
                 pl.BlockSpec((1, D, MAIN_W), lambda i: (layer, 0, 0), pipeline_mode=pl.Buffered(1)),
                 pl.BlockSpec((1, D, 3 * D), lambda i: (layer, 0, 0), pipeline_mode=pl.Buffered(1)),
                 _const_spec((1, MIX)), _const_spec((1, MIX)),
                 _const_spec((MIX, LANES)), _const_spec((2 * LANES, MIX)),
                 _const_spec((1, MIX)), _const_spec((4, LANES, LANES)), _const_spec((LANES, 4))]
    args += [shift, scale, ng, w, wgt, gq, gk, sd, ex, vg, ws, bs]
    out_specs = [row(MIX)] * 4 + [row(MIX), row(3 * D)]
    out_shape = [jax.ShapeDtypeStruct((n, MIX), bf16)] * 4 + [
        jax.ShapeDtypeStruct((n, MIX), f32), jax.ShapeDtypeStruct((n, 3 * D), bf16)]
    if combine:
        out_specs = [row(D)] + out_specs
        out_shape = [jax.ShapeDtypeStruct((n, D), f32)] + out_shape
    return pl.pallas_call(
        functools.partial(_inproj_kernel, combine=combine),
        grid=(n // TM,),
        in_specs=in_specs, out_specs=out_specs, out_shape=out_shape,
        compiler_params=_cparams(("arbitrary",)),
        name="inproj",
    )(*args)


LOG2E = math.log2(math.e)


def _two_pass_attention(nunits, q_of, kcols_of, vcols_of, k_ref, v_ref, qi, diag_bias_of, sub_bias_of,
                        sbuf, mbuf, lbuf, abuf):
    def keys(j, nb):
        return pl.ds(pl.multiple_of(j * TK, TK), nb * TK)

    def score_blocks(j, nb, bias_of):
        for u in range(nunits):
            s = lax.dot_general(q_of(u), k_ref[keys(j, nb), kcols_of(u)], (((1,), (1,)), ((), ())),
                                preferred_element_type=f32)
            if bias_of is not None:
                s = s + bias_of(u)
            m = mbuf[u]
            for b in range(nb):
                sbuf[u, j + b] = s[:, b * TK:(b + 1) * TK]
            for c in range(nb * TK // LANES):
                m = jnp.maximum(m, s[:, c * LANES:(c + 1) * LANES])
            mbuf[u] = m

    def blocks_in_pairs(first, count, fn):
        def pair(jj, _):
            fn(first + 2 * jj, 2)
            return 0

        lax.fori_loop(0, count // 2, pair, 0)

        @pl.when(count % 2 == 1)
        def _():
            fn(first + count - 1, 1)

    for u in range(nunits):
        mbuf[u] = jnp.full(mbuf.shape[1:], NEG, f32)
    if sub_bias_of is None:
        nfar = qi
    else:
        nfar = jnp.maximum(qi - 1, 0)

        @pl.when(qi >= 1)
        def _():
            score_blocks(qi - 1, 1, sub_bias_of)

    blocks_in_pairs(0, nfar, lambda j, nb: score_blocks(j, nb, None))
    score_blocks(qi, 1, diag_bias_of)

    for u in range(nunits):
        mbuf[u] = jnp.broadcast_to(jnp.max(mbuf[u], axis=-1, keepdims=True), mbuf.shape[1:])
        lbuf[u] = jnp.zeros(lbuf.shape[1:], f32)
        abuf[u] = jnp.zeros(abuf.shape[1:], f32)

    def accumulate_blocks(j, nb):
        for u in range(nunits):
            mb = mbuf[u]
            lsum = lbuf[u]
            ps = []
            for b in range(nb):
                s = sbuf[u, j + b]
                for c in range(TK // LANES):
                    p = jnp.exp2(s[:, c * LANES:(c + 1) * LANES] - mb)
                    lsum = lsum + p
                    ps.append(p)
            lbuf[u] = lsum
            p = jnp.concatenate(ps, axis=-1).astype(bf16)
            abuf[u] = abuf[u] + jnp.dot(p, v_ref[keys(j, nb), vcols_of(u)], preferred_element_type=f32)

    blocks_in_pairs(0, qi + 1, accumulate_blocks)


def _normalised(u, lbuf, abuf):
    return abuf[u] / jnp.sum(lbuf[u], axis=-1, keepdims=True)


GA = 4
GC = 8


def _attn_a_kernel(q_ref, k_ref, v_ref, t0_ref, t1_ref, dl_ref, og_ref, o_ref,
                   qstk, sbuf, mbuf, lbuf, abuf, *, lambda_init):
    qi = pl.program_id(2)
    dl = dl_ref[...]
    lam = (jnp.exp(jnp.sum(dl[0:1] * dl[1:2], axis=-1, keepdims=True))
           - jnp.exp(jnp.sum(dl[2:3] * dl[3:4], axis=-1, keepdims=True)) + lambda_init)
    lane = lax.broadcasted_iota(jnp.int32, (TQ, LANES), 1)
    cols = lambda u: slice(u * LANES, (u + 1) * LANES)
    for u in range(GA):
        qh = q_ref[:, cols(u)]
        zero = jnp.zeros_like(qh)
        qstk[u, 0:TQ, :] = jnp.where(lane < 64, qh, zero)
        qstk[u, TQ:2 * TQ, :] = jnp.where(lane >= 64, qh, zero)
    _two_pass_attention(GA, lambda u: qstk[u], cols, cols, k_ref, v_ref, qi,
                        lambda u: t0_ref[u], lambda u: t1_ref[u], sbuf, mbuf, lbuf, abuf)
    for u in range(GA):
        o = _normalised(u, lbuf, abuf)
        oh = o[:TQ] - lam * o[TQ:]
        oh = _rms(oh, og_ref[...]) * (1.0 - lambda_init)
        o_ref[:, cols(u)] = oh.astype(bf16)


def _attn_a(qa, ka, va, t0, t1, dl, og, nb, seq, lambda_init):
    n = qa.shape[0]
    nq = seq // TQ
    w = GA * LANES
    return pl.pallas_call(
        functools.partial(_attn_a_kernel, lambda_init=lambda_init),
        grid=(nb, HA // GA, nq),
        in_specs=[pl.BlockSpec((TQ, w), lambda b, g, i: (b * nq + i, g)),
                  pl.BlockSpec((seq, w), lambda b, g, i: (b, g)),
                  pl.BlockSpec((seq, w), lambda b, g, i: (b, g)),
                  pl.BlockSpec((GA, 2 * TQ, TK), lambda b, g, i: (g, 0, 0)),
                  pl.BlockSpec((GA, 2 * TQ, TK), lambda b, g, i: (g, 0, 0)),
                  _const_spec((4, 64)), _const_spec((1, LANES))],
        out_specs=pl.BlockSpec((TQ, w), lambda b, g, i: (b * nq + i, g)),
        out_shape=jax.ShapeDtypeStruct((n, MIX), bf16),
        scratch_shapes=[pltpu.VMEM((GA, 2 * TQ, LANES), bf16),
                        pltpu.VMEM((GA, seq // TK, 2 * TQ, TK), f32),
                        pltpu.VMEM((GA, 2 * TQ, LANES), f32), pltpu.VMEM((GA, 2 * TQ, LANES), f32),
                        pltpu.VMEM((GA, 2 * TQ, LANES), f32)],
        compiler_params=_cparams(("arbitrary", "arbitrary", "arbitrary")),
        name="attn_diff",
    )(qa, ka, va, t0, t1, dl, og)


def _attn_c_kernel(q_ref, k_ref, v_ref, mk_ref, o_ref, sbuf, mbuf, lbuf, abuf):
    qi = pl.program_id(2)
    lane = lax.broadcasted_iota(jnp.int32, (TQ, LANES), 1)
    cols = lambda u: slice(u * LANES, (u + 1) * LANES)
    _two_pass_attention(GC, lambda u: q_ref[:, cols(u)], cols, lambda u: cols(u // 2), k_ref, v_ref, qi,
                        lambda u: mk_ref[...], None, sbuf, mbuf, lbuf, abuf)
    for hp in range(GC // 2):
        o_ref[:, cols(hp)] = jnp.where(lane < VC, _normalised(2 * hp, lbuf, abuf),
                                       _normalised(2 * hp + 1, lbuf, abuf)).astype(bf16)


def _attn_c(qc, kc, vc, mk, nb, seq):
    n = qc.shape[0]
    nq = seq // TQ
    w = GC * LANES
    wv = GC * VC
    return pl.pallas_call(
        _attn_c_kernel,
        grid=(nb, HC // GC, nq),
        in_specs=[pl.BlockSpec((TQ, w), lambda b, g, i: (b * nq + i, g)),
                  pl.BlockSpec((seq, w), lambda b, g, i: (b, g)),
                  pl.BlockSpec((seq, wv), lambda b, g, i: (b, g)),
                  _const_spec((TQ, TK))],
        out_specs=pl.BlockSpec((TQ, wv), lambda b, g, i: (b * nq + i, g)),
        out_shape=jax.ShapeDtypeStruct((n, MIX), bf16),
        scratch_shapes=[pltpu.VMEM((GC, seq // TK, TQ, TK), f32),
                        pltpu.VMEM((GC, TQ, LANES), f32), pltpu.VMEM((GC, TQ, LANES), f32),
                        pltpu.VMEM((GC, TQ, LANES), f32)],
        compiler_params=_cparams(("arbitrary", "arbitrary", "arbitrary")),
        name="attn_latent",
    )(qc, kc, vc, mk)


def _rope_apply(x, cosf, sinf, lane):
    w = x.shape[-1]
    partner = jnp.where((lane % LANES) < NOPE + ROPE // 2,
                        pltpu.roll(x, w - ROPE // 2, 1), pltpu.roll(x, ROPE // 2, 1))
    return x * cosf + partner * sinf


def _mla_prep_kernel(xc_ref, cos_ref, sin_ref, glq_ref, glkv_ref, wq_ref, wkv_ref, gq_ref, gk_ref, gr_ref,
                     sd_ref, ex_ref, q_ref, k_ref, v_ref):
    cq = _rms(xc_ref[:, 0:Q_LORA], glq_ref[...]).astype(bf16)
    ckv = _rms(xc_ref[:, Q_LORA:Q_LORA + KV_LORA], glkv_ref[...]).astype(bf16)
    cos8 = jnp.concatenate([cos_ref[...]] * HC, axis=-1)
    sin8 = jnp.concatenate([sin_ref[...]] * HC, axis=-1)

    hw = HC * LANES
    q2 = jnp.dot(cq, wq_ref[...], preferred_element_type=f32)
    rinv = _seg_rinv(q2[:, :hw], sd_ref, ex_ref)
    q_ref[...] = ((q2[:, :hw] * rinv * gq_ref[:, :hw]) * cos8
                  + (q2[:, hw:] * rinv * gq_ref[:, hw:]) * sin8).astype(bf16)

    kv = jnp.dot(ckv, wkv_ref[...], preferred_element_type=f32)
    v_ref[...] = kv[:, HC * LANES:].astype(bf16)
    xr = xc_ref[:, Q_LORA + KV_LORA:Q_LORA + KV_LORA + LANES]
    kr = xr * lax.rsqrt(jnp.sum(xr * xr, axis=-1, keepdims=True) * (1.0 / ROPE) + EPS) * gr_ref[...]
    kr = pltpu.roll(kr, NOPE, 1)
    lane1 = lax.broadcasted_iota(jnp.int32, (TM, LANES), 1)
    kr = _rope_apply(kr, cos_ref[...], sin_ref[...], lane1)
    for h in range(HC):
        cols = slice(h * LANES, (h + 1) * LANES)
        kn = kv[:, cols]
        kn = kn * lax.rsqrt(jnp.sum(kn * kn, axis=-1, keepdims=True) * (1.0 / NOPE) + EPS) * gk_ref[...]
        k_ref[:, cols] = (kn + kr).astype(bf16)


def _mla_prep(xc, cosf, sinf, glq, glkv, wq, wkv, gq, gk, gr, sd, ex):
    n = xc.shape[0]
    row = lambda w_: pl.BlockSpec((TM, w_), lambda i: (i, 0))
    return pl.pallas_call(
        _mla_prep_kernel,
        grid=(n // TM,),
        in_specs=[row(MIX), row(LANES), row(LANES), _const_spec((1, Q_LORA)), _const_spec((1, KV_LORA)),
                  _const_spec((Q_LORA, 2 * HC * LANES)), _const_spec((KV_LORA, HC * LANES + MIX)),
                  _const_spec((1, 2 * HC * LANES)), _const_spec((1, LANES)), _const_spec((1, LANES)),
                  _const_spec((HC * LANES, LANES)), _const_spec((2 * LANES, HC * LANES))],
        out_specs=[row(HC * LANES), row(HC * LANES), row(MIX)],
        out_shape=[jax.ShapeDtypeStruct((n, HC * LANES), bf16), jax.ShapeDtypeStruct((n, HC * LANES), bf16),
                   jax.ShapeDtypeStruct((n, MIX), bf16)],
        compiler_params=_cparams(("arbitrary",)),
        name="mla_prep",
    )(xc, cosf, sinf, glq, glkv, wq, wkv, gq, gk, gr, sd, ex)


def _merge_kernel(oa_ref, ob_ref, oc_ref, gt_ref, x_ref, g1_ref, sh_ref, sc_ref, ng_ref, wb_ref, wo_ref,
                  wr_ref, br_ref, x1_ref, h2_ref, meta_ref, cnt_ref, run_ref):
    i = pl.program_id(0)

    @pl.when(i == 0)
    def _():
        run_ref[...] = jnp.zeros_like(run_ref)

    merged = jnp.zeros((TM, D), f32)
    for j, o_ref in enumerate((oa_ref, ob_ref, oc_ref)):
        pj = jnp.dot(o_ref[...], wb_ref[j], preferred_element_type=f32)
        merged = merged + gt_ref[:, j * D:(j + 1) * D].astype(f32) * pj
    y = jnp.dot(merged.astype(bf16), wo_ref[...], preferred_element_type=f32)
    x1 = x_ref[...] + g1_ref[0] * y
    x1_ref[...] = x1
    h2 = _rms(x1, ng_ref[...]) * (1.0 + sc_ref[0]) + sh_ref[0]
    hr = h2.astype(bf16).astype(f32)
    lo = lax.shift_right_logical(lax.bitcast_convert_type(hr[:, 0:D // 2], jnp.uint32), jnp.uint32(16))
    hi = lax.bitcast_convert_type(hr[:, D // 2:D], jnp.uint32) & jnp.uint32(0xFFFF0000)
    words = lax.bitcast_convert_type(lo | hi, f32)
    for s in range(PACK_SLABS):
        h2_ref[pl.ds(s, TM, stride=PACK_SLABS), :] = words[:, s * LANES:(s + 1) * LANES]

    h_hi = h2.astype(bf16)
    h_lo = (h2 - h_hi.astype(f32)).astype(bf16)
    t = jnp.dot(h_hi, wr_ref[...], preferred_element_type=f32)
    lg = (t[:, :LANES] + t[:, LANES:] + jnp.dot(h_lo, wr_ref[:, 0:LANES], preferred_element_type=f32)
          + br_ref[...])
    lane = lax.broadcasted_iota(jnp.int32, (TM, LANES), 1)
    n_e = N_GROUPS * N_EXP
    gmask = (lane >= n_e) & (lane < n_e + N_GROUPS)
    gl = jnp.where(gmask, lg, NEG)
    gmax = jnp.max(gl, axis=-1, keepdims=True)
    g_w = 1.0 / jnp.sum(jnp.where(gmask, jnp.exp(gl - gmax), 0.0), axis=-1, keepdims=True)
    gidx = jnp.min(jnp.where(gl == gmax, lane - n_e, LANES), axis=-1, keepdims=True)
    emask = (lane < n_e) & ((lane // N_EXP) == gidx)
    el = jnp.where(emask, lg, NEG)
    m1 = jnp.max(el, axis=-1, keepdims=True)
    i1 = jnp.min(jnp.where(el == m1, lane, LANES), axis=-1, keepdims=True)
    el2 = jnp.where(lane == i1, NEG, el)
    m2 = jnp.max(el2, axis=-1, keepdims=True)
    i2 = jnp.min(jnp.where(el2 == m2, lane, LANES), axis=-1, keepdims=True)
    t = jnp.exp(m2 - m1)
    w1 = g_w / (1.0 + t)
    w2 = g_w * t / (1.0 + t)
    lo = jnp.minimum(i1, i2)
    hi = jnp.maximum(i1, i2)
    w_lo = jnp.where(i1 < i2, w1, w2)
    w_hi = jnp.where(i1 < i2, w2, w1)
    bucket = gidx * (N_EXP * N_EXP) + (lo % N_EXP) * N_EXP + (hi % N_EXP)

    lane2 = lax.broadcasted_iota(jnp.int32, (TM, 2 * LANES), 1)
    onehot = (lane2 == bucket).astype(f32)
    rr = lax.broadcasted_iota(jnp.int32, (TM, TM), 0)
    cc = lax.broadcasted_iota(jnp.int32, (TM, TM), 1)
    tri = (cc < rr).astype(bf16)
    before = jnp.dot(tri, onehot.astype(bf16), preferred_element_type=f32) + run_ref[...]
    rank = jnp.sum(onehot * before, axis=-1, keepdims=True)
    run_ref[...] = run_ref[...] + jnp.sum(onehot, axis=0, keepdims=True)
    cnt_ref[...] = run_ref[...]

    meta = (jnp.where(lane == i1, w1, 0.0) + jnp.where(lane == i2, w2, 0.0)
            + jnp.where(lane == 32, bucket.astype(f32), 0.0) + jnp.where(lane == 33, rank, 0.0)
            + jnp.where(lane == 34, w_lo, 0.0) + jnp.where(lane == 35, w_hi, 0.0))
    meta_ref[...] = meta


def _merge(oa, ob, oc, gt, x, g1, shift, scale, ng, wb, wo, wr, br, seq):
    n = x.shape[0]
    spb = seq // TM
    row = lambda w_: pl.BlockSpec((TM, w_), lambda i: (i, 0))
    modspec = pl.BlockSpec((1, 1, D), lambda i: (i // spb, 0, 0))
    return pl.pallas_call(
        _merge_kernel,
        grid=(n // TM,),
        in_specs=[row(MIX), row(MIX), row(MIX), row(3 * D), row(D), modspec, modspec, modspec,
                  _const_spec((1, D)), _const_spec((3, MIX, D)), _const_spec((D, D)),
                  _const_spec((D, 2 * LANES)), _const_spec((1, LANES))],
        out_specs=[row(D), pl.BlockSpec((TM * PACK_SLABS, LANES), lambda i: (i, 0)), row(LANES),
                   _const_spec((1, 2 * LANES))],
        out_shape=[jax.ShapeDtypeStruct((n, D), f32), jax.ShapeDtypeStruct((n * PACK_SLABS, LANES), f32),
                   jax.ShapeDtypeStruct((n, LANES), f32), jax.ShapeDtypeStruct((1, 2 * LANES), f32)],
        scratch_shapes=[pltpu.VMEM((1, 2 * LANES), f32)],
        compiler_params=_cparams(("arbitrary",)),
        name="merge_route",
    )(oa, ob, oc, gt, x, g1, shift, scale, ng, wb, wo, wr, br)


TE = 128
N_BUCKETS = N_GROUPS * N_EXP * N_EXP
N_PAIRS = N_GROUPS * (N_EXP * (N_EXP - 1) // 2)


def _num_tiles(n):
    return n // TE + N_PAIRS


TP = 2048


def _tables_kernel(meta_ref, cnt_ref, pos_ref, tt_ref):
    cnt = cnt_ref[...]
    ntile = jnp.floor((cnt + (TE - 1)) * (1.0 / TE))
    bi = lax.broadcasted_iota(jnp.int32, (N_BUCKETS, N_BUCKETS), 0)
    bj = lax.broadcasted_iota(jnp.int32, (N_BUCKETS, N_BUCKETS), 1)
    upper = (bi <= bj).astype(bf16)
    incl = jnp.dot(jnp.broadcast_to(ntile, (8, N_BUCKETS)).astype(bf16), upper, preferred_element_type=f32)[0:1]
    excl = incl - ntile

    meta = meta_ref[...]
    tp = meta.shape[0]
    lane = lax.broadcasted_iota(jnp.int32, (tp, LANES), 1)
    bucket = jnp.sum(jnp.where(lane == 32, meta, 0.0), axis=-1, keepdims=True).astype(jnp.int32)
    rank = jnp.sum(jnp.where(lane == 33, meta, 0.0), axis=-1, keepdims=True)
    lane2 = lax.broadcasted_iota(jnp.int32, (tp, N_BUCKETS), 1)
    first_tile = jnp.sum(jnp.where(lane2 == bucket, excl, 0.0), axis=-1, keepdims=True)
    pos_ref[...] = (first_tile * TE + rank).astype(jnp.int32)

    @pl.when(pl.program_id(0) == 0)
    def _():
        tau = lax.broadcasted_iota(jnp.int32, (N_BUCKETS, 1), 0).astype(f32)
        lane_b = lax.broadcasted_iota(jnp.int32, (1, N_BUCKETS), 1).astype(f32)
        tb = jnp.sum((incl <= tau).astype(f32), axis=-1, keepdims=True)
        last_b = jnp.max(jnp.where(cnt > 0, lane_b, 0.0), axis=-1, keepdims=True)
        own = lane_b == tb
        in_bucket = jnp.sum(jnp.where(own, cnt, 0.0), axis=-1, keepdims=True)
        first = jnp.sum(jnp.where(own, excl, 0.0), axis=-1, keepdims=True)
        valid = jnp.clip(in_bucket - (tau - first) * TE, 0.0, float(TE)).astype(jnp.int32)
        tbi = jnp.minimum(tb, last_b).astype(jnp.int32)
        grp = lax.shift_right_logical(tbi, 6) * N_EXP
        ea = grp + (lax.shift_right_logical(tbi, 3) & 7)
        eb = grp + (tbi & 7)
        total = jnp.sum(ntile, axis=-1, keepdims=True).astype(jnp.int32)
        lane_t = lax.broadcasted_iota(jnp.int32, (N_BUCKETS, LANES), 1)
        tt_ref[...] = jnp.where(lane_t == 0, ea, jnp.where(lane_t == 1, eb, jnp.where(
            lane_t == 2, total, jnp.where(lane_t == 3, valid, 0))))


def _tables(meta, cnt):
    n = meta.shape[0]
    tp = math.gcd(n, TP)
    assert _num_tiles(n) <= N_BUCKETS
    return pl.pallas_call(
        _tables_kernel,
        grid=(n // tp,),
        in_specs=[pl.BlockSpec((tp, LANES), lambda i: (i, 0)), _const_spec((1, N_BUCKETS))],
        out_specs=[pl.BlockSpec((tp, 1), lambda i: (i, 0)), _const_spec((N_BUCKETS, LANES))],
        out_shape=[jax.ShapeDtypeStruct((n, 1), jnp.int32), jax.ShapeDtypeStruct((N_BUCKETS, LANES), jnp.int32)],
        compiler_params=_cparams(("arbitrary",)),
        name="moe_tables",
    )(meta, cnt)


def _plan_kernel(pos_ref, tok_ref, *, n, nslot):
    def fill(s, _):
        tok_ref[s] = -1
        return 0

    lax.fori_loop(0, nslot, fill, 0, unroll=16)

    def place(t, _):
        tok_ref[pos_ref[t]] = t
        return 0

    lax.fori_loop(0, n, place, 0, unroll=16)


def _plan(pos):
    n = pos.shape[0]
    nslot = _num_tiles(n) * TE
    smem = pl.BlockSpec(memory_space=pltpu.SMEM)
    return pl.pallas_call(
        functools.partial(_plan_kernel, n=n, nslot=nslot),
        in_specs=[smem], out_specs=smem,
        out_shape=jax.ShapeDtypeStruct((nslot,), jnp.int32),
        name="moe_plan",
    )(pos)


SCATTER_GROUP = 32


def _moe_kernel(tea_ref, teb_ref, nt_ref, tok_ref, tc_ref, h_ref, m_ref, wg_ref, wu_ref, wd_ref, y_hbm,
                xs, ms, ob0, ob1, ssem, *, n):
    i = pl.program_id(0)
    nt = nt_ref[0]
    obufs = (ob0, ob1)

    def scatter_copy(dst_tok, sl, r):
        dst = y_hbm.at[pl.ds(pl.multiple_of(dst_tok * Y_PITCH, Y_PITCH), Y_PITCH), :]
        return pltpu.make_async_copy(obufs[sl].at[pl.ds(r * O_PITCH, Y_PITCH), :], dst, ssem.at[sl])

    def real_slots(tile):
        return jnp.where(tile >= 0, tc_ref[jnp.maximum(tile, 0)], 0)

    def scatter_groups(tile, fn):
        cnt = real_slots(tile)
        for g in range(TE // SCATTER_GROUP):
            @pl.when(cnt > g * SCATTER_GROUP)
            def _(g=g):
                for r in range(g * SCATTER_GROUP, (g + 1) * SCATTER_GROUP):
                    fn(r)

    def start_scatters(tile, sl):
        base = jnp.maximum(tile, 0) * TE

        def start(r):
            t = tok_ref[base + r]
            scatter_copy(jnp.where(t < 0, n + sl * TE + r, t), sl, r).start(priority=r % 2)

        scatter_groups(tile, start)

    def wait_scatters(tile, sl):
        scatter_groups(tile, lambda r: scatter_copy(n, sl, r).wait())

    @pl.when(i == 0)
    def _():
        for sl in range(2):
            obufs[sl][...] = jnp.zeros_like(obufs[sl])
            spare = pltpu.make_async_copy(obufs[sl].at[pl.ds(0, TE * Y_PITCH), :],
                                          y_hbm.at[pl.ds((n + sl * TE) * Y_PITCH, TE * Y_PITCH), :], ssem.at[sl])
            spare.start()
            spare.wait()

    def step(s):
        o = 1 - s

        wait_scatters(i - 2, s)
        start_scatters(i - 1, o)

        for r in range(TE):
            t = jnp.maximum(tok_ref[i * TE + r], 0)
            xs[pl.ds(r * PACK_SLABS, PACK_SLABS), :] = h_ref[pl.ds(pl.multiple_of(t * PACK_SLABS, PACK_SLABS),
                                                                   PACK_SLABS), :]
            ms[pl.ds(r, 1), :] = m_ref[pl.ds(t, 1), :]
        slab = lambda k: xs[pl.ds(k, TE, stride=PACK_SLABS), :]
        words = lax.bitcast_convert_type(jnp.concatenate([slab(k) for k in range(PACK_SLABS)], axis=-1),
                                         jnp.uint32)
        lo = lax.bitcast_convert_type(lax.shift_left(words, jnp.uint32(16)), f32)
        hi = lax.bitcast_convert_type(words & jnp.uint32(0xFFFF0000), f32)
        hb = jnp.concatenate([lo, hi], axis=-1).astype(bf16)
        meta = ms[...]
        lane = lax.broadcasted_iota(jnp.int32, (TE, LANES), 1)
        w_lo = jnp.sum(jnp.where(lane == 34, meta, 0.0), axis=-1, keepdims=True)
        w_hi = jnp.sum(jnp.where(lane == 35, meta, 0.0), axis=-1, keepdims=True)

        def ffn(e, wcol):
            a = jnp.dot(hb, wg_ref[0, 0, e], preferred_element_type=f32)
            u = jnp.dot(hb, wu_ref[0, 0, e], preferred_element_type=f32)
            act = (a * jax.nn.sigmoid(a)) * u * wcol
            return jnp.dot(act.astype(bf16), wd_ref[0, 0, e], preferred_element_type=f32)

        out = ffn(lax.rem(tea_ref[i], N_EXP), w_lo) + ffn(lax.rem(teb_ref[i], N_EXP), w_hi)
        for k in range(SLABS):
            obufs[s][pl.ds(k, TE, stride=O_PITCH), :] = out[:, k * LANES:(k + 1) * LANES]

        @pl.when(i == nt - 1)
        def _():
            start_scatters(i, s)
            wait_scatters(i - 1, o)
            wait_scatters(i, s)

    for s in range(2):
        @pl.when((i < nt) & (lax.rem(i, 2) == s))
        def _(s=s):
            step(s)


def _moe(h2ext, meta, tok, tea, teb, nt, tc, wg, wu, wd, layer):
    n = h2ext.shape[0] // PACK_SLABS
    ntmax = _num_tiles(n)
    once = pl.Buffered(1)
    wspec_in = pl.BlockSpec((1, 1, N_EXP, D, F_EXP),
                            lambda i, tea, teb, nt, tok, tc: (layer, tea[i] // N_EXP, 0, 0, 0), pipeline_mode=once)
    wspec_out = pl.BlockSpec((1, 1, N_EXP, F_EXP, D),
                             lambda i, tea, teb, nt, tok, tc: (layer, tea[i] // N_EXP, 0, 0, 0), pipeline_mode=once)
    hspec = pl.BlockSpec((n * PACK_SLABS, LANES), lambda i, tea, teb, nt, tok, tc: (0, 0), pipeline_mode=once)
    mspec = pl.BlockSpec((n, LANES), lambda i, tea, teb, nt, tok, tc: (0, 0), pipeline_mode=once)
    return pl.pallas_call(
        functools.partial(_moe_kernel, n=n),
        grid_spec=pltpu.PrefetchScalarGridSpec(
            num_scalar_prefetch=5,
            grid=(ntmax,),
            in_specs=[hspec, mspec, wspec_in, wspec_in, wspec_out],
            out_specs=pl.BlockSpec(memory_space=pl.ANY),
            scratch_shapes=[pltpu.VMEM((TE * PACK_SLABS, LANES), f32), pltpu.VMEM((TE, LANES), f32)] + [
                pltpu.VMEM((TE * O_PITCH, LANES), f32)] * 2 + [pltpu.SemaphoreType.DMA((2,))]),
        out_shape=jax.ShapeDtypeStruct(((n + 2 * TE) * Y_PITCH, LANES), f32),
        compiler_params=_cparams(("arbitrary",), vmem=VMEM_LIMIT_RESIDENT),
        name="moe_sparse",
    )(tea, teb, nt, tok, tc, h2ext, meta, wg, wu, wd)


def _final_kernel(x_ref, y_ref, g_ref, o_ref):
    o_ref[...] = x_ref[...] + g_ref[0] * _rows_from_token_major(y_ref, x_ref.shape[0])


def _final_combine(x1, y, g2, seq):
    n = x1.shape[0]
    t = 512
    spb = seq // t
    row = pl.BlockSpec((t, D), lambda i: (i, 0))
    return pl.pallas_call(
        _final_kernel,
        grid=(n // t,),
        in_specs=[row, pl.BlockSpec((t * Y_PITCH, LANES), lambda i: (i, 0)),
                  pl.BlockSpec((1, 1, D), lambda i: (i // spb, 0, 0))],
        out_specs=row,
        out_shape=jax.ShapeDtypeStruct((n, D), f32),
        compiler_params=_cparams(("arbitrary",)),
        name="final_combine",
    )(x1, y, g2)


def _seg_matrices(width, segs):
    sd = np.zeros((width, LANES), np.float32)
    ex = np.zeros((LANES, width), np.float32)
    for j, (s, ln) in enumerate(segs):
        sd[s:s + ln, j] = 1.0 / ln
        ex[j, s:s + ln] = 1.0
    return jnp.asarray(sd, bf16), jnp.asarray(np.concatenate([ex, ex], axis=0), bf16)


def _head_pad(w, heads, per_head, keep):
    k = w.shape[0]
    w3 = w.reshape(k, heads, per_head)[:, :, :keep]
    return jnp.pad(w3, ((0, 0), (0, 0), (0, LANES - keep))).reshape(k, heads * LANES)


def kernel(x, c, positions, w_ada, b_ada, norm_g, w_in, diff_qk_g, diff_lambda, diff_out_g, rel_bias, sgu_v_g, sgu_w, sgu_b, mla_lat_g, mla_w_uq, mla_w_ukv, mla_qk_g, w_branch, w_out, router_g_w, router_g_b, router_e_w, router_e_b, w_e_gate, w_e_up, w_e_down):
    nb, seq, _ = x.shape
    n = nb * seq
    assert seq % TQ == 0 and seq % TM == 0 and x.shape[2] == D

    mod = _ada(c, w_ada, b_ada)
    cosf, sinf = _rope_tables(positions)
    t0, t1, mk = _bias_tiles(rel_bias)

    sd_a, ex_a = _seg_matrices(MIX, [(s * 64, 64) for s in range(8)])
    segs_q = []
    for h in range(HC):
        segs_q += [(h * LANES, NOPE), (h * LANES + NOPE, ROPE)]
    sd_q, ex_q = _seg_matrices(HC * LANES, segs_q)
    ne = N_GROUPS * N_EXP
    zpad = lambda a, w_: jnp.pad(a, ((0, 0), (0, w_ - a.shape[1])))

    w_main = w_in[:, :, :MAIN_W].astype(bf16)
    w_gates = w_in[:, :, MAIN_W:].astype(bf16)
    wg_all, wu_all, wd_all = w_e_gate.astype(bf16), w_e_up.astype(bf16), w_e_down.astype(bf16)

    xcur = x.reshape(n, D)
    y_prev, g2_prev = None, None
    for l in range(DEPTH):
        m3 = mod[l].reshape(nb, 1, 6 * D)
        shift1, scale1, gate1, shift2, scale2, gate2 = [m3[:, :, k * D:(k + 1) * D] for k in range(6)]
        lambda_init = LAMBDA_INIT_BASE - LAMBDA_INIT_SCALE * math.exp(-LAMBDA_INIT_DECAY * l)

        gq = (jnp.tile(diff_qk_g[l, 0], 8) * (64 ** -0.5 * LOG2E)).reshape(1, MIX)
        gk = jnp.tile(diff_qk_g[l, 1], 8).reshape(1, MIX)
        outs = _inproj(xcur, y_prev, g2_prev, shift1, scale1, norm_g[l, 0].reshape(1, D), w_main, w_gates, l,
                       gq, gk, sd_a, ex_a, sgu_v_g[l].reshape(1, MIX), sgu_w[l], jnp.transpose(sgu_b[l]), seq)
        if y_prev is not None:
            xcur = outs[0]
            outs = outs[1:]
        qa, ka, va, ob, xc, gt = outs

        oa = _attn_a(qa, ka, va, t0, t1, diff_lambda[l], diff_out_g[l].reshape(1, LANES), nb, seq, lambda_init)

        qkg = mla_qk_g[l]
        swap = np.concatenate([np.arange(NOPE), NOPE + ROPE // 2 + np.arange(ROPE // 2), NOPE + np.arange(ROPE // 2)])
        wq3 = mla_w_uq[l].reshape(Q_LORA, HC, NOPE + ROPE)
        wq = jnp.concatenate([_head_pad(mla_w_uq[l], HC, NOPE + ROPE, NOPE + ROPE),
                              _head_pad(wq3[:, :, swap].reshape(Q_LORA, -1), HC, NOPE + ROPE, NOPE + ROPE)],
                             axis=1).astype(bf16)
        wkv3 = mla_w_ukv[l].reshape(KV_LORA, HC, NOPE + VC)
        wk = jnp.pad(wkv3[:, :, :NOPE], ((0, 0), (0, 0), (0, LANES - NOPE))).reshape(KV_LORA, HC * LANES)
        wv = wkv3[:, :, NOPE:].reshape(KV_LORA, HC * VC)
        wkv = jnp.concatenate([wk, wv], axis=1).astype(bf16)
        gq_c = jnp.concatenate([jnp.tile(jnp.pad(g_, (0, LANES - NOPE - ROPE)), HC) for g_ in (qkg[0], qkg[0][swap])]
                               ).reshape(1, 2 * HC * LANES) * ((NOPE + ROPE) ** -0.5 * LOG2E)
        gk_c = jnp.pad(qkg[1, :NOPE], (0, LANES - NOPE)).reshape(1, LANES)
        gr_c = jnp.pad(qkg[1, NOPE:], (0, LANES - ROPE)).reshape(1, LANES)
        qc, kc, vc = _mla_prep(xc, cosf, sinf, mla_lat_g[l, :Q_LORA].reshape(1, Q_LORA),
                               mla_lat_g[l, Q_LORA:].reshape(1, KV_LORA), wq, wkv, gq_c, gk_c, gr_c, sd_q, ex_q)
        oc = _attn_c(qc, kc, vc, mk, nb, seq)

        wr = zpad(jnp.concatenate([router_e_w[l], router_g_w[l]], axis=1), LANES)
        wr_hi = wr.astype(bf16)
        wr = jnp.concatenate([wr_hi, (wr - wr_hi.astype(f32)).astype(bf16)], axis=1)
        br = zpad(jnp.concatenate([router_e_b[l], router_g_b[l]]).reshape(1, ne + N_GROUPS), LANES)
        x1, h2ext, meta, cnt = _merge(oa, ob, oc, gt, xcur, gate1, shift2, scale2, norm_g[l, 1].reshape(1, D),
                                      w_branch[l].astype(bf16), w_out[l].astype(bf16), wr, br, seq)

        pos, tt = _tables(meta, cnt)
        tok = _plan(pos.reshape(n))
        ntmax = _num_tiles(n)
        y_prev = _moe(h2ext, meta, tok, tt[:ntmax, 0], tt[:ntmax, 1], tt[0:1, 2], tt[:ntmax, 3],
                      wg_all, wu_all, wd_all, l)
        g2_prev = gate2
        xcur = x1

    out = _final_combine(xcur, y_prev, g2_prev, seq)
    return out.reshape(nb, seq, D)
```

```python
import functools
import math

import jax
import jax.numpy as jnp
import numpy as np
from jax import lax
from jax.experimental import pallas as pl
from jax.experimental.pallas import tpu as pltpu

f32 = jnp.float32
bf16 = jnp.bfloat16

D = 1024
DEPTH = 4
CHUNK = 64
MIX = 512
HA = 4
HC = 8
Q_LORA, KV_LORA, ROPE = 256, 128, 32
NOPE, VC = 64, 64
N_GROUPS, N_EXP, F_EXP = 4, 8, 256
N_REL_BUCKETS = 32
ROPE_THETA = 10000.0
LAMBDA_INIT_BASE, LAMBDA_INIT_SCALE, LAMBDA_INIT_DECAY = 0.8, 0.6, 0.3
EPS = 1e-6
NEG = -1e30

XC_OFF = 2560
MAIN_W = XC_OFF + Q_LORA + KV_LORA + ROPE

TM = 512
TQ = 256
TK = 256
LANES = 128
VMEM_LIMIT = 56 * 1024 * 1024
VMEM_LIMIT_RESIDENT = 60 * 1024 * 1024
SLABS = D // LANES
PACK_SLABS = SLABS // 2
Y_PITCH = SLABS
O_PITCH = Y_PITCH + 1


def _cparams(sem, vmem=VMEM_LIMIT):
    return pltpu.CompilerParams(dimension_semantics=sem, vmem_limit_bytes=vmem)


def _const_spec(shape):
    nd = len(shape)
    return pl.BlockSpec(shape, lambda *_: (0,) * nd)


def _rms(x, g_row):
    return x * lax.rsqrt(jnp.mean(x * x, axis=-1, keepdims=True) + EPS) * g_row


def _rows_from_token_major(y_ref, rows):
    return jnp.concatenate([y_ref[pl.ds(k, rows, stride=Y_PITCH), :] for k in range(SLABS)], axis=-1)


def _seg_rinv(x, sd_ref, ex_ref):
    ss = jnp.dot((x * x).astype(bf16), sd_ref[...], preferred_element_type=f32)
    hi = ss.astype(bf16)
    lo = (ss - hi.astype(f32)).astype(bf16)
    full = jnp.dot(jnp.concatenate([hi, lo], axis=-1), ex_ref[...], preferred_element_type=f32)
    return lax.rsqrt(full + EPS)


def _ada_kernel(c_ref, w_ref, b_ref, o_ref):
    c = c_ref[...]
    ca = (c * jax.nn.sigmoid(c)).astype(bf16)
    o_ref[0] = jnp.dot(ca, w_ref[0].astype(bf16), preferred_element_type=f32) + b_ref[0]


def _ada(c, w_ada, b_ada):
    nb = c.shape[0]
    tn = 1024
    return pl.pallas_call(
        _ada_kernel,
        grid=(DEPTH, 6 * D // tn),
        in_specs=[pl.BlockSpec((nb, D), lambda l, j: (0, 0)),
                  pl.BlockSpec((1, D, tn), lambda l, j: (l, 0, j)),
                  pl.BlockSpec((1, 1, tn), lambda l, j: (l, 0, j))],
        out_specs=pl.BlockSpec((1, nb, tn), lambda l, j: (l, 0, j)),
        out_shape=jax.ShapeDtypeStruct((DEPTH, nb, 6 * D), f32),
        compiler_params=_cparams(("arbitrary", "arbitrary")),
        name="ada_mod",
    )(c, w_ada, b_ada.reshape(DEPTH, 1, 6 * D))


def _rope_kernel(pos_ref, invf_ref, cos_ref, sin_ref):
    ang = pos_ref[...].astype(f32) * invf_ref[...]
    lane = lax.broadcasted_iota(jnp.int32, ang.shape, 1)
    rot = (lane >= NOPE) & (lane < NOPE + ROPE)
    first = lane < NOPE + ROPE // 2
    cos_ref[...] = jnp.where(rot, jnp.cos(ang), 1.0)
    s = jnp.sin(ang)
    sin_ref[...] = jnp.where(rot, jnp.where(first, -s, s), 0.0)


def _rope_tables(positions):
    n = positions.size
    inv_freq = ROPE_THETA ** (-jnp.arange(0, ROPE, 2, dtype=f32) / ROPE)
    invf = jnp.zeros((1, LANES), f32).at[0, NOPE:NOPE + ROPE].set(jnp.tile(inv_freq, 2))
    t = math.gcd(n, 2048)
    return pl.pallas_call(
        _rope_kernel,
        grid=(n // t,),
        in_specs=[pl.BlockSpec((t, 1), lambda i: (i, 0)), _const_spec((1, LANES))],
        out_specs=[pl.BlockSpec((t, LANES), lambda i: (i, 0))] * 2,
        out_shape=[jax.ShapeDtypeStruct((n, LANES), f32)] * 2,
        compiler_params=_cparams(("arbitrary",)),
        name="rope_tables",
    )(positions.reshape(n, 1), invf)


def _t5_bucket(rel):
    n = jnp.abs(rel)
    large = 8 + sum((n >= t).astype(jnp.int32) for t in (12, 16, 23, 32, 46, 64, 91))
    return jnp.where(rel > 0, 16, 0) + jnp.where(n < 8, n, large)


def _bias_kernel(tbl_ref, t0_ref, t1_ref, mk_ref):
    r = lax.broadcasted_iota(jnp.int32, (TQ, TK), 0)
    c = lax.broadcasted_iota(jnp.int32, (TQ, TK), 1)
    allowed = (c // CHUNK) <= (r // CHUNK)
    mk = jnp.where(allowed, 0.0, NEG).astype(f32)
    mk_ref[...] = mk
    b0 = _t5_bucket(c - r)
    b1 = _t5_bucket(c - r - TK)
    for h in range(HA):
        far = tbl_ref[N_REL_BUCKETS // 2 - 1, h]
        t0 = jnp.zeros((TQ, TK), f32)
        t1 = jnp.zeros((TQ, TK), f32)
        for b in range(N_REL_BUCKETS):
            v = (tbl_ref[b, h] - far) * LOG2E
            t0 = jnp.where(b0 == b, v, t0)
            t1 = jnp.where(b1 == b, v, t1)
        t0 = t0 + mk
        t0_ref[h, 0:TQ, :] = t0
        t0_ref[h, TQ:2 * TQ, :] = t0
        t1_ref[h, 0:TQ, :] = t1
        t1_ref[h, TQ:2 * TQ, :] = t1


def _bias_tiles(rel_bias):
    return pl.pallas_call(
        _bias_kernel,
        in_specs=[pl.BlockSpec(memory_space=pltpu.SMEM)],
        out_specs=[pl.BlockSpec(memory_space=pltpu.VMEM)] * 3,
        out_shape=[jax.ShapeDtypeStruct((HA, 2 * TQ, TK), f32),
                   jax.ShapeDtypeStruct((HA, 2 * TQ, TK), f32),
                   jax.ShapeDtypeStruct((TQ, TK), f32)],
        name="bias_tiles",
    )(rel_bias)


def _inproj_kernel(*refs, combine):
    if combine:
        (xa_ref, y_ref, g2_ref, sh_ref, sc_ref, ng_ref, w_ref, wgt_ref, gq_ref, gk_ref, sd_ref, ex_ref,
         vg_ref, ws_ref, bs_ref,
         xo_ref, qa_ref, ka_ref, va_ref, ob_ref, xc_ref, gt_ref) = refs
        x = xa_ref[...] + g2_ref[0] * _rows_from_token_major(y_ref, TM)
        xo_ref[...] = x
    else:
        (xa_ref, sh_ref, sc_ref, ng_ref, w_ref, wgt_ref, gq_ref, gk_ref, sd_ref, ex_ref,
         vg_ref, ws_ref, bs_ref,
         qa_ref, ka_ref, va_ref, ob_ref, xc_ref, gt_ref) = refs
        x = xa_ref[...]
    h = _rms(x, ng_ref[...]) * (1.0 + sc_ref[0]) + sh_ref[0]
    hb = h.astype(bf16)

    q = jnp.dot(hb, w_ref[0, :, 0:MIX], preferred_element_type=f32)
    qa_ref[...] = (q * _seg_rinv(q, sd_ref, ex_ref) * gq_ref[...]).astype(bf16)
    k = jnp.dot(hb, w_ref[0, :, MIX:2 * MIX], preferred_element_type=f32)
    ka_ref[...] = (k * _seg_rinv(k, sd_ref, ex_ref) * gk_ref[...]).astype(bf16)
    va_ref[...] = jnp.dot(hb, w_ref[0, :, 2 * MIX:3 * MIX], preferred_element_type=f32).astype(bf16)

    z = jnp.dot(hb, w_ref[0, :, 3 * MIX:5 * MIX], preferred_element_type=f32)
    z = 0.5 * z * (1.0 + jnp.tanh(math.sqrt(2.0 / math.pi) * (z + 0.044715 * (z * z * z))))
    u = z[:, :MIX]
    v = _rms(z[:, MIX:], vg_ref[...]).astype(bf16)
    ri = lax.broadcasted_iota(jnp.int32, (LANES, LANES), 0)
    ci = lax.broadcasted_iota(jnp.int32, (LANES, LANES), 1)
    allowed = (ci // CHUNK) <= (ri // CHUNK)
    for g in range(4):
        wm = jnp.where(allowed, ws_ref[g], 0.0).astype(bf16)
        bcol = bs_ref[:, g:g + 1]
        cols = slice(g * LANES, (g + 1) * LANES)
        nw = TM // LANES
        vcat = jnp.concatenate([v[wd * LANES:(wd + 1) * LANES, cols] for wd in range(nw)], axis=-1)
        vs = jnp.dot(wm, vcat, preferred_element_type=f32) + bcol
        for wd in range(nw):
            rows = slice(wd * LANES, (wd + 1) * LANES)
            ob_ref[rows, cols] = (u[rows, cols] * vs[:, wd * LANES:(wd + 1) * LANES]).astype(bf16)

    lat = Q_LORA + KV_LORA
    xc = jnp.dot(hb, w_ref[0, :, XC_OFF:MAIN_W], preferred_element_type=f32)
    xc_ref[:, 0:lat] = xc[:, 0:lat]
    xc_ref[:, lat:lat + LANES] = jnp.concatenate([xc[:, lat:lat + ROPE], jnp.zeros((TM, LANES - ROPE), f32)],
                                                 axis=-1)

    for j in range(3):
        gl = jnp.dot(hb, wgt_ref[0, :, j * D:(j + 1) * D], preferred_element_type=f32)
        gt_ref[:, j * D:(j + 1) * D] = jax.nn.sigmoid(gl).astype(bf16)


def _inproj(xa, y, g2, shift, scale, ng, w, wgt, layer, gq, gk, sd, ex, vg, ws, bs, seq):
    n = xa.shape[0]
    spb = seq // TM
    combine = y is not None
    row = lambda w_: pl.BlockSpec((TM, w_), lambda i: (i, 0))
    modspec = pl.BlockSpec((1, 1, D), lambda i: (i // spb, 0, 0))
    in_specs = [row(D)]
    args = [xa]
    if combine:
        in_specs += [pl.BlockSpec((TM * Y_PITCH, LANES), lambda i: (i, 0)), modspec]
        args += [y, g2]
    in_specs += [modspec, modspec, _const_spec((1, D)),
                 pl.BlockSpec((1, D, MAIN_W), lambda i: (layer, 0, 0), pipeline_mode=pl.Buffered(1)),
                 pl.BlockSpec((1, D, 3 * D), lambda i: (layer, 0, 0), pipeline_mode=pl.Buffered(1)),
                 _const_spec((1, MIX)), _const_spec((1, MIX)),
                 _const_spec((MIX, LANES)), _const_spec((2 * LANES, MIX)),
                 _const_spec((1, MIX)), _const_spec((4, LANES, LANES)), _const_spec((LANES, 4))]
    args += [shift, scale, ng, w, wgt, gq, gk, sd, ex, vg, ws, bs]
    out_specs = [row(MIX)] * 4 + [row(MIX), row(3 * D)]
    out_shape = [jax.ShapeDtypeStruct((n, MIX), bf16)] * 4 + [
        jax.ShapeDtypeStruct((n, MIX), f32), jax.ShapeDtypeStruct((n, 3 * D), bf16)]
    if combine:
        out_specs = [row(D)] + out_specs
        out_shape = [jax.ShapeDtypeStruct((n, D), f32)] + out_shape
    return pl.pallas_call(
        functools.partial(_inproj_kernel, combine=combine),
        grid=(n // TM,),
        in_specs=in_specs, out_specs=out_specs, out_shape=out_shape,
        compiler_params=_cparams(("arbitrary",)),
        name="inproj",
    )(*args)


LOG2E = math.log2(math.e)


def _two_pass_attention(nunits, q_of, kcols_of, vcols_of, k_ref, v_ref, qi, diag_bias_of, sub_bias_of,
                        sbuf, mbuf, lbuf, abuf):
    def keys(j, nb):
        return pl.ds(pl.multiple_of(j * TK, TK), nb * TK)

    def score_blocks(j, nb, bias_of):
        for u in range(nunits):
            s = lax.dot_general(q_of(u), k_ref[keys(j, nb), kcols_of(u)], (((1,), (1,)), ((), ())),
                                preferred_element_type=f32)
            if bias_of is not None:
                s = s + bias_of(u)
            m = mbuf[u]
            for b in range(nb):
                sbuf[u, j + b] = s[:, b * TK:(b + 1) * TK]
            for c in range(nb * TK // LANES):
                m = jnp.maximum(m, s[:, c * LANES:(c + 1) * LANES])
            mbuf[u] = m

    def blocks_in_pairs(first, count, fn):
        def pair(jj, _):
            fn(first + 2 * jj, 2)
            return 0

        lax.fori_loop(0, count // 2, pair, 0)

        @pl.when(count % 2 == 1)
        def _():
            fn(first + count - 1, 1)

    for u in range(nunits):
        mbuf[u] = jnp.full(mbuf.shape[1:], NEG, f32)
    if sub_bias_of is None:
        nfar = qi
    else:
        nfar = jnp.maximum(qi - 1, 0)

        @pl.when(qi >= 1)
        def _():
            score_blocks(qi - 1, 1, sub_bias_of)

    blocks_in_pairs(0, nfar, lambda j, nb: score_blocks(j, nb, None))
    score_blocks(qi, 1, diag_bias_of)

    for u in range(nunits):
        mbuf[u] = jnp.broadcast_to(jnp.max(mbuf[u], axis=-1, keepdims=True), mbuf.shape[1:])
        lbuf[u] = jnp.zeros(lbuf.shape[1:], f32)
        abuf[u] = jnp.zeros(abuf.shape[1:], f32)

    def accumulate_blocks(j, nb):
        for u in range(nunits):
            mb = mbuf[u]
            lsum = lbuf[u]
            ps = []
            for b in range(nb):
                s = sbuf[u, j + b]
                for c in range(TK // LANES):
                    p = jnp.exp2(s[:, c * LANES:(c + 1) * LANES] - mb)
                    lsum = lsum + p
                    ps.append(p)
            lbuf[u] = lsum
            p = jnp.concatenate(ps, axis=-1).astype(bf16)
            abuf[u] = abuf[u] + jnp.dot(p, v_ref[keys(j, nb), vcols_of(u)], preferred_element_type=f32)

    blocks_in_pairs(0, qi + 1, accumulate_blocks)


def _normalised(u, lbuf, abuf):
    part = lbuf[u]
    hi = part.astype(bf16)
    lo = (part - hi.astype(f32)).astype(bf16)
    total = jnp.dot(jnp.concatenate([hi, lo], axis=-1), jnp.ones((2 * LANES, LANES), bf16),
                    preferred_element_type=f32)
    return abuf[u] / total


GA = 4
GC = 8


def _attn_a_kernel(q_ref, k_ref, v_ref, t0_ref, t1_ref, dl_ref, og_ref, o_ref,
                   qstk, sbuf, mbuf, lbuf, abuf, *, lambda_init):
    qi = pl.program_id(2)
    dl = dl_ref[...]
    lam = (jnp.exp(jnp.sum(dl[0:1] * dl[1:2], axis=-1, keepdims=True))
           - jnp.exp(jnp.sum(dl[2:3] * dl[3:4], axis=-1, keepdims=True)) + lambda_init)
    lane = lax.broadcasted_iota(jnp.int32, (TQ, LANES), 1)
    cols = lambda u: slice(u * LANES, (u + 1) * LANES)
    for u in range(GA):
        qh = q_ref[:, cols(u)]
        zero = jnp.zeros_like(qh)
        qstk[u, 0:TQ, :] = jnp.where(lane < 64, qh, zero)
        qstk[u, TQ:2 * TQ, :] = jnp.where(lane >= 64, qh, zero)
    _two_pass_attention(GA, lambda u: qstk[u], cols, cols, k_ref, v_ref, qi,
                        lambda u: t0_ref[u], lambda u: t1_ref[u], sbuf, mbuf, lbuf, abuf)
    for u in range(GA):
        o = _normalised(u, lbuf, abuf)
        oh = o[:TQ] - lam * o[TQ:]
        oh = _rms(oh, og_ref[...]) * (1.0 - lambda_init)
        o_ref[:, cols(u)] = oh.astype(bf16)


def _attn_a(qa, ka, va, t0, t1, dl, og, nb, seq, lambda_init):
    n = qa.shape[0]
    nq = seq // TQ
    w = GA * LANES
    return pl.pallas_call(
        functools.partial(_attn_a_kernel, lambda_init=lambda_init),
        grid=(nb, HA // GA, nq),
        in_specs=[pl.BlockSpec((TQ, w), lambda b, g, i: (b * nq + i, g)),
                  pl.BlockSpec((seq, w), lambda b, g, i: (b, g)),
                  pl.BlockSpec((seq, w), lambda b, g, i: (b, g)),
                  pl.BlockSpec((GA, 2 * TQ, TK), lambda b, g, i: (g, 0, 0)),
                  pl.BlockSpec((GA, 2 * TQ, TK), lambda b, g, i: (g, 0, 0)),
                  _const_spec((4, 64)), _const_spec((1, LANES))],
        out_specs=pl.BlockSpec((TQ, w), lambda b, g, i: (b * nq + i, g)),
        out_shape=jax.ShapeDtypeStruct((n, MIX), bf16),
        scratch_shapes=[pltpu.VMEM((GA, 2 * TQ, LANES), bf16),
                        pltpu.VMEM((GA, seq // TK, 2 * TQ, TK), f32),
                        pltpu.VMEM((GA, 2 * TQ, LANES), f32), pltpu.VMEM((GA, 2 * TQ, LANES), f32),
                        pltpu.VMEM((GA, 2 * TQ, LANES), f32)],
        compiler_params=_cparams(("arbitrary", "arbitrary", "arbitrary")),
        name="attn_diff",
    )(qa, ka, va, t0, t1, dl, og)


def _attn_c_kernel(q_ref, k_ref, v_ref, mk_ref, o_ref, sbuf, mbuf, lbuf, abuf):
    qi = pl.program_id(2)
    lane = lax.broadcasted_iota(jnp.int32, (TQ, LANES), 1)
    cols = lambda u: slice(u * LANES, (u + 1) * LANES)
    _two_pass_attention(GC, lambda u: q_ref[:, cols(u)], cols, lambda u: cols(u // 2), k_ref, v_ref, qi,
                        lambda u: mk_ref[...], None, sbuf, mbuf, lbuf, abuf)
    for hp in range(GC // 2):
        o_ref[:, cols(hp)] = jnp.where(lane < VC, _normalised(2 * hp, lbuf, abuf),
                                       _normalised(2 * hp + 1, lbuf, abuf)).astype(bf16)


def _attn_c(qc, kc, vc, mk, nb, seq):
    n = qc.shape[0]
    nq = seq // TQ
    w = GC * LANES
    wv = GC * VC
    return pl.pallas_call(
        _attn_c_kernel,
        grid=(nb, HC // GC, nq),
        in_specs=[pl.BlockSpec((TQ, w), lambda b, g, i: (b * nq + i, g)),
                  pl.BlockSpec((seq, w), lambda b, g, i: (b, g)),
                  pl.BlockSpec((seq, wv), lambda b, g, i: (b, g)),
                  _const_spec((TQ, TK))],
        out_specs=pl.BlockSpec((TQ, wv), lambda b, g, i: (b * nq + i, g)),
        out_shape=jax.ShapeDtypeStruct((n, MIX), bf16),
        scratch_shapes=[pltpu.VMEM((GC, seq // TK, TQ, TK), f32),
                        pltpu.VMEM((GC, TQ, LANES), f32), pltpu.VMEM((GC, TQ, LANES), f32),
                        pltpu.VMEM((GC, TQ, LANES), f32)],
        compiler_params=_cparams(("arbitrary", "arbitrary", "arbitrary")),
        name="attn_latent",
    )(qc, kc, vc, mk)


def _rope_apply(x, cosf, sinf, lane):
    w = x.shape[-1]
    partner = jnp.where((lane % LANES) < NOPE + ROPE // 2,
                        pltpu.roll(x, w - ROPE // 2, 1), pltpu.roll(x, ROPE // 2, 1))
    return x * cosf + partner * sinf


def _mla_prep_kernel(xc_ref, cos_ref, sin_ref, glq_ref, glkv_ref, wq_ref, wkv_ref, gq_ref, gk_ref, gr_ref,
                     sd_ref, ex_ref, q_ref, k_ref, v_ref):
    cq = _rms(xc_ref[:, 0:Q_LORA], glq_ref[...]).astype(bf16)
    ckv = _rms(xc_ref[:, Q_LORA:Q_LORA + KV_LORA], glkv_ref[...]).astype(bf16)
    cos8 = jnp.concatenate([cos_ref[...]] * HC, axis=-1)
    sin8 = jnp.concatenate([sin_ref[...]] * HC, axis=-1)

    hw = HC * LANES
    q2 = jnp.dot(cq, wq_ref[...], preferred_element_type=f32)
    rinv = _seg_rinv(q2[:, :hw], sd_ref, ex_ref)
    q_ref[...] = ((q2[:, :hw] * rinv * gq_ref[:, :hw]) * cos8
                  + (q2[:, hw:] * rinv * gq_ref[:, hw:]) * sin8).astype(bf16)

    kv = jnp.dot(ckv, wkv_ref[...], preferred_element_type=f32)
    v_ref[...] = kv[:, HC * LANES:].astype(bf16)
    xr = xc_ref[:, Q_LORA + KV_LORA:Q_LORA + KV_LORA + LANES]
    kr = xr * lax.rsqrt(jnp.sum(xr * xr, axis=-1, keepdims=True) * (1.0 / ROPE) + EPS) * gr_ref[...]
    kr = pltpu.roll(kr, NOPE, 1)
    lane1 = lax.broadcasted_iota(jnp.int32, (TM, LANES), 1)
    kr = _rope_apply(kr, cos_ref[...], sin_ref[...], lane1)
    for h in range(HC):
        cols = slice(h * LANES, (h + 1) * LANES)
        kn = kv[:, cols]
        kn = kn * lax.rsqrt(jnp.sum(kn * kn, axis=-1, keepdims=True) * (1.0 / NOPE) + EPS) * gk_ref[...]
        k_ref[:, cols] = (kn + kr).astype(bf16)


def _mla_prep(xc, cosf, sinf, glq, glkv, wq, wkv, gq, gk, gr, sd, ex):
    n = xc.shape[0]
    row = lambda w_: pl.BlockSpec((TM, w_), lambda i: (i, 0))
    return pl.pallas_call(
        _mla_prep_kernel,
        grid=(n // TM,),
        in_specs=[row(MIX), row(LANES), row(LANES), _const_spec((1, Q_LORA)), _const_spec((1, KV_LORA)),
                  _const_spec((Q_LORA, 2 * HC * LANES)), _const_spec((KV_LORA, HC * LANES + MIX)),
                  _const_spec((1, 2 * HC * LANES)), _const_spec((1, LANES)), _const_spec((1, LANES)),
                  _const_spec((HC * LANES, LANES)), _const_spec((2 * LANES, HC * LANES))],
        out_specs=[row(HC * LANES), row(HC * LANES), row(MIX)],
        out_shape=[jax.ShapeDtypeStruct((n, HC * LANES), bf16), jax.ShapeDtypeStruct((n, HC * LANES), bf16),
                   jax.ShapeDtypeStruct((n, MIX), bf16)],
        compiler_params=_cparams(("arbitrary",)),
        name="mla_prep",
    )(xc, cosf, sinf, glq, glkv, wq, wkv, gq, gk, gr, sd, ex)


def _merge_kernel(oa_ref, ob_ref, oc_ref, gt_ref, x_ref, g1_ref, sh_ref, sc_ref, ng_ref, wb_ref, wo_ref,
                  wr_ref, br_ref, x1_ref, h2_ref, meta_ref, cnt_ref, run_ref):
    i = pl.program_id(0)

    @pl.when(i == 0)
    def _():
        run_ref[...] = jnp.zeros_like(run_ref)

    merged = jnp.zeros((TM, D), f32)
    for j, o_ref in enumerate((oa_ref, ob_ref, oc_ref)):
        pj = jnp.dot(o_ref[...], wb_ref[j], preferred_element_type=f32)
        merged = merged + gt_ref[:, j * D:(j + 1) * D].astype(f32) * pj
    y = jnp.dot(merged.astype(bf16), wo_ref[...], preferred_element_type=f32)
    x1 = x_ref[...] + g1_ref[0] * y
    x1_ref[...] = x1
    h2 = _rms(x1, ng_ref[...]) * (1.0 + sc_ref[0]) + sh_ref[0]
    hr = h2.astype(bf16).astype(f32)
    lo = lax.shift_right_logical(lax.bitcast_convert_type(hr[:, 0:D // 2], jnp.uint32), jnp.uint32(16))
    hi = lax.bitcast_convert_type(hr[:, D // 2:D], jnp.uint32) & jnp.uint32(0xFFFF0000)
    words = lax.bitcast_convert_type(lo | hi, f32)
    for s in range(PACK_SLABS):
        h2_ref[pl.ds(s, TM, stride=PACK_SLABS), :] = words[:, s * LANES:(s + 1) * LANES]

    h_hi = h2.astype(bf16)
    h_lo = (h2 - h_hi.astype(f32)).astype(bf16)
    t = jnp.dot(h_hi, wr_ref[...], preferred_element_type=f32)
    lg = (t[:, :LANES] + t[:, LANES:] + jnp.dot(h_lo, wr_ref[:, 0:LANES], preferred_element_type=f32)
          + br_ref[...])
    lane = lax.broadcasted_iota(jnp.int32, (TM, LANES), 1)
    n_e = N_GROUPS * N_EXP
    gmask = (lane >= n_e) & (lane < n_e + N_GROUPS)
    gl = jnp.where(gmask, lg, NEG)
    gmax = jnp.max(gl, axis=-1, keepdims=True)
    g_w = 1.0 / jnp.sum(jnp.where(gmask, jnp.exp(gl - gmax), 0.0), axis=-1, keepdims=True)
    gidx = jnp.min(jnp.where(gl == gmax, lane - n_e, LANES), axis=-1, keepdims=True)
    emask = (lane < n_e) & ((lane // N_EXP) == gidx)
    el = jnp.where(emask, lg, NEG)
    m1 = jnp.max(el, axis=-1, keepdims=True)
    i1 = jnp.min(jnp.where(el == m1, lane, LANES), axis=-1, keepdims=True)
    el2 = jnp.where(lane == i1, NEG, el)
    m2 = jnp.max(el2, axis=-1, keepdims=True)
    i2 = jnp.min(jnp.where(el2 == m2, lane, LANES), axis=-1, keepdims=True)
    t = jnp.exp(m2 - m1)
    w1 = g_w / (1.0 + t)
    w2 = g_w * t / (1.0 + t)
    lo = jnp.minimum(i1, i2)
    hi = jnp.maximum(i1, i2)
    w_lo = jnp.where(i1 < i2, w1, w2)
    w_hi = jnp.where(i1 < i2, w2, w1)
    bucket = gidx * (N_EXP * N_EXP) + (lo % N_EXP) * N_EXP + (hi % N_EXP)

    lane2 = lax.broadcasted_iota(jnp.int32, (TM, 2 * LANES), 1)
    onehot = (lane2 == bucket).astype(f32)
    rr = lax.broadcasted_iota(jnp.int32, (TM, TM), 0)
    cc = lax.broadcasted_iota(jnp.int32, (TM, TM), 1)
    tri = (cc < rr).astype(bf16)
    before = jnp.dot(tri, onehot.astype(bf16), preferred_element_type=f32) + run_ref[...]
    rank = jnp.sum(onehot * before, axis=-1, keepdims=True)
    run_ref[...] = run_ref[...] + jnp.sum(onehot, axis=0, keepdims=True)
    cnt_ref[...] = run_ref[...]

    meta = (jnp.where(lane == i1, w1, 0.0) + jnp.where(lane == i2, w2, 0.0)
            + jnp.where(lane == 32, bucket.astype(f32), 0.0) + jnp.where(lane == 33, rank, 0.0)
            + jnp.where(lane == 34, w_lo, 0.0) + jnp.where(lane == 35, w_hi, 0.0))
    meta_ref[...] = meta


def _merge(oa, ob, oc, gt, x, g1, shift, scale, ng, wb, wo, wr, br, seq):
    n = x.shape[0]
    spb = seq // TM
    row = lambda w_: pl.BlockSpec((TM, w_), lambda i: (i, 0))
    modspec = pl.BlockSpec((1, 1, D), lambda i: (i // spb, 0, 0))
    return pl.pallas_call(
        _merge_kernel,
        grid=(n // TM,),
        in_specs=[row(MIX), row(MIX), row(MIX), row(3 * D), row(D), modspec, modspec, modspec,
                  _const_spec((1, D)), _const_spec((3, MIX, D)), _const_spec((D, D)),
                  _const_spec((D, 2 * LANES)), _const_spec((1, LANES))],
        out_specs=[row(D), pl.BlockSpec((TM * PACK_SLABS, LANES), lambda i: (i, 0)), row(LANES),
                   _const_spec((1, 2 * LANES))],
        out_shape=[jax.ShapeDtypeStruct((n, D), f32), jax.ShapeDtypeStruct((n * PACK_SLABS, LANES), f32),
                   jax.ShapeDtypeStruct((n, LANES), f32), jax.ShapeDtypeStruct((1, 2 * LANES), f32)],
        scratch_shapes=[pltpu.VMEM((1, 2 * LANES), f32)],
        compiler_params=_cparams(("arbitrary",)),
        name="merge_route",
    )(oa, ob, oc, gt, x, g1, shift, scale, ng, wb, wo, wr, br)


TE = 128
N_BUCKETS = N_GROUPS * N_EXP * N_EXP
N_PAIRS = N_GROUPS * (N_EXP * (N_EXP - 1) // 2)


def _num_tiles(n):
    return n // TE + N_PAIRS


TP = 2048


def _tables_kernel(meta_ref, cnt_ref, pos_ref, tt_ref):
    cnt = cnt_ref[...]
    ntile = jnp.floor((cnt + (TE - 1)) * (1.0 / TE))
    bi = lax.broadcasted_iota(jnp.int32, (N_BUCKETS, N_BUCKETS), 0)
    bj = lax.broadcasted_iota(jnp.int32, (N_BUCKETS, N_BUCKETS), 1)
    upper = (bi <= bj).astype(bf16)
    incl = jnp.dot(jnp.broadcast_to(ntile, (8, N_BUCKETS)).astype(bf16), upper, preferred_element_type=f32)[0:1]
    excl = incl - ntile

    meta = meta_ref[...]
    tp = meta.shape[0]
    lane = lax.broadcasted_iota(jnp.int32, (tp, LANES), 1)
    bucket = jnp.sum(jnp.where(lane == 32, meta, 0.0), axis=-1, keepdims=True).astype(jnp.int32)
    rank = jnp.sum(jnp.where(lane == 33, meta, 0.0), axis=-1, keepdims=True)
    lane2 = lax.broadcasted_iota(jnp.int32, (tp, N_BUCKETS), 1)
    first_tile = jnp.sum(jnp.where(lane2 == bucket, excl, 0.0), axis=-1, keepdims=True)
    pos_ref[...] = (first_tile * TE + rank).astype(jnp.int32)

    @pl.when(pl.program_id(0) == 0)
    def _():
        tau = lax.broadcasted_iota(jnp.int32, (N_BUCKETS, 1), 0).astype(f32)
        lane_b = lax.broadcasted_iota(jnp.int32, (1, N_BUCKETS), 1).astype(f32)
        tb = jnp.sum((incl <= tau).astype(f32), axis=-1, keepdims=True)
        last_b = jnp.max(jnp.where(cnt > 0, lane_b, 0.0), axis=-1, keepdims=True)
        own = lane_b == tb
        in_bucket = jnp.sum(jnp.where(own, cnt, 0.0), axis=-1, keepdims=True)
        first = jnp.sum(jnp.where(own, excl, 0.0), axis=-1, keepdims=True)
        valid = jnp.clip(in_bucket - (tau - first) * TE, 0.0, float(TE)).astype(jnp.int32)
        tbi = jnp.minimum(tb, last_b).astype(jnp.int32)
        grp = lax.shift_right_logical(tbi, 6) * N_EXP
        ea = grp + (lax.shift_right_logical(tbi, 3) & 7)
        eb = grp + (tbi & 7)
        total = jnp.sum(ntile, axis=-1, keepdims=True).astype(jnp.int32)
        lane_t = lax.broadcasted_iota(jnp.int32, (N_BUCKETS, LANES), 1)
        tt_ref[...] = jnp.where(lane_t == 0, ea, jnp.where(lane_t == 1, eb, jnp.where(
            lane_t == 2, total, jnp.where(lane_t == 3, valid, 0))))


def _tables(meta, cnt):
    n = meta.shape[0]
    tp = math.gcd(n, TP)
    assert _num_tiles(n) <= N_BUCKETS
    return pl.pallas_call(
        _tables_kernel,
        grid=(n // tp,),
        in_specs=[pl.BlockSpec((tp, LANES), lambda i: (i, 0)), _const_spec((1, N_BUCKETS))],
        out_specs=[pl.BlockSpec((tp, 1), lambda i: (i, 0)), _const_spec((N_BUCKETS, LANES))],
        out_shape=[jax.ShapeDtypeStruct((n, 1), jnp.int32), jax.ShapeDtypeStruct((N_BUCKETS, LANES), jnp.int32)],
        compiler_params=_cparams(("arbitrary",)),
        name="moe_tables",
    )(meta, cnt)


def _plan_kernel(pos_ref, tok_ref, *, n, nslot):
    def fill(s, _):
        tok_ref[s] = -1
        return 0

    lax.fori_loop(0, nslot, fill, 0, unroll=16)

    def place(t, _):
        tok_ref[pos_ref[t]] = t
        return 0

    lax.fori_loop(0, n, place, 0, unroll=16)


def _plan(pos):
    n = pos.shape[0]
    nslot = _num_tiles(n) * TE
    smem = pl.BlockSpec(memory_space=pltpu.SMEM)
    return pl.pallas_call(
        functools.partial(_plan_kernel, n=n, nslot=nslot),
        in_specs=[smem], out_specs=smem,
        out_shape=jax.ShapeDtypeStruct((nslot,), jnp.int32),
        name="moe_plan",
    )(pos)


SCATTER_GROUP = 32


def _moe_kernel(tea_ref, teb_ref, nt_ref, tok_ref, tc_ref, h_ref, m_ref, wg_ref, wu_ref, wd_ref, y_hbm,
                xs, ms, ob0, ob1, ssem, *, n):
    i = pl.program_id(0)
    nt = nt_ref[0]
    obufs = (ob0, ob1)

    def scatter_copy(dst_tok, sl, r):
        dst = y_hbm.at[pl.ds(pl.multiple_of(dst_tok * Y_PITCH, Y_PITCH), Y_PITCH), :]
        return pltpu.make_async_copy(obufs[sl].at[pl.ds(r * O_PITCH, Y_PITCH), :], dst, ssem.at[sl])

    def real_slots(tile):
        return jnp.where(tile >= 0, tc_ref[jnp.maximum(tile, 0)], 0)

    def scatter_groups(tile, fn):
        cnt = real_slots(tile)
        for g in range(TE // SCATTER_GROUP):
            @pl.when(cnt > g * SCATTER_GROUP)
            def _(g=g):
                for r in range(g * SCATTER_GROUP, (g + 1) * SCATTER_GROUP):
                    fn(r)

    def start_scatters(tile, sl):
        base = jnp.maximum(tile, 0) * TE

        def start(r):
            t = tok_ref[base + r]
            scatter_copy(jnp.where(t < 0, n + sl * TE + r, t), sl, r).start(priority=r % 2)

        scatter_groups(tile, start)

    def wait_scatters(tile, sl):
        scatter_groups(tile, lambda r: scatter_copy(n, sl, r).wait())

    @pl.when(i == 0)
    def _():
        for sl in range(2):
            obufs[sl][...] = jnp.zeros_like(obufs[sl])
            spare = pltpu.make_async_copy(obufs[sl].at[pl.ds(0, TE * Y_PITCH), :],
                                          y_hbm.at[pl.ds((n + sl * TE) * Y_PITCH, TE * Y_PITCH), :], ssem.at[sl])
            spare.start()
            spare.wait()

    def step(s):
        o = 1 - s

        wait_scatters(i - 2, s)
        start_scatters(i - 1, o)

        for r in range(TE):
            t = jnp.maximum(tok_ref[i * TE + r], 0)
            xs[pl.ds(r * PACK_SLABS, PACK_SLABS), :] = h_ref[pl.ds(pl.multiple_of(t * PACK_SLABS, PACK_SLABS),
                                                                   PACK_SLABS), :]
            ms[pl.ds(r, 1), :] = m_ref[pl.ds(t, 1), :]
        slab = lambda k: xs[pl.ds(k, TE, stride=PACK_SLABS), :]
        words = lax.bitcast_convert_type(jnp.concatenate([slab(k) for k in range(PACK_SLABS)], axis=-1),
                                         jnp.uint32)
        lo = lax.bitcast_convert_type(lax.shift_left(words, jnp.uint32(16)), f32)
        hi = lax.bitcast_convert_type(words & jnp.uint32(0xFFFF0000), f32)
        hb = jnp.concatenate([lo, hi], axis=-1).astype(bf16)
        meta = ms[...]
        lane = lax.broadcasted_iota(jnp.int32, (TE, LANES), 1)
        w_lo = jnp.sum(jnp.where(lane == 34, meta, 0.0), axis=-1, keepdims=True)
        w_hi = jnp.sum(jnp.where(lane == 35, meta, 0.0), axis=-1, keepdims=True)

        def ffn(e, wcol):
            a = jnp.dot(hb, wg_ref[0, 0, e], preferred_element_type=f32)
            u = jnp.dot(hb, wu_ref[0, 0, e], preferred_element_type=f32)
            act = (a * jax.nn.sigmoid(a)) * u * wcol
            return jnp.dot(act.astype(bf16), wd_ref[0, 0, e], preferred_element_type=f32)

        out = ffn(lax.rem(tea_ref[i], N_EXP), w_lo) + ffn(lax.rem(teb_ref[i], N_EXP), w_hi)
        for k in range(SLABS):
            obufs[s][pl.ds(k, TE, stride=O_PITCH), :] = out[:, k * LANES:(k + 1) * LANES]

        @pl.when(i == nt - 1)
        def _():
            start_scatters(i, s)
            wait_scatters(i - 1, o)
            wait_scatters(i, s)

    for s in range(2):
        @pl.when((i < nt) & (lax.rem(i, 2) == s))
        def _(s=s):
            step(s)


def _moe(h2ext, meta, tok, tea, teb, nt, tc, wg, wu, wd, layer):
    n = h2ext.shape[0] // PACK_SLABS
    ntmax = _num_tiles(n)
    once = pl.Buffered(1)
    wspec_in = pl.BlockSpec((1, 1, N_EXP, D, F_EXP),
                            lambda i, tea, teb, nt, tok, tc: (layer, tea[i] // N_EXP, 0, 0, 0), pipeline_mode=once)
    wspec_out = pl.BlockSpec((1, 1, N_EXP, F_EXP, D),
                             lambda i, tea, teb, nt, tok, tc: (layer, tea[i] // N_EXP, 0, 0, 0), pipeline_mode=once)
    hspec = pl.BlockSpec((n * PACK_SLABS, LANES), lambda i, tea, teb, nt, tok, tc: (0, 0), pipeline_mode=once)
    mspec = pl.BlockSpec((n, LANES), lambda i, tea, teb, nt, tok, tc: (0, 0), pipeline_mode=once)
    return pl.pallas_call(
        functools.partial(_moe_kernel, n=n),
        grid_spec=pltpu.PrefetchScalarGridSpec(
            num_scalar_prefetch=5,
            grid=(ntmax,),
            in_specs=[hspec, mspec, wspec_in, wspec_in, wspec_out],
            out_specs=pl.BlockSpec(memory_space=pl.ANY),
            scratch_shapes=[pltpu.VMEM((TE * PACK_SLABS, LANES), f32), pltpu.VMEM((TE, LANES), f32)] + [
                pltpu.VMEM((TE * O_PITCH, LANES), f32)] * 2 + [pltpu.SemaphoreType.DMA((2,))]),
        out_shape=jax.ShapeDtypeStruct(((n + 2 * TE) * Y_PITCH, LANES), f32),
        compiler_params=_cparams(("arbitrary",), vmem=VMEM_LIMIT_RESIDENT),
        name="moe_sparse",
    )(tea, teb, nt, tok, tc, h2ext, meta, wg, wu, wd)


def _final_kernel(x_ref, y_ref, g_ref, o_ref):
    o_ref[...] = x_ref[...] + g_ref[0] * _rows_from_token_major(y_ref, x_ref.shape[0])


def _final_combine(x1, y, g2, seq):
    n = x1.shape[0]
    t = 512
    spb = seq // t
    row = pl.BlockSpec((t, D), lambda i: (i, 0))
    return pl.pallas_call(
        _final_kernel,
        grid=(n // t,),
        in_specs=[row, pl.BlockSpec((t * Y_PITCH, LANES), lambda i: (i, 0)),
                  pl.BlockSpec((1, 1, D), lambda i: (i // spb, 0, 0))],
        out_specs=row,
        out_shape=jax.ShapeDtypeStruct((n, D), f32),
        compiler_params=_cparams(("arbitrary",)),
        name="final_combine",
    )(x1, y, g2)


def _seg_matrices(width, segs):
    sd = np.zeros((width, LANES), np.float32)
    ex = np.zeros((LANES, width), np.float32)
    for j, (s, ln) in enumerate(segs):
        sd[s:s + ln, j] = 1.0 / ln
        ex[j, s:s + ln] = 1.0
    return jnp.asarray(sd, bf16), jnp.asarray(np.concatenate([ex, ex], axis=0), bf16)


def _head_pad(w, heads, per_head, keep):
    k = w.shape[0]
    w3 = w.reshape(k, heads, per_head)[:, :, :keep]
    return jnp.pad(w3, ((0, 0), (0, 0), (0, LANES - keep))).reshape(k, heads * LANES)


def kernel(x, c, positions, w_ada, b_ada, norm_g, w_in, diff_qk_g, diff_lambda, diff_out_g, rel_bias, sgu_v_g, sgu_w, sgu_b, mla_lat_g, mla_w_uq, mla_w_ukv, mla_qk_g, w_branch, w_out, router_g_w, router_g_b, router_e_w, router_e_b, w_e_gate, w_e_up, w_e_down):
    nb, seq, _ = x.shape
    n = nb * seq
    assert seq % TQ == 0 and seq % TM == 0 and x.shape[2] == D

    mod = _ada(c, w_ada, b_ada)
    cosf, sinf = _rope_tables(positions)
    t0, t1, mk = _bias_tiles(rel_bias)

    sd_a, ex_a = _seg_matrices(MIX, [(s * 64, 64) for s in range(8)])
    segs_q = []
    for h in range(HC):
        segs_q += [(h * LANES, NOPE), (h * LANES + NOPE, ROPE)]
    sd_q, ex_q = _seg_matrices(HC * LANES, segs_q)
    ne = N_GROUPS * N_EXP
    zpad = lambda a, w_: jnp.pad(a, ((0, 0), (0, w_ - a.shape[1])))

    w_main = w_in[:, :, :MAIN_W].astype(bf16)
    w_gates = w_in[:, :, MAIN_W:].astype(bf16)
    wg_all, wu_all, wd_all = w_e_gate.astype(bf16), w_e_up.astype(bf16), w_e_down.astype(bf16)

    xcur = x.reshape(n, D)
    y_prev, g2_prev = None, None
    for l in range(DEPTH):
        m3 = mod[l].reshape(nb, 1, 6 * D)
        shift1, scale1, gate1, shift2, scale2, gate2 = [m3[:, :, k * D:(k + 1) * D] for k in range(6)]
        lambda_init = LAMBDA_INIT_BASE - LAMBDA_INIT_SCALE * math.exp(-LAMBDA_INIT_DECAY * l)

        gq = (jnp.tile(diff_qk_g[l, 0], 8) * (64 ** -0.5 * LOG2E)).reshape(1, MIX)
        gk = jnp.tile(diff_qk_g[l, 1], 8).reshape(1, MIX)
        outs = _inproj(xcur, y_prev, g2_prev, shift1, scale1, norm_g[l, 0].reshape(1, D), w_main, w_gates, l,
                       gq, gk, sd_a, ex_a, sgu_v_g[l].reshape(1, MIX), sgu_w[l], jnp.transpose(sgu_b[l]), seq)
        if y_prev is not None:
            xcur = outs[0]
            outs = outs[1:]
        qa, ka, va, ob, xc, gt = outs

        oa = _attn_a(qa, ka, va, t0, t1, diff_lambda[l], diff_out_g[l].reshape(1, LANES), nb, seq, lambda_init)

        qkg = mla_qk_g[l]
        swap = np.concatenate([np.arange(NOPE), NOPE + ROPE // 2 + np.arange(ROPE // 2), NOPE + np.arange(ROPE // 2)])
        wq3 = mla_w_uq[l].reshape(Q_LORA, HC, NOPE + ROPE)
        wq = jnp.concatenate([_head_pad(mla_w_uq[l], HC, NOPE + ROPE, NOPE + ROPE),
                              _head_pad(wq3[:, :, swap].reshape(Q_LORA, -1), HC, NOPE + ROPE, NOPE + ROPE)],
                             axis=1).astype(bf16)
        wkv3 = mla_w_ukv[l].reshape(KV_LORA, HC, NOPE + VC)
        wk = jnp.pad(wkv3[:, :, :NOPE], ((0, 0), (0, 0), (0, LANES - NOPE))).reshape(KV_LORA, HC * LANES)
        wv = wkv3[:, :, NOPE:].reshape(KV_LORA, HC * VC)
        wkv = jnp.concatenate([wk, wv], axis=1).astype(bf16)
        gq_c = jnp.concatenate([jnp.tile(jnp.pad(g_, (0, LANES - NOPE - ROPE)), HC) for g_ in (qkg[0], qkg[0][swap])]
                               ).reshape(1, 2 * HC * LANES) * ((NOPE + ROPE) ** -0.5 * LOG2E)
        gk_c = jnp.pad(qkg[1, :NOPE], (0, LANES - NOPE)).reshape(1, LANES)
        gr_c = jnp.pad(qkg[1, NOPE:], (0, LANES - ROPE)).reshape(1, LANES)
        qc, kc, vc = _mla_prep(xc, cosf, sinf, mla_lat_g[l, :Q_LORA].reshape(1, Q_LORA),
                               mla_lat_g[l, Q_LORA:].reshape(1, KV_LORA), wq, wkv, gq_c, gk_c, gr_c, sd_q, ex_q)
        oc = _attn_c(qc, kc, vc, mk, nb, seq)

        wr = zpad(jnp.concatenate([router_e_w[l], router_g_w[l]], axis=1), LANES)
        wr_hi = wr.astype(bf16)
        wr = jnp.concatenate([wr_hi, (wr - wr_hi.astype(f32)).astype(bf16)], axis=1)
        br = zpad(jnp.concatenate([router_e_b[l], router_g_b[l]]).reshape(1, ne + N_GROUPS), LANES)
        x1, h2ext, meta, cnt = _merge(oa, ob, oc, gt, xcur, gate1, shift2, scale2, norm_g[l, 1].reshape(1, D),
                                      w_branch[l].astype(bf16), w_out[l].astype(bf16), wr, br, seq)

        pos, tt = _tables(meta, cnt)
        tok = _plan(pos.reshape(n))
        ntmax = _num_tiles(n)
        y_prev = _moe(h2ext, meta, tok, tt[:ntmax, 0], tt[:ntmax, 1], tt[0:1, 2], tt[:ntmax, 3],
                      wg_all, wu_all, wd_all, l)
        g2_prev = gate2
        xcur = x1

    out = _final_combine(xcur, y_prev, g2_prev, seq)
    return out.reshape(nb, seq, D)
```
